```python
import math
import jax, jax.numpy as jnp
from jax import lax
import numpy as np

D_MODEL = 1024
BATCH = 2
SEQ = 8192
DEPTH = 2

HEAD_DIM = 64
N_BRANCH = 4
BRANCH_WIDTH = D_MODEL // 2

SWA_HEADS = BRANCH_WIDTH // HEAD_DIM
SWA_KV_HEADS = SWA_HEADS // 4
SWA_WINDOW = 128
SWA_BLOCK = 128

CONV_CH = BRANCH_WIDTH
CONV_WIDTH = 31

FOX_HEADS = BRANCH_WIDTH // HEAD_DIM
FOX_QBLOCK = 128
FOX_FGATE_BIAS = 3.0

MOBA_HEADS = BRANCH_WIDTH // HEAD_DIM
MOBA_BLOCK = 256
MOBA_TOPK = 3
MOBA_QCHUNK = 128

D_FF = 2816
FFN_CONV_WIDTH = 3

REL_BUCKETS = 32
REL_MAX_DIST = 128
N_BIAS_HEADS = SWA_HEADS + MOBA_HEADS

EPS = 1e-6
NEG = -1e30

IN_SIZES = (
    SWA_HEADS * HEAD_DIM, SWA_KV_HEADS * HEAD_DIM, SWA_KV_HEADS * HEAD_DIM,
    2 * CONV_CH,
    FOX_HEADS * HEAD_DIM, FOX_HEADS * HEAD_DIM, FOX_HEADS * HEAD_DIM, FOX_HEADS,
    MOBA_HEADS * HEAD_DIM, MOBA_HEADS * HEAD_DIM, MOBA_HEADS * HEAD_DIM,
    N_BRANCH * D_MODEL,
)
D_IN = sum(IN_SIZES)
IN_SPLITS = tuple(int(s) for s in np.cumsum(IN_SIZES)[:-1])

kernel_name = 'hybrid_gated_swa_conformer_fox_moba'


def rmsnorm(x, g):
    x32 = x.astype(jnp.float32)
    y = x32 * lax.rsqrt(jnp.mean(x32 * x32, axis=-1, keepdims=True) + EPS)
    return (y * g.astype(jnp.float32)).astype(x.dtype)


def layernorm(x, g, b):
    x32 = x.astype(jnp.float32)
    mu = jnp.mean(x32, axis=-1, keepdims=True)
    xc = x32 - mu
    var = jnp.mean(xc * xc, axis=-1, keepdims=True)
    return (xc * lax.rsqrt(var + EPS) * g.astype(jnp.float32) + b.astype(jnp.float32)).astype(x.dtype)


def causal_depthwise_conv(x, w, b):
    k_width, ch = w.shape
    y = lax.conv_general_dilated(
        x, w[:, None, :].astype(x.dtype), window_strides=(1,), padding=[(k_width - 1, 0)],
        dimension_numbers=('NWC', 'WIO', 'NWC'), feature_group_count=ch)
    return y + b.astype(x.dtype)


def rel_bucket(dist):
    n = jnp.maximum(dist, 0)
    max_exact = REL_BUCKETS // 2
    nf = jnp.maximum(n, 1).astype(jnp.float32)
    large = max_exact + (jnp.log(nf / max_exact) / math.log(REL_MAX_DIST / max_exact)
                         * (REL_BUCKETS - max_exact)).astype(jnp.int32)
    large = jnp.minimum(large, REL_BUCKETS - 1)
    return jnp.where(n < max_exact, n, large)


def sliding_window_gqa(q, k, v, sinks, bias_tab):
    bsz, seq = q.shape[0], q.shape[1]
    nb = seq // SWA_BLOCK
    grp = SWA_HEADS // SWA_KV_HEADS
    qb = q.reshape(bsz, nb, SWA_BLOCK, SWA_KV_HEADS, grp, HEAD_DIM)
    kb = k.reshape(bsz, nb, SWA_BLOCK, SWA_KV_HEADS, HEAD_DIM)
    vb = v.reshape(bsz, nb, SWA_BLOCK, SWA_KV_HEADS, HEAD_DIM)
    shift = lambda t: jnp.concatenate([jnp.zeros_like(t[:, :1]), t[:, :-1]], axis=1)
    kk = jnp.concatenate([shift(kb), kb], axis=2)
    vv = jnp.concatenate([shift(vb), vb], axis=2)
    logits = jnp.einsum('bnqkgd,bnskd->bnkgqs', qb, kk).astype(jnp.float32) * (HEAD_DIM ** -0.5)
    qpos = jnp.arange(SWA_BLOCK) + SWA_BLOCK
    kpos = jnp.arange(2 * SWA_BLOCK)
    dist = qpos[:, None] - kpos[None, :]
    bias = bias_tab.astype(jnp.float32)[rel_bucket(dist)]
    bias = bias.transpose(2, 0, 1).reshape(SWA_KV_HEADS, grp, SWA_BLOCK, 2 * SWA_BLOCK)
    in_window = (dist >= 0) & (dist < SWA_WINDOW)
    has_prev = (jnp.arange(nb)[:, None] > 0) | (kpos[None, :] >= SWA_BLOCK)
    mask = in_window[None, :, :] & has_prev[:, None, :]
    logits = jnp.where(mask[None, :, None, None], logits + bias, NEG)
    sink = sinks.astype(jnp.float32).reshape(SWA_KV_HEADS, grp, 1, 1)
    m = jnp.maximum(jnp.max(logits, axis=-1, keepdims=True), sink)
    p = jnp.exp(logits - m)
    denom = jnp.sum(p, axis=-1, keepdims=True) + jnp.exp(sink - m)
    probs = (p / denom).astype(v.dtype)
    out = jnp.einsum('bnkgqs,bnskd->bnqkgd', probs, vv)
    return out.reshape(bsz, seq, SWA_HEADS * HEAD_DIM)


def conformer_conv(u, w_dw, b_dw, ln_g, ln_b):
    a, gt = jnp.split(u, 2, axis=-1)
    h = a * jax.nn.sigmoid(gt)
    h = causal_depthwise_conv(h, w_dw, b_dw)
    h = layernorm(h, ln_g, ln_b)
    return jax.nn.silu(h)


def forgetting_attention(q, k, v, f_logit, b_f):
    bsz, seq = q.shape[0], q.shape[1]
    nb = seq // FOX_QBLOCK
    q = q.reshape(bsz, seq, FOX_HEADS, HEAD_DIM)
    k = k.reshape(bsz, seq, FOX_HEADS, HEAD_DIM)
    v = v.reshape(bsz, seq, FOX_HEADS, HEAD_DIM)
    log_f = jax.nn.log_sigmoid(f_logit.astype(jnp.float32) + b_f.astype(jnp.float32))
    c = jnp.cumsum(log_f, axis=1).transpose(0, 2, 1)
    qb = q.reshape(bsz, nb, FOX_QBLOCK, FOX_HEADS, HEAD_DIM).transpose(1, 0, 2, 3, 4)
    cb = c.reshape(bsz, FOX_HEADS, nb, FOX_QBLOCK).transpose(2, 0, 1, 3)
    kpos = jnp.arange(seq)

    def block(args):
        qi, ci, i = args
        logits = jnp.einsum('bqhd,bshd->bhqs', qi, k).astype(jnp.float32) * (HEAD_DIM ** -0.5)
        logits = logits + ci[..., None] - c[:, :, None, :]
        qpos = i * FOX_QBLOCK + jnp.arange(FOX_QBLOCK)
        logits = jnp.where(kpos[None, :] <= qpos[:, None], logits, NEG)
        p = jax.nn.softmax(logits, axis=-1).astype(v.dtype)
        return jnp.einsum('bhqs,bshd->bqhd', p, v)

    out = lax.map(block, (qb, cb, jnp.arange(nb)))
    return out.transpose(1, 0, 2, 3, 4).reshape(bsz, seq, FOX_HEADS * HEAD_DIM)


def moba_attention(q, k, v, bias_tab):
    bsz, seq = q.shape[0], q.shape[1]
    nblk = -(-seq // MOBA_BLOCK)
    s_pad = nblk * MOBA_BLOCK
    padw = ((0, 0), (0, s_pad - seq), (0, 0), (0, 0))
    to_heads = lambda t: jnp.pad(t.reshape(bsz, seq, MOBA_HEADS, HEAD_DIM), padw).transpose(0, 2, 1, 3)
    q, k, v = to_heads(q), to_heads(k), to_heads(v)
    kb = k.reshape(bsz, MOBA_HEADS, nblk, MOBA_BLOCK, HEAD_DIM)
    vb = v.reshape(bsz, MOBA_HEADS, nblk, MOBA_BLOCK, HEAD_DIM)
    kmean = jnp.mean(kb.astype(jnp.float32), axis=3)
    gate = jnp.einsum('bhsd,bhnd->bhsn', q.astype(jnp.float32), kmean)
    qblk = jnp.arange(s_pad) // MOBA_BLOCK
    past = jnp.arange(nblk)[None, :] < qblk[:, None]
    gate = jnp.where(past, gate, NEG)
    n_sel = min(MOBA_TOPK, nblk)
    _, sel = lax.top_k(gate, n_sel)
    sel_valid = sel < qblk[None, None, :, None]
    bt = bias_tab.astype(jnp.float32).T
    bidx = jnp.arange(bsz)[:, None, None, None]
    hidx = jnp.arange(MOBA_HEADS)[None, :, None, None]
    hidx5 = jnp.arange(MOBA_HEADS)[None, :, None, None, None]
    scale = HEAD_DIM ** -0.5
    nq = s_pad // MOBA_QCHUNK

    def chunk(i):
        start = i * MOBA_QCHUNK
        qc = lax.dynamic_slice_in_dim(q, start, MOBA_QCHUNK, axis=2)
        selc = lax.dynamic_slice_in_dim(sel, start, MOBA_QCHUNK, axis=2)
        validc = lax.dynamic_slice_in_dim(sel_valid, start, MOBA_QCHUNK, axis=2)
        qpos = start + jnp.arange(MOBA_QCHUNK)
        kg = kb[bidx, hidx, selc]
        vg = vb[bidx, hidx, selc]
        kpos_g = selc[..., None] * MOBA_BLOCK + jnp.arange(MOBA_BLOCK)
        lg = jnp.einsum('bhqd,bhqnkd->bhqnk', qc, kg).astype(jnp.float32) * scale
        lg = lg + bt[hidx5, rel_bucket(qpos[None, None, :, None, None] - kpos_g)]
        lg = jnp.where(validc[..., None], lg, NEG).reshape(bsz, MOBA_HEADS, MOBA_QCHUNK, n_sel * MOBA_BLOCK)
        own = start // MOBA_BLOCK
        ko = lax.dynamic_slice_in_dim(k, own * MOBA_BLOCK, MOBA_BLOCK, axis=2)
        vo = lax.dynamic_slice_in_dim(v, own * MOBA_BLOCK, MOBA_BLOCK, axis=2)
        kpos_o = own * MOBA_BLOCK + jnp.arange(MOBA_BLOCK)
        dist_o = qpos[:, None] - kpos_o[None, :]
        lo = jnp.einsum('bhqd,bhsd->bhqs', qc, ko).astype(jnp.float32) * scale + bt[:, rel_bucket(dist_o)]
        lo = jnp.where(dist_o >= 0, lo, NEG)
        p = jax.nn.softmax(jnp.concatenate([lg, lo], axis=-1), axis=-1).astype(v.dtype)
        pg = p[..., :n_sel * MOBA_BLOCK].reshape(bsz, MOBA_HEADS, MOBA_QCHUNK, n_sel, MOBA_BLOCK)
        po = p[..., n_sel * MOBA_BLOCK:]
        return jnp.einsum('bhqnk,bhqnkd->bhqd', pg, vg) + jnp.einsum('bhqs,bhsd->bhqd', po, vo)

    out = lax.map(chunk, jnp.arange(nq))
    out = out.transpose(1, 0, 3, 2, 4).reshape(bsz, s_pad, MOBA_HEADS * HEAD_DIM)
    return out[:, :seq]


def hybrid_layer(x, ln1_g, w_in, b_gate, b_fgate, sinks, conv_w, conv_b, conv_ln_g, conv_ln_b,
                 w_br, w_out, ln2_g, w_up, ffn_conv_w, ffn_conv_b, w_down, rel_bias):
    bsz, seq = x.shape[0], x.shape[1]
    h = rmsnorm(x, ln1_g)
    z = h @ w_in
    (qa, ka, va, ub, qc, kc, vc, fc, qd, kd, vd, gz) = jnp.split(z, IN_SPLITS, axis=-1)
    gates = jax.nn.sigmoid(gz.reshape(bsz, seq, N_BRANCH, D_MODEL) + b_gate)
    ya = sliding_window_gqa(qa, ka, va, sinks, rel_bias[:, :SWA_HEADS])
    yb = conformer_conv(ub, conv_w, conv_b, conv_ln_g, conv_ln_b)
    yc = forgetting_attention(qc, kc, vc, fc, b_fgate)
    yd = moba_attention(qd, kd, vd, rel_bias[:, SWA_HEADS:])
    branches = jnp.stack([ya, yb, yc, yd], axis=2)
    proj = jnp.einsum('bsnc,ncd->bsnd', branches, w_br)
    merged = jnp.einsum('bsnd,bsnd->bsd', gates, proj)
    x = x + merged @ w_out
    h2 = rmsnorm(x, ln2_g)
    u = causal_depthwise_conv(h2 @ w_up, ffn_conv_w, ffn_conv_b)
    g_ff, v_ff = jnp.split(u, 2, axis=-1)
    return x + (jax.nn.silu(g_ff) * v_ff) @ w_down


def setup_inputs(seed: int = 0) -> dict:
    key = jax.random.key(seed)
    ks = jax.random.split(key, 20)
    f32 = jnp.float32
    nrm = lambda kk, shape, s: jax.random.normal(kk, shape, f32) * s
    return {
        'x': nrm(ks[0], (BATCH, SEQ, D_MODEL), 1.0),
        'ln1_g': 1.0 + nrm(ks[1], (DEPTH, D_MODEL), 0.05),
        'w_in': nrm(ks[2], (DEPTH, D_MODEL, D_IN), D_MODEL ** -0.5),
        'b_gate': nrm(ks[3], (DEPTH, N_BRANCH, D_MODEL), 0.1),
        'b_fgate': FOX_FGATE_BIAS + nrm(ks[4], (DEPTH, FOX_HEADS), 0.1),
        'sinks': nrm(ks[5], (DEPTH, SWA_HEADS), 0.5),
        'conv_w': nrm(ks[6], (DEPTH, CONV_WIDTH, CONV_CH), CONV_WIDTH ** -0.5),
        'conv_b': nrm(ks[7], (DEPTH, CONV_CH), 0.02),
        'conv_ln_g': 1.0 + nrm(ks[8], (DEPTH, CONV_CH), 0.05),
        'conv_ln_b': nrm(ks[9], (DEPTH, CONV_CH), 0.02),
        'w_br': nrm(ks[10], (DEPTH, N_BRANCH, BRANCH_WIDTH, D_MODEL), BRANCH_WIDTH ** -0.5),
        'w_out': nrm(ks[11], (DEPTH, D_MODEL, D_MODEL), D_MODEL ** -0.5),
        'ln2_g': 1.0 + nrm(ks[12], (DEPTH, D_MODEL), 0.05),
        'w_up': nrm(ks[13], (DEPTH, D_MODEL, 2 * D_FF), D_MODEL ** -0.5),
        'ffn_conv_w': nrm(ks[14], (DEPTH, FFN_CONV_WIDTH, 2 * D_FF), FFN_CONV_WIDTH ** -0.5),
        'ffn_conv_b': nrm(ks[15], (DEPTH, 2 * D_FF), 0.02),
        'w_down': nrm(ks[16], (DEPTH, D_FF, D_MODEL), D_FF ** -0.5),
        'rel_bias': nrm(ks[17], (REL_BUCKETS, N_BIAS_HEADS), 0.5),
        'final_g': 1.0 + nrm(ks[18], (D_MODEL,), 0.05),
    }


def reference(x, ln1_g, w_in, b_gate, b_fgate, sinks, conv_w, conv_b, conv_ln_g, conv_ln_b,
              w_br, w_out, ln2_g, w_up, ffn_conv_w, ffn_conv_b, w_down, rel_bias, final_g):
    for layer in range(DEPTH):
        x = hybrid_layer(x, ln1_g[layer], w_in[layer], b_gate[layer], b_fgate[layer], sinks[layer],
                         conv_w[layer], conv_b[layer], conv_ln_g[layer], conv_ln_b[layer],
                         w_br[layer], w_out[layer], ln2_g[layer], w_up[layer], ffn_conv_w[layer],
                         ffn_conv_b[layer], w_down[layer], rel_bias)
    return rmsnorm(x, final_g)
```

```python
import functools
import math

import jax
import jax.numpy as jnp
import numpy as np
from jax import lax
from jax.experimental import pallas as pl
from jax.experimental.pallas import tpu as pltpu

D_MODEL = 1024
HEAD_DIM = 64
N_BRANCH = 4
BRANCH_WIDTH = 512
N_HEADS = 8
N_PAIRS = N_HEADS // 2
SWA_KV_HEADS = 2
SWA_BLOCK = 128
CONV_WIDTH = 31
MOBA_BLOCK = 256
MOBA_TOPK = 3
D_FF = 2816
FFN_CONV_WIDTH = 3
REL_BUCKETS = 32
REL_MAX_DIST = 128
EPS = 1e-6
NEG = -1e30
SCALE = HEAD_DIM ** -0.5

LANES = 128
SUBLANES = 8
VMEM_LIMIT = 56 * 1024 * 1024

TM = 512
TQ = 256
TK = 256
CUM_T = 256
FF_CHUNK = 256
CONV_ROWS = 64
CONV_HALO = 32

BF16 = jnp.bfloat16
F32 = jnp.float32


def _params(*sem):
    return pltpu.CompilerParams(dimension_semantics=sem, vmem_limit_bytes=VMEM_LIMIT)


def _resident(shape):
    zeros = (0,) * len(shape)
    return pl.BlockSpec(shape, lambda *_: zeros, pipeline_mode=pl.Buffered(1))


def _rmsnorm(x, g):
    return (x * lax.rsqrt(jnp.mean(x * x, axis=-1, keepdims=True) + EPS)) * g


def _rel_bucket_np(dist):
    n = np.maximum(dist, 0)
    max_exact = REL_BUCKETS // 2
    nf = np.maximum(n, 1).astype(np.float64)
    large = max_exact + (np.log(nf / max_exact) / math.log(REL_MAX_DIST / max_exact)
                         * (REL_BUCKETS - max_exact) + 1e-9).astype(np.int32)
    large = np.minimum(large, REL_BUCKETS - 1)
    return np.where(n < max_exact, n, large).astype(np.int32)


def _bias_kernel(tab_ref, bs_ref, bo_ref, bp_ref, os_ref, oo_ref, op_ref):
    os_ref[...] = jnp.zeros_like(os_ref)
    oo_ref[...] = jnp.zeros_like(oo_ref)
    op_ref[...] = jnp.zeros_like(op_ref)
    bs, bo, bp = bs_ref[...], bo_ref[...], bp_ref[...]

    def body(b, carry):
        ms, mo, mp = bs == b, bo == b, bp == b
        for h in range(N_HEADS):
            os_ref[h] = jnp.where(ms, tab_ref[b, h], os_ref[h])
            oo_ref[h] = jnp.where(mo, tab_ref[b, N_HEADS + h], oo_ref[h])
            op_ref[h] = jnp.where(mp, tab_ref[b, N_HEADS + h], op_ref[h])
        return carry

    lax.fori_loop(0, REL_BUCKETS, body, 0)


def _expand_bias(rel_bias):
    qi = np.arange(SWA_BLOCK)[:, None]
    kj = np.arange(2 * SWA_BLOCK)[None, :]
    b_swa = _rel_bucket_np(qi + SWA_BLOCK - kj)
    qi = np.arange(MOBA_BLOCK)[:, None]
    kj = np.arange(MOBA_BLOCK)[None, :]
    b_own = _rel_bucket_np(qi - kj)
    b_prev = _rel_bucket_np(qi - kj + MOBA_BLOCK)
    vm = pl.BlockSpec(memory_space=pltpu.VMEM)
    return pl.pallas_call(
        _bias_kernel,
        out_shape=(jax.ShapeDtypeStruct((N_HEADS, SWA_BLOCK, 2 * SWA_BLOCK), F32),
                   jax.ShapeDtypeStruct((N_HEADS, MOBA_BLOCK, MOBA_BLOCK), F32),
                   jax.ShapeDtypeStruct((N_HEADS, MOBA_BLOCK, MOBA_BLOCK), F32)),
        in_specs=[pl.BlockSpec(memory_space=pltpu.SMEM), vm, vm, vm],
        out_specs=(vm, vm, vm),
        name="bias_expand",
    )(rel_bias, jnp.asarray(b_swa), jnp.asarray(b_own), jnp.asarray(b_prev))


def _inproj_kernel(x_ref, g_ref, wa_ref, wb_ref, wc_ref, wf_ref, wd_ref,
                   oa_ref, ob_ref, oc_ref, of_ref, od_ref):
    h = _rmsnorm(x_ref[...], g_ref[...])
    hb = h.astype(BF16)
    oa_ref[...] = jnp.dot(hb, wa_ref[...], preferred_element_type=F32).astype(BF16)
    ob_ref[...] = jnp.dot(hb, wb_ref[...], preferred_element_type=F32)
    oc_ref[...] = jnp.dot(hb, wc_ref[...], preferred_element_type=F32).astype(BF16)
    od_ref[...] = jnp.dot(hb, wd_ref[...], preferred_element_type=F32).astype(BF16)
    of_ref[...] = jnp.dot(h, wf_ref[...], preferred_element_type=F32,
                          precision=lax.Precision.HIGHEST)


def _inproj(x, g, wa, wb, wc, wf, wd):
    n = x.shape[0]
    row = lambda w: pl.BlockSpec((TM, w), lambda i: (i, 0))
    ws = (wa, wb, wc, wf, wd)
    dts = (BF16, F32, BF16, F32, BF16)
    return pl.pallas_call(
        _inproj_kernel,
        grid=(n // TM,),
        in_specs=[row(D_MODEL), _resident(g.shape)] + [_resident(w.shape) for w in ws],
        out_specs=tuple(row(w.shape[1]) for w in ws),
        out_shape=tuple(jax.ShapeDtypeStruct((n, w.shape[1]), dt) for w, dt in zip(ws, dts)),
        compiler_params=_params("arbitrary"),
        name="inproj",
    )(x, g, *ws)


def _swa_kernel(sink_ref, q_ref, kc_ref, kp_ref, vc_ref, vp_ref, bias_ref, o_ref, *, blocks_per_seq):
    blk = pl.program_id(0) % blocks_per_seq
    kk = jnp.concatenate([kp_ref[...], kc_ref[...]], axis=0)
    vv = jnp.concatenate([vp_ref[...], vc_ref[...]], axis=0)
    qi = lax.broadcasted_iota(jnp.int32, (SWA_BLOCK, 2 * SWA_BLOCK), 0)
    kj = lax.broadcasted_iota(jnp.int32, (SWA_BLOCK, 2 * SWA_BLOCK), 1)
    dist = qi + SWA_BLOCK - kj
    mask = (dist >= 0) & (dist < SWA_BLOCK) & ((blk > 0) | (kj >= SWA_BLOCK))
    lane = lax.broadcasted_iota(jnp.int32, (SWA_BLOCK, LANES), 1)
    upper = lane >= HEAD_DIM
    for pair in range(N_PAIRS):
        kvh = (2 * pair) // (N_HEADS // SWA_KV_HEADS)
        q_pair = q_ref[:, pair * LANES:(pair + 1) * LANES] * SCALE
        k_kv = kk[:, kvh * LANES:(kvh + 1) * LANES]
        v_kv = vv[:, kvh * LANES:(kvh + 1) * LANES]
        outs = []
        for hh in range(2):
            h = 2 * pair + hh
            q_h = jnp.where(upper == (hh == 1), q_pair, jnp.zeros_like(q_pair))
            s = lax.dot_general(q_h, k_kv, (((1,), (1,)), ((), ())), preferred_element_type=F32)
            logits = jnp.where(mask, s + bias_ref[h], NEG)
            sink = sink_ref[h]
            m = jnp.maximum(jnp.max(logits, axis=-1, keepdims=True), sink)
            p = jnp.exp(logits - m)
            denom = jnp.sum(p, axis=-1, keepdims=True) + jnp.exp(sink - m)
            probs = (p / denom).astype(BF16)
            outs.append(jnp.dot(probs, v_kv, preferred_element_type=F32))
        o_ref[:, pair * LANES:(pair + 1) * LANES] = jnp.where(upper, outs[1], outs[0]).astype(BF16)


def _swa(qkv, sinks, bias, seq):
    n = qkv.shape[0]
    bps = seq // SWA_BLOCK
    kcol, vcol = BRANCH_WIDTH // (2 * LANES), BRANCH_WIDTH // (2 * LANES) + 1
    prev = lambda i: jnp.where(i % bps == 0, i, i - 1)
    return pl.pallas_call(
        functools.partial(_swa_kernel, blocks_per_seq=bps),
        grid=(n // SWA_BLOCK,),
        in_specs=[pl.BlockSpec(memory_space=pltpu.SMEM),
                  pl.BlockSpec((SWA_BLOCK, BRANCH_WIDTH), lambda i: (i, 0)),
                  pl.BlockSpec((SWA_BLOCK, 2 * LANES), lambda i: (i, kcol)),
                  pl.BlockSpec((SWA_BLOCK, 2 * LANES), lambda i: (prev(i), kcol)),
                  pl.BlockSpec((SWA_BLOCK, 2 * LANES), lambda i: (i, vcol)),
                  pl.BlockSpec((SWA_BLOCK, 2 * LANES), lambda i: (prev(i), vcol)),
                  _resident(bias.shape)],
        out_specs=pl.BlockSpec((SWA_BLOCK, BRANCH_WIDTH), lambda i: (i, 0)),
        out_shape=jax.ShapeDtypeStruct((n, BRANCH_WIDTH), BF16),
        compiler_params=_params("arbitrary"),
        name="swa",
    )(sinks, qkv, qkv, qkv, qkv, qkv, bias)


def _conformer_kernel(u_ref, w_ref, b_ref, g_ref, beta_ref, o_ref, buf_ref):
    @pl.when(pl.program_id(1) == 0)
    def _():
        buf_ref[0:CONV_HALO, :] = jnp.zeros((CONV_HALO, BRANCH_WIDTH), F32)

    a = u_ref[:, 0:BRANCH_WIDTH]
    gt = u_ref[:, BRANCH_WIDTH:2 * BRANCH_WIDTH]
    buf_ref[CONV_HALO:CONV_HALO + TM, :] = a * jax.nn.sigmoid(gt)
    first = CONV_HALO - (CONV_WIDTH - 1)
    for r in range(TM // CONV_ROWS):
        acc = jnp.zeros((CONV_ROWS, BRANCH_WIDTH), F32)
        for k in range(CONV_WIDTH):
            lo = first + r * CONV_ROWS + k
            acc = acc + w_ref[k:k + 1, :] * buf_ref[lo:lo + CONV_ROWS, :]
        y = acc + b_ref[...]
        mu = jnp.mean(y, axis=-1, keepdims=True)
        yc = y - mu
        var = jnp.mean(yc * yc, axis=-1, keepdims=True)
        z = yc * lax.rsqrt(var + EPS) * g_ref[...] + beta_ref[...]
        o_ref[r * CONV_ROWS:(r + 1) * CONV_ROWS, :] = (z * jax.nn.sigmoid(z)).astype(BF16)
    buf_ref[0:CONV_HALO, :] = buf_ref[TM:TM + CONV_HALO, :]


def _conformer(u, w, b, g, beta, bsz, seq):
    n = u.shape[0]
    spt = seq // TM
    return pl.pallas_call(
        _conformer_kernel,
        grid=(bsz, spt),
        in_specs=[pl.BlockSpec((TM, 2 * BRANCH_WIDTH), lambda bi, i: (bi * spt + i, 0)),
                  _resident(w.shape), _resident(b.shape), _resident(g.shape), _resident(beta.shape)],
        out_specs=pl.BlockSpec((TM, BRANCH_WIDTH), lambda bi, i: (bi * spt + i, 0)),
        out_shape=jax.ShapeDtypeStruct((n, BRANCH_WIDTH), BF16),
        scratch_shapes=[pltpu.VMEM((TM + CONV_HALO, BRANCH_WIDTH), F32)],
        compiler_params=_params("arbitrary", "arbitrary"),
        name="conformer",
    )(u, w, b, g, beta)


def _fgate_kernel(f_ref, b_ref, ccol_ref, crow_ref, carry_ref):
    @pl.when(pl.program_id(1) == 0)
    def _():
        carry_ref[...] = jnp.zeros_like(carry_ref)

    z = f_ref[...] + b_ref[...]
    log_f = -(jnp.maximum(-z, 0.0) + jnp.log1p(jnp.exp(-jnp.abs(z))))
    ri = lax.broadcasted_iota(jnp.int32, (CUM_T, CUM_T), 0)
    ci = lax.broadcasted_iota(jnp.int32, (CUM_T, CUM_T), 1)
    tri = (ci <= ri).astype(F32)
    c = jnp.dot(tri, log_f, preferred_element_type=F32,
                precision=lax.Precision.HIGHEST) + carry_ref[0:1, :]
    ccol_ref[...] = c
    crow_ref[...] = c.T[0:N_HEADS, :]
    carry_ref[...] = jnp.broadcast_to(c[CUM_T - 1:CUM_T, :], carry_ref.shape)


def _fgate(f, b, bsz, seq):
    n = f.shape[0]
    spt = seq // CUM_T
    return pl.pallas_call(
        _fgate_kernel,
        grid=(bsz, spt),
        in_specs=[pl.BlockSpec((CUM_T, LANES), lambda bi, i: (bi * spt + i, 0)), _resident(b.shape)],
        out_specs=(pl.BlockSpec((CUM_T, LANES), lambda bi, i: (bi * spt + i, 0)),
                   pl.BlockSpec((None, N_HEADS, CUM_T), lambda bi, i: (bi, 0, i))),
        out_shape=(jax.ShapeDtypeStruct((n, LANES), F32),
                   jax.ShapeDtypeStruct((bsz, N_HEADS, seq), F32)),
        scratch_shapes=[pltpu.VMEM((SUBLANES, LANES), F32)],
        compiler_params=_params("arbitrary", "arbitrary"),
        name="fgate_cumsum",
    )(f, b)


def _wide(col):
    return jnp.concatenate([col] * (TK // LANES), axis=1)


def _softmax_init(hh, s, v_tile, m_ref, l_ref, acc_ref):
    m = jnp.broadcast_to(jnp.max(s, axis=-1, keepdims=True), (TQ, LANES))
    p = jnp.exp(s - _wide(m))
    m_ref[hh] = m
    l_ref[hh] = jnp.broadcast_to(jnp.sum(p, axis=-1, keepdims=True), (TQ, LANES))
    acc_ref[hh] = jnp.dot(p.astype(BF16), v_tile, preferred_element_type=F32)


def _softmax_step(hh, s, v_tile, m_ref, l_ref, acc_ref):
    m_prev = m_ref[hh]
    m_new = jnp.maximum(m_prev, jnp.broadcast_to(jnp.max(s, axis=-1, keepdims=True), (TQ, LANES)))
    alpha = jnp.exp(m_prev - m_new)
    p = jnp.exp(s - _wide(m_new))
    m_ref[hh] = m_new
    l_ref[hh] = alpha * l_ref[hh] + jnp.broadcast_to(jnp.sum(p, axis=-1, keepdims=True), (TQ, LANES))
    acc_ref[hh] = alpha * acc_ref[hh] + jnp.dot(p.astype(BF16), v_tile, preferred_element_type=F32)


def _head_split(q_pair):
    lane = lax.broadcasted_iota(jnp.int32, q_pair.shape, 1)
    zero = jnp.zeros_like(q_pair)
    return jnp.where(lane < HEAD_DIM, q_pair, zero), jnp.where(lane >= HEAD_DIM, q_pair, zero)


def _pair_output(o_ref, l_ref, acc_ref):
    lane = lax.broadcasted_iota(jnp.int32, (TQ, LANES), 1)
    out = jnp.where(lane < HEAD_DIM, acc_ref[0] / l_ref[0], acc_ref[1] / l_ref[1])
    o_ref[...] = out.astype(BF16)


def _qk(q_h, k_tile):
    return lax.dot_general(q_h, k_tile, (((1,), (1,)), ((), ())), preferred_element_type=F32)


def _fox_kernel(q_ref, k_ref, v_ref, ccol_ref, crow_ref, o_ref, m_ref, l_ref, acc_ref):
    pair = pl.program_id(1)
    qt = pl.program_id(2)
    q_heads = _head_split(q_ref[...] * SCALE)
    lane = lax.broadcasted_iota(jnp.int32, (TQ, LANES), 1)
    ccol = ccol_ref[...]
    c_t = [jnp.sum(jnp.where(lane == 2 * pair + hh, ccol, 0.0), axis=-1, keepdims=True)
           for hh in range(2)]

    def logits(hh, k_tile, start):
        c_s = crow_ref[pl.ds(2 * pair + hh, 1), pl.ds(start, TK)]
        return _qk(q_heads[hh], k_tile) + c_t[hh] - c_s

    start = pl.multiple_of(qt * TQ, TQ)
    k_tile, v_tile = k_ref[pl.ds(start, TK), :], v_ref[pl.ds(start, TK), :]
    causal = (lax.broadcasted_iota(jnp.int32, (TQ, TK), 1)
              <= lax.broadcasted_iota(jnp.int32, (TQ, TK), 0))
    for hh in range(2):
        s = jnp.where(causal, logits(hh, k_tile, start), NEG)
        _softmax_init(hh, s, v_tile, m_ref, l_ref, acc_ref)

    def body(j, carry):
        start = pl.multiple_of(j * TK, TK)
        k_tile, v_tile = k_ref[pl.ds(start, TK), :], v_ref[pl.ds(start, TK), :]
        for hh in range(2):
            _softmax_step(hh, logits(hh, k_tile, start), v_tile, m_ref, l_ref, acc_ref)
        return carry

    lax.fori_loop(0, qt, body, 0)
    _pair_output(o_ref, l_ref, acc_ref)


def _fox(qkv, ccol, crow, bsz, seq):
    n = qkv.shape[0]
    qpt = seq // TQ
    koff, voff = BRANCH_WIDTH // LANES, 2 * BRANCH_WIDTH // LANES
    return pl.pallas_call(
        _fox_kernel,
        grid=(bsz, N_PAIRS, qpt),
        in_specs=[pl.BlockSpec((TQ, LANES), lambda b, p, i: (b * qpt + i, p)),
                  pl.BlockSpec((seq, LANES), lambda b, p, i: (b, koff + p)),
                  pl.BlockSpec((seq, LANES), lambda b, p, i: (b, voff + p)),
                  pl.BlockSpec((TQ, LANES), lambda b, p, i: (b * qpt + i, 0)),
                  pl.BlockSpec((None, N_HEADS, seq), lambda b, p, i: (b, 0, 0))],
        out_specs=pl.BlockSpec((TQ, LANES), lambda b, p, i: (b * qpt + i, p)),
        out_shape=jax.ShapeDtypeStruct((n, BRANCH_WIDTH), BF16),
        scratch_shapes=[pltpu.VMEM((2, TQ, LANES), F32)] * 3,
        compiler_params=_params("arbitrary", "arbitrary", "arbitrary"),
        name="fox",
    )(qkv, qkv, qkv, ccol, crow)


def _moba_kernel(tab_ref, q_ref, k_ref, v_ref, bown_ref, bprev_ref, o_ref,
                 kmean_ref, sel_ref, m_ref, l_ref, acc_ref, *, seq):
    pair = pl.program_id(1)
    qt = pl.program_id(2)
    nblk = seq // MOBA_BLOCK

    @pl.when(qt == 0)
    def _():
        blk = lax.broadcasted_iota(jnp.int32, (LANES, seq), 0)
        pos = lax.broadcasted_iota(jnp.int32, (LANES, seq), 1)
        member = jnp.where(pos // MOBA_BLOCK == blk, 1.0 / MOBA_BLOCK, 0.0).astype(BF16)
        kmean_ref[...] = jnp.dot(member, k_ref[...], preferred_element_type=F32)

    q_pair = q_ref[...]
    lane = lax.broadcasted_iota(jnp.int32, (TQ, LANES), 1)
    q_gate = _head_split(q_pair.astype(F32))
    past = lane < qt
    for hh in range(2):
        gate = lax.dot_general(q_gate[hh], kmean_ref[...], (((1,), (1,)), ((), ())),
                               preferred_element_type=F32, precision=lax.Precision.HIGHEST)
        gate = jnp.where(past & (lane < nblk), gate, jnp.where(lane < nblk, NEG, -jnp.inf))
        sel = jnp.zeros((TQ, LANES), F32)
        for _ in range(MOBA_TOPK):
            top = jnp.max(gate, axis=-1, keepdims=True)
            idx = jnp.min(jnp.where(gate == top, lane, LANES), axis=-1, keepdims=True)
            pick = lane == idx
            sel = jnp.where(pick & past, 1.0, sel)
            gate = jnp.where(pick, -jnp.inf, gate)
        sel_ref[hh] = sel

    q_heads = _head_split(q_pair * SCALE)
    start = pl.multiple_of(qt * TQ, TQ)
    k_tile, v_tile = k_ref[pl.ds(start, TK), :], v_ref[pl.ds(start, TK), :]
    causal = (lax.broadcasted_iota(jnp.int32, (TQ, TK), 1)
              <= lax.broadcasted_iota(jnp.int32, (TQ, TK), 0))
    for hh in range(2):
        s = jnp.where(causal, _qk(q_heads[hh], k_tile) + bown_ref[hh], NEG)
        _softmax_init(hh, s, v_tile, m_ref, l_ref, acc_ref)

    def past_block(j, bias_of_head):
        start = pl.multiple_of(j * TK, TK)
        k_tile, v_tile = k_ref[pl.ds(start, TK), :], v_ref[pl.ds(start, TK), :]
        for hh in range(2):
            chosen = jnp.max(jnp.where(lane == j, sel_ref[hh], 0.0), axis=-1, keepdims=True) > 0.5
            s = jnp.where(chosen, _qk(q_heads[hh], k_tile) + bias_of_head(hh), NEG)
            _softmax_step(hh, s, v_tile, m_ref, l_ref, acc_ref)

    def body(j, carry):
        past_block(j, lambda hh: tab_ref[REL_BUCKETS - 1, N_HEADS + 2 * pair + hh])
        return carry

    lax.fori_loop(0, qt - 1, body, 0)

    @pl.when(qt >= 1)
    def _():
        past_block(qt - 1, lambda hh: bprev_ref[hh])

    _pair_output(o_ref, l_ref, acc_ref)


def _moba(qkv, rel_bias, b_own, b_prev, bsz, seq):
    n = qkv.shape[0]
    qpt = seq // TQ
    koff, voff = BRANCH_WIDTH // LANES, 2 * BRANCH_WIDTH // LANES
    pair_bias = pl.BlockSpec((2, MOBA_BLOCK, MOBA_BLOCK), lambda b, p, i: (p, 0, 0))
    return pl.pallas_call(
        functools.partial(_moba_kernel, seq=seq),
        grid=(bsz, N_PAIRS, qpt),
        in_specs=[pl.BlockSpec(memory_space=pltpu.SMEM),
                  pl.BlockSpec((TQ, LANES), lambda b, p, i: (b * qpt + i, p)),
                  pl.BlockSpec((seq, LANES), lambda b, p, i: (b, koff + p)),
                  pl.BlockSpec((seq, LANES), lambda b, p, i: (b, voff + p)),
                  pair_bias, pair_bias],
        out_specs=pl.BlockSpec((TQ, LANES), lambda b, p, i: (b * qpt + i, p)),
        out_shape=jax.ShapeDtypeStruct((n, BRANCH_WIDTH), BF16),
        scratch_shapes=[pltpu.VMEM((LANES, LANES), F32), pltpu.VMEM((2, TQ, LANES), F32)]
        + [pltpu.VMEM((2, TQ, LANES), F32)] * 3,
        compiler_params=_params("arbitrary", "arbitrary", "arbitrary"),
        name="moba",
    )(rel_bias, qkv, qkv, qkv, b_own, b_prev)


def _merge_kernel(x_ref, g_ref, ya_ref, yb_ref, yc_ref, yd_ref, wg_ref, bg_ref, wbr_ref, wout_ref, o_ref):
    x = x_ref[...]
    hb = _rmsnorm(x, g_ref[...]).astype(BF16)
    merged = jnp.zeros((TM, D_MODEL), F32)
    for n, y_ref in enumerate((ya_ref, yb_ref, yc_ref, yd_ref)):
        gz = jnp.dot(hb, wg_ref[:, n * D_MODEL:(n + 1) * D_MODEL], preferred_element_type=F32)
        gate = jax.nn.sigmoid(gz + bg_ref[n:n + 1, :])
        merged = merged + gate * jnp.dot(y_ref[...], wbr_ref[n], preferred_element_type=F32)
    o_ref[...] = x + jnp.dot(merged.astype(BF16), wout_ref[...], preferred_element_type=F32)


def _merge(x, g, ys, wg, bg, wbr, wout):
    n = x.shape[0]
    row = lambda w: pl.BlockSpec((TM, w), lambda i: (i, 0))
    return pl.pallas_call(
        _merge_kernel,
        grid=(n // TM,),
        in_specs=[row(D_MODEL), _resident(g.shape)] + [row(BRANCH_WIDTH)] * N_BRANCH
        + [_resident(wg.shape), _resident(bg.shape), _resident(wbr.shape), _resident(wout.shape)],
        out_specs=row(D_MODEL),
        out_shape=jax.ShapeDtypeStruct((n, D_MODEL), F32),
        compiler_params=_params("arbitrary"),
        name="merge",
    )(x, g, *ys, wg, bg, wbr, wout)


def _ffn_kernel(x_ref, g_ref, wup_ref, cw_ref, cb_ref, wdown_ref, fg_ref, o_ref, halo_ref, buf_ref, *, final):
    @pl.when(pl.program_id(1) == 0)
    def _():
        halo_ref[...] = jnp.zeros_like(halo_ref)

    x = x_ref[...]
    hb = _rmsnorm(x, g_ref[...]).astype(BF16)
    acc = jnp.zeros((TM, D_MODEL), F32)
    for c in range(D_FF // FF_CHUNK):
        halves = []
        for part in range(2):
            col = part * D_FF + c * FF_CHUNK
            buf_ref[0:SUBLANES, :] = halo_ref[:, col:col + FF_CHUNK]
            buf_ref[SUBLANES:SUBLANES + TM, :] = jnp.dot(
                hb, wup_ref[:, col:col + FF_CHUNK], preferred_element_type=F32)
            halo_ref[:, col:col + FF_CHUNK] = buf_ref[TM:TM + SUBLANES, :]
            u = cb_ref[:, col:col + FF_CHUNK]
            for k in range(FFN_CONV_WIDTH):
                lo = SUBLANES - (FFN_CONV_WIDTH - 1) + k
                u = u + cw_ref[k:k + 1, col:col + FF_CHUNK] * buf_ref[lo:lo + TM, :]
            halves.append(u)
        a = (halves[0] * jax.nn.sigmoid(halves[0])) * halves[1]
        acc = acc + jnp.dot(a.astype(BF16), wdown_ref[c * FF_CHUNK:(c + 1) * FF_CHUNK, :],
                            preferred_element_type=F32)
    y = x + acc
    o_ref[...] = _rmsnorm(y, fg_ref[...]) if final else y


def _ffn(x, g, wup, cw, cb, wdown, fg, bsz, seq, final):
    n = x.shape[0]
    spt = seq // TM
    row = pl.BlockSpec((TM, D_MODEL), lambda bi, i: (bi * spt + i, 0))
    return pl.pallas_call(
        functools.partial(_ffn_kernel, final=final),
        grid=(bsz, spt),
        in_specs=[row, _resident(g.shape), _resident(wup.shape), _resident(cw.shape),
                  _resident(cb.shape), _resident(wdown.shape), _resident(fg.shape)],
        out_specs=row,
        out_shape=jax.ShapeDtypeStruct((n, D_MODEL), F32),
        scratch_shapes=[pltpu.VMEM((SUBLANES, 2 * D_FF), F32),
                        pltpu.VMEM((TM + SUBLANES, FF_CHUNK), F32)],
        compiler_params=_params("arbitrary", "arbitrary"),
        name="ffn",
    )(x, g, wup, cw, cb, wdown, fg)


def _split_w_in(w_in):
    hd, bw = HEAD_DIM, BRANCH_WIDTH
    o = 0
    qa = w_in[:, o:o + bw]; o += bw
    ka = w_in[:, o:o + SWA_KV_HEADS * hd]; o += SWA_KV_HEADS * hd
    va = w_in[:, o:o + SWA_KV_HEADS * hd]; o += SWA_KV_HEADS * hd
    ub = w_in[:, o:o + 2 * bw]; o += 2 * bw
    qkv_c = w_in[:, o:o + 3 * bw]; o += 3 * bw
    fc = w_in[:, o:o + N_HEADS]; o += N_HEADS
    qkv_d = w_in[:, o:o + 3 * bw]; o += 3 * bw
    gates = w_in[:, o:o + N_BRANCH * D_MODEL]
    dup = lambda w: jnp.concatenate([w[:, i * hd:(i + 1) * hd] for i in (0, 0, 1, 1)], axis=1)
    wa = jnp.concatenate([qa, dup(ka), dup(va)], axis=1).astype(BF16)
    wf = jnp.pad(fc, ((0, 0), (0, LANES - N_HEADS)))
    return wa, ub.astype(BF16), qkv_c.astype(BF16), wf, qkv_d.astype(BF16), gates.astype(BF16)


def _layer(x, bsz, seq, biases, rel_bias, ln1_g, w_in, b_gate, b_fgate, sinks, conv_w, conv_b,
           conv_ln_g, conv_ln_b, w_br, w_out, ln2_g, w_up, ffn_conv_w, ffn_conv_b, w_down,
           final_g, final):
    bias_swa, bias_own, bias_prev = biases
    row = lambda v: v.reshape(1, -1)
    wa, wb, wc, wf, wd, wg = _split_w_in(w_in)
    za, zb, zc, zf, zd = _inproj(x, row(ln1_g), wa, wb, wc, wf, wd)
    ya = _swa(za, sinks, bias_swa, seq)
    yb = _conformer(zb, conv_w, row(conv_b), row(conv_ln_g), row(conv_ln_b), bsz, seq)
    ccol, crow = _fgate(zf, jnp.pad(row(b_fgate), ((0, 0), (0, LANES - N_HEADS))), bsz, seq)
    yc = _fox(zc, ccol, crow, bsz, seq)
    yd = _moba(zd, rel_bias, bias_own, bias_prev, bsz, seq)
    x = _merge(x, row(ln1_g), (ya, yb, yc, yd), wg, b_gate, w_br.astype(BF16), w_out.astype(BF16))
    return _ffn(x, row(ln2_g), w_up.astype(BF16), ffn_conv_w, row(ffn_conv_b), w_down.astype(BF16),
                row(final_g), bsz, seq, final)


def kernel(x, ln1_g, w_in, b_gate, b_fgate, sinks, conv_w, conv_b, conv_ln_g, conv_ln_b, w_br, w_out,
           ln2_g, w_up, ffn_conv_w, ffn_conv_b, w_down, rel_bias, final_g):
    bsz, seq, _ = x.shape
    depth = w_in.shape[0]
    assert seq % TM == 0 and seq % TQ == 0 and TQ == MOBA_BLOCK and seq // MOBA_BLOCK <= LANES
    biases = _expand_bias(rel_bias)
    y = x.reshape(bsz * seq, D_MODEL)
    for layer in range(depth):
        y = _layer(y, bsz, seq, biases, rel_bias, ln1_g[layer], w_in[layer], b_gate[layer],
                   b_fgate[layer], sinks[layer], conv_w[layer], conv_b[layer], conv_ln_g[layer],
                   conv_ln_b[layer], w_br[layer], w_out[layer], ln2_g[layer], w_up[layer],
                   ffn_conv_w[layer], ffn_conv_b[layer], w_down[layer], final_g,
                   final=layer == depth - 1)
    return y.reshape(bsz, seq, D_MODEL)
```

```python
import functools
import math

import jax
import jax.numpy as jnp
import numpy as np
from jax import lax
from jax.experimental import pallas as pl
from jax.experimental.pallas import tpu as pltpu

D_MODEL = 1024
HEAD_DIM = 64
N_BRANCH = 4
BRANCH_WIDTH = 512
N_HEADS = 8
N_PAIRS = N_HEADS // 2
SWA_KV_HEADS = 2
SWA_BLOCK = 128
CONV_WIDTH = 31
MOBA_BLOCK = 256
MOBA_TOPK = 3
D_FF = 2816
FFN_CONV_WIDTH = 3
REL_BUCKETS = 32
REL_MAX_DIST = 128
EPS = 1e-6
NEG = -1e30
SCALE = HEAD_DIM ** -0.5
LOG2E = math.log2(math.e)

LANES = 128
SUBLANES = 8
VMEM_LIMIT = 56 * 1024 * 1024

TM = 512
TQ = 256
TK = 256
CHUNK_TILES = 4
KV_ROWS = 512
CUM_T = 256
FF_CHUNK = 256
CONV_ROWS = 64
CONV_HALO = 32

N_PIECES = 3
PIECE_ROWS = SUBLANES * N_PIECES
MOBA_SEL_LANE = N_PIECES

BF16 = jnp.bfloat16
F32 = jnp.float32


def _params(*sem):
    return pltpu.CompilerParams(dimension_semantics=sem, vmem_limit_bytes=VMEM_LIMIT)


def _resident(shape):
    zeros = (0,) * len(shape)
    return pl.BlockSpec(shape, lambda *_: zeros, pipeline_mode=pl.Buffered(1))


def _rmsnorm(x, g):
    return (x * lax.rsqrt(jnp.mean(x * x, axis=-1, keepdims=True) + EPS)) * g


def _pieces(x):
    hi = x.astype(BF16).astype(F32)
    rest = x - hi
    mid = rest.astype(BF16).astype(F32)
    return hi, mid, rest - mid


def _rel_bucket_np(dist):
    n = np.maximum(dist, 0)
    max_exact = REL_BUCKETS // 2
    nf = np.maximum(n, 1).astype(np.float64)
    large = max_exact + (np.log(nf / max_exact) / math.log(REL_MAX_DIST / max_exact)
                         * (REL_BUCKETS - max_exact) + 1e-9).astype(np.int32)
    large = np.minimum(large, REL_BUCKETS - 1)
    return np.where(n < max_exact, n, large).astype(np.int32)


def _bias_kernel(tab_ref, bs_ref, bo_ref, bp_ref, os_ref, oo_ref, op_ref):
    os_ref[...] = jnp.zeros_like(os_ref)
    oo_ref[...] = jnp.zeros_like(oo_ref)
    op_ref[...] = jnp.zeros_like(op_ref)
    bs, bo, bp = bs_ref[...], bo_ref[...], bp_ref[...]

    def body(b, carry):
        ms, mo, mp = bs == b, bo == b, bp == b
        for h in range(N_HEADS):
            os_ref[h] = jnp.where(ms, tab_ref[b, h], os_ref[h])
            oo_ref[h] = jnp.where(mo, tab_ref[b, N_HEADS + h], oo_ref[h])
            op_ref[h] = jnp.where(mp, tab_ref[b, N_HEADS + h], op_ref[h])
        return carry

    lax.fori_loop(0, REL_BUCKETS, body, 0)
    causal = (lax.broadcasted_iota(jnp.int32, (MOBA_BLOCK, MOBA_BLOCK), 1)
              <= lax.broadcasted_iota(jnp.int32, (MOBA_BLOCK, MOBA_BLOCK), 0))
    for h in range(N_HEADS):
        far = tab_ref[REL_BUCKETS - 1, N_HEADS + h]
        oo_ref[h] = jnp.where(causal, (oo_ref[h] - far) * LOG2E, NEG)
        op_ref[h] = (op_ref[h] - far) * LOG2E


def _expand_bias(rel_bias):
    qi = np.arange(SWA_BLOCK)[:, None]
    kj = np.arange(2 * SWA_BLOCK)[None, :]
    b_swa = _rel_bucket_np(qi + SWA_BLOCK - kj)
    qi = np.arange(MOBA_BLOCK)[:, None]
    kj = np.arange(MOBA_BLOCK)[None, :]
    b_own = _rel_bucket_np(qi - kj)
    b_prev = _rel_bucket_np(qi - kj + MOBA_BLOCK)
    vm = pl.BlockSpec(memory_space=pltpu.VMEM)
    return pl.pallas_call(
        _bias_kernel,
        out_shape=(jax.ShapeDtypeStruct((N_HEADS, SWA_BLOCK, 2 * SWA_BLOCK), F32),
                   jax.ShapeDtypeStruct((N_HEADS, MOBA_BLOCK, MOBA_BLOCK), F32),
                   jax.ShapeDtypeStruct((N_HEADS, MOBA_BLOCK, MOBA_BLOCK), F32)),
        in_specs=[pl.BlockSpec(memory_space=pltpu.SMEM), vm, vm, vm],
        out_specs=(vm, vm, vm),
        name="bias_expand",
    )(rel_bias, jnp.asarray(b_swa), jnp.asarray(b_own), jnp.asarray(b_prev))


def _inproj_kernel(x_ref, g_ref, wa_ref, wb_ref, wc_ref, wf_ref, wd_ref,
                   oa_ref, ob_ref, oc_ref, of_ref, od_ref):
    h = _rmsnorm(x_ref[...], g_ref[...])
    hb = h.astype(BF16)
    oa_ref[...] = jnp.dot(hb, wa_ref[...], preferred_element_type=F32).astype(BF16)
    ob_ref[...] = jnp.dot(hb, wb_ref[...], preferred_element_type=F32)
    oc_ref[...] = jnp.dot(hb, wc_ref[...], preferred_element_type=F32).astype(BF16)
    od_ref[...] = jnp.dot(hb, wd_ref[...], preferred_element_type=F32).astype(BF16)
    of_ref[...] = jnp.dot(h, wf_ref[...], preferred_element_type=F32,
                          precision=lax.Precision.HIGHEST)


def _inproj(x, g, wa, wb, wc, wf, wd):
    n = x.shape[0]
    row = lambda w: pl.BlockSpec((TM, w), lambda i: (i, 0))
    ws = (wa, wb, wc, wf, wd)
    dts = (BF16, F32, BF16, F32, BF16)
    return pl.pallas_call(
        _inproj_kernel,
        grid=(n // TM,),
        in_specs=[row(D_MODEL), _resident(g.shape)] + [_resident(w.shape) for w in ws],
        out_specs=tuple(row(w.shape[1]) for w in ws),
        out_shape=tuple(jax.ShapeDtypeStruct((n, w.shape[1]), dt) for w, dt in zip(ws, dts)),
        compiler_params=_params("arbitrary"),
        name="inproj",
    )(x, g, *ws)


def _swa_kernel(sink_ref, q_ref, kc_ref, kp_ref, vc_ref, vp_ref, bias_ref, o_ref, *, blocks_per_seq):
    blk = pl.program_id(0) % blocks_per_seq
    kk = jnp.concatenate([kp_ref[...], kc_ref[...]], axis=0)
    vv = jnp.concatenate([vp_ref[...], vc_ref[...]], axis=0)
    qi = lax.broadcasted_iota(jnp.int32, (SWA_BLOCK, 2 * SWA_BLOCK), 0)
    kj = lax.broadcasted_iota(jnp.int32, (SWA_BLOCK, 2 * SWA_BLOCK), 1)
    dist = qi + SWA_BLOCK - kj
    mask = (dist >= 0) & (dist < SWA_BLOCK) & ((blk > 0) | (kj >= SWA_BLOCK))
    lane = lax.broadcasted_iota(jnp.int32, (SWA_BLOCK, LANES), 1)
    upper = lane >= HEAD_DIM
    for pair in range(N_PAIRS):
        kvh = (2 * pair) // (N_HEADS // SWA_KV_HEADS)
        q_pair = q_ref[:, pair * LANES:(pair + 1) * LANES] * SCALE
        k_kv = kk[:, kvh * LANES:(kvh + 1) * LANES]
        v_kv = vv[:, kvh * LANES:(kvh + 1) * LANES]
        outs = []
        for hh in range(2):
            h = 2 * pair + hh
            q_h = jnp.where(upper == (hh == 1), q_pair, jnp.zeros_like(q_pair))
            s = lax.dot_general(q_h, k_kv, (((1,), (1,)), ((), ())), preferred_element_type=F32)
            logits = jnp.where(mask, s + bias_ref[h], NEG)
            sink = sink_ref[h]
            m = jnp.maximum(jnp.max(logits, axis=-1, keepdims=True), sink)
            p = jnp.exp(logits - m)
            denom = jnp.sum(p, axis=-1, keepdims=True) + jnp.exp(sink - m)
            probs = (p / denom).astype(BF16)
            outs.append(jnp.dot(probs, v_kv, preferred_element_type=F32))
        o_ref[:, pair * LANES:(pair + 1) * LANES] = jnp.where(upper, outs[1], outs[0]).astype(BF16)


def _swa(qkv, sinks, bias, seq):
    n = qkv.shape[0]
    bps = seq // SWA_BLOCK
    kcol, vcol = BRANCH_WIDTH // (2 * LANES), BRANCH_WIDTH // (2 * LANES) + 1
    prev = lambda i: jnp.where(i % bps == 0, i, i - 1)
    return pl.pallas_call(
        functools.partial(_swa_kernel, blocks_per_seq=bps),
        grid=(n // SWA_BLOCK,),
        in_specs=[pl.BlockSpec(memory_space=pltpu.SMEM),
                  pl.BlockSpec((SWA_BLOCK, BRANCH_WIDTH), lambda i: (i, 0)),
                  pl.BlockSpec((SWA_BLOCK, 2 * LANES), lambda i: (i, kcol)),
                  pl.BlockSpec((SWA_BLOCK, 2 * LANES), lambda i: (prev(i), kcol)),
                  pl.BlockSpec((SWA_BLOCK, 2 * LANES), lambda i: (i, vcol)),
                  pl.BlockSpec((SWA_BLOCK, 2 * LANES), lambda i: (prev(i), vcol)),
                  _resident(bias.shape)],
        out_specs=pl.BlockSpec((SWA_BLOCK, BRANCH_WIDTH), lambda i: (i, 0)),
        out_shape=jax.ShapeDtypeStruct((n, BRANCH_WIDTH), BF16),
        compiler_params=_params("arbitrary"),
        name="swa",
    )(sinks, qkv, qkv, qkv, qkv, qkv, bias)


def _conformer_kernel(u_ref, w_ref, b_ref, g_ref, beta_ref, o_ref, buf_ref):
    @pl.when(pl.program_id(1) == 0)
    def _():
        buf_ref[0:CONV_HALO, :] = jnp.zeros((CONV_HALO, BRANCH_WIDTH), F32)

    a = u_ref[:, 0:BRANCH_WIDTH]
    gt = u_ref[:, BRANCH_WIDTH:2 * BRANCH_WIDTH]
    buf_ref[CONV_HALO:CONV_HALO + TM, :] = a * jax.nn.sigmoid(gt)
    first = CONV_HALO - (CONV_WIDTH - 1)
    for r in range(TM // CONV_ROWS):
        acc = jnp.zeros((CONV_ROWS, BRANCH_WIDTH), F32)
        for k in range(CONV_WIDTH):
            lo = first + r * CONV_ROWS + k
            acc = acc + w_ref[k:k + 1, :] * buf_ref[lo:lo + CONV_ROWS, :]
        y = acc + b_ref[...]
        mu = jnp.mean(y, axis=-1, keepdims=True)
        yc = y - mu
        var = jnp.mean(yc * yc, axis=-1, keepdims=True)
        z = yc * lax.rsqrt(var + EPS) * g_ref[...] + beta_ref[...]
        o_ref[r * CONV_ROWS:(r + 1) * CONV_ROWS, :] = (z * jax.nn.sigmoid(z)).astype(BF16)
    buf_ref[0:CONV_HALO, :] = buf_ref[TM:TM + CONV_HALO, :]


def _conformer(u, w, b, g, beta, bsz, seq):
    n = u.shape[0]
    spt = seq // TM
    return pl.pallas_call(
        _conformer_kernel,
        grid=(bsz, spt),
        in_specs=[pl.BlockSpec((TM, 2 * BRANCH_WIDTH), lambda bi, i: (bi * spt + i, 0)),
                  _resident(w.shape), _resident(b.shape), _resident(g.shape), _resident(beta.shape)],
        out_specs=pl.BlockSpec((TM, BRANCH_WIDTH), lambda bi, i: (bi * spt + i, 0)),
        out_shape=jax.ShapeDtypeStruct((n, BRANCH_WIDTH), BF16),
        scratch_shapes=[pltpu.VMEM((TM + CONV_HALO, BRANCH_WIDTH), F32)],
        compiler_params=_params("arbitrary", "arbitrary"),
        name="conformer",
    )(u, w, b, g, beta)


def _extra_base(h):
    return (h // 2) * LANES + HEAD_DIM * (1 - h % 2)


def _fgate_placement():
    pk = np.zeros((LANES, BRANCH_WIDTH), np.float32)
    pq = np.zeros((LANES, BRANCH_WIDTH), np.float32)
    for h in range(N_HEADS):
        base = _extra_base(h)
        for i in range(N_PIECES):
            pk[SUBLANES * i + h, base + i] = 1.0
            pq[PIECE_ROWS, base + i] = -1.0
            pk[PIECE_ROWS, base + N_PIECES + i] = 1.0
            pq[SUBLANES * i + h, base + N_PIECES + i] = 1.0
    return jnp.asarray(pk, BF16), jnp.asarray(pq, BF16)


def _fgate_kernel(f_ref, b_ref, pk_ref, pq_ref, kx_ref, qx_ref, carry_ref):
    @pl.when(pl.program_id(1) == 0)
    def _():
        carry_ref[...] = jnp.zeros_like(carry_ref)

    z = f_ref[...] + b_ref[...]
    log_f = -(jnp.maximum(-z, 0.0) + jnp.log1p(jnp.exp(-jnp.abs(z))))
    ri = lax.broadcasted_iota(jnp.int32, (CUM_T, CUM_T), 0)
    ci = lax.broadcasted_iota(jnp.int32, (CUM_T, CUM_T), 1)
    tri = (ci <= ri).astype(F32)
    c = jnp.dot(tri, log_f, preferred_element_type=F32,
                precision=lax.Precision.HIGHEST) + carry_ref[0:1, :]
    carry_ref[...] = jnp.broadcast_to(c[CUM_T - 1:CUM_T, :], carry_ref.shape)
    hi, mid, lo = _pieces(c * LOG2E)
    lane = lax.broadcasted_iota(jnp.int32, (CUM_T, LANES), 1)
    packed = jnp.where(lane < SUBLANES, hi,
                       jnp.where(lane < 2 * SUBLANES, pltpu.roll(mid, SUBLANES, axis=1),
                                 jnp.where(lane < PIECE_ROWS, pltpu.roll(lo, 2 * SUBLANES, axis=1),
                                           jnp.where(lane == PIECE_ROWS, 1.0, 0.0)))).astype(BF16)
    kx_ref[...] = jnp.dot(packed, pk_ref[...], preferred_element_type=F32).astype(BF16)
    qx_ref[...] = jnp.dot(packed, pq_ref[...], preferred_element_type=F32).astype(BF16)


def _fgate(f, b, bsz, seq):
    n = f.shape[0]
    spt = seq // CUM_T
    pk, pq = _fgate_placement()
    row = lambda w: pl.BlockSpec((CUM_T, w), lambda bi, i: (bi * spt + i, 0))
    return pl.pallas_call(
        _fgate_kernel,
        grid=(bsz, spt),
        in_specs=[row(LANES), _resident(b.shape), _resident(pk.shape), _resident(pq.shape)],
        out_specs=(row(BRANCH_WIDTH), row(BRANCH_WIDTH)),
        out_shape=(jax.ShapeDtypeStruct((n, BRANCH_WIDTH), BF16),
                   jax.ShapeDtypeStruct((n, BRANCH_WIDTH), BF16)),
        scratch_shapes=[pltpu.VMEM((SUBLANES, LANES), F32)],
        compiler_params=_params("arbitrary", "arbitrary"),
        name="fgate_cumsum",
    )(f, b, pk, pq)


def _lower_half(rows):
    return lax.broadcasted_iota(jnp.int32, (rows, LANES), 1) < HEAD_DIM


def _build_kv(k_ref, v_ref, extra_of, ka_ref, va_ref, seq):
    lane = lax.broadcasted_iota(jnp.int32, (KV_ROWS, LANES), 1)
    lower = lane < HEAD_DIM
    one_hi = jnp.where(lane == HEAD_DIM, 1.0, 0.0).astype(BF16)
    one_lo = jnp.where(lane == 0, 1.0, 0.0).astype(BF16)

    def body(i, carry):
        rows = pl.ds(pl.multiple_of(i * KV_ROWS, KV_ROWS), KV_ROWS)
        k, v, extra = k_ref[rows, :], v_ref[rows, :], extra_of(i, rows)
        ka_ref[0, rows, :] = jnp.where(lower, k, extra)
        ka_ref[1, rows, :] = jnp.where(lower, extra, k)
        va_ref[0, rows, :] = jnp.where(lower, v, one_hi)
        va_ref[1, rows, :] = jnp.where(lower, one_lo, v)
        return carry

    lax.fori_loop(0, seq // KV_ROWS, body, 0)


def _extend_q(q_scaled, extras):
    lower = _lower_half(TQ)
    return (jnp.where(lower, q_scaled, extras[0]), jnp.where(lower, extras[1], q_scaled))


def _flash_tile(qa, ka_ref, va_ref, m_ref, acc_ref, start, keys=TK, adjust=None, first=False):
    rows = pl.ds(start, keys)
    scores = [lax.dot_general(qa[hh], ka_ref[hh, rows, :], (((1,), (1,)), ((), ())),
                              preferred_element_type=F32) for hh in range(2)]
    if adjust is not None:
        scores = [adjust(hh, scores[hh]) for hh in range(2)]
    for hh in range(2):
        s = scores[hh]
        row_max = jnp.broadcast_to(jnp.max(s, axis=-1, keepdims=True), (TQ, LANES))
        m_new = row_max if first else jnp.maximum(m_ref[hh], row_max)
        p = jnp.exp2(s - jnp.concatenate([m_new] * (keys // LANES), axis=1)).astype(BF16)
        pv = jnp.dot(p, va_ref[hh, rows, :], preferred_element_type=F32)
        if first:
            acc_ref[hh] = pv
        else:
            acc_ref[hh] = jnp.exp2(m_ref[hh] - m_new) * acc_ref[hh] + pv
        m_ref[hh] = m_new


def _pair_output(o_ref, acc_ref):
    lane = lax.broadcasted_iota(jnp.int32, (TQ, LANES), 1)
    a0, a1 = acc_ref[0], acc_ref[1]
    l0 = jnp.sum(jnp.where(lane == HEAD_DIM, a0, 0.0), axis=-1, keepdims=True)
    l1 = jnp.sum(jnp.where(lane == 0, a1, 0.0), axis=-1, keepdims=True)
    o_ref[...] = jnp.where(lane < HEAD_DIM, a0 / l0, a1 / l1).astype(BF16)


def _flash_range(qa, ka_ref, va_ref, m_ref, acc_ref, n_tiles):
    n_chunks = n_tiles // CHUNK_TILES

    def chunk(c, carry):
        _flash_tile(qa, ka_ref, va_ref, m_ref, acc_ref,
                    pl.multiple_of(c * (CHUNK_TILES * TK), CHUNK_TILES * TK), keys=CHUNK_TILES * TK)
        return carry

    def single(j, carry):
        _flash_tile(qa, ka_ref, va_ref, m_ref, acc_ref, pl.multiple_of(j * TK, TK))
        return carry

    lax.fori_loop(0, n_chunks, chunk, 0)
    lax.fori_loop(n_chunks * CHUNK_TILES, n_tiles, single, 0)


_CAUSAL = lambda: (lax.broadcasted_iota(jnp.int32, (TQ, TK), 1)
                   <= lax.broadcasted_iota(jnp.int32, (TQ, TK), 0))


def _fox_kernel(q_ref, qx_ref, k_ref, v_ref, kx_ref, o_ref, ka_ref, va_ref, m_ref, acc_ref, *, seq):
    qt = pl.program_id(2)

    @pl.when(qt == 0)
    def _():
        _build_kv(k_ref, v_ref, lambda i, rows: kx_ref[rows, :], ka_ref, va_ref, seq)

    q_scaled = (q_ref[...].astype(F32) * (SCALE * LOG2E)).astype(BF16)
    qx = qx_ref[...]
    qa = _extend_q(q_scaled, (qx, qx))
    causal = _CAUSAL()
    _flash_tile(qa, ka_ref, va_ref, m_ref, acc_ref, pl.multiple_of(qt * TQ, TQ),
                adjust=lambda hh, s: jnp.where(causal, s, NEG), first=True)
    _flash_range(qa, ka_ref, va_ref, m_ref, acc_ref, qt)
    _pair_output(o_ref, acc_ref)


def _fox(qkv, kx, qx, bsz, seq):
    n = qkv.shape[0]
    qpt = seq // TQ
    koff, voff = BRANCH_WIDTH // LANES, 2 * BRANCH_WIDTH // LANES
    q_tile = lambda off: pl.BlockSpec((TQ, LANES), lambda b, p, i: (b * qpt + i, off + p))
    kv_seq = lambda off: pl.BlockSpec((seq, LANES), lambda b, p, i: (b, off + p))
    return pl.pallas_call(
        functools.partial(_fox_kernel, seq=seq),
        grid=(bsz, N_PAIRS, qpt),
        in_specs=[q_tile(0), q_tile(0), kv_seq(koff), kv_seq(voff), kv_seq(0)],
        out_specs=q_tile(0),
        out_shape=jax.ShapeDtypeStruct((n, BRANCH_WIDTH), BF16),
        scratch_shapes=[pltpu.VMEM((2, seq, LANES), BF16), pltpu.VMEM((2, seq, LANES), BF16),
                        pltpu.VMEM((2, TQ, LANES), F32), pltpu.VMEM((2, TQ, LANES), F32)],
        compiler_params=_params("arbitrary", "arbitrary", "arbitrary"),
        name="fox",
    )(qkv, qx, qkv, qkv, kx)


def _moba_kernel(tab_ref, q_ref, k_ref, v_ref, town_ref, tprev_ref, o_ref,
                 kmean_ref, ka_ref, va_ref, m_ref, acc_ref, *, seq):
    pair = pl.program_id(1)
    qt = pl.program_id(2)
    nblk = seq // MOBA_BLOCK
    blk_shift = MOBA_BLOCK.bit_length() - 1

    @pl.when(qt == 0)
    def _():
        blk = lax.broadcasted_iota(jnp.int32, (LANES, seq), 0)
        pos = lax.broadcasted_iota(jnp.int32, (LANES, seq), 1)
        member = jnp.where(jnp.right_shift(pos, blk_shift) == blk, 1.0 / MOBA_BLOCK, 0.0).astype(BF16)
        kmean_ref[...] = jnp.dot(member, k_ref[...], preferred_element_type=F32)

        def extra_of(i, rows):
            xl = jnp.bitwise_and(lax.broadcasted_iota(jnp.int32, (KV_ROWS, LANES), 1), HEAD_DIM - 1)
            row_blk = jnp.right_shift(
                i * KV_ROWS + lax.broadcasted_iota(jnp.int32, (KV_ROWS, LANES), 0), blk_shift)
            hit = (xl < MOBA_SEL_LANE) | (xl - MOBA_SEL_LANE == row_blk)
            return jnp.where(hit, 1.0, 0.0).astype(BF16)

        _build_kv(k_ref, v_ref, extra_of, ka_ref, va_ref, seq)

    q_pair = q_ref[...]
    lane = lax.broadcasted_iota(jnp.int32, (TQ, LANES), 1)
    lower = lane < HEAD_DIM
    zero = jnp.zeros((TQ, LANES), F32)
    q32 = q_pair.astype(F32)
    q_gate = (jnp.where(lower, q32, zero), jnp.where(lower, zero, q32))
    past = lane < qt
    extras = []
    for hh in range(2):
        gate = lax.dot_general(q_gate[hh], kmean_ref[...], (((1,), (1,)), ((), ())),
                               preferred_element_type=F32, precision=lax.Precision.HIGHEST)
        gate = jnp.where(past, gate, jnp.where(lane < nblk, NEG, -jnp.inf))
        sel = lane == qt
        for _ in range(MOBA_TOPK):
            top = jnp.max(gate, axis=-1, keepdims=True)
            idx = jnp.min(jnp.where(gate == top, lane, LANES), axis=-1, keepdims=True)
            pick = lane == idx
            sel = sel | (pick & past)
            gate = jnp.where(pick, -jnp.inf, gate)
        penalty = jnp.where(sel | (lane >= nblk), 0.0, NEG)
        base = HEAD_DIM * (1 - hh)
        xl = lane - base
        far = jnp.full((TQ, LANES), tab_ref[REL_BUCKETS - 1, N_HEADS + 2 * pair + hh] * LOG2E, F32)
        hi, mid, lo = _pieces(far)
        extra = jnp.where(xl == 0, hi, jnp.where(xl == 1, mid, jnp.where(
            xl == 2, lo, pltpu.roll(penalty, base + MOBA_SEL_LANE, axis=1))))
        extras.append(extra.astype(BF16))

    qa = _extend_q((q32 * (SCALE * LOG2E)).astype(BF16), extras)
    _flash_tile(qa, ka_ref, va_ref, m_ref, acc_ref, pl.multiple_of(qt * TQ, TQ),
                adjust=lambda hh, s: s + town_ref[hh], first=True)
    _flash_range(qa, ka_ref, va_ref, m_ref, acc_ref, jnp.maximum(qt - 1, 0))

    @pl.when(qt >= 1)
    def _():
        _flash_tile(qa, ka_ref, va_ref, m_ref, acc_ref, pl.multiple_of((qt - 1) * TK, TK),
                    adjust=lambda hh, s: s + tprev_ref[hh])

    _pair_output(o_ref, acc_ref)


def _moba(qkv, rel_bias, t_own, t_prev, bsz, seq):
    n = qkv.shape[0]
    qpt = seq // TQ
    koff, voff = BRANCH_WIDTH // LANES, 2 * BRANCH_WIDTH // LANES
    q_tile = pl.BlockSpec((TQ, LANES), lambda b, p, i: (b * qpt + i, p))
    kv_seq = lambda off: pl.BlockSpec((seq, LANES), lambda b, p, i: (b, off + p))
    pair_table = pl.BlockSpec((2, MOBA_BLOCK, MOBA_BLOCK), lambda b, p, i: (p, 0, 0))
    return pl.pallas_call(
        functools.partial(_moba_kernel, seq=seq),
        grid=(bsz, N_PAIRS, qpt),
        in_specs=[pl.BlockSpec(memory_space=pltpu.SMEM), q_tile, kv_seq(koff), kv_seq(voff),
                  pair_table, pair_table],
        out_specs=q_tile,
        out_shape=jax.ShapeDtypeStruct((n, BRANCH_WIDTH), BF16),
        scratch_shapes=[pltpu.VMEM((LANES, LANES), F32),
                        pltpu.VMEM((2, seq, LANES), BF16), pltpu.VMEM((2, seq, LANES), BF16),
                        pltpu.VMEM((2, TQ, LANES), F32), pltpu.VMEM((2, TQ, LANES), F32)],
        compiler_params=_params("arbitrary", "arbitrary", "arbitrary"),
        name="moba",
    )(rel_bias, qkv, qkv, qkv, t_own, t_prev)


def _merge_kernel(x_ref, g_ref, ya_ref, yb_ref, yc_ref, yd_ref, wg_ref, bg_ref, wbr_ref, wout_ref, o_ref):
    x = x_ref[...]
    hb = _rmsnorm(x, g_ref[...]).astype(BF16)
    merged = jnp.zeros((TM, D_MODEL), F32)
    for n, y_ref in enumerate((ya_ref, yb_ref, yc_ref, yd_ref)):
        gz = jnp.dot(hb, wg_ref[:, n * D_MODEL:(n + 1) * D_MODEL], preferred_element_type=F32)
        gate = jax.nn.sigmoid(gz + bg_ref[n:n + 1, :])
        merged = merged + gate * jnp.dot(y_ref[...], wbr_ref[n], preferred_element_type=F32)
    o_ref[...] = x + jnp.dot(merged.astype(BF16), wout_ref[...], preferred_element_type=F32)


def _merge(x, g, ys, wg, bg, wbr, wout):
    n = x.shape[0]
    row = lambda w: pl.BlockSpec((TM, w), lambda i: (i, 0))
    return pl.pallas_call(
        _merge_kernel,
        grid=(n // TM,),
        in_specs=[row(D_MODEL), _resident(g.shape)] + [row(BRANCH_WIDTH)] * N_BRANCH
        + [_resident(wg.shape), _resident(bg.shape), _resident(wbr.shape), _resident(wout.shape)],
        out_specs=row(D_MODEL),
        out_shape=jax.ShapeDtypeStruct((n, D_MODEL), F32),
        compiler_params=_params("arbitrary"),
        name="merge",
    )(x, g, *ys, wg, bg, wbr, wout)


def _ffn_kernel(x_ref, g_ref, wup_ref, cw_ref, cb_ref, wdown_ref, fg_ref, o_ref, halo_ref, buf_ref, *, final):
    @pl.when(pl.program_id(1) == 0)
    def _():
        halo_ref[...] = jnp.zeros_like(halo_ref)

    x = x_ref[...]
    hb = _rmsnorm(x, g_ref[...]).astype(BF16)
    acc = jnp.zeros((TM, D_MODEL), F32)
    for c in range(D_FF // FF_CHUNK):
        halves = []
        for part in range(2):
            col = part * D_FF + c * FF_CHUNK
            buf_ref[0:SUBLANES, :] = halo_ref[:, col:col + FF_CHUNK]
            buf_ref[SUBLANES:SUBLANES + TM, :] = jnp.dot(
                hb, wup_ref[:, col:col + FF_CHUNK], preferred_element_type=F32)
            halo_ref[:, col:col + FF_CHUNK] = buf_ref[TM:TM + SUBLANES, :]
            u = cb_ref[:, col:col + FF_CHUNK]
            for k in range(FFN_CONV_WIDTH):
                lo = SUBLANES - (FFN_CONV_WIDTH - 1) + k
                u = u + cw_ref[k:k + 1, col:col + FF_CHUNK] * buf_ref[lo:lo + TM, :]
            halves.append(u)
        a = (halves[0] * jax.nn.sigmoid(halves[0])) * halves[1]
        acc = acc + jnp.dot(a.astype(BF16), wdown_ref[c * FF_CHUNK:(c + 1) * FF_CHUNK, :],
                            preferred_element_type=F32)
    y = x + acc
    o_ref[...] = _rmsnorm(y, fg_ref[...]) if final else y


def _ffn(x, g, wup, cw, cb, wdown, fg, bsz, seq, final):
    n = x.shape[0]
    spt = seq // TM
    row = pl.BlockSpec((TM, D_MODEL), lambda bi, i: (bi * spt + i, 0))
    return pl.pallas_call(
        functools.partial(_ffn_kernel, final=final),
        grid=(bsz, spt),
        in_specs=[row, _resident(g.shape), _resident(wup.shape), _resident(cw.shape),
                  _resident(cb.shape), _resident(wdown.shape), _resident(fg.shape)],
        out_specs=row,
        out_shape=jax.ShapeDtypeStruct((n, D_MODEL), F32),
        scratch_shapes=[pltpu.VMEM((SUBLANES, 2 * D_FF), F32),
                        pltpu.VMEM((TM + SUBLANES, FF_CHUNK), F32)],
        compiler_params=_params("arbitrary", "arbitrary"),
        name="ffn",
    )(x, g, wup, cw, cb, wdown, fg)


def _split_w_in(w_in):
    hd, bw = HEAD_DIM, BRANCH_WIDTH
    o = 0
    qa = w_in[:, o:o + bw]; o += bw
    ka = w_in[:, o:o + SWA_KV_HEADS * hd]; o += SWA_KV_HEADS * hd
    va = w_in[:, o:o + SWA_KV_HEADS * hd]; o += SWA_KV_HEADS * hd
    ub = w_in[:, o:o + 2 * bw]; o += 2 * bw
    qkv_c = w_in[:, o:o + 3 * bw]; o += 3 * bw
    fc = w_in[:, o:o + N_HEADS]; o += N_HEADS
    qkv_d = w_in[:, o:o + 3 * bw]; o += 3 * bw
    gates = w_in[:, o:o + N_BRANCH * D_MODEL]
    dup = lambda w: jnp.concatenate([w[:, i * hd:(i + 1) * hd] for i in (0, 0, 1, 1)], axis=1)
    wa = jnp.concatenate([qa, dup(ka), dup(va)], axis=1).astype(BF16)
    wf = jnp.pad(fc, ((0, 0), (0, LANES - N_HEADS)))
    return wa, ub.astype(BF16), qkv_c.astype(BF16), wf, qkv_d.astype(BF16), gates.astype(BF16)


def _layer(x, bsz, seq, biases, rel_bias, ln1_g, w_in, b_gate, b_fgate, sinks, conv_w, conv_b,
           conv_ln_g, conv_ln_b, w_br, w_out, ln2_g, w_up, ffn_conv_w, ffn_conv_b, w_down,
           final_g, final):
    bias_swa, table_own, table_prev = biases
    row = lambda v: v.reshape(1, -1)
    wa, wb, wc, wf, wd, wg = _split_w_in(w_in)
    za, zb, zc, zf, zd = _inproj(x, row(ln1_g), wa, wb, wc, wf, wd)
    ya = _swa(za, sinks, bias_swa, seq)
    yb = _conformer(zb, conv_w, row(conv_b), row(conv_ln_g), row(conv_ln_b), bsz, seq)
    kx, qx = _fgate(zf, jnp.pad(row(b_fgate), ((0, 0), (0, LANES - N_HEADS))), bsz, seq)
    yc = _fox(zc, kx, qx, bsz, seq)
    yd = _moba(zd, rel_bias, table_own, table_prev, bsz, seq)
    x = _merge(x, row(ln1_g), (ya, yb, yc, yd), wg, b_gate, w_br.astype(BF16), w_out.astype(BF16))
    return _ffn(x, row(ln2_g), w_up.astype(BF16), ffn_conv_w, row(ffn_conv_b), w_down.astype(BF16),
                row(final_g), bsz, seq, final)


def kernel(x, ln1_g, w_in, b_gate, b_fgate, sinks, conv_w, conv_b, conv_ln_g, conv_ln_b, w_br, w_out,
           ln2_g, w_up, ffn_conv_w, ffn_conv_b, w_down, rel_bias, final_g):
    bsz, seq, _ = x.shape
    depth = w_in.shape[0]
    assert seq % TM == 0 and seq % KV_ROWS == 0 and TQ == TK == MOBA_BLOCK
    assert MOBA_SEL_LANE + seq // MOBA_BLOCK <= HEAD_DIM
    biases = _expand_bias(rel_bias)
    y = x.reshape(bsz * seq, D_MODEL)
    for layer in range(depth):
        y = _layer(y, bsz, seq, biases, rel_bias, ln1_g[layer], w_in[layer], b_gate[layer],
                   b_fgate[layer], sinks[layer], conv_w[layer], conv_b[layer], conv_ln_g[layer],
                   conv_ln_b[layer], w_br[layer], w_out[layer], ln2_g[layer], w_up[layer],
                   ffn_conv_w[layer], ffn_conv_b[layer], w_down[layer], final_g,
                   final=layer == depth - 1)
    return y.reshape(bsz, seq, D_MODEL)
```

```python
import functools
import math

import jax
import jax.numpy as jnp
import numpy as np
from jax import lax
from jax.experimental import pallas as pl
from jax.experimental.pallas import tpu as pltpu

D_MODEL = 1024
HEAD_DIM = 64
N_BRANCH = 4
BRANCH_WIDTH = 512
N_HEADS = 8
N_PAIRS = N_HEADS // 2
SWA_KV_HEADS = 2
SWA_BLOCK = 128
CONV_WIDTH = 31
MOBA_BLOCK = 256
MOBA_TOPK = 3
D_FF = 2816
FFN_CONV_WIDTH = 3
REL_BUCKETS = 32
REL_MAX_DIST = 128
EPS = 1e-6
NEG = -1e30
SCALE = HEAD_DIM ** -0.5
LOG2E = math.log2(math.e)

LANES = 128
SUBLANES = 8
VMEM_LIMIT = 56 * 1024 * 1024

TM = 512
TQ = 256
TK = 256
CHUNK_TILES = 4
KV_ROWS = 512
CUM_T = 256
FF_CHUNK = 256
CONV_ROWS = 64
CONV_HALO = 32

N_PIECES = 3
PIECE_ROWS = SUBLANES * N_PIECES
MOBA_SEL_LANE = N_PIECES

BF16 = jnp.bfloat16
F32 = jnp.float32


def _params(*sem):
    return pltpu.CompilerParams(dimension_semantics=sem, vmem_limit_bytes=VMEM_LIMIT)


def _resident(shape):
    zeros = (0,) * len(shape)
    return pl.BlockSpec(shape, lambda *_: zeros, pipeline_mode=pl.Buffered(1))


def _rmsnorm(x, g):
    return (x * lax.rsqrt(jnp.mean(x * x, axis=-1, keepdims=True) + EPS)) * g


def _pieces(x):
    hi = x.astype(BF16).astype(F32)
    rest = x - hi
    mid = rest.astype(BF16).astype(F32)
    return hi, mid, rest - mid


def _rel_bucket_np(dist):
    n = np.maximum(dist, 0)
    max_exact = REL_BUCKETS // 2
    nf = np.maximum(n, 1).astype(np.float64)
    large = max_exact + (np.log(nf / max_exact) / math.log(REL_MAX_DIST / max_exact)
                         * (REL_BUCKETS - max_exact) + 1e-9).astype(np.int32)
    large = np.minimum(large, REL_BUCKETS - 1)
    return np.where(n < max_exact, n, large).astype(np.int32)


def _bias_kernel(tab_ref, bs_ref, bo_ref, bp_ref, os_ref, oo_ref, op_ref):
    os_ref[...] = jnp.zeros_like(os_ref)
    oo_ref[...] = jnp.zeros_like(oo_ref)
    op_ref[...] = jnp.zeros_like(op_ref)
    bs, bo, bp = bs_ref[...], bo_ref[...], bp_ref[...]

    def body(b, carry):
        ms, mo, mp = bs == b, bo == b, bp == b
        for h in range(N_HEADS):
            os_ref[h] = jnp.where(ms, tab_ref[b, h], os_ref[h])
            oo_ref[h] = jnp.where(mo, tab_ref[b, N_HEADS + h], oo_ref[h])
            op_ref[h] = jnp.where(mp, tab_ref[b, N_HEADS + h], op_ref[h])
        return carry

    lax.fori_loop(0, REL_BUCKETS, body, 0)
    causal = (lax.broadcasted_iota(jnp.int32, (MOBA_BLOCK, MOBA_BLOCK), 1)
              <= lax.broadcasted_iota(jnp.int32, (MOBA_BLOCK, MOBA_BLOCK), 0))
    for h in range(N_HEADS):
        far = tab_ref[REL_BUCKETS - 1, N_HEADS + h]
        oo_ref[h] = jnp.where(causal, (oo_ref[h] - far) * LOG2E, NEG)
        op_ref[h] = (op_ref[h] - far) * LOG2E


def _expand_bias(rel_bias):
    qi = np.arange(SWA_BLOCK)[:, None]
    kj = np.arange(2 * SWA_BLOCK)[None, :]
    b_swa = _rel_bucket_np(qi + SWA_BLOCK - kj)
    qi = np.arange(MOBA_BLOCK)[:, None]
    kj = np.arange(MOBA_BLOCK)[None, :]
    b_own = _rel_bucket_np(qi - kj)
    b_prev = _rel_bucket_np(qi - kj + MOBA_BLOCK)
    vm = pl.BlockSpec(memory_space=pltpu.VMEM)
    return pl.pallas_call(
        _bias_kernel,
        out_shape=(jax.ShapeDtypeStruct((N_HEADS, SWA_BLOCK, 2 * SWA_BLOCK), F32),
                   jax.ShapeDtypeStruct((N_HEADS, MOBA_BLOCK, MOBA_BLOCK), F32),
                   jax.ShapeDtypeStruct((N_HEADS, MOBA_BLOCK, MOBA_BLOCK), F32)),
        in_specs=[pl.BlockSpec(memory_space=pltpu.SMEM), vm, vm, vm],
        out_specs=(vm, vm, vm),
        name="bias_expand",
    )(rel_bias, jnp.asarray(b_swa), jnp.asarray(b_own), jnp.asarray(b_prev))


def _inproj_kernel(x_ref, g_ref, wa_ref, wb_ref, wc_ref, wf_ref, wd_ref,
                   oa_ref, ob_ref, oc_ref, of_ref, od_ref):
    h = _rmsnorm(x_ref[...], g_ref[...])
    hb = h.astype(BF16)
    oa_ref[...] = jnp.dot(hb, wa_ref[...], preferred_element_type=F32).astype(BF16)
    ob_ref[...] = jnp.dot(hb, wb_ref[...], preferred_element_type=F32)
    oc_ref[...] = jnp.dot(hb, wc_ref[...], preferred_element_type=F32).astype(BF16)
    od_ref[...] = jnp.dot(hb, wd_ref[...], preferred_element_type=F32).astype(BF16)
    of_ref[...] = jnp.dot(h, wf_ref[...], preferred_element_type=F32,
                          precision=lax.Precision.HIGHEST)


def _inproj(x, g, wa, wb, wc, wf, wd):
    n = x.shape[0]
    row = lambda w: pl.BlockSpec((TM, w), lambda i: (i, 0))
    ws = (wa, wb, wc, wf, wd)
    dts = (BF16, F32, BF16, F32, BF16)
    return pl.pallas_call(
        _inproj_kernel,
        grid=(n // TM,),
        in_specs=[row(D_MODEL), _resident(g.shape)] + [_resident(w.shape) for w in ws],
        out_specs=tuple(row(w.shape[1]) for w in ws),
        out_shape=tuple(jax.ShapeDtypeStruct((n, w.shape[1]), dt) for w, dt in zip(ws, dts)),
        compiler_params=_params("arbitrary"),
        name="inproj",
    )(x, g, *ws)


def _swa_kernel(sink_ref, q_ref, kc_ref, kp_ref, vc_ref, vp_ref, bias_ref, o_ref, *, blocks_per_seq):
    blk = pl.program_id(0) % blocks_per_seq
    kk = jnp.concatenate([kp_ref[...], kc_ref[...]], axis=0)
    vv = jnp.concatenate([vp_ref[...], vc_ref[...]], axis=0)
    qi = lax.broadcasted_iota(jnp.int32, (SWA_BLOCK, 2 * SWA_BLOCK), 0)
    kj = lax.broadcasted_iota(jnp.int32, (SWA_BLOCK, 2 * SWA_BLOCK), 1)
    dist = qi + SWA_BLOCK - kj
    mask = (dist >= 0) & (dist < SWA_BLOCK) & ((blk > 0) | (kj >= SWA_BLOCK))
    lane = lax.broadcasted_iota(jnp.int32, (SWA_BLOCK, LANES), 1)
    upper = lane >= HEAD_DIM
    for pair in range(N_PAIRS):
        kvh = (2 * pair) // (N_HEADS // SWA_KV_HEADS)
        q_pair = q_ref[:, pair * LANES:(pair + 1) * LANES] * SCALE
        k_kv = kk[:, kvh * LANES:(kvh + 1) * LANES]
        v_kv = vv[:, kvh * LANES:(kvh + 1) * LANES]
        outs = []
        for hh in range(2):
            h = 2 * pair + hh
            q_h = jnp.where(upper == (hh == 1), q_pair, jnp.zeros_like(q_pair))
            s = lax.dot_general(q_h, k_kv, (((1,), (1,)), ((), ())), preferred_element_type=F32)
            logits = jnp.where(mask, s + bias_ref[h], NEG)
            sink = sink_ref[h]
            m = jnp.maximum(jnp.max(logits, axis=-1, keepdims=True), sink)
            p = jnp.exp(logits - m)
            denom = jnp.sum(p, axis=-1, keepdims=True) + jnp.exp(sink - m)
            probs = (p / denom).astype(BF16)
            outs.append(jnp.dot(probs, v_kv, preferred_element_type=F32))
        o_ref[:, pair * LANES:(pair + 1) * LANES] = jnp.where(upper, outs[1], outs[0]).astype(BF16)


def _swa(qkv, sinks, bias, seq):
    n = qkv.shape[0]
    bps = seq // SWA_BLOCK
    kcol, vcol = BRANCH_WIDTH // (2 * LANES), BRANCH_WIDTH // (2 * LANES) + 1
    prev = lambda i: jnp.where(i % bps == 0, i, i - 1)
    return pl.pallas_call(
        functools.partial(_swa_kernel, blocks_per_seq=bps),
        grid=(n // SWA_BLOCK,),
        in_specs=[pl.BlockSpec(memory_space=pltpu.SMEM),
                  pl.BlockSpec((SWA_BLOCK, BRANCH_WIDTH), lambda i: (i, 0)),
                  pl.BlockSpec((SWA_BLOCK, 2 * LANES), lambda i: (i, kcol)),
                  pl.BlockSpec((SWA_BLOCK, 2 * LANES), lambda i: (prev(i), kcol)),
                  pl.BlockSpec((SWA_BLOCK, 2 * LANES), lambda i: (i, vcol)),
                  pl.BlockSpec((SWA_BLOCK, 2 * LANES), lambda i: (prev(i), vcol)),
                  _resident(bias.shape)],
        out_specs=pl.BlockSpec((SWA_BLOCK, BRANCH_WIDTH), lambda i: (i, 0)),
        out_shape=jax.ShapeDtypeStruct((n, BRANCH_WIDTH), BF16),
        compiler_params=_params("arbitrary"),
        name="swa",
    )(sinks, qkv, qkv, qkv, qkv, qkv, bias)


def _conformer_kernel(u_ref, w_ref, b_ref, g_ref, beta_ref, o_ref, buf_ref):
    @pl.when(pl.program_id(1) == 0)
    def _():
        buf_ref[0:CONV_HALO, :] = jnp.zeros((CONV_HALO, BRANCH_WIDTH), F32)

    a = u_ref[:, 0:BRANCH_WIDTH]
    gt = u_ref[:, BRANCH_WIDTH:2 * BRANCH_WIDTH]
    buf_ref[CONV_HALO:CONV_HALO + TM, :] = a * jax.nn.sigmoid(gt)
    first = CONV_HALO - (CONV_WIDTH - 1)
    for r in range(TM // CONV_ROWS):
        acc = jnp.zeros((CONV_ROWS, BRANCH_WIDTH), F32)
        for k in range(CONV_WIDTH):
            lo = first + r * CONV_ROWS + k
            acc = acc + w_ref[k:k + 1, :] * buf_ref[lo:lo + CONV_ROWS, :]
        y = acc + b_ref[...]
        mu = jnp.mean(y, axis=-1, keepdims=True)
        yc = y - mu
        var = jnp.mean(yc * yc, axis=-1, keepdims=True)
        z = yc * lax.rsqrt(var + EPS) * g_ref[...] + beta_ref[...]
        o_ref[r * CONV_ROWS:(r + 1) * CONV_ROWS, :] = (z * jax.nn.sigmoid(z)).astype(BF16)
    buf_ref[0:CONV_HALO, :] = buf_ref[TM:TM + CONV_HALO, :]


def _conformer(u, w, b, g, beta, bsz, seq):
    n = u.shape[0]
    spt = seq // TM
    return pl.pallas_call(
        _conformer_kernel,
        grid=(bsz, spt),
        in_specs=[pl.BlockSpec((TM, 2 * BRANCH_WIDTH), lambda bi, i: (bi * spt + i, 0)),
                  _resident(w.shape), _resident(b.shape), _resident(g.shape), _resident(beta.shape)],
        out_specs=pl.BlockSpec((TM, BRANCH_WIDTH), lambda bi, i: (bi * spt + i, 0)),
        out_shape=jax.ShapeDtypeStruct((n, BRANCH_WIDTH), BF16),
        scratch_shapes=[pltpu.VMEM((TM + CONV_HALO, BRANCH_WIDTH), F32)],
        compiler_params=_params("arbitrary", "arbitrary"),
        name="conformer",
    )(u, w, b, g, beta)


def _extra_base(h):
    return (h // 2) * LANES + HEAD_DIM * (1 - h % 2)


def _fgate_placement():
    pk = np.zeros((LANES, BRANCH_WIDTH), np.float32)
    pq = np.zeros((LANES, BRANCH_WIDTH), np.float32)
    for h in range(N_HEADS):
        base = _extra_base(h)
        for i in range(N_PIECES):
            pk[SUBLANES * i + h, base + i] = 1.0
            pq[PIECE_ROWS, base + i] = -1.0
            pk[PIECE_ROWS, base + N_PIECES + i] = 1.0
            pq[SUBLANES * i + h, base + N_PIECES + i] = 1.0
    return jnp.asarray(pk, BF16), jnp.asarray(pq, BF16)


def _fgate_kernel(f_ref, b_ref, pk_ref, pq_ref, kx_ref, qx_ref, carry_ref):
    @pl.when(pl.program_id(1) == 0)
    def _():
        carry_ref[...] = jnp.zeros_like(carry_ref)

    z = f_ref[...] + b_ref[...]
    log_f = -(jnp.maximum(-z, 0.0) + jnp.log1p(jnp.exp(-jnp.abs(z))))
    ri = lax.broadcasted_iota(jnp.int32, (CUM_T, CUM_T), 0)
    ci = lax.broadcasted_iota(jnp.int32, (CUM_T, CUM_T), 1)
    tri = (ci <= ri).astype(F32)
    c = jnp.dot(tri, log_f, preferred_element_type=F32,
                precision=lax.Precision.HIGHEST) + carry_ref[0:1, :]
    carry_ref[...] = jnp.broadcast_to(c[CUM_T - 1:CUM_T, :], carry_ref.shape)
    hi, mid, lo = _pieces(c * LOG2E)
    lane = lax.broadcasted_iota(jnp.int32, (CUM_T, LANES), 1)
    packed = jnp.where(lane < SUBLANES, hi,
                       jnp.where(lane < 2 * SUBLANES, pltpu.roll(mid, SUBLANES, axis=1),
                                 jnp.where(lane < PIECE_ROWS, pltpu.roll(lo, 2 * SUBLANES, axis=1),
                                           jnp.where(lane == PIECE_ROWS, 1.0, 0.0)))).astype(BF16)
    kx_ref[...] = jnp.dot(packed, pk_ref[...], preferred_element_type=F32).astype(BF16)
    qx_ref[...] = jnp.dot(packed, pq_ref[...], preferred_element_type=F32).astype(BF16)


def _fgate(f, b, bsz, seq):
    n = f.shape[0]
    spt = seq // CUM_T
    pk, pq = _fgate_placement()
    row = lambda w: pl.BlockSpec((CUM_T, w), lambda bi, i: (bi * spt + i, 0))
    return pl.pallas_call(
        _fgate_kernel,
        grid=(bsz, spt),
        in_specs=[row(LANES), _resident(b.shape), _resident(pk.shape), _resident(pq.shape)],
        out_specs=(row(BRANCH_WIDTH), row(BRANCH_WIDTH)),
        out_shape=(jax.ShapeDtypeStruct((n, BRANCH_WIDTH), BF16),
                   jax.ShapeDtypeStruct((n, BRANCH_WIDTH), BF16)),
        scratch_shapes=[pltpu.VMEM((SUBLANES, LANES), F32)],
        compiler_params=_params("arbitrary", "arbitrary"),
        name="fgate_cumsum",
    )(f, b, pk, pq)


def _lower_half(rows):
    return lax.broadcasted_iota(jnp.int32, (rows, LANES), 1) < HEAD_DIM


def _build_kv(k_ref, v_ref, extra_of, ka_ref, va_ref, seq):
    lane = lax.broadcasted_iota(jnp.int32, (KV_ROWS, LANES), 1)
    lower = lane < HEAD_DIM
    one_hi = jnp.where(lane == HEAD_DIM, 1.0, 0.0).astype(BF16)
    one_lo = jnp.where(lane == 0, 1.0, 0.0).astype(BF16)

    def body(i, carry):
        rows = pl.ds(pl.multiple_of(i * KV_ROWS, KV_ROWS), KV_ROWS)
        k, v, extra = k_ref[rows, :], v_ref[rows, :], extra_of(i, rows)
        ka_ref[0, rows, :] = jnp.where(lower, k, extra)
        ka_ref[1, rows, :] = jnp.where(lower, extra, k)
        va_ref[0, rows, :] = jnp.where(lower, v, one_hi)
        va_ref[1, rows, :] = jnp.where(lower, one_lo, v)
        return carry

    lax.fori_loop(0, seq // KV_ROWS, body, 0)


def _extend_q(q_scaled, extras):
    lower = _lower_half(TQ)
    return (jnp.where(lower, q_scaled, extras[0]), jnp.where(lower, extras[1], q_scaled))


def _flash_tile(qa, ka_ref, va_ref, m_ref, acc_ref, start, keys=TK, adjust=None, first=False):
    rows = pl.ds(start, keys)
    scores = [lax.dot_general(qa[hh], ka_ref[hh, rows, :], (((1,), (1,)), ((), ())),
                              preferred_element_type=F32) for hh in range(2)]
    if adjust is not None:
        scores = [adjust(hh, scores[hh]) for hh in range(2)]
    for hh in range(2):
        s = scores[hh]
        row_max = jnp.broadcast_to(jnp.max(s, axis=-1, keepdims=True), (TQ, LANES))
        m_new = row_max if first else jnp.maximum(m_ref[hh], row_max)
        p = jnp.exp2(s - jnp.concatenate([m_new] * (keys // LANES), axis=1)).astype(BF16)
        pv = jnp.dot(p, va_ref[hh, rows, :], preferred_element_type=F32)
        if first:
            acc_ref[hh] = pv
        else:
            acc_ref[hh] = jnp.exp2(m_ref[hh] - m_new) * acc_ref[hh] + pv
        m_ref[hh] = m_new


def _pair_output(o_ref, acc_ref):
    lane = lax.broadcasted_iota(jnp.int32, (TQ, LANES), 1)
    a0, a1 = acc_ref[0], acc_ref[1]
    l0 = jnp.sum(jnp.where(lane == HEAD_DIM, a0, 0.0), axis=-1, keepdims=True)
    l1 = jnp.sum(jnp.where(lane == 0, a1, 0.0), axis=-1, keepdims=True)
    o_ref[...] = jnp.where(lane < HEAD_DIM, a0 / l0, a1 / l1).astype(BF16)


def _flash_chunks(qa, ka_ref, va_ref, m_ref, acc_ref, n_chunks):
    keys = CHUNK_TILES * TK

    def chunk(c, carry):
        _flash_tile(qa, ka_ref, va_ref, m_ref, acc_ref, pl.multiple_of(c * keys, keys), keys=keys)
        return carry

    lax.fori_loop(0, n_chunks, chunk, 0)


def _flash_tail(qa, ka_ref, va_ref, m_ref, acc_ref, last_tile, n_loose, tail_tables, guard=None):
    rem = n_loose % CHUNK_TILES
    for r in range(CHUNK_TILES):
        if isinstance(rem, int) and rem != r:
            continue
        tiles = r + len(tail_tables(0))
        cond = guard if isinstance(rem, int) else (rem == r if guard is None else (rem == r) & guard)

        def adjust(hh, s, r=r):
            parts = [s[:, :r * TK]] if r else []
            parts += [s[:, (r + i) * TK:(r + i + 1) * TK] + t for i, t in enumerate(tail_tables(hh))]
            return jnp.concatenate(parts, axis=1) if len(parts) > 1 else parts[0]

        @pl.when(cond)
        def _(tiles=tiles, adjust=adjust):
            start = pl.multiple_of((last_tile + 1 - tiles) * TK, TK)
            _flash_tile(qa, ka_ref, va_ref, m_ref, acc_ref, start, keys=tiles * TK,
                        adjust=adjust, first=True)


def _fox_kernel(q_ref, qx_ref, k_ref, v_ref, kx_ref, o_ref, ka_ref, va_ref, m_ref, acc_ref, *, seq):
    qt = pl.program_id(2)

    @pl.when(qt == 0)
    def _():
        _build_kv(k_ref, v_ref, lambda i, rows: kx_ref[rows, :], ka_ref, va_ref, seq)

    q_scaled = (q_ref[...].astype(F32) * (SCALE * LOG2E)).astype(BF16)
    qx = qx_ref[...]
    qa = _extend_q(q_scaled, (qx, qx))
    causal = jnp.where(lax.broadcasted_iota(jnp.int32, (TQ, TK), 1)
                       <= lax.broadcasted_iota(jnp.int32, (TQ, TK), 0), 0.0, NEG)
    _flash_tail(qa, ka_ref, va_ref, m_ref, acc_ref, qt, qt, lambda hh: [causal])
    _flash_chunks(qa, ka_ref, va_ref, m_ref, acc_ref, qt // CHUNK_TILES)
    _pair_output(o_ref, acc_ref)


def _fox(qkv, kx, qx, bsz, seq):
    n = qkv.shape[0]
    qpt = seq // TQ
    koff, voff = BRANCH_WIDTH // LANES, 2 * BRANCH_WIDTH // LANES
    q_tile = lambda off: pl.BlockSpec((TQ, LANES), lambda b, p, i: (b * qpt + i, off + p))
    kv_seq = lambda off: pl.BlockSpec((seq, LANES), lambda b, p, i: (b, off + p))
    return pl.pallas_call(
        functools.partial(_fox_kernel, seq=seq),
        grid=(bsz, N_PAIRS, qpt),
        in_specs=[q_tile(0), q_tile(0), kv_seq(koff), kv_seq(voff), kv_seq(0)],
        out_specs=q_tile(0),
        out_shape=jax.ShapeDtypeStruct((n, BRANCH_WIDTH), BF16),
        scratch_shapes=[pltpu.VMEM((2, seq, LANES), BF16), pltpu.VMEM((2, seq, LANES), BF16),
                        pltpu.VMEM((2, TQ, LANES), F32), pltpu.VMEM((2, TQ, LANES), F32)],
        compiler_params=_params("arbitrary", "arbitrary", "arbitrary"),
        name="fox",
    )(qkv, qx, qkv, qkv, kx)


def _moba_kernel(tab_ref, q_ref, k_ref, v_ref, town_ref, tprev_ref, o_ref,
                 kmean_ref, ka_ref, va_ref, m_ref, acc_ref, *, seq):
    pair = pl.program_id(1)
    qt = pl.program_id(2)
    nblk = seq // MOBA_BLOCK
    blk_shift = MOBA_BLOCK.bit_length() - 1

    @pl.when(qt == 0)
    def _():
        blk = lax.broadcasted_iota(jnp.int32, (LANES, seq), 0)
        pos = lax.broadcasted_iota(jnp.int32, (LANES, seq), 1)
        member = jnp.where(jnp.right_shift(pos, blk_shift) == blk, 1.0 / MOBA_BLOCK, 0.0).astype(BF16)
        kmean_ref[...] = jnp.dot(member, k_ref[...], preferred_element_type=F32)

        def extra_of(i, rows):
            xl = jnp.bitwise_and(lax.broadcasted_iota(jnp.int32, (KV_ROWS, LANES), 1), HEAD_DIM - 1)
            row_blk = jnp.right_shift(
                i * KV_ROWS + lax.broadcasted_iota(jnp.int32, (KV_ROWS, LANES), 0), blk_shift)
            hit = (xl < MOBA_SEL_LANE) | (xl - MOBA_SEL_LANE == row_blk)
            return jnp.where(hit, 1.0, 0.0).astype(BF16)

        _build_kv(k_ref, v_ref, extra_of, ka_ref, va_ref, seq)

    q_pair = q_ref[...]
    lane = lax.broadcasted_iota(jnp.int32, (TQ, LANES), 1)
    lower = lane < HEAD_DIM
    zero = jnp.zeros((TQ, LANES), F32)
    q32 = q_pair.astype(F32)
    q_gate = (jnp.where(lower, q32, zero), jnp.where(lower, zero, q32))
    blk = lax.broadcasted_iota(jnp.int32, (nblk, TQ), 0)
    past = blk < qt
    extras = []
    for hh in range(2):
        gate = lax.dot_general(kmean_ref[0:nblk, :], q_gate[hh], (((1,), (1,)), ((), ())),
                               preferred_element_type=F32, precision=lax.Precision.HIGHEST)
        gate = jnp.where(past, gate, NEG)
        sel = blk == qt
        for _ in range(MOBA_TOPK):
            top = jnp.max(gate, axis=0, keepdims=True)
            idx = jnp.min(jnp.where(gate == top, blk, nblk), axis=0, keepdims=True)
            pick = blk == idx
            sel = sel | (pick & past)
            gate = jnp.where(pick, -jnp.inf, gate)
        penalty = jnp.concatenate([jnp.where(sel, 0.0, NEG), jnp.zeros((LANES - nblk, TQ), F32)], axis=0).T
        base = HEAD_DIM * (1 - hh)
        xl = lane - base
        far = jnp.full((TQ, LANES), tab_ref[REL_BUCKETS - 1, N_HEADS + 2 * pair + hh] * LOG2E, F32)
        hi, mid, lo = _pieces(far)
        extra = jnp.where(xl == 0, hi, jnp.where(xl == 1, mid, jnp.where(
            xl == 2, lo, pltpu.roll(penalty, base + MOBA_SEL_LANE, axis=1))))
        extras.append(extra.astype(BF16))

    qa = _extend_q((q32 * (SCALE * LOG2E)).astype(BF16), extras)
    n_far = jnp.maximum(qt - 1, 0)
    _flash_tail(qa, ka_ref, va_ref, m_ref, acc_ref, qt, 0, lambda hh: [town_ref[hh]], guard=qt == 0)
    _flash_tail(qa, ka_ref, va_ref, m_ref, acc_ref, qt, n_far,
                lambda hh: [tprev_ref[hh], town_ref[hh]], guard=qt >= 1)
    _flash_chunks(qa, ka_ref, va_ref, m_ref, acc_ref, n_far // CHUNK_TILES)
    _pair_output(o_ref, acc_ref)


def _moba(qkv, rel_bias, t_own, t_prev, bsz, seq):
    n = qkv.shape[0]
    qpt = seq // TQ
    koff, voff = BRANCH_WIDTH // LANES, 2 * BRANCH_WIDTH // LANES
    q_tile = pl.BlockSpec((TQ, LANES), lambda b, p, i: (b * qpt + i, p))
    kv_seq = lambda off: pl.BlockSpec((seq, LANES), lambda b, p, i: (b, off + p))
    pair_table = pl.BlockSpec((2, MOBA_BLOCK, MOBA_BLOCK), lambda b, p, i: (p, 0, 0))
    return pl.pallas_call(
        functools.partial(_moba_kernel, seq=seq),
        grid=(bsz, N_PAIRS, qpt),
        in_specs=[pl.BlockSpec(memory_space=pltpu.SMEM), q_tile, kv_seq(koff), kv_seq(voff),
                  pair_table, pair_table],
        out_specs=q_tile,
        out_shape=jax.ShapeDtypeStruct((n, BRANCH_WIDTH), BF16),
        scratch_shapes=[pltpu.VMEM((LANES, LANES), F32),
                        pltpu.VMEM((2, seq, LANES), BF16), pltpu.VMEM((2, seq, LANES), BF16),
                        pltpu.VMEM((2, TQ, LANES), F32), pltpu.VMEM((2, TQ, LANES), F32)],
        compiler_params=_params("arbitrary", "arbitrary", "arbitrary"),
        name="moba",
    )(rel_bias, qkv, qkv, qkv, t_own, t_prev)


def _merge_kernel(x_ref, g_ref, ya_ref, yb_ref, yc_ref, yd_ref, wg_ref, bg_ref, wbr_ref, wout_ref, o_ref):
    x = x_ref[...]
    hb = _rmsnorm(x, g_ref[...]).astype(BF16)
    merged = jnp.zeros((TM, D_MODEL), F32)
    for n, y_ref in enumerate((ya_ref, yb_ref, yc_ref, yd_ref)):
        gz = jnp.dot(hb, wg_ref[:, n * D_MODEL:(n + 1) * D_MODEL], preferred_element_type=F32)
        gate = jax.nn.sigmoid(gz + bg_ref[n:n + 1, :])
        merged = merged + gate * jnp.dot(y_ref[...], wbr_ref[n], preferred_element_type=F32)
    o_ref[...] = x + jnp.dot(merged.astype(BF16), wout_ref[...], preferred_element_type=F32)


def _merge(x, g, ys, wg, bg, wbr, wout):
    n = x.shape[0]
    row = lambda w: pl.BlockSpec((TM, w), lambda i: (i, 0))
    return pl.pallas_call(
        _merge_kernel,
        grid=(n // TM,),
        in_specs=[row(D_MODEL), _resident(g.shape)] + [row(BRANCH_WIDTH)] * N_BRANCH
        + [_resident(wg.shape), _resident(bg.shape), _resident(wbr.shape), _resident(wout.shape)],
        out_specs=row(D_MODEL),
        out_shape=jax.ShapeDtypeStruct((n, D_MODEL), F32),
        compiler_params=_params("arbitrary"),
        name="merge",
    )(x, g, *ys, wg, bg, wbr, wout)


def _ffn_kernel(x_ref, g_ref, wup_ref, cw_ref, cb_ref, wdown_ref, fg_ref, o_ref, halo_ref, buf_ref, *, final):
    @pl.when(pl.program_id(1) == 0)
    def _():
        halo_ref[...] = jnp.zeros_like(halo_ref)

    x = x_ref[...]
    hb = _rmsnorm(x, g_ref[...]).astype(BF16)
    acc = jnp.zeros((TM, D_MODEL), F32)
    for c in range(D_FF // FF_CHUNK):
        halves = []
        for part in range(2):
            col = part * D_FF + c * FF_CHUNK
            buf_ref[0:SUBLANES, :] = halo_ref[:, col:col + FF_CHUNK]
            buf_ref[SUBLANES:SUBLANES + TM, :] = jnp.dot(
                hb, wup_ref[:, col:col + FF_CHUNK], preferred_element_type=F32)
            halo_ref[:, col:col + FF_CHUNK] = buf_ref[TM:TM + SUBLANES, :]
            u = cb_ref[:, col:col + FF_CHUNK]
            for k in range(FFN_CONV_WIDTH):
                lo = SUBLANES - (FFN_CONV_WIDTH - 1) + k
                u = u + cw_ref[k:k + 1, col:col + FF_CHUNK] * buf_ref[lo:lo + TM, :]
            halves.append(u)
        a = (halves[0] * jax.nn.sigmoid(halves[0])) * halves[1]
        acc = acc + jnp.dot(a.astype(BF16), wdown_ref[c * FF_CHUNK:(c + 1) * FF_CHUNK, :],
                            preferred_element_type=F32)
    y = x + acc
    o_ref[...] = _rmsnorm(y, fg_ref[...]) if final else y


def _ffn(x, g, wup, cw, cb, wdown, fg, bsz, seq, final):
    n = x.shape[0]
    spt = seq // TM
    row = pl.BlockSpec((TM, D_MODEL), lambda bi, i: (bi * spt + i, 0))
    return pl.pallas_call(
        functools.partial(_ffn_kernel, final=final),
        grid=(bsz, spt),
        in_specs=[row, _resident(g.shape), _resident(wup.shape), _resident(cw.shape),
                  _resident(cb.shape), _resident(wdown.shape), _resident(fg.shape)],
        out_specs=row,
        out_shape=jax.ShapeDtypeStruct((n, D_MODEL), F32),
        scratch_shapes=[pltpu.VMEM((SUBLANES, 2 * D_FF), F32),
                        pltpu.VMEM((TM + SUBLANES, FF_CHUNK), F32)],
        compiler_params=_params("arbitrary", "arbitrary"),
        name="ffn",
    )(x, g, wup, cw, cb, wdown, fg)


def _split_w_in(w_in):
    hd, bw = HEAD_DIM, BRANCH_WIDTH
    o = 0
    qa = w_in[:, o:o + bw]; o += bw
    ka = w_in[:, o:o + SWA_KV_HEADS * hd]; o += SWA_KV_HEADS * hd
    va = w_in[:, o:o + SWA_KV_HEADS * hd]; o += SWA_KV_HEADS * hd
    ub = w_in[:, o:o + 2 * bw]; o += 2 * bw
    qkv_c = w_in[:, o:o + 3 * bw]; o += 3 * bw
    fc = w_in[:, o:o + N_HEADS]; o += N_HEADS
    qkv_d = w_in[:, o:o + 3 * bw]; o += 3 * bw
    gates = w_in[:, o:o + N_BRANCH * D_MODEL]
    dup = lambda w: jnp.concatenate([w[:, i * hd:(i + 1) * hd] for i in (0, 0, 1, 1)], axis=1)
    wa = jnp.concatenate([qa, dup(ka), dup(va)], axis=1).astype(BF16)
    wf = jnp.pad(fc, ((0, 0), (0, LANES - N_HEADS)))
    return wa, ub.astype(BF16), qkv_c.astype(BF16), wf, qkv_d.astype(BF16), gates.astype(BF16)


def _layer(x, bsz, seq, biases, rel_bias, ln1_g, w_in, b_gate, b_fgate, sinks, conv_w, conv_b,
           conv_ln_g, conv_ln_b, w_br, w_out, ln2_g, w_up, ffn_conv_w, ffn_conv_b, w_down,
           final_g, final):
    bias_swa, table_own, table_prev = biases
    row = lambda v: v.reshape(1, -1)
    wa, wb, wc, wf, wd, wg = _split_w_in(w_in)
    za, zb, zc, zf, zd = _inproj(x, row(ln1_g), wa, wb, wc, wf, wd)
    ya = _swa(za, sinks, bias_swa, seq)
    yb = _conformer(zb, conv_w, row(conv_b), row(conv_ln_g), row(conv_ln_b), bsz, seq)
    kx, qx = _fgate(zf, jnp.pad(row(b_fgate), ((0, 0), (0, LANES - N_HEADS))), bsz, seq)
    yc = _fox(zc, kx, qx, bsz, seq)
    yd = _moba(zd, rel_bias, table_own, table_prev, bsz, seq)
    x = _merge(x, row(ln1_g), (ya, yb, yc, yd), wg, b_gate, w_br.astype(BF16), w_out.astype(BF16))
    return _ffn(x, row(ln2_g), w_up.astype(BF16), ffn_conv_w, row(ffn_conv_b), w_down.astype(BF16),
                row(final_g), bsz, seq, final)


def kernel(x, ln1_g, w_in, b_gate, b_fgate, sinks, conv_w, conv_b, conv_ln_g, conv_ln_b, w_br, w_out,
           ln2_g, w_up, ffn_conv_w, ffn_conv_b, w_down, rel_bias, final_g):
    bsz, seq, _ = x.shape
    depth = w_in.shape[0]
    assert seq % TM == 0 and seq % KV_ROWS == 0 and TQ == TK == MOBA_BLOCK
    assert MOBA_SEL_LANE + seq // MOBA_BLOCK <= HEAD_DIM
    biases = _expand_bias(rel_bias)
    y = x.reshape(bsz * seq, D_MODEL)
    for layer in range(depth):
        y = _layer(y, bsz, seq, biases, rel_bias, ln1_g[layer], w_in[layer], b_gate[layer],
                   b_fgate[layer], sinks[layer], conv_w[layer], conv_b[layer], conv_ln_g[layer],
                   conv_ln_b[layer], w_br[layer], w_out[layer], ln2_g[layer], w_up[layer],
                   ffn_conv_w[layer], ffn_conv_b[layer], w_down[layer], final_g,
                   final=layer == depth - 1)
    return y.reshape(bsz, seq, D_MODEL)
```

```python
import functools
import math

import jax
import jax.numpy as jnp
import numpy as np
from jax import lax
from jax.experimental import pallas as pl
from jax.experimental.pallas import tpu as pltpu

D_MODEL = 1024
HEAD_DIM = 64
N_BRANCH = 4
BRANCH_WIDTH = 512
N_HEADS = 8
N_PAIRS = N_HEADS // 2
SWA_KV_HEADS = 2
SWA_BLOCK = 128
CONV_WIDTH = 31
MOBA_BLOCK = 256
MOBA_TOPK = 3
D_FF = 2816
FFN_CONV_WIDTH = 3
REL_BUCKETS = 32
REL_MAX_DIST = 128
EPS = 1e-6
NEG = -1e30
SCALE = HEAD_DIM ** -0.5
LOG2E = math.log2(math.e)

LANES = 128
SUBLANES = 8
VMEM_LIMIT = 56 * 1024 * 1024

TM = 512
TQ = 256
TK = 256
CHUNK_TILES = 8
SCORE_LOOKAHEAD = 3
KV_ROWS = 512
CUM_T = 256
FF_CHUNK = 256
CONV_ROWS = 64
CONV_HALO = 32

N_PIECES = 3
PIECE_ROWS = SUBLANES * N_PIECES
MOBA_SEL_LANE = SUBLANES
V_ROWS = HEAD_DIM + 16

BF16 = jnp.bfloat16
F32 = jnp.float32


def _params(*sem):
    return pltpu.CompilerParams(dimension_semantics=sem, vmem_limit_bytes=VMEM_LIMIT)


def _resident(shape):
    zeros = (0,) * len(shape)
    return pl.BlockSpec(shape, lambda *_: zeros, pipeline_mode=pl.Buffered(1))


def _rmsnorm(x, g):
    return (x * lax.rsqrt(jnp.mean(x * x, axis=-1, keepdims=True) + EPS)) * g


def _pieces(x):
    hi = x.astype(BF16).astype(F32)
    rest = x - hi
    mid = rest.astype(BF16).astype(F32)
    return hi, mid, rest - mid


def _rel_bucket_np(dist):
    n = np.maximum(dist, 0)
    max_exact = REL_BUCKETS // 2
    nf = np.maximum(n, 1).astype(np.float64)
    large = max_exact + (np.log(nf / max_exact) / math.log(REL_MAX_DIST / max_exact)
                         * (REL_BUCKETS - max_exact) + 1e-9).astype(np.int32)
    large = np.minimum(large, REL_BUCKETS - 1)
    return np.where(n < max_exact, n, large).astype(np.int32)


def _bias_kernel(tab_ref, bs_ref, bo_ref, bp_ref, os_ref, oo_ref, op_ref):
    os_ref[...] = jnp.zeros_like(os_ref)
    oo_ref[...] = jnp.zeros_like(oo_ref)
    op_ref[...] = jnp.zeros_like(op_ref)
    bs, bo, bp = bs_ref[...], bo_ref[...], bp_ref[...]

    def body(b, carry):
        ms, mo, mp = bs == b, bo == b, bp == b
        for h in range(N_HEADS):
            os_ref[h] = jnp.where(ms, tab_ref[b, h], os_ref[h])
            oo_ref[h] = jnp.where(mo, tab_ref[b, N_HEADS + h], oo_ref[h])
            op_ref[h] = jnp.where(mp, tab_ref[b, N_HEADS + h], op_ref[h])
        return carry

    lax.fori_loop(0, REL_BUCKETS, body, 0)
    causal = (lax.broadcasted_iota(jnp.int32, (MOBA_BLOCK, MOBA_BLOCK), 0)
              <= lax.broadcasted_iota(jnp.int32, (MOBA_BLOCK, MOBA_BLOCK), 1))
    for h in range(N_HEADS):
        far = tab_ref[REL_BUCKETS - 1, N_HEADS + h]
        oo_ref[h] = jnp.where(causal, (oo_ref[h] - far) * LOG2E, NEG)
        op_ref[h] = (op_ref[h] - far) * LOG2E


def _expand_bias(rel_bias):
    qi = np.arange(SWA_BLOCK)[:, None]
    kj = np.arange(2 * SWA_BLOCK)[None, :]
    b_swa = _rel_bucket_np(qi + SWA_BLOCK - kj)
    qi = np.arange(MOBA_BLOCK)[:, None]
    kj = np.arange(MOBA_BLOCK)[None, :]
    b_own = np.ascontiguousarray(_rel_bucket_np(qi - kj).T)
    b_prev = np.ascontiguousarray(_rel_bucket_np(qi - kj + MOBA_BLOCK).T)
    vm = pl.BlockSpec(memory_space=pltpu.VMEM)
    return pl.pallas_call(
        _bias_kernel,
        out_shape=(jax.ShapeDtypeStruct((N_HEADS, SWA_BLOCK, 2 * SWA_BLOCK), F32),
                   jax.ShapeDtypeStruct((N_HEADS, MOBA_BLOCK, MOBA_BLOCK), F32),
                   jax.ShapeDtypeStruct((N_HEADS, MOBA_BLOCK, MOBA_BLOCK), F32)),
        in_specs=[pl.BlockSpec(memory_space=pltpu.SMEM), vm, vm, vm],
        out_specs=(vm, vm, vm),
        name="bias_expand",
    )(rel_bias, jnp.asarray(b_swa), jnp.asarray(b_own), jnp.asarray(b_prev))


def _inproj_kernel(x_ref, g_ref, wa_ref, wb_ref, wc_ref, wf_ref, wd_ref,
                   oa_ref, ob_ref, oc_ref, of_ref, od_ref):
    h = _rmsnorm(x_ref[...], g_ref[...])
    hb = h.astype(BF16)
    oa_ref[...] = jnp.dot(hb, wa_ref[...], preferred_element_type=F32).astype(BF16)
    ob_ref[...] = jnp.dot(hb, wb_ref[...], preferred_element_type=F32)
    oc_ref[...] = jnp.dot(hb, wc_ref[...], preferred_element_type=F32).astype(BF16)
    od_ref[...] = jnp.dot(hb, wd_ref[...], preferred_element_type=F32).astype(BF16)
    of_ref[...] = jnp.dot(h, wf_ref[...], preferred_element_type=F32,
                          precision=lax.Precision.HIGHEST)


def _inproj(x, g, wa, wb, wc, wf, wd):
    n = x.shape[0]
    row = lambda w: pl.BlockSpec((TM, w), lambda i: (i, 0))
    ws = (wa, wb, wc, wf, wd)
    dts = (BF16, F32, BF16, F32, BF16)
    return pl.pallas_call(
        _inproj_kernel,
        grid=(n // TM,),
        in_specs=[row(D_MODEL), _resident(g.shape)] + [_resident(w.shape) for w in ws],
        out_specs=tuple(row(w.shape[1]) for w in ws),
        out_shape=tuple(jax.ShapeDtypeStruct((n, w.shape[1]), dt) for w, dt in zip(ws, dts)),
        compiler_params=_params("arbitrary"),
        name="inproj",
    )(x, g, *ws)


def _swa_kernel(sink_ref, q_ref, kc_ref, kp_ref, vc_ref, vp_ref, bias_ref, o_ref, *, blocks_per_seq):
    blk = pl.program_id(0) % blocks_per_seq
    kk = jnp.concatenate([kp_ref[...], kc_ref[...]], axis=0)
    vv = jnp.concatenate([vp_ref[...], vc_ref[...]], axis=0)
    qi = lax.broadcasted_iota(jnp.int32, (SWA_BLOCK, 2 * SWA_BLOCK), 0)
    kj = lax.broadcasted_iota(jnp.int32, (SWA_BLOCK, 2 * SWA_BLOCK), 1)
    dist = qi + SWA_BLOCK - kj
    mask = (dist >= 0) & (dist < SWA_BLOCK) & ((blk > 0) | (kj >= SWA_BLOCK))
    lane = lax.broadcasted_iota(jnp.int32, (SWA_BLOCK, LANES), 1)
    upper = lane >= HEAD_DIM
    kv_of = lambda h: h // (N_HEADS // SWA_KV_HEADS)
    scores = []
    for h in range(N_HEADS):
        q_pair = q_ref[:, (h // 2) * LANES:(h // 2 + 1) * LANES] * SCALE
        q_h = jnp.where(upper == (h % 2 == 1), q_pair, jnp.zeros_like(q_pair))
        k_kv = kk[:, kv_of(h) * LANES:(kv_of(h) + 1) * LANES]
        scores.append(lax.dot_general(q_h, k_kv, (((1,), (1,)), ((), ())), preferred_element_type=F32))
    probs = []
    for h in range(N_HEADS):
        logits = jnp.where(mask, scores[h] + bias_ref[h], NEG)
        sink = sink_ref[h]
        m = jnp.maximum(jnp.max(logits, axis=-1, keepdims=True), sink)
        p = jnp.exp(logits - m)
        denom = jnp.sum(p, axis=-1, keepdims=True) + jnp.exp(sink - m)
        probs.append((p / denom).astype(BF16))
    outs = [jnp.dot(probs[h], vv[:, kv_of(h) * LANES:(kv_of(h) + 1) * LANES], preferred_element_type=F32)
            for h in range(N_HEADS)]
    for pair in range(N_PAIRS):
        o_ref[:, pair * LANES:(pair + 1) * LANES] = jnp.where(
            upper, outs[2 * pair + 1], outs[2 * pair]).astype(BF16)


def _swa(qkv, sinks, bias, seq):
    n = qkv.shape[0]
    bps = seq // SWA_BLOCK
    kcol, vcol = BRANCH_WIDTH // (2 * LANES), BRANCH_WIDTH // (2 * LANES) + 1
    prev = lambda i: jnp.where(i % bps == 0, i, i - 1)
    return pl.pallas_call(
        functools.partial(_swa_kernel, blocks_per_seq=bps),
        grid=(n // SWA_BLOCK,),
        in_specs=[pl.BlockSpec(memory_space=pltpu.SMEM),
                  pl.BlockSpec((SWA_BLOCK, BRANCH_WIDTH), lambda i: (i, 0)),
                  pl.BlockSpec((SWA_BLOCK, 2 * LANES), lambda i: (i, kcol)),
                  pl.BlockSpec((SWA_BLOCK, 2 * LANES), lambda i: (prev(i), kcol)),
                  pl.BlockSpec((SWA_BLOCK, 2 * LANES), lambda i: (i, vcol)),
                  pl.BlockSpec((SWA_BLOCK, 2 * LANES), lambda i: (prev(i), vcol)),
                  _resident(bias.shape)],
        out_specs=pl.BlockSpec((SWA_BLOCK, BRANCH_WIDTH), lambda i: (i, 0)),
        out_shape=jax.ShapeDtypeStruct((n, BRANCH_WIDTH), BF16),
        compiler_params=_params("arbitrary"),
        name="swa",
    )(sinks, qkv, qkv, qkv, qkv, qkv, bias)


def _conformer_kernel(u_ref, w_ref, b_ref, g_ref, beta_ref, o_ref, buf_ref):
    @pl.when(pl.program_id(1) == 0)
    def _():
        buf_ref[0:CONV_HALO, :] = jnp.zeros((CONV_HALO, BRANCH_WIDTH), F32)

    a = u_ref[:, 0:BRANCH_WIDTH]
    gt = u_ref[:, BRANCH_WIDTH:2 * BRANCH_WIDTH]
    buf_ref[CONV_HALO:CONV_HALO + TM, :] = a * jax.nn.sigmoid(gt)
    first = CONV_HALO - (CONV_WIDTH - 1)
    for r in range(TM // CONV_ROWS):
        acc = jnp.zeros((CONV_ROWS, BRANCH_WIDTH), F32)
        for k in range(CONV_WIDTH):
            lo = first + r * CONV_ROWS + k
            acc = acc + w_ref[k:k + 1, :] * buf_ref[lo:lo + CONV_ROWS, :]
        y = acc + b_ref[...]
        mu = jnp.mean(y, axis=-1, keepdims=True)
        yc = y - mu
        var = jnp.mean(yc * yc, axis=-1, keepdims=True)
        z = yc * lax.rsqrt(var + EPS) * g_ref[...] + beta_ref[...]
        o_ref[r * CONV_ROWS:(r + 1) * CONV_ROWS, :] = (z * jax.nn.sigmoid(z)).astype(BF16)
    buf_ref[0:CONV_HALO, :] = buf_ref[TM:TM + CONV_HALO, :]


def _conformer(u, w, b, g, beta, bsz, seq):
    n = u.shape[0]
    spt = seq // TM
    return pl.pallas_call(
        _conformer_kernel,
        grid=(bsz, spt),
        in_specs=[pl.BlockSpec((TM, 2 * BRANCH_WIDTH), lambda bi, i: (bi * spt + i, 0)),
                  _resident(w.shape), _resident(b.shape), _resident(g.shape), _resident(beta.shape)],
        out_specs=pl.BlockSpec((TM, BRANCH_WIDTH), lambda bi, i: (bi * spt + i, 0)),
        out_shape=jax.ShapeDtypeStruct((n, BRANCH_WIDTH), BF16),
        scratch_shapes=[pltpu.VMEM((TM + CONV_HALO, BRANCH_WIDTH), F32)],
        compiler_params=_params("arbitrary", "arbitrary"),
        name="conformer",
    )(u, w, b, g, beta)


def _extra_base(h):
    return (h // 2) * LANES + HEAD_DIM * (1 - h % 2)


def _fgate_placement():
    pk = np.zeros((LANES, BRANCH_WIDTH), np.float32)
    pq = np.zeros((LANES, BRANCH_WIDTH), np.float32)
    for h in range(N_HEADS):
        base = _extra_base(h)
        for i in range(N_PIECES):
            pk[SUBLANES * i + h, base + i] = 1.0
            pq[PIECE_ROWS, base + i] = -1.0
            pk[PIECE_ROWS, base + N_PIECES + i] = 1.0
            pq[SUBLANES * i + h, base + N_PIECES + i] = 1.0
    return jnp.asarray(pk, BF16), jnp.asarray(pq, BF16)


def _fgate_kernel(f_ref, b_ref, pk_ref, pq_ref, kx_ref, qx_ref, carry_ref):
    @pl.when(pl.program_id(1) == 0)
    def _():
        carry_ref[...] = jnp.zeros_like(carry_ref)

    z = f_ref[...] + b_ref[...]
    log_f = -(jnp.maximum(-z, 0.0) + jnp.log1p(jnp.exp(-jnp.abs(z))))
    ri = lax.broadcasted_iota(jnp.int32, (CUM_T, CUM_T), 0)
    ci = lax.broadcasted_iota(jnp.int32, (CUM_T, CUM_T), 1)
    tri = (ci <= ri).astype(F32)
    c = jnp.dot(tri, log_f, preferred_element_type=F32,
                precision=lax.Precision.HIGHEST) + carry_ref[0:1, :]
    carry_ref[...] = jnp.broadcast_to(c[CUM_T - 1:CUM_T, :], carry_ref.shape)
    hi, mid, lo = _pieces(c * LOG2E)
    lane = lax.broadcasted_iota(jnp.int32, (CUM_T, LANES), 1)
    packed = jnp.where(lane < SUBLANES, hi,
                       jnp.where(lane < 2 * SUBLANES, pltpu.roll(mid, SUBLANES, axis=1),
                                 jnp.where(lane < PIECE_ROWS, pltpu.roll(lo, 2 * SUBLANES, axis=1),
                                           jnp.where(lane == PIECE_ROWS, 1.0, 0.0)))).astype(BF16)
    kx_ref[...] = jnp.dot(packed, pk_ref[...], preferred_element_type=F32).astype(BF16)
    qx_ref[...] = jnp.dot(packed, pq_ref[...], preferred_element_type=F32).astype(BF16)


def _fgate(f, b, bsz, seq):
    n = f.shape[0]
    spt = seq // CUM_T
    pk, pq = _fgate_placement()
    row = lambda w: pl.BlockSpec((CUM_T, w), lambda bi, i: (bi * spt + i, 0))
    return pl.pallas_call(
        _fgate_kernel,
        grid=(bsz, spt),
        in_specs=[row(LANES), _resident(b.shape), _resident(pk.shape), _resident(pq.shape)],
        out_specs=(row(BRANCH_WIDTH), row(BRANCH_WIDTH)),
        out_shape=(jax.ShapeDtypeStruct((n, BRANCH_WIDTH), BF16),
                   jax.ShapeDtypeStruct((n, BRANCH_WIDTH), BF16)),
        scratch_shapes=[pltpu.VMEM((SUBLANES, LANES), F32)],
        compiler_params=_params("arbitrary", "arbitrary"),
        name="fgate_cumsum",
    )(f, b, pk, pq)


def _build_kv(k_ref, v_ref, extra_of, ka_ref, vt_ref, seq):
    lane = lax.broadcasted_iota(jnp.int32, (KV_ROWS, LANES), 1)
    lower = lane < HEAD_DIM
    ones_rows = jnp.where(lax.broadcasted_iota(jnp.int32, (V_ROWS - HEAD_DIM, KV_ROWS), 0) == 0, 1.0, 0.0)

    def body(i, carry):
        rows = pl.ds(pl.multiple_of(i * KV_ROWS, KV_ROWS), KV_ROWS)
        k, extra = k_ref[rows, :], extra_of(i, rows)
        ka_ref[0, rows, :] = jnp.where(lower, k, extra)
        ka_ref[1, rows, :] = jnp.where(lower, extra, k)
        v_t = v_ref[rows, :].astype(F32).T
        for hh in range(2):
            vt_ref[hh, :, rows] = jnp.concatenate(
                [v_t[hh * HEAD_DIM:(hh + 1) * HEAD_DIM], ones_rows], axis=0).astype(BF16)
        return carry

    lax.fori_loop(0, seq // KV_ROWS, body, 0)


def _extend_q_t(q_t, extras_t):
    return (jnp.concatenate([q_t[0:HEAD_DIM], extras_t[0]], axis=0).astype(BF16),
            jnp.concatenate([extras_t[1], q_t[HEAD_DIM:LANES]], axis=0).astype(BF16))


def _flash_step(qa_t, ka_ref, vt_ref, m_ref, acc_ref, start, n_tiles, tables=None, first=False):
    m = [None if first else m_ref[hh, 0:1, :] for hh in range(2)]
    acc = [None if first else acc_ref[hh] for hh in range(2)]
    n_plain = n_tiles - (len(tables(0)) if tables else 0)
    tile_rows = lambda t: pl.ds(pl.multiple_of(start + t * TK, TK), TK)
    score = lambda t: [jnp.dot(ka_ref[hh, tile_rows(t), :], qa_t[hh], preferred_element_type=F32)
                       for hh in range(2)]
    ahead = [score(t) for t in range(min(SCORE_LOOKAHEAD, n_tiles))]
    for t in range(n_tiles):
        rows = tile_rows(t)
        if t + SCORE_LOOKAHEAD < n_tiles:
            ahead.append(score(t + SCORE_LOOKAHEAD))
        scores = ahead.pop(0)
        for hh in range(2):
            s = scores[hh]
            if t >= n_plain:
                s = s + tables(hh)[t - n_plain]
            col_max = jnp.max(s, axis=0, keepdims=True)
            m_new = col_max if m[hh] is None else jnp.maximum(m[hh], col_max)
            p = jnp.exp2(s - m_new).astype(BF16)
            pv = jnp.dot(vt_ref[hh, :, rows], p, preferred_element_type=F32)
            acc[hh] = pv if acc[hh] is None else jnp.exp2(m[hh] - m_new) * acc[hh] + pv
            m[hh] = m_new
    for hh in range(2):
        m_ref[hh] = jnp.broadcast_to(m[hh], (SUBLANES, TQ))
        acc_ref[hh] = acc[hh]


def _pair_output(o_ref, acc_ref):
    out_t = jnp.concatenate(
        [acc_ref[hh, 0:HEAD_DIM, :] / acc_ref[hh, HEAD_DIM:HEAD_DIM + 1, :] for hh in range(2)], axis=0)
    o_ref[...] = out_t.T.astype(BF16)


def _flash_scratch(seq):
    return [pltpu.VMEM((2, seq, LANES), BF16), pltpu.VMEM((2, V_ROWS, seq), BF16),
            pltpu.VMEM((2, SUBLANES, TQ), F32), pltpu.VMEM((2, V_ROWS, TQ), F32)]


def _flash_chunks(qa, ka_ref, vt_ref, m_ref, acc_ref, n_chunks):
    def chunk(c, carry):
        _flash_step(qa, ka_ref, vt_ref, m_ref, acc_ref, c * (CHUNK_TILES * TK), CHUNK_TILES)
        return carry

    lax.fori_loop(0, n_chunks, chunk, 0)


def _flash_tail(qa, ka_ref, vt_ref, m_ref, acc_ref, last_tile, n_loose, tail_tables, guard=None):
    rem = n_loose % CHUNK_TILES
    for r in range(CHUNK_TILES):
        if isinstance(rem, int) and rem != r:
            continue
        tiles = r + len(tail_tables(0))
        cond = guard if isinstance(rem, int) else (rem == r if guard is None else (rem == r) & guard)

        @pl.when(cond)
        def _(tiles=tiles):
            _flash_step(qa, ka_ref, vt_ref, m_ref, acc_ref, (last_tile + 1 - tiles) * TK, tiles,
                        tables=tail_tables, first=True)


def _fox_kernel(q_ref, qx_ref, k_ref, v_ref, kx_ref, o_ref, ka_ref, vt_ref, m_ref, acc_ref, *, seq):
    qt = pl.program_id(2)

    @pl.when(qt == 0)
    def _():
        _build_kv(k_ref, v_ref, lambda i, rows: kx_ref[rows, :], ka_ref, vt_ref, seq)

    q_t = (q_ref[...].astype(F32) * (SCALE * LOG2E)).T
    qx_t = qx_ref[...].astype(F32).T
    qa = _extend_q_t(q_t, (qx_t[HEAD_DIM:LANES], qx_t[0:HEAD_DIM]))
    causal = jnp.where(lax.broadcasted_iota(jnp.int32, (TK, TQ), 0)
                       <= lax.broadcasted_iota(jnp.int32, (TK, TQ), 1), 0.0, NEG)
    _flash_tail(qa, ka_ref, vt_ref, m_ref, acc_ref, qt, qt, lambda hh: [causal])
    _flash_chunks(qa, ka_ref, vt_ref, m_ref, acc_ref, qt // CHUNK_TILES)
    _pair_output(o_ref, acc_ref)


def _fox(qkv, kx, qx, bsz, seq):
    n = qkv.shape[0]
    qpt = seq // TQ
    koff, voff = BRANCH_WIDTH // LANES, 2 * BRANCH_WIDTH // LANES
    q_tile = lambda off: pl.BlockSpec((TQ, LANES), lambda b, p, i: (b * qpt + i, off + p))
    kv_seq = lambda off: pl.BlockSpec((seq, LANES), lambda b, p, i: (b, off + p))
    return pl.pallas_call(
        functools.partial(_fox_kernel, seq=seq),
        grid=(bsz, N_PAIRS, qpt),
        in_specs=[q_tile(0), q_tile(0), kv_seq(koff), kv_seq(voff), kv_seq(0)],
        out_specs=q_tile(0),
        out_shape=jax.ShapeDtypeStruct((n, BRANCH_WIDTH), BF16),
        scratch_shapes=_flash_scratch(seq),
        compiler_params=_params("arbitrary", "arbitrary", "arbitrary"),
        name="fox",
    )(qkv, qx, qkv, qkv, kx)


def _moba_kernel(tab_ref, q_ref, k_ref, v_ref, town_ref, tprev_ref, o_ref,
                 kmean_ref, ka_ref, vt_ref, m_ref, acc_ref, *, seq):
    pair = pl.program_id(1)
    qt = pl.program_id(2)
    nblk = seq // MOBA_BLOCK
    blk_shift = MOBA_BLOCK.bit_length() - 1

    @pl.when(qt == 0)
    def _():
        blk = lax.broadcasted_iota(jnp.int32, (LANES, seq), 0)
        pos = lax.broadcasted_iota(jnp.int32, (LANES, seq), 1)
        member = jnp.where(jnp.right_shift(pos, blk_shift) == blk, 1.0 / MOBA_BLOCK, 0.0).astype(BF16)
        kmean_ref[...] = jnp.dot(member, k_ref[...], preferred_element_type=F32)

        def extra_of(i, rows):
            xl = jnp.bitwise_and(lax.broadcasted_iota(jnp.int32, (KV_ROWS, LANES), 1), HEAD_DIM - 1)
            row_blk = jnp.right_shift(
                i * KV_ROWS + lax.broadcasted_iota(jnp.int32, (KV_ROWS, LANES), 0), blk_shift)
            hit = (xl < N_PIECES) | (xl - MOBA_SEL_LANE == row_blk)
            return jnp.where(hit, 1.0, 0.0).astype(BF16)

        _build_kv(k_ref, v_ref, extra_of, ka_ref, vt_ref, seq)

    q_t = q_ref[...].astype(F32).T
    row = lax.broadcasted_iota(jnp.int32, (LANES, TQ), 0)
    blk = lax.broadcasted_iota(jnp.int32, (nblk, TQ), 0)
    piece_row = lax.broadcasted_iota(jnp.int32, (MOBA_SEL_LANE, TQ), 0)
    past = blk < qt
    extras = []
    for hh in range(2):
        q_head = jnp.where((row >= HEAD_DIM) == (hh == 1), q_t, 0.0)
        gate = jnp.dot(kmean_ref[0:nblk, :], q_head, preferred_element_type=F32,
                       precision=lax.Precision.HIGHEST)
        gate = jnp.where(past, gate, NEG)
        sel = blk == qt
        for _ in range(MOBA_TOPK):
            top = jnp.max(gate, axis=0, keepdims=True)
            idx = jnp.min(jnp.where(gate == top, blk, nblk), axis=0, keepdims=True)
            pick = blk == idx
            sel = sel | (pick & past)
            gate = jnp.where(pick, -jnp.inf, gate)
        far = jnp.full((MOBA_SEL_LANE, TQ), tab_ref[REL_BUCKETS - 1, N_HEADS + 2 * pair + hh] * LOG2E, F32)
        hi, mid, lo = _pieces(far)
        far_rows = jnp.where(piece_row == 0, hi, jnp.where(piece_row == 1, mid, jnp.where(piece_row == 2, lo, 0.0)))
        extras.append(jnp.concatenate(
            [far_rows, jnp.where(sel, 0.0, NEG),
             jnp.zeros((HEAD_DIM - MOBA_SEL_LANE - nblk, TQ), F32)], axis=0))

    qa = _extend_q_t(q_t * (SCALE * LOG2E), extras)
    n_far = jnp.maximum(qt - 1, 0)
    _flash_tail(qa, ka_ref, vt_ref, m_ref, acc_ref, qt, 0, lambda hh: [town_ref[hh]], guard=qt == 0)
    _flash_tail(qa, ka_ref, vt_ref, m_ref, acc_ref, qt, n_far,
                lambda hh: [tprev_ref[hh], town_ref[hh]], guard=qt >= 1)
    _flash_chunks(qa, ka_ref, vt_ref, m_ref, acc_ref, n_far // CHUNK_TILES)
    _pair_output(o_ref, acc_ref)


def _moba(qkv, rel_bias, t_own, t_prev, bsz, seq):
    n = qkv.shape[0]
    qpt = seq // TQ
    koff, voff = BRANCH_WIDTH // LANES, 2 * BRANCH_WIDTH // LANES
    q_tile = pl.BlockSpec((TQ, LANES), lambda b, p, i: (b * qpt + i, p))
    kv_seq = lambda off: pl.BlockSpec((seq, LANES), lambda b, p, i: (b, off + p))
    pair_table = pl.BlockSpec((2, MOBA_BLOCK, MOBA_BLOCK), lambda b, p, i: (p, 0, 0))
    return pl.pallas_call(
        functools.partial(_moba_kernel, seq=seq),
        grid=(bsz, N_PAIRS, qpt),
        in_specs=[pl.BlockSpec(memory_space=pltpu.SMEM), q_tile, kv_seq(koff), kv_seq(voff),
                  pair_table, pair_table],
        out_specs=q_tile,
        out_shape=jax.ShapeDtypeStruct((n, BRANCH_WIDTH), BF16),
        scratch_shapes=[pltpu.VMEM((LANES, LANES), F32)] + _flash_scratch(seq),
        compiler_params=_params("arbitrary", "arbitrary", "arbitrary"),
        name="moba",
    )(rel_bias, qkv, qkv, qkv, t_own, t_prev)


def _merge_kernel(x_ref, g_ref, ya_ref, yb_ref, yc_ref, yd_ref, wg_ref, bg_ref, wbr_ref, wout_ref, o_ref):
    x = x_ref[...]
    hb = _rmsnorm(x, g_ref[...]).astype(BF16)
    merged = jnp.zeros((TM, D_MODEL), F32)
    for n, y_ref in enumerate((ya_ref, yb_ref, yc_ref, yd_ref)):
        gz = jnp.dot(hb, wg_ref[:, n * D_MODEL:(n + 1) * D_MODEL], preferred_element_type=F32)
        gate = jax.nn.sigmoid(gz + bg_ref[n:n + 1, :])
        merged = merged + gate * jnp.dot(y_ref[...], wbr_ref[n], preferred_element_type=F32)
    o_ref[...] = x + jnp.dot(merged.astype(BF16), wout_ref[...], preferred_element_type=F32)


def _merge(x, g, ys, wg, bg, wbr, wout):
    n = x.shape[0]
    row = lambda w: pl.BlockSpec((TM, w), lambda i: (i, 0))
    return pl.pallas_call(
        _merge_kernel,
        grid=(n // TM,),
        in_specs=[row(D_MODEL), _resident(g.shape)] + [row(BRANCH_WIDTH)] * N_BRANCH
        + [_resident(wg.shape), _resident(bg.shape), _resident(wbr.shape), _resident(wout.shape)],
        out_specs=row(D_MODEL),
        out_shape=jax.ShapeDtypeStruct((n, D_MODEL), F32),
        compiler_params=_params("arbitrary"),
        name="merge",
    )(x, g, *ys, wg, bg, wbr, wout)


def _ffn_kernel(x_ref, g_ref, wup_ref, cw_ref, cb_ref, wdown_ref, fg_ref, o_ref, halo_ref, buf_ref, *, final):
    @pl.when(pl.program_id(1) == 0)
    def _():
        halo_ref[...] = jnp.zeros_like(halo_ref)

    x = x_ref[...]
    hb = _rmsnorm(x, g_ref[...]).astype(BF16)
    acc = jnp.zeros((TM, D_MODEL), F32)
    n_chunks = D_FF // FF_CHUNK
    col_of = lambda c, part: part * D_FF + c * FF_CHUNK
    up = lambda c: [jnp.dot(hb, wup_ref[:, col_of(c, part):col_of(c, part) + FF_CHUNK],
                            preferred_element_type=F32) for part in range(2)]
    ahead = up(0)
    for c in range(n_chunks):
        pre, ahead = ahead, (up(c + 1) if c + 1 < n_chunks else None)
        halves = []
        for part in range(2):
            col = col_of(c, part)
            buf_ref[c % 2, part,0:SUBLANES, :] = halo_ref[:, col:col + FF_CHUNK]
            buf_ref[c % 2, part,SUBLANES:SUBLANES + TM, :] = pre[part]
            halo_ref[:, col:col + FF_CHUNK] = buf_ref[c % 2, part,TM:TM + SUBLANES, :]
            u = cb_ref[:, col:col + FF_CHUNK]
            for k in range(FFN_CONV_WIDTH):
                lo = SUBLANES - (FFN_CONV_WIDTH - 1) + k
                u = u + cw_ref[k:k + 1, col:col + FF_CHUNK] * buf_ref[c % 2, part, lo:lo + TM, :]
            halves.append(u)
        a = (halves[0] * jax.nn.sigmoid(halves[0])) * halves[1]
        acc = acc + jnp.dot(a.astype(BF16), wdown_ref[c * FF_CHUNK:(c + 1) * FF_CHUNK, :],
                            preferred_element_type=F32)
    y = x + acc
    o_ref[...] = _rmsnorm(y, fg_ref[...]) if final else y


def _ffn(x, g, wup, cw, cb, wdown, fg, bsz, seq, final):
    n = x.shape[0]
    spt = seq // TM
    row = pl.BlockSpec((TM, D_MODEL), lambda bi, i: (bi * spt + i, 0))
    return pl.pallas_call(
        functools.partial(_ffn_kernel, final=final),
        grid=(bsz, spt),
        in_specs=[row, _resident(g.shape), _resident(wup.shape), _resident(cw.shape),
                  _resident(cb.shape), _resident(wdown.shape), _resident(fg.shape)],
        out_specs=row,
        out_shape=jax.ShapeDtypeStruct((n, D_MODEL), F32),
        scratch_shapes=[pltpu.VMEM((SUBLANES, 2 * D_FF), F32),
                        pltpu.VMEM((2, 2, TM + SUBLANES, FF_CHUNK), F32)],
        compiler_params=_params("arbitrary", "arbitrary"),
        name="ffn",
    )(x, g, wup, cw, cb, wdown, fg)


def _split_w_in(w_in):
    hd, bw = HEAD_DIM, BRANCH_WIDTH
    o = 0
    qa = w_in[:, o:o + bw]; o += bw
    ka = w_in[:, o:o + SWA_KV_HEADS * hd]; o += SWA_KV_HEADS * hd
    va = w_in[:, o:o + SWA_KV_HEADS * hd]; o += SWA_KV_HEADS * hd
    ub = w_in[:, o:o + 2 * bw]; o += 2 * bw
    qkv_c = w_in[:, o:o + 3 * bw]; o += 3 * bw
    fc = w_in[:, o:o + N_HEADS]; o += N_HEADS
    qkv_d = w_in[:, o:o + 3 * bw]; o += 3 * bw
    gates = w_in[:, o:o + N_BRANCH * D_MODEL]
    dup = lambda w: jnp.concatenate([w[:, i * hd:(i + 1) * hd] for i in (0, 0, 1, 1)], axis=1)
    wa = jnp.concatenate([qa, dup(ka), dup(va)], axis=1).astype(BF16)
    wf = jnp.pad(fc, ((0, 0), (0, LANES - N_HEADS)))
    return wa, ub.astype(BF16), qkv_c.astype(BF16), wf, qkv_d.astype(BF16), gates.astype(BF16)


def _layer(x, bsz, seq, biases, rel_bias, ln1_g, w_in, b_gate, b_fgate, sinks, conv_w, conv_b,
           conv_ln_g, conv_ln_b, w_br, w_out, ln2_g, w_up, ffn_conv_w, ffn_conv_b, w_down,
           final_g, final):
    bias_swa, table_own, table_prev = biases
    row = lambda v: v.reshape(1, -1)
    wa, wb, wc, wf, wd, wg = _split_w_in(w_in)
    za, zb, zc, zf, zd = _inproj(x, row(ln1_g), wa, wb, wc, wf, wd)
    ya = _swa(za, sinks, bias_swa, seq)
    yb = _conformer(zb, conv_w, row(conv_b), row(conv_ln_g), row(conv_ln_b), bsz, seq)
    kx, qx = _fgate(zf, jnp.pad(row(b_fgate), ((0, 0), (0, LANES - N_HEADS))), bsz, seq)
    yc = _fox(zc, kx, qx, bsz, seq)
    yd = _moba(zd, rel_bias, table_own, table_prev, bsz, seq)
    x = _merge(x, row(ln1_g), (ya, yb, yc, yd), wg, b_gate, w_br.astype(BF16), w_out.astype(BF16))
    return _ffn(x, row(ln2_g), w_up.astype(BF16), ffn_conv_w, row(ffn_conv_b), w_down.astype(BF16),
                row(final_g), bsz, seq, final)


def kernel(x, ln1_g, w_in, b_gate, b_fgate, sinks, conv_w, conv_b, conv_ln_g, conv_ln_b, w_br, w_out,
           ln2_g, w_up, ffn_conv_w, ffn_conv_b, w_down, rel_bias, final_g):
    bsz, seq, _ = x.shape
    depth = w_in.shape[0]
    assert seq % TM == 0 and seq % KV_ROWS == 0 and TQ == TK == MOBA_BLOCK
    assert MOBA_SEL_LANE + seq // MOBA_BLOCK <= HEAD_DIM
    biases = _expand_bias(rel_bias)
    y = x.reshape(bsz * seq, D_MODEL)
    for layer in range(depth):
        y = _layer(y, bsz, seq, biases, rel_bias, ln1_g[layer], w_in[layer], b_gate[layer],
                   b_fgate[layer], sinks[layer], conv_w[layer], conv_b[layer], conv_ln_g[layer],
                   conv_ln_b[layer], w_br[layer], w_out[layer], ln2_g[layer], w_up[layer],
                   ffn_conv_w[layer], ffn_conv_b[layer], w_down[layer], final_g,
                   final=layer == depth - 1)
    return y.reshape(bsz, seq, D_MODEL)
```

```python
import functools
import math

import jax
import jax.numpy as jnp
import numpy as np
from jax import lax
from jax.experimental import pallas as pl
from jax.experimental.pallas import tpu as pltpu

D_MODEL = 1024
HEAD_DIM = 64
N_BRANCH = 4
BRANCH_WIDTH = 512
N_HEADS = 8
N_PAIRS = N_HEADS // 2
SWA_KV_HEADS = 2
SWA_BLOCK = 128
CONV_WIDTH = 31
MOBA_BLOCK = 256
MOBA_TOPK = 3
D_FF = 2816
FFN_CONV_WIDTH = 3
REL_BUCKETS = 32
REL_MAX_DIST = 128
EPS = 1e-6
NEG = -1e30
SCALE = HEAD_DIM ** -0.5
LOG2E = math.log2(math.e)

LANES = 128
SUBLANES = 8
VMEM_LIMIT = 56 * 1024 * 1024

TM = 512
TQ = 256
TK = 256
CHUNK_TILES = 8
SCORE_LOOKAHEAD = 3
KV_ROWS = 512
CUM_T = 256
FF_CHUNK = 256
CONV_ROWS = 64
CONV_HALO = 32

N_PIECES = 3
PIECE_ROWS = SUBLANES * N_PIECES
MOBA_SEL_LANE = SUBLANES
STEP_HEADS = 2
STEP_LANES = STEP_HEADS * HEAD_DIM
V_ROWS = HEAD_DIM + 16

BF16 = jnp.bfloat16
F32 = jnp.float32


def _params(*sem):
    return pltpu.CompilerParams(dimension_semantics=sem, vmem_limit_bytes=VMEM_LIMIT)


def _resident(shape, layer=None):
    if layer is None:
        zeros = (0,) * len(shape)
        return pl.BlockSpec(shape, lambda *_: zeros, pipeline_mode=pl.Buffered(1))
    index = (layer,) + (0,) * (len(shape) - 1)
    return pl.BlockSpec((None,) + tuple(shape[1:]), lambda *_: index, pipeline_mode=pl.Buffered(1))


def _rmsnorm(x, g):
    return (x * lax.rsqrt(jnp.mean(x * x, axis=-1, keepdims=True) + EPS)) * g


def _pieces(x):
    hi = x.astype(BF16).astype(F32)
    rest = x - hi
    mid = rest.astype(BF16).astype(F32)
    return hi, mid, rest - mid


def _rel_bucket_np(dist):
    n = np.maximum(dist, 0)
    max_exact = REL_BUCKETS // 2
    nf = np.maximum(n, 1).astype(np.float64)
    large = max_exact + (np.log(nf / max_exact) / math.log(REL_MAX_DIST / max_exact)
                         * (REL_BUCKETS - max_exact) + 1e-9).astype(np.int32)
    large = np.minimum(large, REL_BUCKETS - 1)
    return np.where(n < max_exact, n, large).astype(np.int32)


def _bias_kernel(tab_ref, bs_ref, bo_ref, bp_ref, os_ref, oo_ref, op_ref):
    os_ref[...] = jnp.zeros_like(os_ref)
    oo_ref[...] = jnp.zeros_like(oo_ref)
    op_ref[...] = jnp.zeros_like(op_ref)
    bs, bo, bp = bs_ref[...], bo_ref[...], bp_ref[...]

    def body(b, carry):
        ms, mo, mp = bs == b, bo == b, bp == b
        for h in range(N_HEADS):
            os_ref[h] = jnp.where(ms, tab_ref[b, h], os_ref[h])
            oo_ref[h] = jnp.where(mo, tab_ref[b, N_HEADS + h], oo_ref[h])
            op_ref[h] = jnp.where(mp, tab_ref[b, N_HEADS + h], op_ref[h])
        return carry

    lax.fori_loop(0, REL_BUCKETS, body, 0)
    causal = (lax.broadcasted_iota(jnp.int32, (MOBA_BLOCK, MOBA_BLOCK), 0)
              <= lax.broadcasted_iota(jnp.int32, (MOBA_BLOCK, MOBA_BLOCK), 1))
    for h in range(N_HEADS):
        far = tab_ref[REL_BUCKETS - 1, N_HEADS + h]
        oo_ref[h] = jnp.where(causal, (oo_ref[h] - far) * LOG2E, NEG)
        op_ref[h] = (op_ref[h] - far) * LOG2E


def _expand_bias(rel_bias):
    qi = np.arange(SWA_BLOCK)[:, None]
    kj = np.arange(2 * SWA_BLOCK)[None, :]
    b_swa = _rel_bucket_np(qi + SWA_BLOCK - kj)
    qi = np.arange(MOBA_BLOCK)[:, None]
    kj = np.arange(MOBA_BLOCK)[None, :]
    b_own = np.ascontiguousarray(_rel_bucket_np(qi - kj).T)
    b_prev = np.ascontiguousarray(_rel_bucket_np(qi - kj + MOBA_BLOCK).T)
    vm = pl.BlockSpec(memory_space=pltpu.VMEM)
    return pl.pallas_call(
        _bias_kernel,
        out_shape=(jax.ShapeDtypeStruct((N_HEADS, SWA_BLOCK, 2 * SWA_BLOCK), F32),
                   jax.ShapeDtypeStruct((N_HEADS, MOBA_BLOCK, MOBA_BLOCK), F32),
                   jax.ShapeDtypeStruct((N_HEADS, MOBA_BLOCK, MOBA_BLOCK), F32)),
        in_specs=[pl.BlockSpec(memory_space=pltpu.SMEM), vm, vm, vm],
        out_specs=(vm, vm, vm),
        name="bias_expand",
    )(rel_bias, jnp.asarray(b_swa), jnp.asarray(b_own), jnp.asarray(b_prev))


def _inproj_kernel(x_ref, g_ref, wa_ref, wb_ref, wc_ref, wf_ref, wd_ref,
                   oa_ref, ob_ref, oc_ref, of_ref, od_ref):
    h = _rmsnorm(x_ref[...], g_ref[...])
    hb = h.astype(BF16)
    oa_ref[...] = jnp.dot(hb, wa_ref[...], preferred_element_type=F32).astype(BF16)
    ob_ref[...] = jnp.dot(hb, wb_ref[...], preferred_element_type=F32)
    oc_ref[...] = jnp.dot(hb, wc_ref[...], preferred_element_type=F32).astype(BF16)
    od_ref[...] = jnp.dot(hb, wd_ref[...], preferred_element_type=F32).astype(BF16)
    of_ref[...] = jnp.dot(hb, wf_ref[...], preferred_element_type=F32)


def _inproj(x, g, ws, layer):
    n = x.shape[0]
    row = lambda w: pl.BlockSpec((TM, w), lambda i: (i, 0))
    dts = (BF16, F32, BF16, F32, BF16)
    return pl.pallas_call(
        _inproj_kernel,
        grid=(n // TM,),
        in_specs=[row(D_MODEL), _resident(g.shape, layer)] + [_resident(w.shape, layer) for w in ws],
        out_specs=tuple(row(w.shape[-1]) for w in ws),
        out_shape=tuple(jax.ShapeDtypeStruct((n, w.shape[-1]), dt) for w, dt in zip(ws, dts)),
        compiler_params=_params("arbitrary"),
        name="inproj",
    )(x, g, *ws)


def _swa_kernel(sink_ref, q_ref, kc_ref, kp_ref, vc_ref, vp_ref, bias_ref, o_ref, *, blocks_per_seq,
                layer):
    blk = pl.program_id(0) % blocks_per_seq
    kk = jnp.concatenate([kp_ref[...], kc_ref[...]], axis=0)
    vv = jnp.concatenate([vp_ref[...], vc_ref[...]], axis=0)
    qi = lax.broadcasted_iota(jnp.int32, (SWA_BLOCK, 2 * SWA_BLOCK), 0)
    kj = lax.broadcasted_iota(jnp.int32, (SWA_BLOCK, 2 * SWA_BLOCK), 1)
    dist = qi + SWA_BLOCK - kj
    mask = (dist >= 0) & (dist < SWA_BLOCK) & ((blk > 0) | (kj >= SWA_BLOCK))
    lane = lax.broadcasted_iota(jnp.int32, (SWA_BLOCK, LANES), 1)
    upper = lane >= HEAD_DIM
    kv_of = lambda h: h // (N_HEADS // SWA_KV_HEADS)
    scores = []
    for h in range(N_HEADS):
        q_pair = q_ref[:, (h // 2) * LANES:(h // 2 + 1) * LANES] * SCALE
        q_h = jnp.where(upper == (h % 2 == 1), q_pair, jnp.zeros_like(q_pair))
        k_kv = kk[:, kv_of(h) * LANES:(kv_of(h) + 1) * LANES]
        scores.append(lax.dot_general(q_h, k_kv, (((1,), (1,)), ((), ())), preferred_element_type=F32))
    probs = []
    for h in range(N_HEADS):
        logits = jnp.where(mask, scores[h] + bias_ref[h], NEG)
        sink = sink_ref[layer, h]
        m = jnp.maximum(jnp.max(logits, axis=-1, keepdims=True), sink)
        p = jnp.exp(logits - m)
        denom = jnp.sum(p, axis=-1, keepdims=True) + jnp.exp(sink - m)
        probs.append((p / denom).astype(BF16))
    outs = [jnp.dot(probs[h], vv[:, kv_of(h) * LANES:(kv_of(h) + 1) * LANES], preferred_element_type=F32)
            for h in range(N_HEADS)]
    for pair in range(N_PAIRS):
        o_ref[:, pair * LANES:(pair + 1) * LANES] = jnp.where(
            upper, outs[2 * pair + 1], outs[2 * pair]).astype(BF16)


def _swa(qkv, sinks, bias, seq, layer):
    n = qkv.shape[0]
    bps = seq // SWA_BLOCK
    kcol, vcol = BRANCH_WIDTH // (2 * LANES), BRANCH_WIDTH // (2 * LANES) + 1
    prev = lambda i: jnp.where(i % bps == 0, i, i - 1)
    return pl.pallas_call(
        functools.partial(_swa_kernel, blocks_per_seq=bps, layer=layer),
        grid=(n // SWA_BLOCK,),
        in_specs=[pl.BlockSpec(memory_space=pltpu.SMEM),
                  pl.BlockSpec((SWA_BLOCK, BRANCH_WIDTH), lambda i: (i, 0)),
                  pl.BlockSpec((SWA_BLOCK, 2 * LANES), lambda i: (i, kcol)),
                  pl.BlockSpec((SWA_BLOCK, 2 * LANES), lambda i: (prev(i), kcol)),
                  pl.BlockSpec((SWA_BLOCK, 2 * LANES), lambda i: (i, vcol)),
                  pl.BlockSpec((SWA_BLOCK, 2 * LANES), lambda i: (prev(i), vcol)),
                  _resident(bias.shape)],
        out_specs=pl.BlockSpec((SWA_BLOCK, BRANCH_WIDTH), lambda i: (i, 0)),
        out_shape=jax.ShapeDtypeStruct((n, BRANCH_WIDTH), BF16),
        compiler_params=_params("arbitrary"),
        name="swa",
    )(sinks, qkv, qkv, qkv, qkv, qkv, bias)


def _conformer_kernel(u_ref, w_ref, b_ref, g_ref, beta_ref, o_ref, buf_ref):
    @pl.when(pl.program_id(1) == 0)
    def _():
        buf_ref[0:CONV_HALO, :] = jnp.zeros((CONV_HALO, BRANCH_WIDTH), F32)

    a = u_ref[:, 0:BRANCH_WIDTH]
    gt = u_ref[:, BRANCH_WIDTH:2 * BRANCH_WIDTH]
    buf_ref[CONV_HALO:CONV_HALO + TM, :] = a * jax.nn.sigmoid(gt)
    first = CONV_HALO - (CONV_WIDTH - 1)
    for r in range(TM // CONV_ROWS):
        acc = jnp.zeros((CONV_ROWS, BRANCH_WIDTH), F32)
        for k in range(CONV_WIDTH):
            lo = first + r * CONV_ROWS + k
            acc = acc + w_ref[k:k + 1, :] * buf_ref[lo:lo + CONV_ROWS, :]
        y = acc + b_ref[...]
        mu = jnp.mean(y, axis=-1, keepdims=True)
        yc = y - mu
        var = jnp.mean(yc * yc, axis=-1, keepdims=True)
        z = yc * lax.rsqrt(var + EPS) * g_ref[...] + beta_ref[...]
        o_ref[r * CONV_ROWS:(r + 1) * CONV_ROWS, :] = (z * jax.nn.sigmoid(z)).astype(BF16)
    buf_ref[0:CONV_HALO, :] = buf_ref[TM:TM + CONV_HALO, :]


def _conformer(u, w, b, g, beta, bsz, seq, layer):
    n = u.shape[0]
    spt = seq // TM
    return pl.pallas_call(
        _conformer_kernel,
        grid=(bsz, spt),
        in_specs=[pl.BlockSpec((TM, 2 * BRANCH_WIDTH), lambda bi, i: (bi * spt + i, 0)),
                  _resident(w.shape, layer), _resident(b.shape, layer), _resident(g.shape, layer),
                  _resident(beta.shape, layer)],
        out_specs=pl.BlockSpec((TM, BRANCH_WIDTH), lambda bi, i: (bi * spt + i, 0)),
        out_shape=jax.ShapeDtypeStruct((n, BRANCH_WIDTH), BF16),
        scratch_shapes=[pltpu.VMEM((TM + CONV_HALO, BRANCH_WIDTH), F32)],
        compiler_params=_params("arbitrary", "arbitrary"),
        name="conformer",
    )(u, w, b, g, beta)


def _extra_base(h):
    return (h // 2) * LANES + HEAD_DIM * (1 - h % 2)


def _fgate_placement():
    pk = np.zeros((LANES, BRANCH_WIDTH), np.float32)
    pq = np.zeros((LANES, BRANCH_WIDTH), np.float32)
    for h in range(N_HEADS):
        base = _extra_base(h)
        for i in range(N_PIECES):
            pk[SUBLANES * i + h, base + i] = 1.0
            pq[PIECE_ROWS, base + i] = -1.0
            pk[PIECE_ROWS, base + N_PIECES + i] = 1.0
            pq[SUBLANES * i + h, base + N_PIECES + i] = 1.0
    return jnp.asarray(pk, BF16), jnp.asarray(pq, BF16)


def _fgate_kernel(f_ref, b_ref, pk_ref, pq_ref, kx_ref, qx_ref, carry_ref):
    @pl.when(pl.program_id(1) == 0)
    def _():
        carry_ref[...] = jnp.zeros_like(carry_ref)

    z = f_ref[...] + b_ref[...]
    log_f = -(jnp.maximum(-z, 0.0) + jnp.log1p(jnp.exp(-jnp.abs(z))))
    ri = lax.broadcasted_iota(jnp.int32, (CUM_T, CUM_T), 0)
    ci = lax.broadcasted_iota(jnp.int32, (CUM_T, CUM_T), 1)
    tri = (ci <= ri).astype(F32)
    c = jnp.dot(tri, log_f, preferred_element_type=F32,
                precision=lax.Precision.HIGHEST) + carry_ref[0:1, :]
    carry_ref[...] = jnp.broadcast_to(c[CUM_T - 1:CUM_T, :], carry_ref.shape)
    hi, mid, lo = _pieces(c * LOG2E)
    lane = lax.broadcasted_iota(jnp.int32, (CUM_T, LANES), 1)
    packed = jnp.where(lane < SUBLANES, hi,
                       jnp.where(lane < 2 * SUBLANES, pltpu.roll(mid, SUBLANES, axis=1),
                                 jnp.where(lane < PIECE_ROWS, pltpu.roll(lo, 2 * SUBLANES, axis=1),
                                           jnp.where(lane == PIECE_ROWS, 1.0, 0.0)))).astype(BF16)
    kx_ref[...] = jnp.dot(packed, pk_ref[...], preferred_element_type=F32).astype(BF16)
    qx_ref[...] = jnp.dot(packed, pq_ref[...], preferred_element_type=F32).astype(BF16)


def _fgate(f, b, bsz, seq, layer):
    n = f.shape[0]
    spt = seq // CUM_T
    pk, pq = _fgate_placement()
    row = lambda w: pl.BlockSpec((CUM_T, w), lambda bi, i: (bi * spt + i, 0))
    return pl.pallas_call(
        _fgate_kernel,
        grid=(bsz, spt),
        in_specs=[row(LANES), _resident(b.shape, layer), _resident(pk.shape), _resident(pq.shape)],
        out_specs=(row(BRANCH_WIDTH), row(BRANCH_WIDTH)),
        out_shape=(jax.ShapeDtypeStruct((n, BRANCH_WIDTH), BF16),
                   jax.ShapeDtypeStruct((n, BRANCH_WIDTH), BF16)),
        scratch_shapes=[pltpu.VMEM((SUBLANES, LANES), F32)],
        compiler_params=_params("arbitrary", "arbitrary"),
        name="fgate_cumsum",
    )(f, b, pk, pq)


def _build_kv(k_ref, v_ref, extra_of, ka_ref, vt_ref, seq):
    lane = lax.broadcasted_iota(jnp.int32, (KV_ROWS, LANES), 1)
    lower = lane < HEAD_DIM
    ones_rows = jnp.where(lax.broadcasted_iota(jnp.int32, (V_ROWS - HEAD_DIM, KV_ROWS), 0) == 0, 1.0, 0.0)

    def body(i, carry):
        rows = pl.ds(pl.multiple_of(i * KV_ROWS, KV_ROWS), KV_ROWS)
        extra = extra_of(i, rows)
        for pair in range(STEP_HEADS // 2):
            lanes = slice(pair * LANES, (pair + 1) * LANES)
            k, x = k_ref[rows, lanes], extra[:, lanes]
            ka_ref[2 * pair, rows, :] = jnp.where(lower, k, x)
            ka_ref[2 * pair + 1, rows, :] = jnp.where(lower, x, k)
            v_t = v_ref[rows, lanes].astype(F32).T
            for hh in range(2):
                vt_ref[2 * pair + hh, :, rows] = jnp.concatenate(
                    [v_t[hh * HEAD_DIM:(hh + 1) * HEAD_DIM], ones_rows], axis=0).astype(BF16)
        return carry

    lax.fori_loop(0, seq // KV_ROWS, body, 0)


def _extend_q_t(q_t, extras_t):
    out = []
    for hh in range(STEP_HEADS):
        q_h = q_t[hh * HEAD_DIM:(hh + 1) * HEAD_DIM]
        parts = [q_h, extras_t[hh]] if hh % 2 == 0 else [extras_t[hh], q_h]
        out.append(jnp.concatenate(parts, axis=0).astype(BF16))
    return out


def _flash_step(qa_t, ka_ref, vt_ref, m_ref, acc_ref, start, n_tiles, tables=None, first=False):
    heads = range(STEP_HEADS)
    m = [None if first else m_ref[hh, 0:1, :] for hh in heads]
    acc = [None if first else acc_ref[hh] for hh in heads]
    n_plain = n_tiles - (len(tables(0)) if tables else 0)
    tile_rows = lambda t: pl.ds(pl.multiple_of(start + t * TK, TK), TK)
    score = lambda t: [jnp.dot(ka_ref[hh, tile_rows(t), :], qa_t[hh], preferred_element_type=F32)
                       for hh in heads]
    ahead = [score(t) for t in range(min(SCORE_LOOKAHEAD, n_tiles))]
    for t in range(n_tiles):
        rows = tile_rows(t)
        if t + SCORE_LOOKAHEAD < n_tiles:
            ahead.append(score(t + SCORE_LOOKAHEAD))
        scores = ahead.pop(0)
        for hh in heads:
            s = scores[hh]
            if t >= n_plain:
                s = s + tables(hh)[t - n_plain]
            col_max = jnp.max(s, axis=0, keepdims=True)
            m_new = col_max if m[hh] is None else jnp.maximum(m[hh], col_max)
            p = jnp.exp2(s - m_new).astype(BF16)
            pv = jnp.dot(vt_ref[hh, :, rows], p, preferred_element_type=F32)
            acc[hh] = pv if acc[hh] is None else jnp.exp2(m[hh] - m_new) * acc[hh] + pv
            m[hh] = m_new
    for hh in heads:
        m_ref[hh] = jnp.broadcast_to(m[hh], (SUBLANES, TQ))
        acc_ref[hh] = acc[hh]


def _pair_output(o_ref, acc_ref):
    out_t = jnp.concatenate(
        [acc_ref[hh, 0:HEAD_DIM, :] / acc_ref[hh, HEAD_DIM:HEAD_DIM + 1, :] for hh in range(STEP_HEADS)],
        axis=0)
    o_ref[...] = out_t.T.astype(BF16)


def _flash_scratch(seq):
    return [pltpu.VMEM((STEP_HEADS, seq, LANES), BF16), pltpu.VMEM((STEP_HEADS, V_ROWS, seq), BF16),
            pltpu.VMEM((STEP_HEADS, SUBLANES, TQ), F32), pltpu.VMEM((STEP_HEADS, V_ROWS, TQ), F32)]


def _flash_chunks(qa, ka_ref, vt_ref, m_ref, acc_ref, n_chunks):
    def chunk(c, carry):
        _flash_step(qa, ka_ref, vt_ref, m_ref, acc_ref, c * (CHUNK_TILES * TK), CHUNK_TILES)
        return carry

    lax.fori_loop(0, n_chunks, chunk, 0)


def _flash_tail(qa, ka_ref, vt_ref, m_ref, acc_ref, last_tile, n_loose, tail_tables, guard=None):
    rem = n_loose % CHUNK_TILES
    for r in range(CHUNK_TILES):
        if isinstance(rem, int) and rem != r:
            continue
        tiles = r + len(tail_tables(0))
        cond = guard if isinstance(rem, int) else (rem == r if guard is None else (rem == r) & guard)

        @pl.when(cond)
        def _(tiles=tiles):
            _flash_step(qa, ka_ref, vt_ref, m_ref, acc_ref, (last_tile + 1 - tiles) * TK, tiles,
                        tables=tail_tables, first=True)


def _fox_kernel(q_ref, qx_ref, k_ref, v_ref, kx_ref, o_ref, ka_ref, vt_ref, m_ref, acc_ref, *, seq):
    qt = pl.program_id(2)

    @pl.when(qt == 0)
    def _():
        _build_kv(k_ref, v_ref, lambda i, rows: kx_ref[rows, :], ka_ref, vt_ref, seq)

    q_t = (q_ref[...].astype(F32) * (SCALE * LOG2E)).T
    qx_t = qx_ref[...].astype(F32).T
    qa = _extend_q_t(q_t, [qx_t[(hh ^ 1) * HEAD_DIM:((hh ^ 1) + 1) * HEAD_DIM] for hh in range(STEP_HEADS)])
    causal = jnp.where(lax.broadcasted_iota(jnp.int32, (TK, TQ), 0)
                       <= lax.broadcasted_iota(jnp.int32, (TK, TQ), 1), 0.0, NEG)
    _flash_tail(qa, ka_ref, vt_ref, m_ref, acc_ref, qt, qt, lambda hh: [causal])
    _flash_chunks(qa, ka_ref, vt_ref, m_ref, acc_ref, qt // CHUNK_TILES)
    _pair_output(o_ref, acc_ref)


def _fox(qkv, kx, qx, bsz, seq):
    n = qkv.shape[0]
    qpt = seq // TQ
    koff, voff = BRANCH_WIDTH // STEP_LANES, 2 * BRANCH_WIDTH // STEP_LANES
    q_tile = lambda off: pl.BlockSpec((TQ, STEP_LANES), lambda b, p, i: (b * qpt + i, off + p))
    kv_seq = lambda off: pl.BlockSpec((seq, STEP_LANES), lambda b, p, i: (b, off + p),
                                      pipeline_mode=pl.Buffered(1))
    return pl.pallas_call(
        functools.partial(_fox_kernel, seq=seq),
        grid=(bsz, N_HEADS // STEP_HEADS, qpt),
        in_specs=[q_tile(0), q_tile(0), kv_seq(koff), kv_seq(voff), kv_seq(0)],
        out_specs=q_tile(0),
        out_shape=jax.ShapeDtypeStruct((n, BRANCH_WIDTH), BF16),
        scratch_shapes=_flash_scratch(seq),
        compiler_params=_params("arbitrary", "arbitrary", "arbitrary"),
        name="fox",
    )(qkv, qx, qkv, qkv, kx)


def _moba_kernel(tab_ref, q_ref, k_ref, v_ref, town_ref, tprev_ref, o_ref,
                 kmean_ref, ka_ref, vt_ref, m_ref, acc_ref, *, seq):
    group = pl.program_id(1)
    qt = pl.program_id(2)
    nblk = seq // MOBA_BLOCK
    blk_shift = MOBA_BLOCK.bit_length() - 1

    @pl.when(qt == 0)
    def _():
        blk = lax.broadcasted_iota(jnp.int32, (LANES, seq), 0)
        pos = lax.broadcasted_iota(jnp.int32, (LANES, seq), 1)
        member = jnp.where(jnp.right_shift(pos, blk_shift) == blk, 1.0 / MOBA_BLOCK, 0.0).astype(BF16)
        kmean_ref[...] = jnp.dot(member, k_ref[...], preferred_element_type=F32)

        def extra_of(i, rows):
            xl = jnp.bitwise_and(lax.broadcasted_iota(jnp.int32, (KV_ROWS, STEP_LANES), 1), HEAD_DIM - 1)
            row_blk = jnp.right_shift(
                i * KV_ROWS + lax.broadcasted_iota(jnp.int32, (KV_ROWS, STEP_LANES), 0), blk_shift)
            hit = (xl < N_PIECES) | (xl - MOBA_SEL_LANE == row_blk)
            return jnp.where(hit, 1.0, 0.0).astype(BF16)

        _build_kv(k_ref, v_ref, extra_of, ka_ref, vt_ref, seq)

    q_t = q_ref[...].astype(F32).T
    row_head = jnp.right_shift(lax.broadcasted_iota(jnp.int32, (STEP_LANES, TQ), 0),
                               HEAD_DIM.bit_length() - 1)
    blk = lax.broadcasted_iota(jnp.int32, (nblk, TQ), 0)
    piece_row = lax.broadcasted_iota(jnp.int32, (MOBA_SEL_LANE, TQ), 0)
    past = blk < qt
    extras = []
    for hh in range(STEP_HEADS):
        q_head = jnp.where(row_head == hh, q_t, 0.0)
        gate = jnp.dot(kmean_ref[0:nblk, :], q_head, preferred_element_type=F32,
                       precision=lax.Precision.HIGHEST)
        gate = jnp.where(past, gate, NEG)
        sel = blk == qt
        for _ in range(MOBA_TOPK):
            top = jnp.max(gate, axis=0, keepdims=True)
            idx = jnp.min(jnp.where(gate == top, blk, nblk), axis=0, keepdims=True)
            pick = blk == idx
            sel = sel | (pick & past)
            gate = jnp.where(pick, -jnp.inf, gate)
        far = jnp.full((MOBA_SEL_LANE, TQ), tab_ref[REL_BUCKETS - 1, N_HEADS + STEP_HEADS * group + hh] * LOG2E, F32)
        hi, mid, lo = _pieces(far)
        far_rows = jnp.where(piece_row == 0, hi, jnp.where(piece_row == 1, mid, jnp.where(piece_row == 2, lo, 0.0)))
        extras.append(jnp.concatenate(
            [far_rows, jnp.where(sel, 0.0, NEG),
             jnp.zeros((HEAD_DIM - MOBA_SEL_LANE - nblk, TQ), F32)], axis=0))

    qa = _extend_q_t(q_t * (SCALE * LOG2E), extras)
    n_far = jnp.maximum(qt - 1, 0)
    _flash_tail(qa, ka_ref, vt_ref, m_ref, acc_ref, qt, 0, lambda hh: [town_ref[hh]], guard=qt == 0)
    _flash_tail(qa, ka_ref, vt_ref, m_ref, acc_ref, qt, n_far,
                lambda hh: [tprev_ref[hh], town_ref[hh]], guard=qt >= 1)
    _flash_chunks(qa, ka_ref, vt_ref, m_ref, acc_ref, n_far // CHUNK_TILES)
    _pair_output(o_ref, acc_ref)


def _moba(qkv, rel_bias, t_own, t_prev, bsz, seq):
    n = qkv.shape[0]
    qpt = seq // TQ
    koff, voff = BRANCH_WIDTH // STEP_LANES, 2 * BRANCH_WIDTH // STEP_LANES
    q_tile = pl.BlockSpec((TQ, STEP_LANES), lambda b, p, i: (b * qpt + i, p))
    kv_seq = lambda off: pl.BlockSpec((seq, STEP_LANES), lambda b, p, i: (b, off + p),
                                      pipeline_mode=pl.Buffered(1))
    head_tables = pl.BlockSpec((STEP_HEADS, MOBA_BLOCK, MOBA_BLOCK), lambda b, p, i: (p, 0, 0))
    return pl.pallas_call(
        functools.partial(_moba_kernel, seq=seq),
        grid=(bsz, N_HEADS // STEP_HEADS, qpt),
        in_specs=[pl.BlockSpec(memory_space=pltpu.SMEM), q_tile, kv_seq(koff), kv_seq(voff),
                  head_tables, head_tables],
        out_specs=q_tile,
        out_shape=jax.ShapeDtypeStruct((n, BRANCH_WIDTH), BF16),
        scratch_shapes=[pltpu.VMEM((LANES, STEP_LANES), F32)] + _flash_scratch(seq),
        compiler_params=_params("arbitrary", "arbitrary", "arbitrary"),
        name="moba",
    )(rel_bias, qkv, qkv, qkv, t_own, t_prev)


def _merge_kernel(x_ref, g_ref, ya_ref, yb_ref, yc_ref, yd_ref, wg_ref, bg_ref, wbr_ref, wout_ref, o_ref):
    x = x_ref[...]
    hb = _rmsnorm(x, g_ref[...]).astype(BF16)
    merged = jnp.zeros((TM, D_MODEL), F32)
    for n, y_ref in enumerate((ya_ref, yb_ref, yc_ref, yd_ref)):
        gz = jnp.dot(hb, wg_ref[:, n * D_MODEL:(n + 1) * D_MODEL], preferred_element_type=F32)
        gate = jax.nn.sigmoid(gz + bg_ref[n:n + 1, :])
        merged = merged + gate * jnp.dot(y_ref[...], wbr_ref[n], preferred_element_type=F32)
    o_ref[...] = x + jnp.dot(merged.astype(BF16), wout_ref[...], preferred_element_type=F32)


def _merge(x, g, ys, wg, bg, wbr, wout, layer):
    n = x.shape[0]
    row = lambda w: pl.BlockSpec((TM, w), lambda i: (i, 0))
    return pl.pallas_call(
        _merge_kernel,
        grid=(n // TM,),
        in_specs=[row(D_MODEL), _resident(g.shape, layer)] + [row(BRANCH_WIDTH)] * N_BRANCH
        + [_resident(w.shape, layer) for w in (wg, bg, wbr, wout)],
        out_specs=row(D_MODEL),
        out_shape=jax.ShapeDtypeStruct((n, D_MODEL), F32),
        compiler_params=_params("arbitrary"),
        name="merge",
    )(x, g, *ys, wg, bg, wbr, wout)


def _ffn_kernel(x_ref, g_ref, wup_ref, cw_ref, cb_ref, wdown_ref, fg_ref, o_ref, halo_ref, buf_ref, *, final):
    @pl.when(pl.program_id(1) == 0)
    def _():
        halo_ref[...] = jnp.zeros_like(halo_ref)

    x = x_ref[...]
    hb = _rmsnorm(x, g_ref[...]).astype(BF16)
    acc = jnp.zeros((TM, D_MODEL), F32)
    n_chunks = D_FF // FF_CHUNK
    col_of = lambda c, part: part * D_FF + c * FF_CHUNK
    up = lambda c: [jnp.dot(hb, wup_ref[:, col_of(c, part):col_of(c, part) + FF_CHUNK],
                            preferred_element_type=F32) for part in range(2)]
    ahead = up(0)
    for c in range(n_chunks):
        pre, ahead = ahead, (up(c + 1) if c + 1 < n_chunks else None)
        halves = []
        for part in range(2):
            col = col_of(c, part)
            buf_ref[c % 2, part,0:SUBLANES, :] = halo_ref[:, col:col + FF_CHUNK]
            buf_ref[c % 2, part,SUBLANES:SUBLANES + TM, :] = pre[part]
            halo_ref[:, col:col + FF_CHUNK] = buf_ref[c % 2, part,TM:TM + SUBLANES, :]
            u = cb_ref[:, col:col + FF_CHUNK]
            for k in range(FFN_CONV_WIDTH):
                lo = SUBLANES - (FFN_CONV_WIDTH - 1) + k
                u = u + cw_ref[k:k + 1, col:col + FF_CHUNK] * buf_ref[c % 2, part, lo:lo + TM, :]
            halves.append(u)
        a = (halves[0] * jax.nn.sigmoid(halves[0])) * halves[1]
        acc = acc + jnp.dot(a.astype(BF16), wdown_ref[c * FF_CHUNK:(c + 1) * FF_CHUNK, :],
                            preferred_element_type=F32)
    y = x + acc
    o_ref[...] = _rmsnorm(y, fg_ref[...]) if final else y


def _ffn(x, g, wup, cw, cb, wdown, fg, bsz, seq, layer, final):
    n = x.shape[0]
    spt = seq // TM
    row = pl.BlockSpec((TM, D_MODEL), lambda bi, i: (bi * spt + i, 0))
    return pl.pallas_call(
        functools.partial(_ffn_kernel, final=final),
        grid=(bsz, spt),
        in_specs=[row] + [_resident(w.shape, layer) for w in (g, wup, cw, cb, wdown)]
        + [_resident(fg.shape)],
        out_specs=row,
        out_shape=jax.ShapeDtypeStruct((n, D_MODEL), F32),
        scratch_shapes=[pltpu.VMEM((SUBLANES, 2 * D_FF), F32),
                        pltpu.VMEM((2, 2, TM + SUBLANES, FF_CHUNK), F32)],
        compiler_params=_params("arbitrary", "arbitrary"),
        name="ffn",
    )(x, g, wup, cw, cb, wdown, fg)


def _split_w_in(w_in):
    hd, bw = HEAD_DIM, BRANCH_WIDTH
    o = 0
    qa = w_in[..., o:o + bw]; o += bw
    ka = w_in[..., o:o + SWA_KV_HEADS * hd]; o += SWA_KV_HEADS * hd
    va = w_in[..., o:o + SWA_KV_HEADS * hd]; o += SWA_KV_HEADS * hd
    ub = w_in[..., o:o + 2 * bw]; o += 2 * bw
    qkv_c = w_in[..., o:o + 3 * bw]; o += 3 * bw
    fc = w_in[..., o:o + N_HEADS]; o += N_HEADS
    qkv_d = w_in[..., o:o + 3 * bw]; o += 3 * bw
    gates = w_in[..., o:o + N_BRANCH * D_MODEL]
    dup = lambda w: jnp.concatenate([w[..., i * hd:(i + 1) * hd] for i in (0, 0, 1, 1)], axis=-1)
    wa = jnp.concatenate([qa, dup(ka), dup(va)], axis=-1)
    wf = jnp.pad(fc, ((0, 0), (0, 0), (0, LANES - N_HEADS)))
    return tuple(w.astype(BF16) for w in (wa, ub, qkv_c, wf, qkv_d)), gates.astype(BF16)


def _layer(x, bsz, seq, layer, final, biases, rel_bias, p):
    bias_swa, table_own, table_prev = biases
    za, zb, zc, zf, zd = _inproj(x, p["ln1_g"], p["w_in_groups"], layer)
    ya = _swa(za, p["sinks"], bias_swa, seq, layer)
    yb = _conformer(zb, p["conv_w"], p["conv_b"], p["conv_ln_g"], p["conv_ln_b"], bsz, seq, layer)
    kx, qx = _fgate(zf, p["b_fgate"], bsz, seq, layer)
    yc = _fox(zc, kx, qx, bsz, seq)
    yd = _moba(zd, rel_bias, table_own, table_prev, bsz, seq)
    x = _merge(x, p["ln1_g"], (ya, yb, yc, yd), p["w_gates"], p["b_gate"], p["w_br"], p["w_out"], layer)
    return _ffn(x, p["ln2_g"], p["w_up"], p["ffn_conv_w"], p["ffn_conv_b"], p["w_down"], p["final_g"],
                bsz, seq, layer, final)


def kernel(x, ln1_g, w_in, b_gate, b_fgate, sinks, conv_w, conv_b, conv_ln_g, conv_ln_b, w_br, w_out,
           ln2_g, w_up, ffn_conv_w, ffn_conv_b, w_down, rel_bias, final_g):
    bsz, seq, _ = x.shape
    depth = w_in.shape[0]
    assert seq % TM == 0 and seq % KV_ROWS == 0 and TQ == TK == MOBA_BLOCK
    assert MOBA_SEL_LANE + seq // MOBA_BLOCK <= HEAD_DIM
    biases = _expand_bias(rel_bias)
    rows = lambda v: v.reshape(depth, 1, -1)
    w_in_groups, w_gates = _split_w_in(w_in)
    params = dict(
        ln1_g=rows(ln1_g), w_in_groups=w_in_groups, w_gates=w_gates, b_gate=b_gate,
        b_fgate=rows(jnp.pad(b_fgate, ((0, 0), (0, LANES - N_HEADS)))), sinks=sinks,
        conv_w=conv_w, conv_b=rows(conv_b), conv_ln_g=rows(conv_ln_g), conv_ln_b=rows(conv_ln_b),
        w_br=w_br.astype(BF16), w_out=w_out.astype(BF16), ln2_g=rows(ln2_g), w_up=w_up.astype(BF16),
        ffn_conv_w=ffn_conv_w, ffn_conv_b=rows(ffn_conv_b), w_down=w_down.astype(BF16),
        final_g=final_g.reshape(1, -1))
    y = x.reshape(bsz * seq, D_MODEL)
    for layer in range(depth):
        y = _layer(y, bsz, seq, layer, layer == depth - 1, biases, rel_bias, params)
    return y.reshape(bsz, seq, D_MODEL)
```

```python
import functools
import math

import jax
import jax.numpy as jnp
import numpy as np
from jax import lax
from jax.experimental import pallas as pl
from jax.experimental.pallas import tpu as pltpu

D_MODEL = 1024
HEAD_DIM = 64
N_BRANCH = 4
BRANCH_WIDTH = 512
N_HEADS = 8
N_PAIRS = N_HEADS // 2
SWA_KV_HEADS = 2
SWA_BLOCK = 128
CONV_WIDTH = 31
MOBA_BLOCK = 256
MOBA_TOPK = 3
D_FF = 2816
FFN_CONV_WIDTH = 3
REL_BUCKETS = 32
REL_MAX_DIST = 128
EPS = 1e-6
NEG = -1e30
SCALE = HEAD_DIM ** -0.5
LOG2E = math.log2(math.e)

LANES = 128
SUBLANES = 8
VMEM_LIMIT = 56 * 1024 * 1024

TM = 512
TQ = 256
TK = 256
CHUNK_TILES = 8
SCORE_LOOKAHEAD = 3
KV_ROWS = 512
CUM_T = 256
FF_CHUNK = 256
CONV_ROWS = 64
CONV_HALO = 32

N_PIECES = 3
PIECE_ROWS = SUBLANES * N_PIECES
MOBA_SEL_LANE = SUBLANES
STEP_HEADS = 2
STEP_LANES = STEP_HEADS * HEAD_DIM
V_ROWS = HEAD_DIM + 16

BF16 = jnp.bfloat16
F32 = jnp.float32


def _params(*sem):
    return pltpu.CompilerParams(dimension_semantics=sem, vmem_limit_bytes=VMEM_LIMIT)


def _resident(shape, layer=None):
    if layer is None:
        zeros = (0,) * len(shape)
        return pl.BlockSpec(shape, lambda *_: zeros, pipeline_mode=pl.Buffered(1))
    index = (layer,) + (0,) * (len(shape) - 1)
    return pl.BlockSpec((None,) + tuple(shape[1:]), lambda *_: index, pipeline_mode=pl.Buffered(1))


def _rmsnorm(x, g):
    return (x * lax.rsqrt(jnp.mean(x * x, axis=-1, keepdims=True) + EPS)) * g


def _pieces(x):
    hi = x.astype(BF16).astype(F32)
    rest = x - hi
    mid = rest.astype(BF16).astype(F32)
    return hi, mid, rest - mid


def _rel_bucket_np(dist):
    n = np.maximum(dist, 0)
    max_exact = REL_BUCKETS // 2
    nf = np.maximum(n, 1).astype(np.float64)
    large = max_exact + (np.log(nf / max_exact) / math.log(REL_MAX_DIST / max_exact)
                         * (REL_BUCKETS - max_exact) + 1e-9).astype(np.int32)
    large = np.minimum(large, REL_BUCKETS - 1)
    return np.where(n < max_exact, n, large).astype(np.int32)


def _bias_kernel(tab_ref, bs_ref, bo_ref, bp_ref, os_ref, oo_ref, op_ref):
    os_ref[...] = jnp.zeros_like(os_ref)
    oo_ref[...] = jnp.zeros_like(oo_ref)
    op_ref[...] = jnp.zeros_like(op_ref)
    bs, bo, bp = bs_ref[...], bo_ref[...], bp_ref[...]

    def body(b, carry):
        ms, mo, mp = bs == b, bo == b, bp == b
        for h in range(N_HEADS):
            os_ref[h] = jnp.where(ms, tab_ref[b, h], os_ref[h])
            oo_ref[h] = jnp.where(mo, tab_ref[b, N_HEADS + h], oo_ref[h])
            op_ref[h] = jnp.where(mp, tab_ref[b, N_HEADS + h], op_ref[h])
        return carry

    lax.fori_loop(0, REL_BUCKETS, body, 0)
    causal = (lax.broadcasted_iota(jnp.int32, (MOBA_BLOCK, MOBA_BLOCK), 0)
              <= lax.broadcasted_iota(jnp.int32, (MOBA_BLOCK, MOBA_BLOCK), 1))
    for h in range(N_HEADS):
        far = tab_ref[REL_BUCKETS - 1, N_HEADS + h]
        oo_ref[h] = jnp.where(causal, (oo_ref[h] - far) * LOG2E, NEG)
        op_ref[h] = (op_ref[h] - far) * LOG2E


def _expand_bias(rel_bias):
    qi = np.arange(SWA_BLOCK)[:, None]
    kj = np.arange(2 * SWA_BLOCK)[None, :]
    b_swa = _rel_bucket_np(qi + SWA_BLOCK - kj)
    qi = np.arange(MOBA_BLOCK)[:, None]
    kj = np.arange(MOBA_BLOCK)[None, :]
    b_own = np.ascontiguousarray(_rel_bucket_np(qi - kj).T)
    b_prev = np.ascontiguousarray(_rel_bucket_np(qi - kj + MOBA_BLOCK).T)
    vm = pl.BlockSpec(memory_space=pltpu.VMEM)
    return pl.pallas_call(
        _bias_kernel,
        out_shape=(jax.ShapeDtypeStruct((N_HEADS, SWA_BLOCK, 2 * SWA_BLOCK), F32),
                   jax.ShapeDtypeStruct((N_HEADS, MOBA_BLOCK, MOBA_BLOCK), F32),
                   jax.ShapeDtypeStruct((N_HEADS, MOBA_BLOCK, MOBA_BLOCK), F32)),
        in_specs=[pl.BlockSpec(memory_space=pltpu.SMEM), vm, vm, vm],
        out_specs=(vm, vm, vm),
        name="bias_expand",
    )(rel_bias, jnp.asarray(b_swa), jnp.asarray(b_own), jnp.asarray(b_prev))


def _inproj_kernel(x_ref, g_ref, wa_ref, wb_ref, wc_ref, wf_ref, wd_ref,
                   oa_ref, ob_ref, oc_ref, of_ref, od_ref):
    h = _rmsnorm(x_ref[...], g_ref[...])
    hb = h.astype(BF16)
    oa_ref[...] = jnp.dot(hb, wa_ref[...], preferred_element_type=F32).astype(BF16)
    ob_ref[...] = jnp.dot(hb, wb_ref[...], preferred_element_type=F32)
    oc_ref[...] = jnp.dot(hb, wc_ref[...], preferred_element_type=F32).astype(BF16)
    od_ref[...] = jnp.dot(hb, wd_ref[...], preferred_element_type=F32).astype(BF16)
    of_ref[...] = jnp.dot(hb, wf_ref[...], preferred_element_type=F32)


def _inproj(x, g, ws, layer):
    n = x.shape[0]
    row = lambda w: pl.BlockSpec((TM, w), lambda i: (i, 0))
    dts = (BF16, F32, BF16, F32, BF16)
    return pl.pallas_call(
        _inproj_kernel,
        grid=(n // TM,),
        in_specs=[row(D_MODEL), _resident(g.shape, layer)] + [_resident(w.shape, layer) for w in ws],
        out_specs=tuple(row(w.shape[-1]) for w in ws),
        out_shape=tuple(jax.ShapeDtypeStruct((n, w.shape[-1]), dt) for w, dt in zip(ws, dts)),
        compiler_params=_params("arbitrary"),
        name="inproj",
    )(x, g, *ws)


def _swa_kernel(sink_ref, q_ref, kc_ref, kp_ref, vc_ref, vp_ref, bias_ref, o_ref, *, blocks_per_seq,
                layer):
    blk = pl.program_id(0) % blocks_per_seq
    kk = jnp.concatenate([kp_ref[...], kc_ref[...]], axis=0)
    vv = jnp.concatenate([vp_ref[...], vc_ref[...]], axis=0)
    qi = lax.broadcasted_iota(jnp.int32, (SWA_BLOCK, 2 * SWA_BLOCK), 0)
    kj = lax.broadcasted_iota(jnp.int32, (SWA_BLOCK, 2 * SWA_BLOCK), 1)
    dist = qi + SWA_BLOCK - kj
    mask = (dist >= 0) & (dist < SWA_BLOCK) & ((blk > 0) | (kj >= SWA_BLOCK))
    lane = lax.broadcasted_iota(jnp.int32, (SWA_BLOCK, LANES), 1)
    upper = lane >= HEAD_DIM
    kv_of = lambda h: h // (N_HEADS // SWA_KV_HEADS)
    scores = []
    for h in range(N_HEADS):
        q_pair = q_ref[:, (h // 2) * LANES:(h // 2 + 1) * LANES] * SCALE
        q_h = jnp.where(upper == (h % 2 == 1), q_pair, jnp.zeros_like(q_pair))
        k_kv = kk[:, kv_of(h) * LANES:(kv_of(h) + 1) * LANES]
        scores.append(lax.dot_general(q_h, k_kv, (((1,), (1,)), ((), ())), preferred_element_type=F32))
    probs = []
    for h in range(N_HEADS):
        logits = jnp.where(mask, scores[h] + bias_ref[h], NEG)
        sink = sink_ref[layer, h]
        m = jnp.maximum(jnp.max(logits, axis=-1, keepdims=True), sink)
        p = jnp.exp(logits - m)
        denom = jnp.sum(p, axis=-1, keepdims=True) + jnp.exp(sink - m)
        probs.append((p / denom).astype(BF16))
    outs = [jnp.dot(probs[h], vv[:, kv_of(h) * LANES:(kv_of(h) + 1) * LANES], preferred_element_type=F32)
            for h in range(N_HEADS)]
    for pair in range(N_PAIRS):
        o_ref[:, pair * LANES:(pair + 1) * LANES] = jnp.where(
            upper, outs[2 * pair + 1], outs[2 * pair]).astype(BF16)


def _swa(qkv, sinks, bias, seq, layer):
    n = qkv.shape[0]
    bps = seq // SWA_BLOCK
    kcol, vcol = BRANCH_WIDTH // (2 * LANES), BRANCH_WIDTH // (2 * LANES) + 1
    prev = lambda i: jnp.where(i % bps == 0, i, i - 1)
    return pl.pallas_call(
        functools.partial(_swa_kernel, blocks_per_seq=bps, layer=layer),
        grid=(n // SWA_BLOCK,),
        in_specs=[pl.BlockSpec(memory_space=pltpu.SMEM),
                  pl.BlockSpec((SWA_BLOCK, BRANCH_WIDTH), lambda i: (i, 0)),
                  pl.BlockSpec((SWA_BLOCK, 2 * LANES), lambda i: (i, kcol)),
                  pl.BlockSpec((SWA_BLOCK, 2 * LANES), lambda i: (prev(i), kcol)),
                  pl.BlockSpec((SWA_BLOCK, 2 * LANES), lambda i: (i, vcol)),
                  pl.BlockSpec((SWA_BLOCK, 2 * LANES), lambda i: (prev(i), vcol)),
                  _resident(bias.shape)],
        out_specs=pl.BlockSpec((SWA_BLOCK, BRANCH_WIDTH), lambda i: (i, 0)),
        out_shape=jax.ShapeDtypeStruct((n, BRANCH_WIDTH), BF16),
        compiler_params=_params("arbitrary"),
        name="swa",
    )(sinks, qkv, qkv, qkv, qkv, qkv, bias)


def _conformer_kernel(u_ref, w_ref, b_ref, g_ref, beta_ref, o_ref, buf_ref, y_ref):
    @pl.when(pl.program_id(1) == 0)
    def _():
        buf_ref[0:CONV_HALO, :] = jnp.zeros((CONV_HALO, BRANCH_WIDTH), F32)

    a = u_ref[:, 0:BRANCH_WIDTH]
    gt = u_ref[:, BRANCH_WIDTH:2 * BRANCH_WIDTH]
    buf_ref[CONV_HALO:CONV_HALO + TM, :] = a * jax.nn.sigmoid(gt)
    n_shifts = SUBLANES
    for r in range(TM // CONV_ROWS):
        row0 = r * CONV_ROWS
        for cg in range(BRANCH_WIDTH // LANES):
            cols = slice(cg * LANES, (cg + 1) * LANES)
            slabs = []
            for step in range((CONV_WIDTH - 1) // n_shifts + 1):
                lo = CONV_HALO - n_shifts * (step + 1) + row0
                slabs.append(buf_ref[lo:lo + CONV_ROWS + n_shifts, cols])
            y = None
            for shift in range(n_shifts):
                z = None
                for step in range((CONV_WIDTH - 1 - shift) // n_shifts + 1):
                    k = CONV_WIDTH - 1 - (n_shifts * step + shift)
                    term = w_ref[k:k + 1, cols] * slabs[step]
                    z = term if z is None else z + term
                shifted = z[n_shifts - shift:n_shifts - shift + CONV_ROWS]
                y = shifted if y is None else y + shifted
            y_ref[row0:row0 + CONV_ROWS, cols] = y
        y = y_ref[row0:row0 + CONV_ROWS, :] + b_ref[...]
        mu = jnp.mean(y, axis=-1, keepdims=True)
        yc = y - mu
        var = jnp.mean(yc * yc, axis=-1, keepdims=True)
        z = yc * lax.rsqrt(var + EPS) * g_ref[...] + beta_ref[...]
        o_ref[r * CONV_ROWS:(r + 1) * CONV_ROWS, :] = (z * jax.nn.sigmoid(z)).astype(BF16)
    buf_ref[0:CONV_HALO, :] = buf_ref[TM:TM + CONV_HALO, :]


def _conformer(u, w, b, g, beta, bsz, seq, layer):
    n = u.shape[0]
    spt = seq // TM
    return pl.pallas_call(
        _conformer_kernel,
        grid=(bsz, spt),
        in_specs=[pl.BlockSpec((TM, 2 * BRANCH_WIDTH), lambda bi, i: (bi * spt + i, 0)),
                  _resident(w.shape, layer), _resident(b.shape, layer), _resident(g.shape, layer),
                  _resident(beta.shape, layer)],
        out_specs=pl.BlockSpec((TM, BRANCH_WIDTH), lambda bi, i: (bi * spt + i, 0)),
        out_shape=jax.ShapeDtypeStruct((n, BRANCH_WIDTH), BF16),
        scratch_shapes=[pltpu.VMEM((TM + CONV_HALO, BRANCH_WIDTH), F32),
                        pltpu.VMEM((TM, BRANCH_WIDTH), F32)],
        compiler_params=_params("arbitrary", "arbitrary"),
        name="conformer",
    )(u, w, b, g, beta)


def _extra_base(h):
    return (h // 2) * LANES + HEAD_DIM * (1 - h % 2)


def _fgate_placement():
    pk = np.zeros((LANES, BRANCH_WIDTH), np.float32)
    pq = np.zeros((LANES, BRANCH_WIDTH), np.float32)
    for h in range(N_HEADS):
        base = _extra_base(h)
        for i in range(N_PIECES):
            pk[SUBLANES * i + h, base + i] = 1.0
            pq[PIECE_ROWS, base + i] = -1.0
            pk[PIECE_ROWS, base + N_PIECES + i] = 1.0
            pq[SUBLANES * i + h, base + N_PIECES + i] = 1.0
    return jnp.asarray(pk, BF16), jnp.asarray(pq, BF16)


def _fgate_kernel(f_ref, b_ref, pk_ref, pq_ref, kx_ref, qx_ref, carry_ref):
    @pl.when(pl.program_id(1) == 0)
    def _():
        carry_ref[...] = jnp.zeros_like(carry_ref)

    z = f_ref[...] + b_ref[...]
    log_f = -(jnp.maximum(-z, 0.0) + jnp.log1p(jnp.exp(-jnp.abs(z))))
    ri = lax.broadcasted_iota(jnp.int32, (CUM_T, CUM_T), 0)
    ci = lax.broadcasted_iota(jnp.int32, (CUM_T, CUM_T), 1)
    tri = (ci <= ri).astype(F32)
    c = jnp.dot(tri, log_f, preferred_element_type=F32,
                precision=lax.Precision.HIGHEST) + carry_ref[0:1, :]
    carry_ref[...] = jnp.broadcast_to(c[CUM_T - 1:CUM_T, :], carry_ref.shape)
    hi, mid, lo = _pieces(c * LOG2E)
    lane = lax.broadcasted_iota(jnp.int32, (CUM_T, LANES), 1)
    packed = jnp.where(lane < SUBLANES, hi,
                       jnp.where(lane < 2 * SUBLANES, pltpu.roll(mid, SUBLANES, axis=1),
                                 jnp.where(lane < PIECE_ROWS, pltpu.roll(lo, 2 * SUBLANES, axis=1),
                                           jnp.where(lane == PIECE_ROWS, 1.0, 0.0)))).astype(BF16)
    kx_ref[...] = jnp.dot(packed, pk_ref[...], preferred_element_type=F32).astype(BF16)
    qx_ref[...] = jnp.dot(packed, pq_ref[...], preferred_element_type=F32).astype(BF16)


def _fgate(f, b, bsz, seq, layer):
    n = f.shape[0]
    spt = seq // CUM_T
    pk, pq = _fgate_placement()
    row = lambda w: pl.BlockSpec((CUM_T, w), lambda bi, i: (bi * spt + i, 0))
    return pl.pallas_call(
        _fgate_kernel,
        grid=(bsz, spt),
        in_specs=[row(LANES), _resident(b.shape, layer), _resident(pk.shape), _resident(pq.shape)],
        out_specs=(row(BRANCH_WIDTH), row(BRANCH_WIDTH)),
        out_shape=(jax.ShapeDtypeStruct((n, BRANCH_WIDTH), BF16),
                   jax.ShapeDtypeStruct((n, BRANCH_WIDTH), BF16)),
        scratch_shapes=[pltpu.VMEM((SUBLANES, LANES), F32)],
        compiler_params=_params("arbitrary", "arbitrary"),
        name="fgate_cumsum",
    )(f, b, pk, pq)


def _build_kv(k_ref, v_ref, extra_of, ka_ref, vt_ref, seq):
    lane = lax.broadcasted_iota(jnp.int32, (KV_ROWS, LANES), 1)
    lower = lane < HEAD_DIM
    ones_rows = jnp.where(lax.broadcasted_iota(jnp.int32, (V_ROWS - HEAD_DIM, KV_ROWS), 0) == 0, 1.0, 0.0)

    def body(i, carry):
        rows = pl.ds(pl.multiple_of(i * KV_ROWS, KV_ROWS), KV_ROWS)
        extra = extra_of(i, rows)
        for pair in range(STEP_HEADS // 2):
            lanes = slice(pair * LANES, (pair + 1) * LANES)
            k, x = k_ref[rows, lanes], extra[:, lanes]
            ka_ref[2 * pair, rows, :] = jnp.where(lower, k, x)
            ka_ref[2 * pair + 1, rows, :] = jnp.where(lower, x, k)
            v_t = v_ref[rows, lanes].astype(F32).T
            for hh in range(2):
                vt_ref[2 * pair + hh, :, rows] = jnp.concatenate(
                    [v_t[hh * HEAD_DIM:(hh + 1) * HEAD_DIM], ones_rows], axis=0).astype(BF16)
        return carry

    lax.fori_loop(0, seq // KV_ROWS, body, 0)


def _extend_q_t(q_t, extras_t):
    out = []
    for hh in range(STEP_HEADS):
        q_h = q_t[hh * HEAD_DIM:(hh + 1) * HEAD_DIM]
        parts = [q_h, extras_t[hh]] if hh % 2 == 0 else [extras_t[hh], q_h]
        out.append(jnp.concatenate(parts, axis=0).astype(BF16))
    return out


def _flash_step(qa_t, ka_ref, vt_ref, m_ref, acc_ref, start, n_tiles, tables=None, first=False):
    heads = range(STEP_HEADS)
    m = [None if first else m_ref[hh, 0:1, :] for hh in heads]
    acc = [None if first else acc_ref[hh] for hh in heads]
    n_plain = n_tiles - (len(tables(0)) if tables else 0)
    tile_rows = lambda t: pl.ds(pl.multiple_of(start + t * TK, TK), TK)
    score = lambda t: [jnp.dot(ka_ref[hh, tile_rows(t), :], qa_t[hh], preferred_element_type=F32)
                       for hh in heads]
    ahead = [score(t) for t in range(min(SCORE_LOOKAHEAD, n_tiles))]
    for t in range(n_tiles):
        rows = tile_rows(t)
        if t + SCORE_LOOKAHEAD < n_tiles:
            ahead.append(score(t + SCORE_LOOKAHEAD))
        scores = ahead.pop(0)
        for hh in heads:
            s = scores[hh]
            if t >= n_plain:
                s = s + tables(hh)[t - n_plain]
            col_max = jnp.max(s, axis=0, keepdims=True)
            m_new = col_max if m[hh] is None else jnp.maximum(m[hh], col_max)
            p = jnp.exp2(s - m_new).astype(BF16)
            pv = jnp.dot(vt_ref[hh, :, rows], p, preferred_element_type=F32)
            acc[hh] = pv if acc[hh] is None else jnp.exp2(m[hh] - m_new) * acc[hh] + pv
            m[hh] = m_new
    for hh in heads:
        m_ref[hh] = jnp.broadcast_to(m[hh], (SUBLANES, TQ))
        acc_ref[hh] = acc[hh]


def _pair_output(o_ref, acc_ref):
    out_t = jnp.concatenate(
        [acc_ref[hh, 0:HEAD_DIM, :] / acc_ref[hh, HEAD_DIM:HEAD_DIM + 1, :] for hh in range(STEP_HEADS)],
        axis=0)
    o_ref[...] = out_t.T.astype(BF16)


def _flash_scratch(seq):
    return [pltpu.VMEM((STEP_HEADS, seq, LANES), BF16), pltpu.VMEM((STEP_HEADS, V_ROWS, seq), BF16),
            pltpu.VMEM((STEP_HEADS, SUBLANES, TQ), F32), pltpu.VMEM((STEP_HEADS, V_ROWS, TQ), F32)]


def _flash_chunks(qa, ka_ref, vt_ref, m_ref, acc_ref, n_chunks):
    def chunk(c, carry):
        _flash_step(qa, ka_ref, vt_ref, m_ref, acc_ref, c * (CHUNK_TILES * TK), CHUNK_TILES)
        return carry

    lax.fori_loop(0, n_chunks, chunk, 0)


def _flash_tail(qa, ka_ref, vt_ref, m_ref, acc_ref, last_tile, n_loose, tail_tables, guard=None):
    rem = n_loose % CHUNK_TILES
    for r in range(CHUNK_TILES):
        if isinstance(rem, int) and rem != r:
            continue
        tiles = r + len(tail_tables(0))
        cond = guard if isinstance(rem, int) else (rem == r if guard is None else (rem == r) & guard)

        @pl.when(cond)
        def _(tiles=tiles):
            _flash_step(qa, ka_ref, vt_ref, m_ref, acc_ref, (last_tile + 1 - tiles) * TK, tiles,
                        tables=tail_tables, first=True)


def _fox_kernel(q_ref, qx_ref, k_ref, v_ref, kx_ref, o_ref, ka_ref, vt_ref, m_ref, acc_ref, *, seq):
    qt = pl.program_id(2)

    @pl.when(qt == 0)
    def _():
        _build_kv(k_ref, v_ref, lambda i, rows: kx_ref[rows, :], ka_ref, vt_ref, seq)

    q_t = (q_ref[...].astype(F32) * (SCALE * LOG2E)).T
    qx_t = qx_ref[...].astype(F32).T
    qa = _extend_q_t(q_t, [qx_t[(hh ^ 1) * HEAD_DIM:((hh ^ 1) + 1) * HEAD_DIM] for hh in range(STEP_HEADS)])
    causal = jnp.where(lax.broadcasted_iota(jnp.int32, (TK, TQ), 0)
                       <= lax.broadcasted_iota(jnp.int32, (TK, TQ), 1), 0.0, NEG)
    _flash_tail(qa, ka_ref, vt_ref, m_ref, acc_ref, qt, qt, lambda hh: [causal])
    _flash_chunks(qa, ka_ref, vt_ref, m_ref, acc_ref, qt // CHUNK_TILES)
    _pair_output(o_ref, acc_ref)


def _fox(qkv, kx, qx, bsz, seq):
    n = qkv.shape[0]
    qpt = seq // TQ
    koff, voff = BRANCH_WIDTH // STEP_LANES, 2 * BRANCH_WIDTH // STEP_LANES
    q_tile = lambda off: pl.BlockSpec((TQ, STEP_LANES), lambda b, p, i: (b * qpt + i, off + p))
    kv_seq = lambda off: pl.BlockSpec((seq, STEP_LANES), lambda b, p, i: (b, off + p))
    return pl.pallas_call(
        functools.partial(_fox_kernel, seq=seq),
        grid=(bsz, N_HEADS // STEP_HEADS, qpt),
        in_specs=[q_tile(0), q_tile(0), kv_seq(koff), kv_seq(voff), kv_seq(0)],
        out_specs=q_tile(0),
        out_shape=jax.ShapeDtypeStruct((n, BRANCH_WIDTH), BF16),
        scratch_shapes=_flash_scratch(seq),
        compiler_params=_params("arbitrary", "arbitrary", "arbitrary"),
        name="fox",
    )(qkv, qx, qkv, qkv, kx)


def _moba_kernel(tab_ref, q_ref, k_ref, v_ref, town_ref, tprev_ref, o_ref,
                 kmean_ref, ka_ref, vt_ref, m_ref, acc_ref, *, seq):
    group = pl.program_id(1)
    qt = pl.program_id(2)
    nblk = seq // MOBA_BLOCK
    blk_shift = MOBA_BLOCK.bit_length() - 1

    @pl.when(qt == 0)
    def _():
        blk = lax.broadcasted_iota(jnp.int32, (LANES, seq), 0)
        pos = lax.broadcasted_iota(jnp.int32, (LANES, seq), 1)
        member = jnp.where(jnp.right_shift(pos, blk_shift) == blk, 1.0 / MOBA_BLOCK, 0.0).astype(BF16)
        kmean_ref[...] = jnp.dot(member, k_ref[...], preferred_element_type=F32)

        def extra_of(i, rows):
            xl = jnp.bitwise_and(lax.broadcasted_iota(jnp.int32, (KV_ROWS, STEP_LANES), 1), HEAD_DIM - 1)
            row_blk = jnp.right_shift(
                i * KV_ROWS + lax.broadcasted_iota(jnp.int32, (KV_ROWS, STEP_LANES), 0), blk_shift)
            hit = (xl < N_PIECES) | (xl - MOBA_SEL_LANE == row_blk)
            return jnp.where(hit, 1.0, 0.0).astype(BF16)

        _build_kv(k_ref, v_ref, extra_of, ka_ref, vt_ref, seq)

    q_t = q_ref[...].astype(F32).T
    row_head = jnp.right_shift(lax.broadcasted_iota(jnp.int32, (STEP_LANES, TQ), 0),
                               HEAD_DIM.bit_length() - 1)
    blk = lax.broadcasted_iota(jnp.int32, (nblk, TQ), 0)
    piece_row = lax.broadcasted_iota(jnp.int32, (MOBA_SEL_LANE, TQ), 0)
    past = blk < qt
    extras = []
    for hh in range(STEP_HEADS):
        q_head = jnp.where(row_head == hh, q_t, 0.0)
        gate = jnp.dot(kmean_ref[0:nblk, :], q_head, preferred_element_type=F32,
                       precision=lax.Precision.HIGHEST)
        gate = jnp.where(past, gate, NEG)
        sel = blk == qt
        for _ in range(MOBA_TOPK):
            top = jnp.max(gate, axis=0, keepdims=True)
            idx = jnp.min(jnp.where(gate == top, blk, nblk), axis=0, keepdims=True)
            pick = blk == idx
            sel = sel | (pick & past)
            gate = jnp.where(pick, -jnp.inf, gate)
        far = jnp.full((MOBA_SEL_LANE, TQ), tab_ref[REL_BUCKETS - 1, N_HEADS + STEP_HEADS * group + hh] * LOG2E, F32)
        hi, mid, lo = _pieces(far)
        far_rows = jnp.where(piece_row == 0, hi, jnp.where(piece_row == 1, mid, jnp.where(piece_row == 2, lo, 0.0)))
        extras.append(jnp.concatenate(
            [far_rows, jnp.where(sel, 0.0, NEG),
             jnp.zeros((HEAD_DIM - MOBA_SEL_LANE - nblk, TQ), F32)], axis=0))

    qa = _extend_q_t(q_t * (SCALE * LOG2E), extras)
    n_far = jnp.maximum(qt - 1, 0)
    _flash_tail(qa, ka_ref, vt_ref, m_ref, acc_ref, qt, 0, lambda hh: [town_ref[hh]], guard=qt == 0)
    _flash_tail(qa, ka_ref, vt_ref, m_ref, acc_ref, qt, n_far,
                lambda hh: [tprev_ref[hh], town_ref[hh]], guard=qt >= 1)
    _flash_chunks(qa, ka_ref, vt_ref, m_ref, acc_ref, n_far // CHUNK_TILES)
    _pair_output(o_ref, acc_ref)


def _moba(qkv, rel_bias, t_own, t_prev, bsz, seq):
    n = qkv.shape[0]
    qpt = seq // TQ
    koff, voff = BRANCH_WIDTH // STEP_LANES, 2 * BRANCH_WIDTH // STEP_LANES
    q_tile = pl.BlockSpec((TQ, STEP_LANES), lambda b, p, i: (b * qpt + i, p))
    kv_seq = lambda off: pl.BlockSpec((seq, STEP_LANES), lambda b, p, i: (b, off + p))
    head_tables = pl.BlockSpec((STEP_HEADS, MOBA_BLOCK, MOBA_BLOCK), lambda b, p, i: (p, 0, 0))
    return pl.pallas_call(
        functools.partial(_moba_kernel, seq=seq),
        grid=(bsz, N_HEADS // STEP_HEADS, qpt),
        in_specs=[pl.BlockSpec(memory_space=pltpu.SMEM), q_tile, kv_seq(koff), kv_seq(voff),
                  head_tables, head_tables],
        out_specs=q_tile,
        out_shape=jax.ShapeDtypeStruct((n, BRANCH_WIDTH), BF16),
        scratch_shapes=[pltpu.VMEM((LANES, STEP_LANES), F32)] + _flash_scratch(seq),
        compiler_params=_params("arbitrary", "arbitrary", "arbitrary"),
        name="moba",
    )(rel_bias, qkv, qkv, qkv, t_own, t_prev)


def _merge_kernel(x_ref, g_ref, ya_ref, yb_ref, yc_ref, yd_ref, wg_ref, bg_ref, wbr_ref, wout_ref, o_ref):
    x = x_ref[...]
    hb = _rmsnorm(x, g_ref[...]).astype(BF16)
    merged = jnp.zeros((TM, D_MODEL), F32)
    for n, y_ref in enumerate((ya_ref, yb_ref, yc_ref, yd_ref)):
        gz = jnp.dot(hb, wg_ref[:, n * D_MODEL:(n + 1) * D_MODEL], preferred_element_type=F32)
        gate = jax.nn.sigmoid(gz + bg_ref[n:n + 1, :])
        merged = merged + gate * jnp.dot(y_ref[...], wbr_ref[n], preferred_element_type=F32)
    o_ref[...] = x + jnp.dot(merged.astype(BF16), wout_ref[...], preferred_element_type=F32)


def _merge(x, g, ys, wg, bg, wbr, wout, layer):
    n = x.shape[0]
    row = lambda w: pl.BlockSpec((TM, w), lambda i: (i, 0))
    return pl.pallas_call(
        _merge_kernel,
        grid=(n // TM,),
        in_specs=[row(D_MODEL), _resident(g.shape, layer)] + [row(BRANCH_WIDTH)] * N_BRANCH
        + [_resident(w.shape, layer) for w in (wg, bg, wbr, wout)],
        out_specs=row(D_MODEL),
        out_shape=jax.ShapeDtypeStruct((n, D_MODEL), F32),
        compiler_params=_params("arbitrary"),
        name="merge",
    )(x, g, *ys, wg, bg, wbr, wout)


def _ffn_kernel(x_ref, g_ref, wup_ref, cw_ref, cb_ref, wdown_ref, fg_ref, o_ref, halo_ref, buf_ref, *, final):
    @pl.when(pl.program_id(1) == 0)
    def _():
        halo_ref[...] = jnp.zeros_like(halo_ref)

    x = x_ref[...]
    hb = _rmsnorm(x, g_ref[...]).astype(BF16)
    acc = jnp.zeros((TM, D_MODEL), F32)
    n_chunks = D_FF // FF_CHUNK
    col_of = lambda c, part: part * D_FF + c * FF_CHUNK
    up = lambda c: [jnp.dot(hb, wup_ref[:, col_of(c, part):col_of(c, part) + FF_CHUNK],
                            preferred_element_type=F32) for part in range(2)]
    ahead = up(0)
    for c in range(n_chunks):
        pre, ahead = ahead, (up(c + 1) if c + 1 < n_chunks else None)
        halves = []
        for part in range(2):
            col = col_of(c, part)
            buf_ref[c % 2, part,0:SUBLANES, :] = halo_ref[:, col:col + FF_CHUNK]
            buf_ref[c % 2, part,SUBLANES:SUBLANES + TM, :] = pre[part]
            halo_ref[:, col:col + FF_CHUNK] = buf_ref[c % 2, part,TM:TM + SUBLANES, :]
            u = cb_ref[:, col:col + FF_CHUNK]
            for k in range(FFN_CONV_WIDTH):
                lo = SUBLANES - (FFN_CONV_WIDTH - 1) + k
                u = u + cw_ref[k:k + 1, col:col + FF_CHUNK] * buf_ref[c % 2, part, lo:lo + TM, :]
            halves.append(u)
        a = (halves[0] * jax.nn.sigmoid(halves[0])) * halves[1]
        acc = acc + jnp.dot(a.astype(BF16), wdown_ref[c * FF_CHUNK:(c + 1) * FF_CHUNK, :],
                            preferred_element_type=F32)
    y = x + acc
    o_ref[...] = _rmsnorm(y, fg_ref[...]) if final else y


def _ffn(x, g, wup, cw, cb, wdown, fg, bsz, seq, layer, final):
    n = x.shape[0]
    spt = seq // TM
    row = pl.BlockSpec((TM, D_MODEL), lambda bi, i: (bi * spt + i, 0))
    return pl.pallas_call(
        functools.partial(_ffn_kernel, final=final),
        grid=(bsz, spt),
        in_specs=[row] + [_resident(w.shape, layer) for w in (g, wup, cw, cb, wdown)]
        + [_resident(fg.shape)],
        out_specs=row,
        out_shape=jax.ShapeDtypeStruct((n, D_MODEL), F32),
        scratch_shapes=[pltpu.VMEM((SUBLANES, 2 * D_FF), F32),
                        pltpu.VMEM((2, 2, TM + SUBLANES, FF_CHUNK), F32)],
        compiler_params=_params("arbitrary", "arbitrary"),
        name="ffn",
    )(x, g, wup, cw, cb, wdown, fg)


def _split_w_in(w_in):
    hd, bw = HEAD_DIM, BRANCH_WIDTH
    o = 0
    qa = w_in[..., o:o + bw]; o += bw
    ka = w_in[..., o:o + SWA_KV_HEADS * hd]; o += SWA_KV_HEADS * hd
    va = w_in[..., o:o + SWA_KV_HEADS * hd]; o += SWA_KV_HEADS * hd
    ub = w_in[..., o:o + 2 * bw]; o += 2 * bw
    qkv_c = w_in[..., o:o + 3 * bw]; o += 3 * bw
    fc = w_in[..., o:o + N_HEADS]; o += N_HEADS
    qkv_d = w_in[..., o:o + 3 * bw]; o += 3 * bw
    gates = w_in[..., o:o + N_BRANCH * D_MODEL]
    dup = lambda w: jnp.concatenate([w[..., i * hd:(i + 1) * hd] for i in (0, 0, 1, 1)], axis=-1)
    wa = jnp.concatenate([qa, dup(ka), dup(va)], axis=-1)
    wf = jnp.pad(fc, ((0, 0), (0, 0), (0, LANES - N_HEADS)))
    return tuple(w.astype(BF16) for w in (wa, ub, qkv_c, wf, qkv_d)), gates.astype(BF16)


def _layer(x, bsz, seq, layer, final, biases, rel_bias, p):
    bias_swa, table_own, table_prev = biases
    za, zb, zc, zf, zd = _inproj(x, p["ln1_g"], p["w_in_groups"], layer)
    ya = _swa(za, p["sinks"], bias_swa, seq, layer)
    yb = _conformer(zb, p["conv_w"], p["conv_b"], p["conv_ln_g"], p["conv_ln_b"], bsz, seq, layer)
    kx, qx = _fgate(zf, p["b_fgate"], bsz, seq, layer)
    yc = _fox(zc, kx, qx, bsz, seq)
    yd = _moba(zd, rel_bias, table_own, table_prev, bsz, seq)
    x = _merge(x, p["ln1_g"], (ya, yb, yc, yd), p["w_gates"], p["b_gate"], p["w_br"], p["w_out"], layer)
    return _ffn(x, p["ln2_g"], p["w_up"], p["ffn_conv_w"], p["ffn_conv_b"], p["w_down"], p["final_g"],
                bsz, seq, layer, final)


def kernel(x, ln1_g, w_in, b_gate, b_fgate, sinks, conv_w, conv_b, conv_ln_g, conv_ln_b, w_br, w_out,
           ln2_g, w_up, ffn_conv_w, ffn_conv_b, w_down, rel_bias, final_g):
    bsz, seq, _ = x.shape
    depth = w_in.shape[0]
    assert seq % TM == 0 and seq % KV_ROWS == 0 and TQ == TK == MOBA_BLOCK
    assert MOBA_SEL_LANE + seq // MOBA_BLOCK <= HEAD_DIM
    biases = _expand_bias(rel_bias)
    rows = lambda v: v.reshape(depth, 1, -1)
    w_in_groups, w_gates = _split_w_in(w_in)
    params = dict(
        ln1_g=rows(ln1_g), w_in_groups=w_in_groups, w_gates=w_gates, b_gate=b_gate,
        b_fgate=rows(jnp.pad(b_fgate, ((0, 0), (0, LANES - N_HEADS)))), sinks=sinks,
        conv_w=conv_w, conv_b=rows(conv_b), conv_ln_g=rows(conv_ln_g), conv_ln_b=rows(conv_ln_b),
        w_br=w_br.astype(BF16), w_out=w_out.astype(BF16), ln2_g=rows(ln2_g), w_up=w_up.astype(BF16),
        ffn_conv_w=ffn_conv_w, ffn_conv_b=rows(ffn_conv_b), w_down=w_down.astype(BF16),
        final_g=final_g.reshape(1, -1))
    y = x.reshape(bsz * seq, D_MODEL)
    for layer in range(depth):
        y = _layer(y, bsz, seq, layer, layer == depth - 1, biases, rel_bias, params)
    return y.reshape(bsz, seq, D_MODEL)
```

```python
import functools
import math

import jax
import jax.numpy as jnp
import numpy as np
from jax import lax
from jax.experimental import pallas as pl
from jax.experimental.pallas import tpu as pltpu

D_MODEL = 1024
HEAD_DIM = 64
N_BRANCH = 4
BRANCH_WIDTH = 512
N_HEADS = 8
N_PAIRS = N_HEADS // 2
SWA_KV_HEADS = 2
SWA_BLOCK = 128
CONV_WIDTH = 31
MOBA_BLOCK = 256
MOBA_TOPK = 3
D_FF = 2816
FFN_CONV_WIDTH = 3
REL_BUCKETS = 32
REL_MAX_DIST = 128
EPS = 1e-6
NEG = -1e30
SCALE = HEAD_DIM ** -0.5
LOG2E = math.log2(math.e)

LANES = 128
SUBLANES = 8
VMEM_LIMIT = 56 * 1024 * 1024

TM = 512
TQ = 256
TK = 256
CHUNK_TILES = 8
SCORE_LOOKAHEAD = 3
KV_ROWS = 512
CUM_T = 256
FF_CHUNK = 256
CONV_ROWS = 64
CONV_HALO = 32

N_PIECES = 3
PIECE_ROWS = SUBLANES * N_PIECES
MOBA_SEL_LANE = SUBLANES
STEP_HEADS = 2
STEP_LANES = STEP_HEADS * HEAD_DIM
V_ROWS = HEAD_DIM + 16

BF16 = jnp.bfloat16
F32 = jnp.float32


def _params(*sem):
    return pltpu.CompilerParams(dimension_semantics=sem, vmem_limit_bytes=VMEM_LIMIT)


def _resident(shape, layer=None):
    if layer is None:
        zeros = (0,) * len(shape)
        return pl.BlockSpec(shape, lambda *_: zeros, pipeline_mode=pl.Buffered(1))
    index = (layer,) + (0,) * (len(shape) - 1)
    return pl.BlockSpec((None,) + tuple(shape[1:]), lambda *_: index, pipeline_mode=pl.Buffered(1))


def _rmsnorm(x, g):
    return (x * lax.rsqrt(jnp.mean(x * x, axis=-1, keepdims=True) + EPS)) * g


def _pieces(x):
    hi = x.astype(BF16).astype(F32)
    rest = x - hi
    mid = rest.astype(BF16).astype(F32)
    return hi, mid, rest - mid


def _rel_bucket_np(dist):
    n = np.maximum(dist, 0)
    max_exact = REL_BUCKETS // 2
    nf = np.maximum(n, 1).astype(np.float64)
    large = max_exact + (np.log(nf / max_exact) / math.log(REL_MAX_DIST / max_exact)
                         * (REL_BUCKETS - max_exact) + 1e-9).astype(np.int32)
    large = np.minimum(large, REL_BUCKETS - 1)
    return np.where(n < max_exact, n, large).astype(np.int32)


def _bias_kernel(tab_ref, bs_ref, bo_ref, bp_ref, os_ref, oo_ref, op_ref):
    os_ref[...] = jnp.zeros_like(os_ref)
    oo_ref[...] = jnp.zeros_like(oo_ref)
    op_ref[...] = jnp.zeros_like(op_ref)
    bs, bo, bp = bs_ref[...], bo_ref[...], bp_ref[...]

    def body(b, carry):
        ms, mo, mp = bs == b, bo == b, bp == b
        for h in range(N_HEADS):
            os_ref[h] = jnp.where(ms, tab_ref[b, h], os_ref[h])
            oo_ref[h] = jnp.where(mo, tab_ref[b, N_HEADS + h], oo_ref[h])
            op_ref[h] = jnp.where(mp, tab_ref[b, N_HEADS + h], op_ref[h])
        return carry

    lax.fori_loop(0, REL_BUCKETS, body, 0)
    causal = (lax.broadcasted_iota(jnp.int32, (MOBA_BLOCK, MOBA_BLOCK), 0)
              <= lax.broadcasted_iota(jnp.int32, (MOBA_BLOCK, MOBA_BLOCK), 1))
    for h in range(N_HEADS):
        far = tab_ref[REL_BUCKETS - 1, N_HEADS + h]
        oo_ref[h] = jnp.where(causal, (oo_ref[h] - far) * LOG2E, NEG)
        op_ref[h] = (op_ref[h] - far) * LOG2E


def _expand_bias(rel_bias):
    qi = np.arange(SWA_BLOCK)[:, None]
    kj = np.arange(2 * SWA_BLOCK)[None, :]
    b_swa = _rel_bucket_np(qi + SWA_BLOCK - kj)
    qi = np.arange(MOBA_BLOCK)[:, None]
    kj = np.arange(MOBA_BLOCK)[None, :]
    b_own = np.ascontiguousarray(_rel_bucket_np(qi - kj).T)
    b_prev = np.ascontiguousarray(_rel_bucket_np(qi - kj + MOBA_BLOCK).T)
    vm = pl.BlockSpec(memory_space=pltpu.VMEM)
    return pl.pallas_call(
        _bias_kernel,
        out_shape=(jax.ShapeDtypeStruct((N_HEADS, SWA_BLOCK, 2 * SWA_BLOCK), F32),
                   jax.ShapeDtypeStruct((N_HEADS, MOBA_BLOCK, MOBA_BLOCK), F32),
                   jax.ShapeDtypeStruct((N_HEADS, MOBA_BLOCK, MOBA_BLOCK), F32)),
        in_specs=[pl.BlockSpec(memory_space=pltpu.SMEM), vm, vm, vm],
        out_specs=(vm, vm, vm),
        name="bias_expand",
    )(rel_bias, jnp.asarray(b_swa), jnp.asarray(b_own), jnp.asarray(b_prev))


def _inproj_kernel(x_ref, g_ref, wa_ref, wb_ref, wc_ref, wf_ref, wd_ref,
                   oa_ref, ob_ref, oc_ref, of_ref, od_ref):
    h = _rmsnorm(x_ref[...], g_ref[...])
    hb = h.astype(BF16)
    oa_ref[...] = jnp.dot(hb, wa_ref[...], preferred_element_type=F32).astype(BF16)
    ob_ref[...] = jnp.dot(hb, wb_ref[...], preferred_element_type=F32)
    oc_ref[...] = jnp.dot(hb, wc_ref[...], preferred_element_type=F32).astype(BF16)
    od_ref[...] = jnp.dot(hb, wd_ref[...], preferred_element_type=F32).astype(BF16)
    of_ref[...] = jnp.dot(hb, wf_ref[...], preferred_element_type=F32)


def _inproj(x, g, ws, layer):
    n = x.shape[0]
    row = lambda w: pl.BlockSpec((TM, w), lambda i: (i, 0))
    dts = (BF16, F32, BF16, F32, BF16)
    return pl.pallas_call(
        _inproj_kernel,
        grid=(n // TM,),
        in_specs=[row(D_MODEL), _resident(g.shape, layer)] + [_resident(w.shape, layer) for w in ws],
        out_specs=tuple(row(w.shape[-1]) for w in ws),
        out_shape=tuple(jax.ShapeDtypeStruct((n, w.shape[-1]), dt) for w, dt in zip(ws, dts)),
        compiler_params=_params("arbitrary"),
        name="inproj",
    )(x, g, *ws)


def _swa_kernel(sink_ref, q_ref, kc_ref, kp_ref, vc_ref, vp_ref, bias_ref, o_ref, *, blocks_per_seq,
                layer):
    blk = pl.program_id(0) % blocks_per_seq
    kk = jnp.concatenate([kp_ref[...], kc_ref[...]], axis=0)
    vv = jnp.concatenate([vp_ref[...], vc_ref[...]], axis=0)
    qi = lax.broadcasted_iota(jnp.int32, (SWA_BLOCK, 2 * SWA_BLOCK), 0)
    kj = lax.broadcasted_iota(jnp.int32, (SWA_BLOCK, 2 * SWA_BLOCK), 1)
    dist = qi + SWA_BLOCK - kj
    mask = (dist >= 0) & (dist < SWA_BLOCK) & ((blk > 0) | (kj >= SWA_BLOCK))
    lane = lax.broadcasted_iota(jnp.int32, (SWA_BLOCK, LANES), 1)
    upper = lane >= HEAD_DIM
    kv_of = lambda h: h // (N_HEADS // SWA_KV_HEADS)
    scores = []
    for h in range(N_HEADS):
        q_pair = q_ref[:, (h // 2) * LANES:(h // 2 + 1) * LANES] * SCALE
        q_h = jnp.where(upper == (h % 2 == 1), q_pair, jnp.zeros_like(q_pair))
        k_kv = kk[:, kv_of(h) * LANES:(kv_of(h) + 1) * LANES]
        scores.append(lax.dot_general(q_h, k_kv, (((1,), (1,)), ((), ())), preferred_element_type=F32))
    probs = []
    for h in range(N_HEADS):
        logits = jnp.where(mask, scores[h] + bias_ref[h], NEG)
        sink = sink_ref[layer, h]
        m = jnp.maximum(jnp.max(logits, axis=-1, keepdims=True), sink)
        p = jnp.exp(logits - m)
        denom = jnp.sum(p, axis=-1, keepdims=True) + jnp.exp(sink - m)
        probs.append((p / denom).astype(BF16))
    outs = [jnp.dot(probs[h], vv[:, kv_of(h) * LANES:(kv_of(h) + 1) * LANES], preferred_element_type=F32)
            for h in range(N_HEADS)]
    for pair in range(N_PAIRS):
        o_ref[:, pair * LANES:(pair + 1) * LANES] = jnp.where(
            upper, outs[2 * pair + 1], outs[2 * pair]).astype(BF16)


def _swa(qkv, sinks, bias, seq, layer):
    n = qkv.shape[0]
    bps = seq // SWA_BLOCK
    kcol, vcol = BRANCH_WIDTH // (2 * LANES), BRANCH_WIDTH // (2 * LANES) + 1
    prev = lambda i: jnp.where(i % bps == 0, i, i - 1)
    return pl.pallas_call(
        functools.partial(_swa_kernel, blocks_per_seq=bps, layer=layer),
        grid=(n // SWA_BLOCK,),
        in_specs=[pl.BlockSpec(memory_space=pltpu.SMEM),
                  pl.BlockSpec((SWA_BLOCK, BRANCH_WIDTH), lambda i: (i, 0)),
                  pl.BlockSpec((SWA_BLOCK, 2 * LANES), lambda i: (i, kcol)),
                  pl.BlockSpec((SWA_BLOCK, 2 * LANES), lambda i: (prev(i), kcol)),
                  pl.BlockSpec((SWA_BLOCK, 2 * LANES), lambda i: (i, vcol)),
                  pl.BlockSpec((SWA_BLOCK, 2 * LANES), lambda i: (prev(i), vcol)),
                  _resident(bias.shape)],
        out_specs=pl.BlockSpec((SWA_BLOCK, BRANCH_WIDTH), lambda i: (i, 0)),
        out_shape=jax.ShapeDtypeStruct((n, BRANCH_WIDTH), BF16),
        compiler_params=_params("arbitrary"),
        name="swa",
    )(sinks, qkv, qkv, qkv, qkv, qkv, bias)


def _conformer_kernel(u_ref, w_ref, b_ref, g_ref, beta_ref, o_ref, buf_ref, y_ref):
    @pl.when(pl.program_id(1) == 0)
    def _():
        buf_ref[0:CONV_HALO, :] = jnp.zeros((CONV_HALO, BRANCH_WIDTH), F32)

    a = u_ref[:, 0:BRANCH_WIDTH]
    gt = u_ref[:, BRANCH_WIDTH:2 * BRANCH_WIDTH]
    buf_ref[CONV_HALO:CONV_HALO + TM, :] = a * jax.nn.sigmoid(gt)
    n_shifts = SUBLANES
    for r in range(TM // CONV_ROWS):
        row0 = r * CONV_ROWS
        for cg in range(BRANCH_WIDTH // LANES):
            cols = slice(cg * LANES, (cg + 1) * LANES)
            slabs = []
            for step in range((CONV_WIDTH - 1) // n_shifts + 1):
                lo = CONV_HALO - n_shifts * (step + 1) + row0
                slabs.append(buf_ref[lo:lo + CONV_ROWS + n_shifts, cols])
            y = None
            for shift in range(n_shifts):
                z = None
                for step in range((CONV_WIDTH - 1 - shift) // n_shifts + 1):
                    k = CONV_WIDTH - 1 - (n_shifts * step + shift)
                    term = w_ref[k:k + 1, cols] * slabs[step]
                    z = term if z is None else z + term
                shifted = z[n_shifts - shift:n_shifts - shift + CONV_ROWS]
                y = shifted if y is None else y + shifted
            y_ref[row0:row0 + CONV_ROWS, cols] = y
        y = y_ref[row0:row0 + CONV_ROWS, :] + b_ref[...]
        mu = jnp.mean(y, axis=-1, keepdims=True)
        yc = y - mu
        var = jnp.mean(yc * yc, axis=-1, keepdims=True)
        z = yc * lax.rsqrt(var + EPS) * g_ref[...] + beta_ref[...]
        o_ref[r * CONV_ROWS:(r + 1) * CONV_ROWS, :] = (z * jax.nn.sigmoid(z)).astype(BF16)
    buf_ref[0:CONV_HALO, :] = buf_ref[TM:TM + CONV_HALO, :]


def _conformer(u, w, b, g, beta, bsz, seq, layer):
    n = u.shape[0]
    spt = seq // TM
    return pl.pallas_call(
        _conformer_kernel,
        grid=(bsz, spt),
        in_specs=[pl.BlockSpec((TM, 2 * BRANCH_WIDTH), lambda bi, i: (bi * spt + i, 0)),
                  _resident(w.shape, layer), _resident(b.shape, layer), _resident(g.shape, layer),
                  _resident(beta.shape, layer)],
        out_specs=pl.BlockSpec((TM, BRANCH_WIDTH), lambda bi, i: (bi * spt + i, 0)),
        out_shape=jax.ShapeDtypeStruct((n, BRANCH_WIDTH), BF16),
        scratch_shapes=[pltpu.VMEM((TM + CONV_HALO, BRANCH_WIDTH), F32),
                        pltpu.VMEM((TM, BRANCH_WIDTH), F32)],
        compiler_params=_params("arbitrary", "arbitrary"),
        name="conformer",
    )(u, w, b, g, beta)


def _extra_base(h):
    return (h // 2) * LANES + HEAD_DIM * (1 - h % 2)


def _fgate_placement():
    pk = np.zeros((LANES, BRANCH_WIDTH), np.float32)
    pq = np.zeros((LANES, BRANCH_WIDTH), np.float32)
    for h in range(N_HEADS):
        base = _extra_base(h)
        for i in range(N_PIECES):
            pk[SUBLANES * i + h, base + i] = 1.0
            pq[PIECE_ROWS, base + i] = -1.0
            pk[PIECE_ROWS, base + N_PIECES + i] = 1.0
            pq[SUBLANES * i + h, base + N_PIECES + i] = 1.0
    return jnp.asarray(pk, BF16), jnp.asarray(pq, BF16)


def _fgate_kernel(f_ref, b_ref, pk_ref, pq_ref, kx_ref, qx_ref, carry_ref):
    @pl.when(pl.program_id(1) == 0)
    def _():
        carry_ref[...] = jnp.zeros_like(carry_ref)

    z = f_ref[...] + b_ref[...]
    log_f = -(jnp.maximum(-z, 0.0) + jnp.log1p(jnp.exp(-jnp.abs(z))))
    ri = lax.broadcasted_iota(jnp.int32, (CUM_T, CUM_T), 0)
    ci = lax.broadcasted_iota(jnp.int32, (CUM_T, CUM_T), 1)
    tri = (ci <= ri).astype(F32)
    c = jnp.dot(tri, log_f, preferred_element_type=F32,
                precision=lax.Precision.HIGHEST) + carry_ref[0:1, :]
    carry_ref[...] = jnp.broadcast_to(c[CUM_T - 1:CUM_T, :], carry_ref.shape)
    hi, mid, lo = _pieces(c * LOG2E)
    lane = lax.broadcasted_iota(jnp.int32, (CUM_T, LANES), 1)
    packed = jnp.where(lane < SUBLANES, hi,
                       jnp.where(lane < 2 * SUBLANES, pltpu.roll(mid, SUBLANES, axis=1),
                                 jnp.where(lane < PIECE_ROWS, pltpu.roll(lo, 2 * SUBLANES, axis=1),
                                           jnp.where(lane == PIECE_ROWS, 1.0, 0.0)))).astype(BF16)
    kx_ref[...] = jnp.dot(packed, pk_ref[...], preferred_element_type=F32).astype(BF16)
    qx_ref[...] = jnp.dot(packed, pq_ref[...], preferred_element_type=F32).astype(BF16)


def _fgate(f, b, bsz, seq, layer):
    n = f.shape[0]
    spt = seq // CUM_T
    pk, pq = _fgate_placement()
    row = lambda w: pl.BlockSpec((CUM_T, w), lambda bi, i: (bi * spt + i, 0))
    return pl.pallas_call(
        _fgate_kernel,
        grid=(bsz, spt),
        in_specs=[row(LANES), _resident(b.shape, layer), _resident(pk.shape), _resident(pq.shape)],
        out_specs=(row(BRANCH_WIDTH), row(BRANCH_WIDTH)),
        out_shape=(jax.ShapeDtypeStruct((n, BRANCH_WIDTH), BF16),
                   jax.ShapeDtypeStruct((n, BRANCH_WIDTH), BF16)),
        scratch_shapes=[pltpu.VMEM((SUBLANES, LANES), F32)],
        compiler_params=_params("arbitrary", "arbitrary"),
        name="fgate_cumsum",
    )(f, b, pk, pq)


def _build_kv(k_ref, v_ref, extra_of, ka_ref, vt_ref, seq):
    lane = lax.broadcasted_iota(jnp.int32, (KV_ROWS, LANES), 1)
    lower = lane < HEAD_DIM
    ones_rows = jnp.where(lax.broadcasted_iota(jnp.int32, (V_ROWS - HEAD_DIM, KV_ROWS), 0) == 0, 1.0, 0.0)

    def body(i, carry):
        rows = pl.ds(pl.multiple_of(i * KV_ROWS, KV_ROWS), KV_ROWS)
        extra = extra_of(i, rows)
        for pair in range(STEP_HEADS // 2):
            lanes = slice(pair * LANES, (pair + 1) * LANES)
            k, x = k_ref[rows, lanes], extra[:, lanes]
            ka_ref[2 * pair, rows, :] = jnp.where(lower, k, x)
            ka_ref[2 * pair + 1, rows, :] = jnp.where(lower, x, k)
            v_t = v_ref[rows, lanes].astype(F32).T
            for hh in range(2):
                vt_ref[2 * pair + hh, :, rows] = jnp.concatenate(
                    [v_t[hh * HEAD_DIM:(hh + 1) * HEAD_DIM], ones_rows], axis=0).astype(BF16)
        return carry

    lax.fori_loop(0, seq // KV_ROWS, body, 0)


def _extend_q_t(q_t, extras_t):
    out = []
    for hh in range(STEP_HEADS):
        q_h = q_t[hh * HEAD_DIM:(hh + 1) * HEAD_DIM]
        parts = [q_h, extras_t[hh]] if hh % 2 == 0 else [extras_t[hh], q_h]
        out.append(jnp.concatenate(parts, axis=0).astype(BF16))
    return out


def _flash_step(qa_t, ka_ref, vt_ref, m_ref, acc_ref, s_ref,start, n_tiles, tables=None, first=False):
    heads = range(STEP_HEADS)
    m = [None if first else m_ref[hh, 0:1, :] for hh in heads]
    acc = [None if first else acc_ref[hh] for hh in heads]
    n_plain = n_tiles - (len(tables(0)) if tables else 0)
    tile_rows = lambda t: pl.ds(pl.multiple_of(start + t * TK, TK), TK)
    ring = SCORE_LOOKAHEAD + 1

    def score(t):
        for hh in heads:
            s_ref[t % ring, hh] = jnp.dot(ka_ref[hh, tile_rows(t), :], qa_t[hh],
                                          preferred_element_type=F32)

    for t in range(min(SCORE_LOOKAHEAD, n_tiles)):
        score(t)
    for t in range(n_tiles):
        rows = tile_rows(t)
        if t + SCORE_LOOKAHEAD < n_tiles:
            score(t + SCORE_LOOKAHEAD)
        for hh in heads:
            s = s_ref[t % ring, hh]
            if t >= n_plain:
                s = s + tables(hh)[t - n_plain]
            col_max = jnp.max(s, axis=0, keepdims=True)
            m_new = col_max if m[hh] is None else jnp.maximum(m[hh], col_max)
            p = jnp.exp2(s - m_new).astype(BF16)
            pv = jnp.dot(vt_ref[hh, :, rows], p, preferred_element_type=F32)
            acc[hh] = pv if acc[hh] is None else jnp.exp2(m[hh] - m_new) * acc[hh] + pv
            m[hh] = m_new
    for hh in heads:
        m_ref[hh] = jnp.broadcast_to(m[hh], (SUBLANES, TQ))
        acc_ref[hh] = acc[hh]


def _pair_output(o_ref, acc_ref):
    out_t = jnp.concatenate(
        [acc_ref[hh, 0:HEAD_DIM, :] / acc_ref[hh, HEAD_DIM:HEAD_DIM + 1, :] for hh in range(STEP_HEADS)],
        axis=0)
    o_ref[...] = out_t.T.astype(BF16)


def _flash_scratch(seq):
    return [pltpu.VMEM((STEP_HEADS, seq, LANES), BF16), pltpu.VMEM((STEP_HEADS, V_ROWS, seq), BF16),
            pltpu.VMEM((STEP_HEADS, SUBLANES, TQ), F32), pltpu.VMEM((STEP_HEADS, V_ROWS, TQ), F32),
            pltpu.VMEM((SCORE_LOOKAHEAD + 1, STEP_HEADS, TK, TQ), F32)]


def _flash_chunks(qa, ka_ref, vt_ref, m_ref, acc_ref, s_ref,n_chunks):
    def chunk(c, carry):
        _flash_step(qa, ka_ref, vt_ref, m_ref, acc_ref, s_ref,c * (CHUNK_TILES * TK), CHUNK_TILES)
        return carry

    lax.fori_loop(0, n_chunks, chunk, 0)


def _flash_tail(qa, ka_ref, vt_ref, m_ref, acc_ref, s_ref,last_tile, n_loose, tail_tables, guard=None):
    rem = n_loose % CHUNK_TILES
    for r in range(CHUNK_TILES):
        if isinstance(rem, int) and rem != r:
            continue
        tiles = r + len(tail_tables(0))
        cond = guard if isinstance(rem, int) else (rem == r if guard is None else (rem == r) & guard)

        @pl.when(cond)
        def _(tiles=tiles):
            _flash_step(qa, ka_ref, vt_ref, m_ref, acc_ref, s_ref,(last_tile + 1 - tiles) * TK, tiles,
                        tables=tail_tables, first=True)


def _fox_kernel(q_ref, qx_ref, k_ref, v_ref, kx_ref, o_ref, ka_ref, vt_ref, m_ref, acc_ref, s_ref,*, seq):
    qt = pl.program_id(2)

    @pl.when(qt == 0)
    def _():
        _build_kv(k_ref, v_ref, lambda i, rows: kx_ref[rows, :], ka_ref, vt_ref, seq)

    q_t = (q_ref[...].astype(F32) * (SCALE * LOG2E)).T
    qx_t = qx_ref[...].astype(F32).T
    qa = _extend_q_t(q_t, [qx_t[(hh ^ 1) * HEAD_DIM:((hh ^ 1) + 1) * HEAD_DIM] for hh in range(STEP_HEADS)])
    causal = jnp.where(lax.broadcasted_iota(jnp.int32, (TK, TQ), 0)
                       <= lax.broadcasted_iota(jnp.int32, (TK, TQ), 1), 0.0, NEG)
    _flash_tail(qa, ka_ref, vt_ref, m_ref, acc_ref, s_ref,qt, qt, lambda hh: [causal])
    _flash_chunks(qa, ka_ref, vt_ref, m_ref, acc_ref, s_ref,qt // CHUNK_TILES)
    _pair_output(o_ref, acc_ref)


def _fox(qkv, kx, qx, bsz, seq):
    n = qkv.shape[0]
    qpt = seq // TQ
    koff, voff = BRANCH_WIDTH // STEP_LANES, 2 * BRANCH_WIDTH // STEP_LANES
    q_tile = lambda off: pl.BlockSpec((TQ, STEP_LANES), lambda b, p, i: (b * qpt + i, off + p))
    kv_seq = lambda off: pl.BlockSpec((seq, STEP_LANES), lambda b, p, i: (b, off + p))
    return pl.pallas_call(
        functools.partial(_fox_kernel, seq=seq),
        grid=(bsz, N_HEADS // STEP_HEADS, qpt),
        in_specs=[q_tile(0), q_tile(0), kv_seq(koff), kv_seq(voff), kv_seq(0)],
        out_specs=q_tile(0),
        out_shape=jax.ShapeDtypeStruct((n, BRANCH_WIDTH), BF16),
        scratch_shapes=_flash_scratch(seq),
        compiler_params=_params("arbitrary", "arbitrary", "arbitrary"),
        name="fox",
    )(qkv, qx, qkv, qkv, kx)


def _moba_kernel(tab_ref, q_ref, k_ref, v_ref, town_ref, tprev_ref, o_ref,
                 kmean_ref, ka_ref, vt_ref, m_ref, acc_ref, s_ref,*, seq):
    group = pl.program_id(1)
    qt = pl.program_id(2)
    nblk = seq // MOBA_BLOCK
    blk_shift = MOBA_BLOCK.bit_length() - 1

    @pl.when(qt == 0)
    def _():
        blk = lax.broadcasted_iota(jnp.int32, (LANES, seq), 0)
        pos = lax.broadcasted_iota(jnp.int32, (LANES, seq), 1)
        member = jnp.where(jnp.right_shift(pos, blk_shift) == blk, 1.0 / MOBA_BLOCK, 0.0).astype(BF16)
        kmean_ref[...] = jnp.dot(member, k_ref[...], preferred_element_type=F32)

        def extra_of(i, rows):
            xl = jnp.bitwise_and(lax.broadcasted_iota(jnp.int32, (KV_ROWS, STEP_LANES), 1), HEAD_DIM - 1)
            row_blk = jnp.right_shift(
                i * KV_ROWS + lax.broadcasted_iota(jnp.int32, (KV_ROWS, STEP_LANES), 0), blk_shift)
            hit = (xl < N_PIECES) | (xl - MOBA_SEL_LANE == row_blk)
            return jnp.where(hit, 1.0, 0.0).astype(BF16)

        _build_kv(k_ref, v_ref, extra_of, ka_ref, vt_ref, seq)

    q_t = q_ref[...].astype(F32).T
    row_head = jnp.right_shift(lax.broadcasted_iota(jnp.int32, (STEP_LANES, TQ), 0),
                               HEAD_DIM.bit_length() - 1)
    blk = lax.broadcasted_iota(jnp.int32, (nblk, TQ), 0)
    piece_row = lax.broadcasted_iota(jnp.int32, (MOBA_SEL_LANE, TQ), 0)
    past = blk < qt
    extras = []
    for hh in range(STEP_HEADS):
        q_head = jnp.where(row_head == hh, q_t, 0.0)
        gate = jnp.dot(kmean_ref[0:nblk, :], q_head, preferred_element_type=F32,
                       precision=lax.Precision.HIGHEST)
        gate = jnp.where(past, gate, NEG)
        sel = blk == qt
        for _ in range(MOBA_TOPK):
            top = jnp.max(gate, axis=0, keepdims=True)
            idx = jnp.min(jnp.where(gate == top, blk, nblk), axis=0, keepdims=True)
            pick = blk == idx
            sel = sel | (pick & past)
            gate = jnp.where(pick, -jnp.inf, gate)
        far = jnp.full((MOBA_SEL_LANE, TQ), tab_ref[REL_BUCKETS - 1, N_HEADS + STEP_HEADS * group + hh] * LOG2E, F32)
        hi, mid, lo = _pieces(far)
        far_rows = jnp.where(piece_row == 0, hi, jnp.where(piece_row == 1, mid, jnp.where(piece_row == 2, lo, 0.0)))
        extras.append(jnp.concatenate(
            [far_rows, jnp.where(sel, 0.0, NEG),
             jnp.zeros((HEAD_DIM - MOBA_SEL_LANE - nblk, TQ), F32)], axis=0))

    qa = _extend_q_t(q_t * (SCALE * LOG2E), extras)
    n_far = jnp.maximum(qt - 1, 0)
    _flash_tail(qa, ka_ref, vt_ref, m_ref, acc_ref, s_ref,qt, 0, lambda hh: [town_ref[hh]], guard=qt == 0)
    _flash_tail(qa, ka_ref, vt_ref, m_ref, acc_ref, s_ref,qt, n_far,
                lambda hh: [tprev_ref[hh], town_ref[hh]], guard=qt >= 1)
    _flash_chunks(qa, ka_ref, vt_ref, m_ref, acc_ref, s_ref,n_far // CHUNK_TILES)
    _pair_output(o_ref, acc_ref)


def _moba(qkv, rel_bias, t_own, t_prev, bsz, seq):
    n = qkv.shape[0]
    qpt = seq // TQ
    koff, voff = BRANCH_WIDTH // STEP_LANES, 2 * BRANCH_WIDTH // STEP_LANES
    q_tile = pl.BlockSpec((TQ, STEP_LANES), lambda b, p, i: (b * qpt + i, p))
    kv_seq = lambda off: pl.BlockSpec((seq, STEP_LANES), lambda b, p, i: (b, off + p))
    head_tables = pl.BlockSpec((STEP_HEADS, MOBA_BLOCK, MOBA_BLOCK), lambda b, p, i: (p, 0, 0))
    return pl.pallas_call(
        functools.partial(_moba_kernel, seq=seq),
        grid=(bsz, N_HEADS // STEP_HEADS, qpt),
        in_specs=[pl.BlockSpec(memory_space=pltpu.SMEM), q_tile, kv_seq(koff), kv_seq(voff),
                  head_tables, head_tables],
        out_specs=q_tile,
        out_shape=jax.ShapeDtypeStruct((n, BRANCH_WIDTH), BF16),
        scratch_shapes=[pltpu.VMEM((LANES, STEP_LANES), F32)] + _flash_scratch(seq),
        compiler_params=_params("arbitrary", "arbitrary", "arbitrary"),
        name="moba",
    )(rel_bias, qkv, qkv, qkv, t_own, t_prev)


def _merge_kernel(x_ref, g_ref, ya_ref, yb_ref, yc_ref, yd_ref, wg_ref, bg_ref, wbr_ref, wout_ref, o_ref):
    x = x_ref[...]
    hb = _rmsnorm(x, g_ref[...]).astype(BF16)
    merged = jnp.zeros((TM, D_MODEL), F32)
    for n, y_ref in enumerate((ya_ref, yb_ref, yc_ref, yd_ref)):
        gz = jnp.dot(hb, wg_ref[:, n * D_MODEL:(n + 1) * D_MODEL], preferred_element_type=F32)
        gate = jax.nn.sigmoid(gz + bg_ref[n:n + 1, :])
        merged = merged + gate * jnp.dot(y_ref[...], wbr_ref[n], preferred_element_type=F32)
    o_ref[...] = x + jnp.dot(merged.astype(BF16), wout_ref[...], preferred_element_type=F32)


def _merge(x, g, ys, wg, bg, wbr, wout, layer):
    n = x.shape[0]
    row = lambda w: pl.BlockSpec((TM, w), lambda i: (i, 0))
    return pl.pallas_call(
        _merge_kernel,
        grid=(n // TM,),
        in_specs=[row(D_MODEL), _resident(g.shape, layer)] + [row(BRANCH_WIDTH)] * N_BRANCH
        + [_resident(w.shape, layer) for w in (wg, bg, wbr, wout)],
        out_specs=row(D_MODEL),
        out_shape=jax.ShapeDtypeStruct((n, D_MODEL), F32),
        compiler_params=_params("arbitrary"),
        name="merge",
    )(x, g, *ys, wg, bg, wbr, wout)


def _ffn_kernel(x_ref, g_ref, wup_ref, cw_ref, cb_ref, wdown_ref, fg_ref, o_ref, halo_ref, buf_ref, *, final):
    @pl.when(pl.program_id(1) == 0)
    def _():
        halo_ref[...] = jnp.zeros_like(halo_ref)

    x = x_ref[...]
    hb = _rmsnorm(x, g_ref[...]).astype(BF16)
    acc = jnp.zeros((TM, D_MODEL), F32)
    n_chunks = D_FF // FF_CHUNK
    col_of = lambda c, part: part * D_FF + c * FF_CHUNK
    up = lambda c: [jnp.dot(hb, wup_ref[:, col_of(c, part):col_of(c, part) + FF_CHUNK],
                            preferred_element_type=F32) for part in range(2)]
    ahead = up(0)
    for c in range(n_chunks):
        pre, ahead = ahead, (up(c + 1) if c + 1 < n_chunks else None)
        halves = []
        for part in range(2):
            col = col_of(c, part)
            buf_ref[c % 2, part,0:SUBLANES, :] = halo_ref[:, col:col + FF_CHUNK]
            buf_ref[c % 2, part,SUBLANES:SUBLANES + TM, :] = pre[part]
            halo_ref[:, col:col + FF_CHUNK] = buf_ref[c % 2, part,TM:TM + SUBLANES, :]
            u = cb_ref[:, col:col + FF_CHUNK]
            for k in range(FFN_CONV_WIDTH):
                lo = SUBLANES - (FFN_CONV_WIDTH - 1) + k
                u = u + cw_ref[k:k + 1, col:col + FF_CHUNK] * buf_ref[c % 2, part, lo:lo + TM, :]
            halves.append(u)
        a = (halves[0] * jax.nn.sigmoid(halves[0])) * halves[1]
        acc = acc + jnp.dot(a.astype(BF16), wdown_ref[c * FF_CHUNK:(c + 1) * FF_CHUNK, :],
                            preferred_element_type=F32)
    y = x + acc
    o_ref[...] = _rmsnorm(y, fg_ref[...]) if final else y


def _ffn(x, g, wup, cw, cb, wdown, fg, bsz, seq, layer, final):
    n = x.shape[0]
    spt = seq // TM
    row = pl.BlockSpec((TM, D_MODEL), lambda bi, i: (bi * spt + i, 0))
    return pl.pallas_call(
        functools.partial(_ffn_kernel, final=final),
        grid=(bsz, spt),
        in_specs=[row] + [_resident(w.shape, layer) for w in (g, wup, cw, cb, wdown)]
        + [_resident(fg.shape)],
        out_specs=row,
        out_shape=jax.ShapeDtypeStruct((n, D_MODEL), F32),
        scratch_shapes=[pltpu.VMEM((SUBLANES, 2 * D_FF), F32),
                        pltpu.VMEM((2, 2, TM + SUBLANES, FF_CHUNK), F32)],
        compiler_params=_params("arbitrary", "arbitrary"),
        name="ffn",
    )(x, g, wup, cw, cb, wdown, fg)


def _split_w_in(w_in):
    hd, bw = HEAD_DIM, BRANCH_WIDTH
    o = 0
    qa = w_in[..., o:o + bw]; o += bw
    ka = w_in[..., o:o + SWA_KV_HEADS * hd]; o += SWA_KV_HEADS * hd
    va = w_in[..., o:o + SWA_KV_HEADS * hd]; o += SWA_KV_HEADS * hd
    ub = w_in[..., o:o + 2 * bw]; o += 2 * bw
    qkv_c = w_in[..., o:o + 3 * bw]; o += 3 * bw
    fc = w_in[..., o:o + N_HEADS]; o += N_HEADS
    qkv_d = w_in[..., o:o + 3 * bw]; o += 3 * bw
    gates = w_in[..., o:o + N_BRANCH * D_MODEL]
    dup = lambda w: jnp.concatenate([w[..., i * hd:(i + 1) * hd] for i in (0, 0, 1, 1)], axis=-1)
    wa = jnp.concatenate([qa, dup(ka), dup(va)], axis=-1)
    wf = jnp.pad(fc, ((0, 0), (0, 0), (0, LANES - N_HEADS)))
    return tuple(w.astype(BF16) for w in (wa, ub, qkv_c, wf, qkv_d)), gates.astype(BF16)


def _layer(x, bsz, seq, layer, final, biases, rel_bias, p):
    bias_swa, table_own, table_prev = biases
    za, zb, zc, zf, zd = _inproj(x, p["ln1_g"], p["w_in_groups"], layer)
    ya = _swa(za, p["sinks"], bias_swa, seq, layer)
    yb = _conformer(zb, p["conv_w"], p["conv_b"], p["conv_ln_g"], p["conv_ln_b"], bsz, seq, layer)
    kx, qx = _fgate(zf, p["b_fgate"], bsz, seq, layer)
    yc = _fox(zc, kx, qx, bsz, seq)
    yd = _moba(zd, rel_bias, table_own, table_prev, bsz, seq)
    x = _merge(x, p["ln1_g"], (ya, yb, yc, yd), p["w_gates"], p["b_gate"], p["w_br"], p["w_out"], layer)
    return _ffn(x, p["ln2_g"], p["w_up"], p["ffn_conv_w"], p["ffn_conv_b"], p["w_down"], p["final_g"],
                bsz, seq, layer, final)


def kernel(x, ln1_g, w_in, b_gate, b_fgate, sinks, conv_w, conv_b, conv_ln_g, conv_ln_b, w_br, w_out,
           ln2_g, w_up, ffn_conv_w, ffn_conv_b, w_down, rel_bias, final_g):
    bsz, seq, _ = x.shape
    depth = w_in.shape[0]
    assert seq % TM == 0 and seq % KV_ROWS == 0 and TQ == TK == MOBA_BLOCK
    assert MOBA_SEL_LANE + seq // MOBA_BLOCK <= HEAD_DIM
    biases = _expand_bias(rel_bias)
    rows = lambda v: v.reshape(depth, 1, -1)
    w_in_groups, w_gates = _split_w_in(w_in)
    params = dict(
        ln1_g=rows(ln1_g), w_in_groups=w_in_groups, w_gates=w_gates, b_gate=b_gate,
        b_fgate=rows(jnp.pad(b_fgate, ((0, 0), (0, LANES - N_HEADS)))), sinks=sinks,
        conv_w=conv_w, conv_b=rows(conv_b), conv_ln_g=rows(conv_ln_g), conv_ln_b=rows(conv_ln_b),
        w_br=w_br.astype(BF16), w_out=w_out.astype(BF16), ln2_g=rows(ln2_g), w_up=w_up.astype(BF16),
        ffn_conv_w=ffn_conv_w, ffn_conv_b=rows(ffn_conv_b), w_down=w_down.astype(BF16),
        final_g=final_g.reshape(1, -1))
    y = x.reshape(bsz * seq, D_MODEL)
    for layer in range(depth):
        y = _layer(y, bsz, seq, layer, layer == depth - 1, biases, rel_bias, params)
    return y.reshape(bsz, seq, D_MODEL)
```

```python
import functools
import math

import jax
import jax.numpy as jnp
import numpy as np
from jax import lax
from jax.experimental import pallas as pl
from jax.experimental.pallas import tpu as pltpu

D_MODEL = 1024
HEAD_DIM = 64
N_BRANCH = 4
BRANCH_WIDTH = 512
N_HEADS = 8
N_PAIRS = N_HEADS // 2
SWA_KV_HEADS = 2
SWA_BLOCK = 128
CONV_WIDTH = 31
MOBA_BLOCK = 256
MOBA_TOPK = 3
D_FF = 2816
FFN_CONV_WIDTH = 3
REL_BUCKETS = 32
REL_MAX_DIST = 128
EPS = 1e-6
NEG = -1e30
SCALE = HEAD_DIM ** -0.5
LOG2E = math.log2(math.e)

LANES = 128
SUBLANES = 8
VMEM_LIMIT = 56 * 1024 * 1024

TM = 512
TQ = 256
TK = 256
CHUNK_TILES = 16
SCORE_LOOKAHEAD = 3
KV_ROWS = 512
CUM_T = 256
FF_CHUNK = 256
CONV_ROWS = 64
CONV_HALO = 32

N_PIECES = 3
PIECE_ROWS = SUBLANES * N_PIECES
MOBA_SEL_LANE = SUBLANES
STEP_HEADS = 2
STEP_LANES = STEP_HEADS * HEAD_DIM
V_ROWS = HEAD_DIM + 16

BF16 = jnp.bfloat16
F32 = jnp.float32


def _params(*sem):
    return pltpu.CompilerParams(dimension_semantics=sem, vmem_limit_bytes=VMEM_LIMIT)


def _resident(shape, layer=None):
    if layer is None:
        zeros = (0,) * len(shape)
        return pl.BlockSpec(shape, lambda *_: zeros, pipeline_mode=pl.Buffered(1))
    index = (layer,) + (0,) * (len(shape) - 1)
    return pl.BlockSpec((None,) + tuple(shape[1:]), lambda *_: index, pipeline_mode=pl.Buffered(1))


def _rmsnorm(x, g):
    return (x * lax.rsqrt(jnp.mean(x * x, axis=-1, keepdims=True) + EPS)) * g


def _pieces(x):
    hi = x.astype(BF16).astype(F32)
    rest = x - hi
    mid = rest.astype(BF16).astype(F32)
    return hi, mid, rest - mid


def _rel_bucket_np(dist):
    n = np.maximum(dist, 0)
    max_exact = REL_BUCKETS // 2
    nf = np.maximum(n, 1).astype(np.float64)
    large = max_exact + (np.log(nf / max_exact) / math.log(REL_MAX_DIST / max_exact)
                         * (REL_BUCKETS - max_exact) + 1e-9).astype(np.int32)
    large = np.minimum(large, REL_BUCKETS - 1)
    return np.where(n < max_exact, n, large).astype(np.int32)


def _bias_kernel(tab_ref, bs_ref, bo_ref, bp_ref, os_ref, oo_ref, op_ref):
    os_ref[...] = jnp.zeros_like(os_ref)
    oo_ref[...] = jnp.zeros_like(oo_ref)
    op_ref[...] = jnp.zeros_like(op_ref)
    bs, bo, bp = bs_ref[...], bo_ref[...], bp_ref[...]

    def body(b, carry):
        ms, mo, mp = bs == b, bo == b, bp == b
        for h in range(N_HEADS):
            os_ref[h] = jnp.where(ms, tab_ref[b, h], os_ref[h])
            oo_ref[h] = jnp.where(mo, tab_ref[b, N_HEADS + h], oo_ref[h])
            op_ref[h] = jnp.where(mp, tab_ref[b, N_HEADS + h], op_ref[h])
        return carry

    lax.fori_loop(0, REL_BUCKETS, body, 0)
    causal = (lax.broadcasted_iota(jnp.int32, (MOBA_BLOCK, MOBA_BLOCK), 0)
              <= lax.broadcasted_iota(jnp.int32, (MOBA_BLOCK, MOBA_BLOCK), 1))
    for h in range(N_HEADS):
        far = tab_ref[REL_BUCKETS - 1, N_HEADS + h]
        oo_ref[h] = jnp.where(causal, (oo_ref[h] - far) * LOG2E, NEG)
        op_ref[h] = (op_ref[h] - far) * LOG2E


def _expand_bias(rel_bias):
    qi = np.arange(SWA_BLOCK)[:, None]
    kj = np.arange(2 * SWA_BLOCK)[None, :]
    b_swa = _rel_bucket_np(qi + SWA_BLOCK - kj)
    qi = np.arange(MOBA_BLOCK)[:, None]
    kj = np.arange(MOBA_BLOCK)[None, :]
    b_own = np.ascontiguousarray(_rel_bucket_np(qi - kj).T)
    b_prev = np.ascontiguousarray(_rel_bucket_np(qi - kj + MOBA_BLOCK).T)
    vm = pl.BlockSpec(memory_space=pltpu.VMEM)
    return pl.pallas_call(
        _bias_kernel,
        out_shape=(jax.ShapeDtypeStruct((N_HEADS, SWA_BLOCK, 2 * SWA_BLOCK), F32),
                   jax.ShapeDtypeStruct((N_HEADS, MOBA_BLOCK, MOBA_BLOCK), F32),
                   jax.ShapeDtypeStruct((N_HEADS, MOBA_BLOCK, MOBA_BLOCK), F32)),
        in_specs=[pl.BlockSpec(memory_space=pltpu.SMEM), vm, vm, vm],
        out_specs=(vm, vm, vm),
        name="bias_expand",
    )(rel_bias, jnp.asarray(b_swa), jnp.asarray(b_own), jnp.asarray(b_prev))


def _inproj_kernel(x_ref, g_ref, wa_ref, wb_ref, wc_ref, wf_ref, wd_ref,
                   oa_ref, ob_ref, oc_ref, of_ref, od_ref):
    h = _rmsnorm(x_ref[...], g_ref[...])
    hb = h.astype(BF16)
    oa_ref[...] = jnp.dot(hb, wa_ref[...], preferred_element_type=F32).astype(BF16)
    ob_ref[...] = jnp.dot(hb, wb_ref[...], preferred_element_type=F32)
    oc_ref[...] = jnp.dot(hb, wc_ref[...], preferred_element_type=F32).astype(BF16)
    od_ref[...] = jnp.dot(hb, wd_ref[...], preferred_element_type=F32).astype(BF16)
    of_ref[...] = jnp.dot(hb, wf_ref[...], preferred_element_type=F32)


def _inproj(x, g, ws, layer):
    n = x.shape[0]
    row = lambda w: pl.BlockSpec((TM, w), lambda i: (i, 0))
    dts = (BF16, F32, BF16, F32, BF16)
    return pl.pallas_call(
        _inproj_kernel,
        grid=(n // TM,),
        in_specs=[row(D_MODEL), _resident(g.shape, layer)] + [_resident(w.shape, layer) for w in ws],
        out_specs=tuple(row(w.shape[-1]) for w in ws),
        out_shape=tuple(jax.ShapeDtypeStruct((n, w.shape[-1]), dt) for w, dt in zip(ws, dts)),
        compiler_params=_params("arbitrary"),
        name="inproj",
    )(x, g, *ws)


def _swa_kernel(sink_ref, q_ref, kc_ref, kp_ref, vc_ref, vp_ref, bias_ref, o_ref, *, blocks_per_seq,
                layer):
    blk = pl.program_id(0) % blocks_per_seq
    kk = jnp.concatenate([kp_ref[...], kc_ref[...]], axis=0)
    vv = jnp.concatenate([vp_ref[...], vc_ref[...]], axis=0)
    qi = lax.broadcasted_iota(jnp.int32, (SWA_BLOCK, 2 * SWA_BLOCK), 0)
    kj = lax.broadcasted_iota(jnp.int32, (SWA_BLOCK, 2 * SWA_BLOCK), 1)
    dist = qi + SWA_BLOCK - kj
    mask = (dist >= 0) & (dist < SWA_BLOCK) & ((blk > 0) | (kj >= SWA_BLOCK))
    lane = lax.broadcasted_iota(jnp.int32, (SWA_BLOCK, LANES), 1)
    upper = lane >= HEAD_DIM
    kv_of = lambda h: h // (N_HEADS // SWA_KV_HEADS)
    scores = []
    for h in range(N_HEADS):
        q_pair = q_ref[:, (h // 2) * LANES:(h // 2 + 1) * LANES] * SCALE
        q_h = jnp.where(upper == (h % 2 == 1), q_pair, jnp.zeros_like(q_pair))
        k_kv = kk[:, kv_of(h) * LANES:(kv_of(h) + 1) * LANES]
        scores.append(lax.dot_general(q_h, k_kv, (((1,), (1,)), ((), ())), preferred_element_type=F32))
    probs = []
    for h in range(N_HEADS):
        logits = jnp.where(mask, scores[h] + bias_ref[h], NEG)
        sink = sink_ref[layer, h]
        m = jnp.maximum(jnp.max(logits, axis=-1, keepdims=True), sink)
        p = jnp.exp(logits - m)
        denom = jnp.sum(p, axis=-1, keepdims=True) + jnp.exp(sink - m)
        probs.append((p / denom).astype(BF16))
    outs = [jnp.dot(probs[h], vv[:, kv_of(h) * LANES:(kv_of(h) + 1) * LANES], preferred_element_type=F32)
            for h in range(N_HEADS)]
    for pair in range(N_PAIRS):
        o_ref[:, pair * LANES:(pair + 1) * LANES] = jnp.where(
            upper, outs[2 * pair + 1], outs[2 * pair]).astype(BF16)


def _swa(qkv, sinks, bias, seq, layer):
    n = qkv.shape[0]
    bps = seq // SWA_BLOCK
    kcol, vcol = BRANCH_WIDTH // (2 * LANES), BRANCH_WIDTH // (2 * LANES) + 1
    prev = lambda i: jnp.where(i % bps == 0, i, i - 1)
    return pl.pallas_call(
        functools.partial(_swa_kernel, blocks_per_seq=bps, layer=layer),
        grid=(n // SWA_BLOCK,),
        in_specs=[pl.BlockSpec(memory_space=pltpu.SMEM),
                  pl.BlockSpec((SWA_BLOCK, BRANCH_WIDTH), lambda i: (i, 0)),
                  pl.BlockSpec((SWA_BLOCK, 2 * LANES), lambda i: (i, kcol)),
                  pl.BlockSpec((SWA_BLOCK, 2 * LANES), lambda i: (prev(i), kcol)),
                  pl.BlockSpec((SWA_BLOCK, 2 * LANES), lambda i: (i, vcol)),
                  pl.BlockSpec((SWA_BLOCK, 2 * LANES), lambda i: (prev(i), vcol)),
                  _resident(bias.shape)],
        out_specs=pl.BlockSpec((SWA_BLOCK, BRANCH_WIDTH), lambda i: (i, 0)),
        out_shape=jax.ShapeDtypeStruct((n, BRANCH_WIDTH), BF16),
        compiler_params=_params("arbitrary"),
        name="swa",
    )(sinks, qkv, qkv, qkv, qkv, qkv, bias)


def _conformer_kernel(u_ref, w_ref, b_ref, g_ref, beta_ref, o_ref, buf_ref, y_ref):
    @pl.when(pl.program_id(1) == 0)
    def _():
        buf_ref[0:CONV_HALO, :] = jnp.zeros((CONV_HALO, BRANCH_WIDTH), F32)

    a = u_ref[:, 0:BRANCH_WIDTH]
    gt = u_ref[:, BRANCH_WIDTH:2 * BRANCH_WIDTH]
    buf_ref[CONV_HALO:CONV_HALO + TM, :] = a * jax.nn.sigmoid(gt)
    n_shifts = SUBLANES
    for r in range(TM // CONV_ROWS):
        row0 = r * CONV_ROWS
        for cg in range(BRANCH_WIDTH // LANES):
            cols = slice(cg * LANES, (cg + 1) * LANES)
            slabs = []
            for step in range((CONV_WIDTH - 1) // n_shifts + 1):
                lo = CONV_HALO - n_shifts * (step + 1) + row0
                slabs.append(buf_ref[lo:lo + CONV_ROWS + n_shifts, cols])
            y = None
            for shift in range(n_shifts):
                z = None
                for step in range((CONV_WIDTH - 1 - shift) // n_shifts + 1):
                    k = CONV_WIDTH - 1 - (n_shifts * step + shift)
                    term = w_ref[k:k + 1, cols] * slabs[step]
                    z = term if z is None else z + term
                shifted = z[n_shifts - shift:n_shifts - shift + CONV_ROWS]
                y = shifted if y is None else y + shifted
            y_ref[row0:row0 + CONV_ROWS, cols] = y
        y = y_ref[row0:row0 + CONV_ROWS, :] + b_ref[...]
        mu = jnp.mean(y, axis=-1, keepdims=True)
        yc = y - mu
        var = jnp.mean(yc * yc, axis=-1, keepdims=True)
        z = yc * lax.rsqrt(var + EPS) * g_ref[...] + beta_ref[...]
        o_ref[r * CONV_ROWS:(r + 1) * CONV_ROWS, :] = (z * jax.nn.sigmoid(z)).astype(BF16)
    buf_ref[0:CONV_HALO, :] = buf_ref[TM:TM + CONV_HALO, :]


def _conformer(u, w, b, g, beta, bsz, seq, layer):
    n = u.shape[0]
    spt = seq // TM
    return pl.pallas_call(
        _conformer_kernel,
        grid=(bsz, spt),
        in_specs=[pl.BlockSpec((TM, 2 * BRANCH_WIDTH), lambda bi, i: (bi * spt + i, 0)),
                  _resident(w.shape, layer), _resident(b.shape, layer), _resident(g.shape, layer),
                  _resident(beta.shape, layer)],
        out_specs=pl.BlockSpec((TM, BRANCH_WIDTH), lambda bi, i: (bi * spt + i, 0)),
        out_shape=jax.ShapeDtypeStruct((n, BRANCH_WIDTH), BF16),
        scratch_shapes=[pltpu.VMEM((TM + CONV_HALO, BRANCH_WIDTH), F32),
                        pltpu.VMEM((TM, BRANCH_WIDTH), F32)],
        compiler_params=_params("arbitrary", "arbitrary"),
        name="conformer",
    )(u, w, b, g, beta)


def _extra_base(h):
    return (h // 2) * LANES + HEAD_DIM * (1 - h % 2)


def _fgate_placement():
    pk = np.zeros((LANES, BRANCH_WIDTH), np.float32)
    pq = np.zeros((LANES, BRANCH_WIDTH), np.float32)
    for h in range(N_HEADS):
        base = _extra_base(h)
        for i in range(N_PIECES):
            pk[SUBLANES * i + h, base + i] = 1.0
            pq[PIECE_ROWS, base + i] = -1.0
            pk[PIECE_ROWS, base + N_PIECES + i] = 1.0
            pq[SUBLANES * i + h, base + N_PIECES + i] = 1.0
    return jnp.asarray(pk, BF16), jnp.asarray(pq, BF16)


def _fgate_kernel(f_ref, b_ref, pk_ref, pq_ref, kx_ref, qx_ref, carry_ref):
    @pl.when(pl.program_id(1) == 0)
    def _():
        carry_ref[...] = jnp.zeros_like(carry_ref)

    z = f_ref[...] + b_ref[...]
    log_f = -(jnp.maximum(-z, 0.0) + jnp.log1p(jnp.exp(-jnp.abs(z))))
    ri = lax.broadcasted_iota(jnp.int32, (CUM_T, CUM_T), 0)
    ci = lax.broadcasted_iota(jnp.int32, (CUM_T, CUM_T), 1)
    tri = (ci <= ri).astype(F32)
    c = jnp.dot(tri, log_f, preferred_element_type=F32,
                precision=lax.Precision.HIGHEST) + carry_ref[0:1, :]
    carry_ref[...] = jnp.broadcast_to(c[CUM_T - 1:CUM_T, :], carry_ref.shape)
    hi, mid, lo = _pieces(c * LOG2E)
    lane = lax.broadcasted_iota(jnp.int32, (CUM_T, LANES), 1)
    packed = jnp.where(lane < SUBLANES, hi,
                       jnp.where(lane < 2 * SUBLANES, pltpu.roll(mid, SUBLANES, axis=1),
                                 jnp.where(lane < PIECE_ROWS, pltpu.roll(lo, 2 * SUBLANES, axis=1),
                                           jnp.where(lane == PIECE_ROWS, 1.0, 0.0)))).astype(BF16)
    kx_ref[...] = jnp.dot(packed, pk_ref[...], preferred_element_type=F32).astype(BF16)
    qx_ref[...] = jnp.dot(packed, pq_ref[...], preferred_element_type=F32).astype(BF16)


def _fgate(f, b, bsz, seq, layer):
    n = f.shape[0]
    spt = seq // CUM_T
    pk, pq = _fgate_placement()
    row = lambda w: pl.BlockSpec((CUM_T, w), lambda bi, i: (bi * spt + i, 0))
    return pl.pallas_call(
        _fgate_kernel,
        grid=(bsz, spt),
        in_specs=[row(LANES), _resident(b.shape, layer), _resident(pk.shape), _resident(pq.shape)],
        out_specs=(row(BRANCH_WIDTH), row(BRANCH_WIDTH)),
        out_shape=(jax.ShapeDtypeStruct((n, BRANCH_WIDTH), BF16),
                   jax.ShapeDtypeStruct((n, BRANCH_WIDTH), BF16)),
        scratch_shapes=[pltpu.VMEM((SUBLANES, LANES), F32)],
        compiler_params=_params("arbitrary", "arbitrary"),
        name="fgate_cumsum",
    )(f, b, pk, pq)


def _build_kv(k_ref, v_ref, extra_of, ka_ref, vt_ref, seq):
    lane = lax.broadcasted_iota(jnp.int32, (KV_ROWS, LANES), 1)
    lower = lane < HEAD_DIM
    ones_rows = jnp.where(lax.broadcasted_iota(jnp.int32, (V_ROWS - HEAD_DIM, KV_ROWS), 0) == 0, 1.0, 0.0)

    def body(i, carry):
        rows = pl.ds(pl.multiple_of(i * KV_ROWS, KV_ROWS), KV_ROWS)
        extra = extra_of(i, rows)
        for pair in range(STEP_HEADS // 2):
            lanes = slice(pair * LANES, (pair + 1) * LANES)
            k, x = k_ref[rows, lanes], extra[:, lanes]
            ka_ref[2 * pair, rows, :] = jnp.where(lower, k, x)
            ka_ref[2 * pair + 1, rows, :] = jnp.where(lower, x, k)
            v_t = v_ref[rows, lanes].astype(F32).T
            for hh in range(2):
                vt_ref[2 * pair + hh, :, rows] = jnp.concatenate(
                    [v_t[hh * HEAD_DIM:(hh + 1) * HEAD_DIM], ones_rows], axis=0).astype(BF16)
        return carry

    lax.fori_loop(0, seq // KV_ROWS, body, 0)


def _extend_q_t(q_t, extras_t):
    out = []
    for hh in range(STEP_HEADS):
        q_h = q_t[hh * HEAD_DIM:(hh + 1) * HEAD_DIM]
        parts = [q_h, extras_t[hh]] if hh % 2 == 0 else [extras_t[hh], q_h]
        out.append(jnp.concatenate(parts, axis=0).astype(BF16))
    return out


def _flash_step(qa_t, ka_ref, vt_ref, m_ref, acc_ref, s_ref,start, n_tiles, tables=None, first=False):
    heads = range(STEP_HEADS)
    m = [None if first else m_ref[hh, 0:1, :] for hh in heads]
    acc = [None if first else acc_ref[hh] for hh in heads]
    n_plain = n_tiles - (len(tables(0)) if tables else 0)
    tile_rows = lambda t: pl.ds(pl.multiple_of(start + t * TK, TK), TK)
    ring = SCORE_LOOKAHEAD + 1

    def score(t):
        for hh in heads:
            s_ref[t % ring, hh] = jnp.dot(ka_ref[hh, tile_rows(t), :], qa_t[hh],
                                          preferred_element_type=F32)

    for t in range(min(SCORE_LOOKAHEAD, n_tiles)):
        score(t)
    for t in range(n_tiles):
        rows = tile_rows(t)
        if t + SCORE_LOOKAHEAD < n_tiles:
            score(t + SCORE_LOOKAHEAD)
        for hh in heads:
            s = s_ref[t % ring, hh]
            if t >= n_plain:
                s = s + tables(hh)[t - n_plain]
            col_max = jnp.max(s, axis=0, keepdims=True)
            m_new = col_max if m[hh] is None else jnp.maximum(m[hh], col_max)
            p = jnp.exp2(s - m_new).astype(BF16)
            pv = jnp.dot(vt_ref[hh, :, rows], p, preferred_element_type=F32)
            acc[hh] = pv if acc[hh] is None else jnp.exp2(m[hh] - m_new) * acc[hh] + pv
            m[hh] = m_new
    for hh in heads:
        m_ref[hh] = jnp.broadcast_to(m[hh], (SUBLANES, TQ))
        acc_ref[hh] = acc[hh]


def _pair_output(o_ref, acc_ref):
    out_t = jnp.concatenate(
        [acc_ref[hh, 0:HEAD_DIM, :] / acc_ref[hh, HEAD_DIM:HEAD_DIM + 1, :] for hh in range(STEP_HEADS)],
        axis=0)
    o_ref[...] = out_t.T.astype(BF16)


def _flash_scratch(seq):
    return [pltpu.VMEM((STEP_HEADS, seq, LANES), BF16), pltpu.VMEM((STEP_HEADS, V_ROWS, seq), BF16),
            pltpu.VMEM((STEP_HEADS, SUBLANES, TQ), F32), pltpu.VMEM((STEP_HEADS, V_ROWS, TQ), F32),
            pltpu.VMEM((SCORE_LOOKAHEAD + 1, STEP_HEADS, TK, TQ), F32)]


def _flash_chunks(qa, ka_ref, vt_ref, m_ref, acc_ref, s_ref,n_chunks):
    def chunk(c, carry):
        _flash_step(qa, ka_ref, vt_ref, m_ref, acc_ref, s_ref,c * (CHUNK_TILES * TK), CHUNK_TILES)
        return carry

    lax.fori_loop(0, n_chunks, chunk, 0)


def _flash_tail(qa, ka_ref, vt_ref, m_ref, acc_ref, s_ref,last_tile, n_loose, tail_tables, guard=None):
    rem = n_loose % CHUNK_TILES
    for r in range(CHUNK_TILES):
        if isinstance(rem, int) and rem != r:
            continue
        tiles = r + len(tail_tables(0))
        cond = guard if isinstance(rem, int) else (rem == r if guard is None else (rem == r) & guard)

        @pl.when(cond)
        def _(tiles=tiles):
            _flash_step(qa, ka_ref, vt_ref, m_ref, acc_ref, s_ref,(last_tile + 1 - tiles) * TK, tiles,
                        tables=tail_tables, first=True)


def _fox_kernel(q_ref, qx_ref, k_ref, v_ref, kx_ref, o_ref, ka_ref, vt_ref, m_ref, acc_ref, s_ref,*, seq):
    qt = pl.program_id(2)

    @pl.when(qt == 0)
    def _():
        _build_kv(k_ref, v_ref, lambda i, rows: kx_ref[rows, :], ka_ref, vt_ref, seq)

    q_t = (q_ref[...].astype(F32) * (SCALE * LOG2E)).T
    qx_t = qx_ref[...].astype(F32).T
    qa = _extend_q_t(q_t, [qx_t[(hh ^ 1) * HEAD_DIM:((hh ^ 1) + 1) * HEAD_DIM] for hh in range(STEP_HEADS)])
    causal = jnp.where(lax.broadcasted_iota(jnp.int32, (TK, TQ), 0)
                       <= lax.broadcasted_iota(jnp.int32, (TK, TQ), 1), 0.0, NEG)
    _flash_tail(qa, ka_ref, vt_ref, m_ref, acc_ref, s_ref,qt, qt, lambda hh: [causal])
    _flash_chunks(qa, ka_ref, vt_ref, m_ref, acc_ref, s_ref,qt // CHUNK_TILES)
    _pair_output(o_ref, acc_ref)


def _fox(qkv, kx, qx, bsz, seq):
    n = qkv.shape[0]
    qpt = seq // TQ
    koff, voff = BRANCH_WIDTH // STEP_LANES, 2 * BRANCH_WIDTH // STEP_LANES
    q_tile = lambda off: pl.BlockSpec((TQ, STEP_LANES), lambda b, p, i: (b * qpt + i, off + p))
    kv_seq = lambda off: pl.BlockSpec((seq, STEP_LANES), lambda b, p, i: (b, off + p))
    return pl.pallas_call(
        functools.partial(_fox_kernel, seq=seq),
        grid=(bsz, N_HEADS // STEP_HEADS, qpt),
        in_specs=[q_tile(0), q_tile(0), kv_seq(koff), kv_seq(voff), kv_seq(0)],
        out_specs=q_tile(0),
        out_shape=jax.ShapeDtypeStruct((n, BRANCH_WIDTH), BF16),
        scratch_shapes=_flash_scratch(seq),
        compiler_params=_params("arbitrary", "arbitrary", "arbitrary"),
        name="fox",
    )(qkv, qx, qkv, qkv, kx)


def _moba_kernel(tab_ref, q_ref, k_ref, v_ref, town_ref, tprev_ref, o_ref,
                 kmean_ref, ka_ref, vt_ref, m_ref, acc_ref, s_ref,*, seq):
    group = pl.program_id(1)
    qt = pl.program_id(2)
    nblk = seq // MOBA_BLOCK
    blk_shift = MOBA_BLOCK.bit_length() - 1

    @pl.when(qt == 0)
    def _():
        blk = lax.broadcasted_iota(jnp.int32, (LANES, seq), 0)
        pos = lax.broadcasted_iota(jnp.int32, (LANES, seq), 1)
        member = jnp.where(jnp.right_shift(pos, blk_shift) == blk, 1.0 / MOBA_BLOCK, 0.0).astype(BF16)
        kmean_ref[...] = jnp.dot(member, k_ref[...], preferred_element_type=F32)

        def extra_of(i, rows):
            xl = jnp.bitwise_and(lax.broadcasted_iota(jnp.int32, (KV_ROWS, STEP_LANES), 1), HEAD_DIM - 1)
            row_blk = jnp.right_shift(
                i * KV_ROWS + lax.broadcasted_iota(jnp.int32, (KV_ROWS, STEP_LANES), 0), blk_shift)
            hit = (xl < N_PIECES) | (xl - MOBA_SEL_LANE == row_blk)
            return jnp.where(hit, 1.0, 0.0).astype(BF16)

        _build_kv(k_ref, v_ref, extra_of, ka_ref, vt_ref, seq)

    q_t = q_ref[...].astype(F32).T
    row_head = jnp.right_shift(lax.broadcasted_iota(jnp.int32, (STEP_LANES, TQ), 0),
                               HEAD_DIM.bit_length() - 1)
    blk = lax.broadcasted_iota(jnp.int32, (nblk, TQ), 0)
    piece_row = lax.broadcasted_iota(jnp.int32, (MOBA_SEL_LANE, TQ), 0)
    past = blk < qt
    extras = []
    for hh in range(STEP_HEADS):
        q_head = jnp.where(row_head == hh, q_t, 0.0)
        gate = jnp.dot(kmean_ref[0:nblk, :], q_head, preferred_element_type=F32,
                       precision=lax.Precision.HIGHEST)
        gate = jnp.where(past, gate, NEG)
        sel = blk == qt
        for _ in range(MOBA_TOPK):
            top = jnp.max(gate, axis=0, keepdims=True)
            idx = jnp.min(jnp.where(gate == top, blk, nblk), axis=0, keepdims=True)
            pick = blk == idx
            sel = sel | (pick & past)
            gate = jnp.where(pick, -jnp.inf, gate)
        far = jnp.full((MOBA_SEL_LANE, TQ), tab_ref[REL_BUCKETS - 1, N_HEADS + STEP_HEADS * group + hh] * LOG2E, F32)
        hi, mid, lo = _pieces(far)
        far_rows = jnp.where(piece_row == 0, hi, jnp.where(piece_row == 1, mid, jnp.where(piece_row == 2, lo, 0.0)))
        extras.append(jnp.concatenate(
            [far_rows, jnp.where(sel, 0.0, NEG),
             jnp.zeros((HEAD_DIM - MOBA_SEL_LANE - nblk, TQ), F32)], axis=0))

    qa = _extend_q_t(q_t * (SCALE * LOG2E), extras)
    n_far = jnp.maximum(qt - 1, 0)
    _flash_tail(qa, ka_ref, vt_ref, m_ref, acc_ref, s_ref,qt, 0, lambda hh: [town_ref[hh]], guard=qt == 0)
    _flash_tail(qa, ka_ref, vt_ref, m_ref, acc_ref, s_ref,qt, n_far,
                lambda hh: [tprev_ref[hh], town_ref[hh]], guard=qt >= 1)
    _flash_chunks(qa, ka_ref, vt_ref, m_ref, acc_ref, s_ref,n_far // CHUNK_TILES)
    _pair_output(o_ref, acc_ref)


def _moba(qkv, rel_bias, t_own, t_prev, bsz, seq):
    n = qkv.shape[0]
    qpt = seq // TQ
    koff, voff = BRANCH_WIDTH // STEP_LANES, 2 * BRANCH_WIDTH // STEP_LANES
    q_tile = pl.BlockSpec((TQ, STEP_LANES), lambda b, p, i: (b * qpt + i, p))
    kv_seq = lambda off: pl.BlockSpec((seq, STEP_LANES), lambda b, p, i: (b, off + p))
    head_tables = pl.BlockSpec((STEP_HEADS, MOBA_BLOCK, MOBA_BLOCK), lambda b, p, i: (p, 0, 0))
    return pl.pallas_call(
        functools.partial(_moba_kernel, seq=seq),
        grid=(bsz, N_HEADS // STEP_HEADS, qpt),
        in_specs=[pl.BlockSpec(memory_space=pltpu.SMEM), q_tile, kv_seq(koff), kv_seq(voff),
                  head_tables, head_tables],
        out_specs=q_tile,
        out_shape=jax.ShapeDtypeStruct((n, BRANCH_WIDTH), BF16),
        scratch_shapes=[pltpu.VMEM((LANES, STEP_LANES), F32)] + _flash_scratch(seq),
        compiler_params=_params("arbitrary", "arbitrary", "arbitrary"),
        name="moba",
    )(rel_bias, qkv, qkv, qkv, t_own, t_prev)


def _merge_kernel(x_ref, g_ref, ya_ref, yb_ref, yc_ref, yd_ref, wg_ref, bg_ref, wbr_ref, wout_ref, o_ref):
    x = x_ref[...]
    hb = _rmsnorm(x, g_ref[...]).astype(BF16)
    merged = jnp.zeros((TM, D_MODEL), F32)
    for n, y_ref in enumerate((ya_ref, yb_ref, yc_ref, yd_ref)):
        gz = jnp.dot(hb, wg_ref[:, n * D_MODEL:(n + 1) * D_MODEL], preferred_element_type=F32)
        gate = jax.nn.sigmoid(gz + bg_ref[n:n + 1, :])
        merged = merged + gate * jnp.dot(y_ref[...], wbr_ref[n], preferred_element_type=F32)
    o_ref[...] = x + jnp.dot(merged.astype(BF16), wout_ref[...], preferred_element_type=F32)


def _merge(x, g, ys, wg, bg, wbr, wout, layer):
    n = x.shape[0]
    row = lambda w: pl.BlockSpec((TM, w), lambda i: (i, 0))
    return pl.pallas_call(
        _merge_kernel,
        grid=(n // TM,),
        in_specs=[row(D_MODEL), _resident(g.shape, layer)] + [row(BRANCH_WIDTH)] * N_BRANCH
        + [_resident(w.shape, layer) for w in (wg, bg, wbr, wout)],
        out_specs=row(D_MODEL),
        out_shape=jax.ShapeDtypeStruct((n, D_MODEL), F32),
        compiler_params=_params("arbitrary"),
        name="merge",
    )(x, g, *ys, wg, bg, wbr, wout)


def _ffn_kernel(x_ref, g_ref, wup_ref, cw_ref, cb_ref, wdown_ref, fg_ref, o_ref, halo_ref, buf_ref, *, final):
    @pl.when(pl.program_id(1) == 0)
    def _():
        halo_ref[...] = jnp.zeros_like(halo_ref)

    x = x_ref[...]
    hb = _rmsnorm(x, g_ref[...]).astype(BF16)
    acc = jnp.zeros((TM, D_MODEL), F32)
    n_chunks = D_FF // FF_CHUNK
    col_of = lambda c, part: part * D_FF + c * FF_CHUNK
    up = lambda c: [jnp.dot(hb, wup_ref[:, col_of(c, part):col_of(c, part) + FF_CHUNK],
                            preferred_element_type=F32) for part in range(2)]
    ahead = up(0)
    for c in range(n_chunks):
        pre, ahead = ahead, (up(c + 1) if c + 1 < n_chunks else None)
        halves = []
        for part in range(2):
            col = col_of(c, part)
            buf_ref[c % 2, part,0:SUBLANES, :] = halo_ref[:, col:col + FF_CHUNK]
            buf_ref[c % 2, part,SUBLANES:SUBLANES + TM, :] = pre[part]
            halo_ref[:, col:col + FF_CHUNK] = buf_ref[c % 2, part,TM:TM + SUBLANES, :]
            u = cb_ref[:, col:col + FF_CHUNK]
            for k in range(FFN_CONV_WIDTH):
                lo = SUBLANES - (FFN_CONV_WIDTH - 1) + k
                u = u + cw_ref[k:k + 1, col:col + FF_CHUNK] * buf_ref[c % 2, part, lo:lo + TM, :]
            halves.append(u)
        a = (halves[0] * jax.nn.sigmoid(halves[0])) * halves[1]
        acc = acc + jnp.dot(a.astype(BF16), wdown_ref[c * FF_CHUNK:(c + 1) * FF_CHUNK, :],
                            preferred_element_type=F32)
    y = x + acc
    o_ref[...] = _rmsnorm(y, fg_ref[...]) if final else y


def _ffn(x, g, wup, cw, cb, wdown, fg, bsz, seq, layer, final):
    n = x.shape[0]
    spt = seq // TM
    row = pl.BlockSpec((TM, D_MODEL), lambda bi, i: (bi * spt + i, 0))
    return pl.pallas_call(
        functools.partial(_ffn_kernel, final=final),
        grid=(bsz, spt),
        in_specs=[row] + [_resident(w.shape, layer) for w in (g, wup, cw, cb, wdown)]
        + [_resident(fg.shape)],
        out_specs=row,
        out_shape=jax.ShapeDtypeStruct((n, D_MODEL), F32),
        scratch_shapes=[pltpu.VMEM((SUBLANES, 2 * D_FF), F32),
                        pltpu.VMEM((2, 2, TM + SUBLANES, FF_CHUNK), F32)],
        compiler_params=_params("arbitrary", "arbitrary"),
        name="ffn",
    )(x, g, wup, cw, cb, wdown, fg)


def _split_w_in(w_in):
    hd, bw = HEAD_DIM, BRANCH_WIDTH
    o = 0
    qa = w_in[..., o:o + bw]; o += bw
    ka = w_in[..., o:o + SWA_KV_HEADS * hd]; o += SWA_KV_HEADS * hd
    va = w_in[..., o:o + SWA_KV_HEADS * hd]; o += SWA_KV_HEADS * hd
    ub = w_in[..., o:o + 2 * bw]; o += 2 * bw
    qkv_c = w_in[..., o:o + 3 * bw]; o += 3 * bw
    fc = w_in[..., o:o + N_HEADS]; o += N_HEADS
    qkv_d = w_in[..., o:o + 3 * bw]; o += 3 * bw
    gates = w_in[..., o:o + N_BRANCH * D_MODEL]
    dup = lambda w: jnp.concatenate([w[..., i * hd:(i + 1) * hd] for i in (0, 0, 1, 1)], axis=-1)
    wa = jnp.concatenate([qa, dup(ka), dup(va)], axis=-1)
    wf = jnp.pad(fc, ((0, 0), (0, 0), (0, LANES - N_HEADS)))
    return tuple(w.astype(BF16) for w in (wa, ub, qkv_c, wf, qkv_d)), gates.astype(BF16)


def _layer(x, bsz, seq, layer, final, biases, rel_bias, p):
    bias_swa, table_own, table_prev = biases
    za, zb, zc, zf, zd = _inproj(x, p["ln1_g"], p["w_in_groups"], layer)
    ya = _swa(za, p["sinks"], bias_swa, seq, layer)
    yb = _conformer(zb, p["conv_w"], p["conv_b"], p["conv_ln_g"], p["conv_ln_b"], bsz, seq, layer)
    kx, qx = _fgate(zf, p["b_fgate"], bsz, seq, layer)
    yc = _fox(zc, kx, qx, bsz, seq)
    yd = _moba(zd, rel_bias, table_own, table_prev, bsz, seq)
    x = _merge(x, p["ln1_g"], (ya, yb, yc, yd), p["w_gates"], p["b_gate"], p["w_br"], p["w_out"], layer)
    return _ffn(x, p["ln2_g"], p["w_up"], p["ffn_conv_w"], p["ffn_conv_b"], p["w_down"], p["final_g"],
                bsz, seq, layer, final)


def kernel(x, ln1_g, w_in, b_gate, b_fgate, sinks, conv_w, conv_b, conv_ln_g, conv_ln_b, w_br, w_out,
           ln2_g, w_up, ffn_conv_w, ffn_conv_b, w_down, rel_bias, final_g):
    bsz, seq, _ = x.shape
    depth = w_in.shape[0]
    assert seq % TM == 0 and seq % KV_ROWS == 0 and TQ == TK == MOBA_BLOCK
    assert MOBA_SEL_LANE + seq // MOBA_BLOCK <= HEAD_DIM
    biases = _expand_bias(rel_bias)
    rows = lambda v: v.reshape(depth, 1, -1)
    w_in_groups, w_gates = _split_w_in(w_in)
    params = dict(
        ln1_g=rows(ln1_g), w_in_groups=w_in_groups, w_gates=w_gates, b_gate=b_gate,
        b_fgate=rows(jnp.pad(b_fgate, ((0, 0), (0, LANES - N_HEADS)))), sinks=sinks,
        conv_w=conv_w, conv_b=rows(conv_b), conv_ln_g=rows(conv_ln_g), conv_ln_b=rows(conv_ln_b),
        w_br=w_br.astype(BF16), w_out=w_out.astype(BF16), ln2_g=rows(ln2_g), w_up=w_up.astype(BF16),
        ffn_conv_w=ffn_conv_w, ffn_conv_b=rows(ffn_conv_b), w_down=w_down.astype(BF16),
        final_g=final_g.reshape(1, -1))
    y = x.reshape(bsz * seq, D_MODEL)
    for layer in range(depth):
        y = _layer(y, bsz, seq, layer, layer == depth - 1, biases, rel_bias, params)
    return y.reshape(bsz, seq, D_MODEL)
```

```python
import functools
import math

import jax
import jax.numpy as jnp
import numpy as np
from jax import lax
from jax.experimental import pallas as pl
from jax.experimental.pallas import tpu as pltpu

D_MODEL = 1024
HEAD_DIM = 64
N_BRANCH = 4
BRANCH_WIDTH = 512
N_HEADS = 8
N_PAIRS = N_HEADS // 2
SWA_KV_HEADS = 2
SWA_BLOCK = 128
CONV_WIDTH = 31
MOBA_BLOCK = 256
MOBA_TOPK = 3
D_FF = 2816
FFN_CONV_WIDTH = 3
REL_BUCKETS = 32
REL_MAX_DIST = 128
EPS = 1e-6
NEG = -1e30
SCALE = HEAD_DIM ** -0.5
LOG2E = math.log2(math.e)

LANES = 128
SUBLANES = 8
VMEM_LIMIT = 56 * 1024 * 1024

TM = 512
TQ = 256
TK = 256
CHUNK_TILES = 16
SCORE_LOOKAHEAD = 3
KV_ROWS = 512
SELECT_COLS = 1024
CUM_T = 256
FF_CHUNK = 256
CONV_ROWS = 64
CONV_HALO = 32

N_PIECES = 3
PIECE_ROWS = SUBLANES * N_PIECES
MOBA_SEL_LANE = SUBLANES
STEP_HEADS = 2
STEP_LANES = STEP_HEADS * HEAD_DIM
V_ROWS = HEAD_DIM + 16

BF16 = jnp.bfloat16
F32 = jnp.float32


def _params(*sem):
    return pltpu.CompilerParams(dimension_semantics=sem, vmem_limit_bytes=VMEM_LIMIT)


def _resident(shape, layer=None):
    if layer is None:
        zeros = (0,) * len(shape)
        return pl.BlockSpec(shape, lambda *_: zeros, pipeline_mode=pl.Buffered(1))
    index = (layer,) + (0,) * (len(shape) - 1)
    return pl.BlockSpec((None,) + tuple(shape[1:]), lambda *_: index, pipeline_mode=pl.Buffered(1))


def _rmsnorm(x, g):
    return (x * lax.rsqrt(jnp.mean(x * x, axis=-1, keepdims=True) + EPS)) * g


def _pieces(x):
    hi = x.astype(BF16).astype(F32)
    rest = x - hi
    mid = rest.astype(BF16).astype(F32)
    return hi, mid, rest - mid


def _rel_bucket_np(dist):
    n = np.maximum(dist, 0)
    max_exact = REL_BUCKETS // 2
    nf = np.maximum(n, 1).astype(np.float64)
    large = max_exact + (np.log(nf / max_exact) / math.log(REL_MAX_DIST / max_exact)
                         * (REL_BUCKETS - max_exact) + 1e-9).astype(np.int32)
    large = np.minimum(large, REL_BUCKETS - 1)
    return np.where(n < max_exact, n, large).astype(np.int32)


def _bias_kernel(tab_ref, bs_ref, bo_ref, bp_ref, os_ref, oo_ref, op_ref):
    os_ref[...] = jnp.zeros_like(os_ref)
    oo_ref[...] = jnp.zeros_like(oo_ref)
    op_ref[...] = jnp.zeros_like(op_ref)
    bs, bo, bp = bs_ref[...], bo_ref[...], bp_ref[...]

    def body(b, carry):
        ms, mo, mp = bs == b, bo == b, bp == b
        for h in range(N_HEADS):
            os_ref[h] = jnp.where(ms, tab_ref[b, h], os_ref[h])
            oo_ref[h] = jnp.where(mo, tab_ref[b, N_HEADS + h], oo_ref[h])
            op_ref[h] = jnp.where(mp, tab_ref[b, N_HEADS + h], op_ref[h])
        return carry

    lax.fori_loop(0, REL_BUCKETS, body, 0)
    causal = (lax.broadcasted_iota(jnp.int32, (MOBA_BLOCK, MOBA_BLOCK), 0)
              <= lax.broadcasted_iota(jnp.int32, (MOBA_BLOCK, MOBA_BLOCK), 1))
    for h in range(N_HEADS):
        far = tab_ref[REL_BUCKETS - 1, N_HEADS + h]
        oo_ref[h] = jnp.where(causal, (oo_ref[h] - far) * LOG2E, NEG)
        op_ref[h] = (op_ref[h] - far) * LOG2E


def _expand_bias(rel_bias):
    qi = np.arange(SWA_BLOCK)[:, None]
    kj = np.arange(2 * SWA_BLOCK)[None, :]
    b_swa = _rel_bucket_np(qi + SWA_BLOCK - kj)
    qi = np.arange(MOBA_BLOCK)[:, None]
    kj = np.arange(MOBA_BLOCK)[None, :]
    b_own = np.ascontiguousarray(_rel_bucket_np(qi - kj).T)
    b_prev = np.ascontiguousarray(_rel_bucket_np(qi - kj + MOBA_BLOCK).T)
    vm = pl.BlockSpec(memory_space=pltpu.VMEM)
    return pl.pallas_call(
        _bias_kernel,
        out_shape=(jax.ShapeDtypeStruct((N_HEADS, SWA_BLOCK, 2 * SWA_BLOCK), F32),
                   jax.ShapeDtypeStruct((N_HEADS, MOBA_BLOCK, MOBA_BLOCK), F32),
                   jax.ShapeDtypeStruct((N_HEADS, MOBA_BLOCK, MOBA_BLOCK), F32)),
        in_specs=[pl.BlockSpec(memory_space=pltpu.SMEM), vm, vm, vm],
        out_specs=(vm, vm, vm),
        name="bias_expand",
    )(rel_bias, jnp.asarray(b_swa), jnp.asarray(b_own), jnp.asarray(b_prev))


def _inproj_kernel(x_ref, g_ref, wa_ref, wb_ref, wc_ref, wf_ref, wd_ref,
                   oa_ref, ob_ref, oc_ref, of_ref, od_ref):
    h = _rmsnorm(x_ref[...], g_ref[...])
    hb = h.astype(BF16)
    oa_ref[...] = jnp.dot(hb, wa_ref[...], preferred_element_type=F32).astype(BF16)
    ob_ref[...] = jnp.dot(hb, wb_ref[...], preferred_element_type=F32)
    oc_ref[...] = jnp.dot(hb, wc_ref[...], preferred_element_type=F32).astype(BF16)
    od_ref[...] = jnp.dot(hb, wd_ref[...], preferred_element_type=F32).astype(BF16)
    of_ref[...] = jnp.dot(hb, wf_ref[...], preferred_element_type=F32)


def _inproj(x, g, ws, layer):
    n = x.shape[0]
    row = lambda w: pl.BlockSpec((TM, w), lambda i: (i, 0))
    dts = (BF16, F32, BF16, F32, BF16)
    return pl.pallas_call(
        _inproj_kernel,
        grid=(n // TM,),
        in_specs=[row(D_MODEL), _resident(g.shape, layer)] + [_resident(w.shape, layer) for w in ws],
        out_specs=tuple(row(w.shape[-1]) for w in ws),
        out_shape=tuple(jax.ShapeDtypeStruct((n, w.shape[-1]), dt) for w, dt in zip(ws, dts)),
        compiler_params=_params("arbitrary"),
        name="inproj",
    )(x, g, *ws)


def _swa_kernel(sink_ref, q_ref, kc_ref, kp_ref, vc_ref, vp_ref, bias_ref, o_ref, s_ref, *,
                blocks_per_seq, layer):
    blk = pl.program_id(0) % blocks_per_seq
    kk = jnp.concatenate([kp_ref[...], kc_ref[...]], axis=0)
    vv = jnp.concatenate([vp_ref[...], vc_ref[...]], axis=0)
    qi = lax.broadcasted_iota(jnp.int32, (SWA_BLOCK, 2 * SWA_BLOCK), 0)
    kj = lax.broadcasted_iota(jnp.int32, (SWA_BLOCK, 2 * SWA_BLOCK), 1)
    dist = qi + SWA_BLOCK - kj
    mask = (dist >= 0) & (dist < SWA_BLOCK) & ((blk > 0) | (kj >= SWA_BLOCK))
    lane = lax.broadcasted_iota(jnp.int32, (SWA_BLOCK, LANES), 1)
    upper = lane >= HEAD_DIM
    kv_of = lambda h: h // (N_HEADS // SWA_KV_HEADS)
    for h in range(N_HEADS):
        q_pair = q_ref[:, (h // 2) * LANES:(h // 2 + 1) * LANES] * SCALE
        q_h = jnp.where(upper == (h % 2 == 1), q_pair, jnp.zeros_like(q_pair))
        k_kv = kk[:, kv_of(h) * LANES:(kv_of(h) + 1) * LANES]
        s_ref[h] = lax.dot_general(q_h, k_kv, (((1,), (1,)), ((), ())), preferred_element_type=F32)
    probs = []
    for h in range(N_HEADS):
        logits = jnp.where(mask, s_ref[h] + bias_ref[h], NEG)
        sink = sink_ref[layer, h]
        m = jnp.maximum(jnp.max(logits, axis=-1, keepdims=True), sink)
        p = jnp.exp(logits - m)
        denom = jnp.sum(p, axis=-1, keepdims=True) + jnp.exp(sink - m)
        probs.append((p / denom).astype(BF16))
    outs = [jnp.dot(probs[h], vv[:, kv_of(h) * LANES:(kv_of(h) + 1) * LANES], preferred_element_type=F32)
            for h in range(N_HEADS)]
    for pair in range(N_PAIRS):
        o_ref[:, pair * LANES:(pair + 1) * LANES] = jnp.where(
            upper, outs[2 * pair + 1], outs[2 * pair]).astype(BF16)


def _swa(qkv, sinks, bias, seq, layer):
    n = qkv.shape[0]
    bps = seq // SWA_BLOCK
    kcol, vcol = BRANCH_WIDTH // (2 * LANES), BRANCH_WIDTH // (2 * LANES) + 1
    prev = lambda i: jnp.where(i % bps == 0, i, i - 1)
    return pl.pallas_call(
        functools.partial(_swa_kernel, blocks_per_seq=bps, layer=layer),
        grid=(n // SWA_BLOCK,),
        in_specs=[pl.BlockSpec(memory_space=pltpu.SMEM),
                  pl.BlockSpec((SWA_BLOCK, BRANCH_WIDTH), lambda i: (i, 0)),
                  pl.BlockSpec((SWA_BLOCK, 2 * LANES), lambda i: (i, kcol)),
                  pl.BlockSpec((SWA_BLOCK, 2 * LANES), lambda i: (prev(i), kcol)),
                  pl.BlockSpec((SWA_BLOCK, 2 * LANES), lambda i: (i, vcol)),
                  pl.BlockSpec((SWA_BLOCK, 2 * LANES), lambda i: (prev(i), vcol)),
                  _resident(bias.shape)],
        out_specs=pl.BlockSpec((SWA_BLOCK, BRANCH_WIDTH), lambda i: (i, 0)),
        out_shape=jax.ShapeDtypeStruct((n, BRANCH_WIDTH), BF16),
        scratch_shapes=[pltpu.VMEM((N_HEADS, SWA_BLOCK, 2 * SWA_BLOCK), F32)],
        compiler_params=_params("arbitrary"),
        name="swa",
    )(sinks, qkv, qkv, qkv, qkv, qkv, bias)


def _conformer_kernel(u_ref, w_ref, b_ref, g_ref, beta_ref, o_ref, buf_ref, y_ref):
    @pl.when(pl.program_id(1) == 0)
    def _():
        buf_ref[0:CONV_HALO, :] = jnp.zeros((CONV_HALO, BRANCH_WIDTH), F32)

    a = u_ref[:, 0:BRANCH_WIDTH]
    gt = u_ref[:, BRANCH_WIDTH:2 * BRANCH_WIDTH]
    buf_ref[CONV_HALO:CONV_HALO + TM, :] = a * jax.nn.sigmoid(gt)
    n_shifts = SUBLANES
    for r in range(TM // CONV_ROWS):
        row0 = r * CONV_ROWS
        for cg in range(BRANCH_WIDTH // LANES):
            cols = slice(cg * LANES, (cg + 1) * LANES)
            slabs = []
            for step in range((CONV_WIDTH - 1) // n_shifts + 1):
                lo = CONV_HALO - n_shifts * (step + 1) + row0
                slabs.append(buf_ref[lo:lo + CONV_ROWS + n_shifts, cols])
            y = None
            for shift in range(n_shifts):
                z = None
                for step in range((CONV_WIDTH - 1 - shift) // n_shifts + 1):
                    k = CONV_WIDTH - 1 - (n_shifts * step + shift)
                    term = w_ref[k:k + 1, cols] * slabs[step]
                    z = term if z is None else z + term
                shifted = z[n_shifts - shift:n_shifts - shift + CONV_ROWS]
                y = shifted if y is None else y + shifted
            y_ref[row0:row0 + CONV_ROWS, cols] = y
        y = y_ref[row0:row0 + CONV_ROWS, :] + b_ref[...]
        mu = jnp.mean(y, axis=-1, keepdims=True)
        yc = y - mu
        var = jnp.mean(yc * yc, axis=-1, keepdims=True)
        z = yc * lax.rsqrt(var + EPS) * g_ref[...] + beta_ref[...]
        o_ref[r * CONV_ROWS:(r + 1) * CONV_ROWS, :] = (z * jax.nn.sigmoid(z)).astype(BF16)
    buf_ref[0:CONV_HALO, :] = buf_ref[TM:TM + CONV_HALO, :]


def _conformer(u, w, b, g, beta, bsz, seq, layer):
    n = u.shape[0]
    spt = seq // TM
    return pl.pallas_call(
        _conformer_kernel,
        grid=(bsz, spt),
        in_specs=[pl.BlockSpec((TM, 2 * BRANCH_WIDTH), lambda bi, i: (bi * spt + i, 0)),
                  _resident(w.shape, layer), _resident(b.shape, layer), _resident(g.shape, layer),
                  _resident(beta.shape, layer)],
        out_specs=pl.BlockSpec((TM, BRANCH_WIDTH), lambda bi, i: (bi * spt + i, 0)),
        out_shape=jax.ShapeDtypeStruct((n, BRANCH_WIDTH), BF16),
        scratch_shapes=[pltpu.VMEM((TM + CONV_HALO, BRANCH_WIDTH), F32),
                        pltpu.VMEM((TM, BRANCH_WIDTH), F32)],
        compiler_params=_params("arbitrary", "arbitrary"),
        name="conformer",
    )(u, w, b, g, beta)


def _extra_base(h):
    return (h // 2) * LANES + HEAD_DIM * (1 - h % 2)


def _fgate_placement():
    pk = np.zeros((LANES, BRANCH_WIDTH), np.float32)
    pq = np.zeros((LANES, BRANCH_WIDTH), np.float32)
    for h in range(N_HEADS):
        base = _extra_base(h)
        for i in range(N_PIECES):
            pk[SUBLANES * i + h, base + i] = 1.0
            pq[PIECE_ROWS, base + i] = -1.0
            pk[PIECE_ROWS, base + N_PIECES + i] = 1.0
            pq[SUBLANES * i + h, base + N_PIECES + i] = 1.0
    return jnp.asarray(pk, BF16), jnp.asarray(pq, BF16)


def _fgate_kernel(f_ref, b_ref, pk_ref, pq_ref, kx_ref, qx_ref, carry_ref):
    @pl.when(pl.program_id(1) == 0)
    def _():
        carry_ref[...] = jnp.zeros_like(carry_ref)

    z = f_ref[...] + b_ref[...]
    log_f = -(jnp.maximum(-z, 0.0) + jnp.log1p(jnp.exp(-jnp.abs(z))))
    ri = lax.broadcasted_iota(jnp.int32, (CUM_T, CUM_T), 0)
    ci = lax.broadcasted_iota(jnp.int32, (CUM_T, CUM_T), 1)
    tri = (ci <= ri).astype(F32)
    c = jnp.dot(tri, log_f, preferred_element_type=F32,
                precision=lax.Precision.HIGHEST) + carry_ref[0:1, :]
    carry_ref[...] = jnp.broadcast_to(c[CUM_T - 1:CUM_T, :], carry_ref.shape)
    hi, mid, lo = _pieces(c * LOG2E)
    lane = lax.broadcasted_iota(jnp.int32, (CUM_T, LANES), 1)
    packed = jnp.where(lane < SUBLANES, hi,
                       jnp.where(lane < 2 * SUBLANES, pltpu.roll(mid, SUBLANES, axis=1),
                                 jnp.where(lane < PIECE_ROWS, pltpu.roll(lo, 2 * SUBLANES, axis=1),
                                           jnp.where(lane == PIECE_ROWS, 1.0, 0.0)))).astype(BF16)
    kx_ref[...] = jnp.dot(packed, pk_ref[...], preferred_element_type=F32).astype(BF16)
    qx_ref[...] = jnp.dot(packed, pq_ref[...], preferred_element_type=F32).astype(BF16)


def _fgate(f, b, bsz, seq, layer):
    n = f.shape[0]
    spt = seq // CUM_T
    pk, pq = _fgate_placement()
    row = lambda w: pl.BlockSpec((CUM_T, w), lambda bi, i: (bi * spt + i, 0))
    return pl.pallas_call(
        _fgate_kernel,
        grid=(bsz, spt),
        in_specs=[row(LANES), _resident(b.shape, layer), _resident(pk.shape), _resident(pq.shape)],
        out_specs=(row(BRANCH_WIDTH), row(BRANCH_WIDTH)),
        out_shape=(jax.ShapeDtypeStruct((n, BRANCH_WIDTH), BF16),
                   jax.ShapeDtypeStruct((n, BRANCH_WIDTH), BF16)),
        scratch_shapes=[pltpu.VMEM((SUBLANES, LANES), F32)],
        compiler_params=_params("arbitrary", "arbitrary"),
        name="fgate_cumsum",
    )(f, b, pk, pq)


def _build_kv(k_ref, v_ref, extra_of, ka_ref, vt_ref, seq):
    lane = lax.broadcasted_iota(jnp.int32, (KV_ROWS, LANES), 1)
    lower = lane < HEAD_DIM
    ones_rows = jnp.where(lax.broadcasted_iota(jnp.int32, (V_ROWS - HEAD_DIM, KV_ROWS), 0) == 0, 1.0, 0.0)

    def body(i, carry):
        rows = pl.ds(pl.multiple_of(i * KV_ROWS, KV_ROWS), KV_ROWS)
        extra = extra_of(i, rows)
        for pair in range(STEP_HEADS // 2):
            lanes = slice(pair * LANES, (pair + 1) * LANES)
            k, x = k_ref[rows, lanes], extra[:, lanes]
            ka_ref[2 * pair, rows, :] = jnp.where(lower, k, x)
            ka_ref[2 * pair + 1, rows, :] = jnp.where(lower, x, k)
            v_t = v_ref[rows, lanes].astype(F32).T
            for hh in range(2):
                vt_ref[2 * pair + hh, :, rows] = jnp.concatenate(
                    [v_t[hh * HEAD_DIM:(hh + 1) * HEAD_DIM], ones_rows], axis=0).astype(BF16)
        return carry

    lax.fori_loop(0, seq // KV_ROWS, body, 0)


def _extend_q_t(q_t, extras_t):
    out = []
    for hh in range(STEP_HEADS):
        q_h = q_t[hh * HEAD_DIM:(hh + 1) * HEAD_DIM]
        parts = [q_h, extras_t[hh]] if hh % 2 == 0 else [extras_t[hh], q_h]
        out.append(jnp.concatenate(parts, axis=0).astype(BF16))
    return out


def _flash_step(qa_t, ka_ref, vt_ref, m_ref, acc_ref, s_ref,start, n_tiles, tables=None, first=False):
    heads = range(STEP_HEADS)
    m = [None if first else m_ref[hh, 0:1, :] for hh in heads]
    acc = [None if first else acc_ref[hh] for hh in heads]
    n_plain = n_tiles - (len(tables(0)) if tables else 0)
    tile_rows = lambda t: pl.ds(pl.multiple_of(start + t * TK, TK), TK)
    ring = SCORE_LOOKAHEAD + 1

    def score(t):
        for hh in heads:
            s_ref[t % ring, hh] = jnp.dot(ka_ref[hh, tile_rows(t), :], qa_t[hh],
                                          preferred_element_type=F32)

    for t in range(min(SCORE_LOOKAHEAD, n_tiles)):
        score(t)
    for t in range(n_tiles):
        rows = tile_rows(t)
        if t + SCORE_LOOKAHEAD < n_tiles:
            score(t + SCORE_LOOKAHEAD)
        for hh in heads:
            s = s_ref[t % ring, hh]
            if t >= n_plain:
                s = s + tables(hh)[t - n_plain]
            col_max = jnp.max(s, axis=0, keepdims=True)
            m_new = col_max if m[hh] is None else jnp.maximum(m[hh], col_max)
            p = jnp.exp2(s - m_new).astype(BF16)
            pv = jnp.dot(vt_ref[hh, :, rows], p, preferred_element_type=F32)
            acc[hh] = pv if acc[hh] is None else jnp.exp2(m[hh] - m_new) * acc[hh] + pv
            m[hh] = m_new
    for hh in heads:
        m_ref[hh] = jnp.broadcast_to(m[hh], (SUBLANES, TQ))
        acc_ref[hh] = acc[hh]


def _pair_output(o_ref, acc_ref):
    out_t = jnp.concatenate(
        [acc_ref[hh, 0:HEAD_DIM, :] / acc_ref[hh, HEAD_DIM:HEAD_DIM + 1, :] for hh in range(STEP_HEADS)],
        axis=0)
    o_ref[...] = out_t.T.astype(BF16)


def _flash_scratch(seq):
    return [pltpu.VMEM((STEP_HEADS, seq, LANES), BF16), pltpu.VMEM((STEP_HEADS, V_ROWS, seq), BF16),
            pltpu.VMEM((STEP_HEADS, SUBLANES, TQ), F32), pltpu.VMEM((STEP_HEADS, V_ROWS, TQ), F32),
            pltpu.VMEM((SCORE_LOOKAHEAD + 1, STEP_HEADS, TK, TQ), F32)]


def _flash_chunks(qa, ka_ref, vt_ref, m_ref, acc_ref, s_ref,n_chunks):
    def chunk(c, carry):
        _flash_step(qa, ka_ref, vt_ref, m_ref, acc_ref, s_ref,c * (CHUNK_TILES * TK), CHUNK_TILES)
        return carry

    lax.fori_loop(0, n_chunks, chunk, 0)


def _flash_tail(qa, ka_ref, vt_ref, m_ref, acc_ref, s_ref,last_tile, n_loose, tail_tables, guard=None):
    rem = n_loose % CHUNK_TILES
    for r in range(CHUNK_TILES):
        if isinstance(rem, int) and rem != r:
            continue
        tiles = r + len(tail_tables(0))
        cond = guard if isinstance(rem, int) else (rem == r if guard is None else (rem == r) & guard)

        @pl.when(cond)
        def _(tiles=tiles):
            _flash_step(qa, ka_ref, vt_ref, m_ref, acc_ref, s_ref,(last_tile + 1 - tiles) * TK, tiles,
                        tables=tail_tables, first=True)


def _fox_kernel(q_ref, qx_ref, k_ref, v_ref, kx_ref, o_ref, ka_ref, vt_ref, m_ref, acc_ref, s_ref,*, seq):
    qt = pl.program_id(2)

    @pl.when(qt == 0)
    def _():
        _build_kv(k_ref, v_ref, lambda i, rows: kx_ref[rows, :], ka_ref, vt_ref, seq)

    q_t = (q_ref[...].astype(F32) * (SCALE * LOG2E)).T
    qx_t = qx_ref[...].astype(F32).T
    qa = _extend_q_t(q_t, [qx_t[(hh ^ 1) * HEAD_DIM:((hh ^ 1) + 1) * HEAD_DIM] for hh in range(STEP_HEADS)])
    causal = jnp.where(lax.broadcasted_iota(jnp.int32, (TK, TQ), 0)
                       <= lax.broadcasted_iota(jnp.int32, (TK, TQ), 1), 0.0, NEG)
    _flash_tail(qa, ka_ref, vt_ref, m_ref, acc_ref, s_ref,qt, qt, lambda hh: [causal])
    _flash_chunks(qa, ka_ref, vt_ref, m_ref, acc_ref, s_ref,qt // CHUNK_TILES)
    _pair_output(o_ref, acc_ref)


def _fox(qkv, kx, qx, bsz, seq):
    n = qkv.shape[0]
    qpt = seq // TQ
    koff, voff = BRANCH_WIDTH // STEP_LANES, 2 * BRANCH_WIDTH // STEP_LANES
    q_tile = lambda off: pl.BlockSpec((TQ, STEP_LANES), lambda b, p, i: (b * qpt + i, off + p))
    kv_seq = lambda off: pl.BlockSpec((seq, STEP_LANES), lambda b, p, i: (b, off + p))
    return pl.pallas_call(
        functools.partial(_fox_kernel, seq=seq),
        grid=(bsz, N_HEADS // STEP_HEADS, qpt),
        in_specs=[q_tile(0), q_tile(0), kv_seq(koff), kv_seq(voff), kv_seq(0)],
        out_specs=q_tile(0),
        out_shape=jax.ShapeDtypeStruct((n, BRANCH_WIDTH), BF16),
        scratch_shapes=_flash_scratch(seq),
        compiler_params=_params("arbitrary", "arbitrary", "arbitrary"),
        name="fox",
    )(qkv, qx, qkv, qkv, kx)


def _moba_select(qseq_ref, kmean_ref, pen_ref, seq):
    nblk = seq // MOBA_BLOCK
    blk_shift = MOBA_BLOCK.bit_length() - 1
    kmean = kmean_ref[0:nblk, :]
    lane_head = jnp.right_shift(lax.broadcasted_iota(jnp.int32, (nblk, STEP_LANES), 1),
                                HEAD_DIM.bit_length() - 1)
    blk = lax.broadcasted_iota(jnp.int32, (nblk, SELECT_COLS), 0)

    def body(i, carry):
        cols = pl.ds(pl.multiple_of(i * SELECT_COLS, SELECT_COLS), SELECT_COLS)
        q = qseq_ref[cols, :].astype(F32)
        own = jnp.right_shift(i * SELECT_COLS + lax.broadcasted_iota(jnp.int32, (nblk, SELECT_COLS), 1),
                              blk_shift)
        past = blk < own
        for hh in range(STEP_HEADS):
            gate = lax.dot_general(jnp.where(lane_head == hh, kmean, 0.0), q, (((1,), (1,)), ((), ())),
                                   preferred_element_type=F32, precision=lax.Precision.HIGHEST)
            gate = jnp.where(past, gate, NEG)
            sel = blk == own
            for _ in range(MOBA_TOPK):
                top = jnp.max(gate, axis=0, keepdims=True)
                idx = jnp.min(jnp.where(gate == top, blk, nblk), axis=0, keepdims=True)
                pick = blk == idx
                sel = sel | (pick & past)
                gate = jnp.where(pick, -jnp.inf, gate)
            pen_ref[hh, :, cols] = jnp.where(sel, 0.0, NEG)
        return carry

    lax.fori_loop(0, seq // SELECT_COLS, body, 0)


def _moba_kernel(tab_ref, q_ref, qseq_ref, k_ref, v_ref, town_ref, tprev_ref, o_ref,
                 kmean_ref, pen_ref, ka_ref, vt_ref, m_ref, acc_ref, s_ref, *, seq):
    group = pl.program_id(1)
    qt = pl.program_id(2)
    nblk = seq // MOBA_BLOCK
    blk_shift = MOBA_BLOCK.bit_length() - 1

    @pl.when(qt == 0)
    def _():
        blk = lax.broadcasted_iota(jnp.int32, (LANES, seq), 0)
        pos = lax.broadcasted_iota(jnp.int32, (LANES, seq), 1)
        member = jnp.where(jnp.right_shift(pos, blk_shift) == blk, 1.0 / MOBA_BLOCK, 0.0).astype(BF16)
        kmean_ref[...] = jnp.dot(member, k_ref[...], preferred_element_type=F32)
        _moba_select(qseq_ref, kmean_ref, pen_ref, seq)

        def extra_of(i, rows):
            xl = jnp.bitwise_and(lax.broadcasted_iota(jnp.int32, (KV_ROWS, STEP_LANES), 1), HEAD_DIM - 1)
            row_blk = jnp.right_shift(
                i * KV_ROWS + lax.broadcasted_iota(jnp.int32, (KV_ROWS, STEP_LANES), 0), blk_shift)
            hit = (xl < N_PIECES) | (xl - MOBA_SEL_LANE == row_blk)
            return jnp.where(hit, 1.0, 0.0).astype(BF16)

        _build_kv(k_ref, v_ref, extra_of, ka_ref, vt_ref, seq)

    q_t = q_ref[...].astype(F32).T
    piece_row = lax.broadcasted_iota(jnp.int32, (MOBA_SEL_LANE, TQ), 0)
    tile_cols = pl.ds(pl.multiple_of(qt * TQ, TQ), TQ)
    extras = []
    for hh in range(STEP_HEADS):
        far = jnp.full((MOBA_SEL_LANE, TQ), tab_ref[REL_BUCKETS - 1, N_HEADS + STEP_HEADS * group + hh] * LOG2E, F32)
        hi, mid, lo = _pieces(far)
        far_rows = jnp.where(piece_row == 0, hi, jnp.where(piece_row == 1, mid, jnp.where(piece_row == 2, lo, 0.0)))
        extras.append(jnp.concatenate(
            [far_rows, pen_ref[hh, :, tile_cols],
             jnp.zeros((HEAD_DIM - MOBA_SEL_LANE - nblk, TQ), F32)], axis=0))

    qa = _extend_q_t(q_t * (SCALE * LOG2E), extras)
    n_far = jnp.maximum(qt - 1, 0)
    _flash_tail(qa, ka_ref, vt_ref, m_ref, acc_ref, s_ref,qt, 0, lambda hh: [town_ref[hh]], guard=qt == 0)
    _flash_tail(qa, ka_ref, vt_ref, m_ref, acc_ref, s_ref,qt, n_far,
                lambda hh: [tprev_ref[hh], town_ref[hh]], guard=qt >= 1)
    _flash_chunks(qa, ka_ref, vt_ref, m_ref, acc_ref, s_ref,n_far // CHUNK_TILES)
    _pair_output(o_ref, acc_ref)


def _moba(qkv, rel_bias, t_own, t_prev, bsz, seq):
    n = qkv.shape[0]
    qpt = seq // TQ
    koff, voff = BRANCH_WIDTH // STEP_LANES, 2 * BRANCH_WIDTH // STEP_LANES
    q_tile = pl.BlockSpec((TQ, STEP_LANES), lambda b, p, i: (b * qpt + i, p))
    kv_seq = lambda off: pl.BlockSpec((seq, STEP_LANES), lambda b, p, i: (b, off + p))
    head_tables = pl.BlockSpec((STEP_HEADS, MOBA_BLOCK, MOBA_BLOCK), lambda b, p, i: (p, 0, 0))
    return pl.pallas_call(
        functools.partial(_moba_kernel, seq=seq),
        grid=(bsz, N_HEADS // STEP_HEADS, qpt),
        in_specs=[pl.BlockSpec(memory_space=pltpu.SMEM), q_tile, kv_seq(0), kv_seq(koff), kv_seq(voff),
                  head_tables, head_tables],
        out_specs=q_tile,
        out_shape=jax.ShapeDtypeStruct((n, BRANCH_WIDTH), BF16),
        scratch_shapes=[pltpu.VMEM((LANES, STEP_LANES), F32),
                        pltpu.VMEM((STEP_HEADS, seq // MOBA_BLOCK, seq), F32)] + _flash_scratch(seq),
        compiler_params=_params("arbitrary", "arbitrary", "arbitrary"),
        name="moba",
    )(rel_bias, qkv, qkv, qkv, qkv, t_own, t_prev)


def _merge_kernel(x_ref, g_ref, ya_ref, yb_ref, yc_ref, yd_ref, wg_ref, bg_ref, wbr_ref, wout_ref, o_ref):
    x = x_ref[...]
    hb = _rmsnorm(x, g_ref[...]).astype(BF16)
    merged = jnp.zeros((TM, D_MODEL), F32)
    for n, y_ref in enumerate((ya_ref, yb_ref, yc_ref, yd_ref)):
        gz = jnp.dot(hb, wg_ref[:, n * D_MODEL:(n + 1) * D_MODEL], preferred_element_type=F32)
        gate = jax.nn.sigmoid(gz + bg_ref[n:n + 1, :])
        merged = merged + gate * jnp.dot(y_ref[...], wbr_ref[n], preferred_element_type=F32)
    o_ref[...] = x + jnp.dot(merged.astype(BF16), wout_ref[...], preferred_element_type=F32)


def _merge(x, g, ys, wg, bg, wbr, wout, layer):
    n = x.shape[0]
    row = lambda w: pl.BlockSpec((TM, w), lambda i: (i, 0))
    return pl.pallas_call(
        _merge_kernel,
        grid=(n // TM,),
        in_specs=[row(D_MODEL), _resident(g.shape, layer)] + [row(BRANCH_WIDTH)] * N_BRANCH
        + [_resident(w.shape, layer) for w in (wg, bg, wbr, wout)],
        out_specs=row(D_MODEL),
        out_shape=jax.ShapeDtypeStruct((n, D_MODEL), F32),
        compiler_params=_params("arbitrary"),
        name="merge",
    )(x, g, *ys, wg, bg, wbr, wout)


def _ffn_kernel(x_ref, g_ref, wup_ref, cw_ref, cb_ref, wdown_ref, fg_ref, o_ref, halo_ref, buf_ref, a_ref,
                *, final):
    @pl.when(pl.program_id(1) == 0)
    def _():
        halo_ref[...] = jnp.zeros_like(halo_ref)

    x = x_ref[...]
    hb = _rmsnorm(x, g_ref[...]).astype(BF16)
    n_chunks = D_FF // FF_CHUNK
    col_of = lambda c, part: part * D_FF + c * FF_CHUNK

    def up(c):
        for part in range(2):
            col = col_of(c, part)
            buf_ref[c % 2, part, SUBLANES:SUBLANES + TM, :] = jnp.dot(
                hb, wup_ref[:, col:col + FF_CHUNK], preferred_element_type=F32)

    up(0)
    for c in range(n_chunks):
        if c + 1 < n_chunks:
            up(c + 1)
        halves = []
        for part in range(2):
            col = col_of(c, part)
            buf_ref[c % 2, part, 0:SUBLANES, :] = halo_ref[:, col:col + FF_CHUNK]
            halo_ref[:, col:col + FF_CHUNK] = buf_ref[c % 2, part, TM:TM + SUBLANES, :]
            u = cb_ref[:, col:col + FF_CHUNK]
            for k in range(FFN_CONV_WIDTH):
                lo = SUBLANES - (FFN_CONV_WIDTH - 1) + k
                u = u + cw_ref[k:k + 1, col:col + FF_CHUNK] * buf_ref[c % 2, part, lo:lo + TM, :]
            halves.append(u)
        a_ref[:, c * FF_CHUNK:(c + 1) * FF_CHUNK] = (
            (halves[0] * jax.nn.sigmoid(halves[0])) * halves[1]).astype(BF16)
    y = x + jnp.dot(a_ref[...], wdown_ref[...], preferred_element_type=F32)
    o_ref[...] = _rmsnorm(y, fg_ref[...]) if final else y


def _ffn(x, g, wup, cw, cb, wdown, fg, bsz, seq, layer, final):
    n = x.shape[0]
    spt = seq // TM
    row = pl.BlockSpec((TM, D_MODEL), lambda bi, i: (bi * spt + i, 0))
    return pl.pallas_call(
        functools.partial(_ffn_kernel, final=final),
        grid=(bsz, spt),
        in_specs=[row] + [_resident(w.shape, layer) for w in (g, wup, cw, cb, wdown)]
        + [_resident(fg.shape)],
        out_specs=row,
        out_shape=jax.ShapeDtypeStruct((n, D_MODEL), F32),
        scratch_shapes=[pltpu.VMEM((SUBLANES, 2 * D_FF), F32),
                        pltpu.VMEM((2, 2, TM + SUBLANES, FF_CHUNK), F32),
                        pltpu.VMEM((TM, D_FF), BF16)],
        compiler_params=_params("arbitrary", "arbitrary"),
        name="ffn",
    )(x, g, wup, cw, cb, wdown, fg)


def _split_w_in(w_in):
    hd, bw = HEAD_DIM, BRANCH_WIDTH
    o = 0
    qa = w_in[..., o:o + bw]; o += bw
    ka = w_in[..., o:o + SWA_KV_HEADS * hd]; o += SWA_KV_HEADS * hd
    va = w_in[..., o:o + SWA_KV_HEADS * hd]; o += SWA_KV_HEADS * hd
    ub = w_in[..., o:o + 2 * bw]; o += 2 * bw
    qkv_c = w_in[..., o:o + 3 * bw]; o += 3 * bw
    fc = w_in[..., o:o + N_HEADS]; o += N_HEADS
    qkv_d = w_in[..., o:o + 3 * bw]; o += 3 * bw
    gates = w_in[..., o:o + N_BRANCH * D_MODEL]
    dup = lambda w: jnp.concatenate([w[..., i * hd:(i + 1) * hd] for i in (0, 0, 1, 1)], axis=-1)
    wa = jnp.concatenate([qa, dup(ka), dup(va)], axis=-1)
    wf = jnp.pad(fc, ((0, 0), (0, 0), (0, LANES - N_HEADS)))
    return tuple(w.astype(BF16) for w in (wa, ub, qkv_c, wf, qkv_d)), gates.astype(BF16)


def _layer(x, bsz, seq, layer, final, biases, rel_bias, p):
    bias_swa, table_own, table_prev = biases
    za, zb, zc, zf, zd = _inproj(x, p["ln1_g"], p["w_in_groups"], layer)
    ya = _swa(za, p["sinks"], bias_swa, seq, layer)
    yb = _conformer(zb, p["conv_w"], p["conv_b"], p["conv_ln_g"], p["conv_ln_b"], bsz, seq, layer)
    kx, qx = _fgate(zf, p["b_fgate"], bsz, seq, layer)
    yc = _fox(zc, kx, qx, bsz, seq)
    yd = _moba(zd, rel_bias, table_own, table_prev, bsz, seq)
    x = _merge(x, p["ln1_g"], (ya, yb, yc, yd), p["w_gates"], p["b_gate"], p["w_br"], p["w_out"], layer)
    return _ffn(x, p["ln2_g"], p["w_up"], p["ffn_conv_w"], p["ffn_conv_b"], p["w_down"], p["final_g"],
                bsz, seq, layer, final)


def kernel(x, ln1_g, w_in, b_gate, b_fgate, sinks, conv_w, conv_b, conv_ln_g, conv_ln_b, w_br, w_out,
           ln2_g, w_up, ffn_conv_w, ffn_conv_b, w_down, rel_bias, final_g):
    bsz, seq, _ = x.shape
    depth = w_in.shape[0]
    assert seq % TM == 0 and seq % KV_ROWS == 0 and TQ == TK == MOBA_BLOCK
    assert MOBA_SEL_LANE + seq // MOBA_BLOCK <= HEAD_DIM
    biases = _expand_bias(rel_bias)
    rows = lambda v: v.reshape(depth, 1, -1)
    w_in_groups, w_gates = _split_w_in(w_in)
    params = dict(
        ln1_g=rows(ln1_g), w_in_groups=w_in_groups, w_gates=w_gates, b_gate=b_gate,
        b_fgate=rows(jnp.pad(b_fgate, ((0, 0), (0, LANES - N_HEADS)))), sinks=sinks,
        conv_w=conv_w, conv_b=rows(conv_b), conv_ln_g=rows(conv_ln_g), conv_ln_b=rows(conv_ln_b),
        w_br=w_br.astype(BF16), w_out=w_out.astype(BF16), ln2_g=rows(ln2_g), w_up=w_up.astype(BF16),
        ffn_conv_w=ffn_conv_w, ffn_conv_b=rows(ffn_conv_b), w_down=w_down.astype(BF16),
        final_g=final_g.reshape(1, -1))
    y = x.reshape(bsz * seq, D_MODEL)
    for layer in range(depth):
        y = _layer(y, bsz, seq, layer, layer == depth - 1, biases, rel_bias, params)
    return y.reshape(bsz, seq, D_MODEL)
```

```python
import functools
import math

import jax
import jax.numpy as jnp
import numpy as np
from jax import lax
from jax.experimental import pallas as pl
from jax.experimental.pallas import tpu as pltpu

D_MODEL = 1024
HEAD_DIM = 64
N_BRANCH = 4
BRANCH_WIDTH = 512
N_HEADS = 8
N_PAIRS = N_HEADS // 2
SWA_KV_HEADS = 2
SWA_BLOCK = 128
CONV_WIDTH = 31
MOBA_BLOCK = 256
MOBA_TOPK = 3
D_FF = 2816
FFN_CONV_WIDTH = 3
REL_BUCKETS = 32
REL_MAX_DIST = 128
EPS = 1e-6
NEG = -1e30
SCALE = HEAD_DIM ** -0.5
LOG2E = math.log2(math.e)

LANES = 128
SUBLANES = 8
VMEM_LIMIT = 56 * 1024 * 1024

TM = 512
TQ = 256
TK = 256
CHUNK_TILES = 32
SCORE_LOOKAHEAD = 3
KV_ROWS = 512
SELECT_COLS = 1024
CUM_T = 256
FF_CHUNK = 256
CONV_ROWS = 64
CONV_HALO = 32

N_PIECES = 3
PIECE_ROWS = SUBLANES * N_PIECES
MOBA_SEL_LANE = SUBLANES
STEP_HEADS = 2
STEP_LANES = STEP_HEADS * HEAD_DIM
V_ROWS = HEAD_DIM + 16

BF16 = jnp.bfloat16
F32 = jnp.float32


def _params(*sem):
    return pltpu.CompilerParams(dimension_semantics=sem, vmem_limit_bytes=VMEM_LIMIT)


def _resident(shape, layer=None):
    if layer is None:
        zeros = (0,) * len(shape)
        return pl.BlockSpec(shape, lambda *_: zeros, pipeline_mode=pl.Buffered(1))
    index = (layer,) + (0,) * (len(shape) - 1)
    return pl.BlockSpec((None,) + tuple(shape[1:]), lambda *_: index, pipeline_mode=pl.Buffered(1))


def _rmsnorm(x, g):
    return (x * lax.rsqrt(jnp.mean(x * x, axis=-1, keepdims=True) + EPS)) * g


def _pieces(x):
    hi = x.astype(BF16).astype(F32)
    rest = x - hi
    mid = rest.astype(BF16).astype(F32)
    return hi, mid, rest - mid


def _rel_bucket_np(dist):
    n = np.maximum(dist, 0)
    max_exact = REL_BUCKETS // 2
    nf = np.maximum(n, 1).astype(np.float64)
    large = max_exact + (np.log(nf / max_exact) / math.log(REL_MAX_DIST / max_exact)
                         * (REL_BUCKETS - max_exact) + 1e-9).astype(np.int32)
    large = np.minimum(large, REL_BUCKETS - 1)
    return np.where(n < max_exact, n, large).astype(np.int32)


def _bias_kernel(tab_ref, bs_ref, bo_ref, bp_ref, os_ref, oo_ref, op_ref):
    os_ref[...] = jnp.zeros_like(os_ref)
    oo_ref[...] = jnp.zeros_like(oo_ref)
    op_ref[...] = jnp.zeros_like(op_ref)
    bs, bo, bp = bs_ref[...], bo_ref[...], bp_ref[...]

    def body(b, carry):
        ms, mo, mp = bs == b, bo == b, bp == b
        for h in range(N_HEADS):
            os_ref[h] = jnp.where(ms, tab_ref[b, h], os_ref[h])
            oo_ref[h] = jnp.where(mo, tab_ref[b, N_HEADS + h], oo_ref[h])
            op_ref[h] = jnp.where(mp, tab_ref[b, N_HEADS + h], op_ref[h])
        return carry

    lax.fori_loop(0, REL_BUCKETS, body, 0)
    causal = (lax.broadcasted_iota(jnp.int32, (MOBA_BLOCK, MOBA_BLOCK), 0)
              <= lax.broadcasted_iota(jnp.int32, (MOBA_BLOCK, MOBA_BLOCK), 1))
    for h in range(N_HEADS):
        far = tab_ref[REL_BUCKETS - 1, N_HEADS + h]
        oo_ref[h] = jnp.where(causal, (oo_ref[h] - far) * LOG2E, NEG)
        op_ref[h] = (op_ref[h] - far) * LOG2E


def _expand_bias(rel_bias):
    qi = np.arange(SWA_BLOCK)[:, None]
    kj = np.arange(2 * SWA_BLOCK)[None, :]
    b_swa = _rel_bucket_np(qi + SWA_BLOCK - kj)
    qi = np.arange(MOBA_BLOCK)[:, None]
    kj = np.arange(MOBA_BLOCK)[None, :]
    b_own = np.ascontiguousarray(_rel_bucket_np(qi - kj).T)
    b_prev = np.ascontiguousarray(_rel_bucket_np(qi - kj + MOBA_BLOCK).T)
    vm = pl.BlockSpec(memory_space=pltpu.VMEM)
    return pl.pallas_call(
        _bias_kernel,
        out_shape=(jax.ShapeDtypeStruct((N_HEADS, SWA_BLOCK, 2 * SWA_BLOCK), F32),
                   jax.ShapeDtypeStruct((N_HEADS, MOBA_BLOCK, MOBA_BLOCK), F32),
                   jax.ShapeDtypeStruct((N_HEADS, MOBA_BLOCK, MOBA_BLOCK), F32)),
        in_specs=[pl.BlockSpec(memory_space=pltpu.SMEM), vm, vm, vm],
        out_specs=(vm, vm, vm),
        name="bias_expand",
    )(rel_bias, jnp.asarray(b_swa), jnp.asarray(b_own), jnp.asarray(b_prev))


def _inproj_kernel(x_ref, g_ref, wa_ref, wb_ref, wc_ref, wf_ref, wd_ref,
                   oa_ref, ob_ref, oc_ref, of_ref, od_ref):
    h = _rmsnorm(x_ref[...], g_ref[...])
    hb = h.astype(BF16)
    oa_ref[...] = jnp.dot(hb, wa_ref[...], preferred_element_type=F32).astype(BF16)
    ob_ref[...] = jnp.dot(hb, wb_ref[...], preferred_element_type=F32)
    oc_ref[...] = jnp.dot(hb, wc_ref[...], preferred_element_type=F32).astype(BF16)
    od_ref[...] = jnp.dot(hb, wd_ref[...], preferred_element_type=F32).astype(BF16)
    of_ref[...] = jnp.dot(hb, wf_ref[...], preferred_element_type=F32)


def _inproj(x, g, ws, layer):
    n = x.shape[0]
    row = lambda w: pl.BlockSpec((TM, w), lambda i: (i, 0))
    dts = (BF16, F32, BF16, F32, BF16)
    return pl.pallas_call(
        _inproj_kernel,
        grid=(n // TM,),
        in_specs=[row(D_MODEL), _resident(g.shape, layer)] + [_resident(w.shape, layer) for w in ws],
        out_specs=tuple(row(w.shape[-1]) for w in ws),
        out_shape=tuple(jax.ShapeDtypeStruct((n, w.shape[-1]), dt) for w, dt in zip(ws, dts)),
        compiler_params=_params("arbitrary"),
        name="inproj",
    )(x, g, *ws)


def _swa_kernel(sink_ref, q_ref, kc_ref, kp_ref, vc_ref, vp_ref, bias_ref, o_ref, s_ref, *,
                blocks_per_seq, layer):
    blk = pl.program_id(0) % blocks_per_seq
    kk = jnp.concatenate([kp_ref[...], kc_ref[...]], axis=0)
    vv = jnp.concatenate([vp_ref[...], vc_ref[...]], axis=0)
    qi = lax.broadcasted_iota(jnp.int32, (SWA_BLOCK, 2 * SWA_BLOCK), 0)
    kj = lax.broadcasted_iota(jnp.int32, (SWA_BLOCK, 2 * SWA_BLOCK), 1)
    dist = qi + SWA_BLOCK - kj
    mask = (dist >= 0) & (dist < SWA_BLOCK) & ((blk > 0) | (kj >= SWA_BLOCK))
    lane = lax.broadcasted_iota(jnp.int32, (SWA_BLOCK, LANES), 1)
    upper = lane >= HEAD_DIM
    kv_of = lambda h: h // (N_HEADS // SWA_KV_HEADS)
    for h in range(N_HEADS):
        q_pair = q_ref[:, (h // 2) * LANES:(h // 2 + 1) * LANES] * SCALE
        q_h = jnp.where(upper == (h % 2 == 1), q_pair, jnp.zeros_like(q_pair))
        k_kv = kk[:, kv_of(h) * LANES:(kv_of(h) + 1) * LANES]
        s_ref[h] = lax.dot_general(q_h, k_kv, (((1,), (1,)), ((), ())), preferred_element_type=F32)
    probs = []
    for h in range(N_HEADS):
        logits = jnp.where(mask, s_ref[h] + bias_ref[h], NEG)
        sink = sink_ref[layer, h]
        m = jnp.maximum(jnp.max(logits, axis=-1, keepdims=True), sink)
        p = jnp.exp(logits - m)
        denom = jnp.sum(p, axis=-1, keepdims=True) + jnp.exp(sink - m)
        probs.append((p / denom).astype(BF16))
    outs = [jnp.dot(probs[h], vv[:, kv_of(h) * LANES:(kv_of(h) + 1) * LANES], preferred_element_type=F32)
            for h in range(N_HEADS)]
    for pair in range(N_PAIRS):
        o_ref[:, pair * LANES:(pair + 1) * LANES] = jnp.where(
            upper, outs[2 * pair + 1], outs[2 * pair]).astype(BF16)


def _swa(qkv, sinks, bias, seq, layer):
    n = qkv.shape[0]
    bps = seq // SWA_BLOCK
    kcol, vcol = BRANCH_WIDTH // (2 * LANES), BRANCH_WIDTH // (2 * LANES) + 1
    prev = lambda i: jnp.where(i % bps == 0, i, i - 1)
    return pl.pallas_call(
        functools.partial(_swa_kernel, blocks_per_seq=bps, layer=layer),
        grid=(n // SWA_BLOCK,),
        in_specs=[pl.BlockSpec(memory_space=pltpu.SMEM),
                  pl.BlockSpec((SWA_BLOCK, BRANCH_WIDTH), lambda i: (i, 0)),
                  pl.BlockSpec((SWA_BLOCK, 2 * LANES), lambda i: (i, kcol)),
                  pl.BlockSpec((SWA_BLOCK, 2 * LANES), lambda i: (prev(i), kcol)),
                  pl.BlockSpec((SWA_BLOCK, 2 * LANES), lambda i: (i, vcol)),
                  pl.BlockSpec((SWA_BLOCK, 2 * LANES), lambda i: (prev(i), vcol)),
                  _resident(bias.shape)],
        out_specs=pl.BlockSpec((SWA_BLOCK, BRANCH_WIDTH), lambda i: (i, 0)),
        out_shape=jax.ShapeDtypeStruct((n, BRANCH_WIDTH), BF16),
        scratch_shapes=[pltpu.VMEM((N_HEADS, SWA_BLOCK, 2 * SWA_BLOCK), F32)],
        compiler_params=_params("arbitrary"),
        name="swa",
    )(sinks, qkv, qkv, qkv, qkv, qkv, bias)


def _conformer_kernel(u_ref, w_ref, b_ref, g_ref, beta_ref, o_ref, buf_ref, y_ref):
    @pl.when(pl.program_id(1) == 0)
    def _():
        buf_ref[0:CONV_HALO, :] = jnp.zeros((CONV_HALO, BRANCH_WIDTH), F32)

    a = u_ref[:, 0:BRANCH_WIDTH]
    gt = u_ref[:, BRANCH_WIDTH:2 * BRANCH_WIDTH]
    buf_ref[CONV_HALO:CONV_HALO + TM, :] = a * jax.nn.sigmoid(gt)
    n_shifts = SUBLANES
    for r in range(TM // CONV_ROWS):
        row0 = r * CONV_ROWS
        for cg in range(BRANCH_WIDTH // LANES):
            cols = slice(cg * LANES, (cg + 1) * LANES)
            slabs = []
            for step in range((CONV_WIDTH - 1) // n_shifts + 1):
                lo = CONV_HALO - n_shifts * (step + 1) + row0
                slabs.append(buf_ref[lo:lo + CONV_ROWS + n_shifts, cols])
            y = None
            for shift in range(n_shifts):
                z = None
                for step in range((CONV_WIDTH - 1 - shift) // n_shifts + 1):
                    k = CONV_WIDTH - 1 - (n_shifts * step + shift)
                    term = w_ref[k:k + 1, cols] * slabs[step]
                    z = term if z is None else z + term
                shifted = z[n_shifts - shift:n_shifts - shift + CONV_ROWS]
                y = shifted if y is None else y + shifted
            y_ref[row0:row0 + CONV_ROWS, cols] = y
        y = y_ref[row0:row0 + CONV_ROWS, :] + b_ref[...]
        mu = jnp.mean(y, axis=-1, keepdims=True)
        yc = y - mu
        var = jnp.mean(yc * yc, axis=-1, keepdims=True)
        z = yc * lax.rsqrt(var + EPS) * g_ref[...] + beta_ref[...]
        o_ref[r * CONV_ROWS:(r + 1) * CONV_ROWS, :] = (z * jax.nn.sigmoid(z)).astype(BF16)
    buf_ref[0:CONV_HALO, :] = buf_ref[TM:TM + CONV_HALO, :]


def _conformer(u, w, b, g, beta, bsz, seq, layer):
    n = u.shape[0]
    spt = seq // TM
    return pl.pallas_call(
        _conformer_kernel,
        grid=(bsz, spt),
        in_specs=[pl.BlockSpec((TM, 2 * BRANCH_WIDTH), lambda bi, i: (bi * spt + i, 0)),
                  _resident(w.shape, layer), _resident(b.shape, layer), _resident(g.shape, layer),
                  _resident(beta.shape, layer)],
        out_specs=pl.BlockSpec((TM, BRANCH_WIDTH), lambda bi, i: (bi * spt + i, 0)),
        out_shape=jax.ShapeDtypeStruct((n, BRANCH_WIDTH), BF16),
        scratch_shapes=[pltpu.VMEM((TM + CONV_HALO, BRANCH_WIDTH), F32),
                        pltpu.VMEM((TM, BRANCH_WIDTH), F32)],
        compiler_params=_params("arbitrary", "arbitrary"),
        name="conformer",
    )(u, w, b, g, beta)


def _extra_base(h):
    return (h // 2) * LANES + HEAD_DIM * (1 - h % 2)


def _fgate_placement():
    pk = np.zeros((LANES, BRANCH_WIDTH), np.float32)
    pq = np.zeros((LANES, BRANCH_WIDTH), np.float32)
    for h in range(N_HEADS):
        base = _extra_base(h)
        for i in range(N_PIECES):
            pk[SUBLANES * i + h, base + i] = 1.0
            pq[PIECE_ROWS, base + i] = -1.0
            pk[PIECE_ROWS, base + N_PIECES + i] = 1.0
            pq[SUBLANES * i + h, base + N_PIECES + i] = 1.0
    return jnp.asarray(pk, BF16), jnp.asarray(pq, BF16)


def _fgate_kernel(f_ref, b_ref, pk_ref, pq_ref, kx_ref, qx_ref, carry_ref):
    @pl.when(pl.program_id(1) == 0)
    def _():
        carry_ref[...] = jnp.zeros_like(carry_ref)

    z = f_ref[...] + b_ref[...]
    log_f = -(jnp.maximum(-z, 0.0) + jnp.log1p(jnp.exp(-jnp.abs(z))))
    ri = lax.broadcasted_iota(jnp.int32, (CUM_T, CUM_T), 0)
    ci = lax.broadcasted_iota(jnp.int32, (CUM_T, CUM_T), 1)
    tri = (ci <= ri).astype(F32)
    c = jnp.dot(tri, log_f, preferred_element_type=F32,
                precision=lax.Precision.HIGHEST) + carry_ref[0:1, :]
    carry_ref[...] = jnp.broadcast_to(c[CUM_T - 1:CUM_T, :], carry_ref.shape)
    hi, mid, lo = _pieces(c * LOG2E)
    lane = lax.broadcasted_iota(jnp.int32, (CUM_T, LANES), 1)
    packed = jnp.where(lane < SUBLANES, hi,
                       jnp.where(lane < 2 * SUBLANES, pltpu.roll(mid, SUBLANES, axis=1),
                                 jnp.where(lane < PIECE_ROWS, pltpu.roll(lo, 2 * SUBLANES, axis=1),
                                           jnp.where(lane == PIECE_ROWS, 1.0, 0.0)))).astype(BF16)
    kx_ref[...] = jnp.dot(packed, pk_ref[...], preferred_element_type=F32).astype(BF16)
    qx_ref[...] = jnp.dot(packed, pq_ref[...], preferred_element_type=F32).astype(BF16)


def _fgate(f, b, bsz, seq, layer):
    n = f.shape[0]
    spt = seq // CUM_T
    pk, pq = _fgate_placement()
    row = lambda w: pl.BlockSpec((CUM_T, w), lambda bi, i: (bi * spt + i, 0))
    return pl.pallas_call(
        _fgate_kernel,
        grid=(bsz, spt),
        in_specs=[row(LANES), _resident(b.shape, layer), _resident(pk.shape), _resident(pq.shape)],
        out_specs=(row(BRANCH_WIDTH), row(BRANCH_WIDTH)),
        out_shape=(jax.ShapeDtypeStruct((n, BRANCH_WIDTH), BF16),
                   jax.ShapeDtypeStruct((n, BRANCH_WIDTH), BF16)),
        scratch_shapes=[pltpu.VMEM((SUBLANES, LANES), F32)],
        compiler_params=_params("arbitrary", "arbitrary"),
        name="fgate_cumsum",
    )(f, b, pk, pq)


def _build_kv(k_ref, v_ref, extra_of, ka_ref, vt_ref, seq):
    lane = lax.broadcasted_iota(jnp.int32, (KV_ROWS, LANES), 1)
    lower = lane < HEAD_DIM
    ones_rows = jnp.where(lax.broadcasted_iota(jnp.int32, (V_ROWS - HEAD_DIM, KV_ROWS), 0) == 0, 1.0, 0.0)

    def body(i, carry):
        rows = pl.ds(pl.multiple_of(i * KV_ROWS, KV_ROWS), KV_ROWS)
        extra = extra_of(i, rows)
        for pair in range(STEP_HEADS // 2):
            lanes = slice(pair * LANES, (pair + 1) * LANES)
            k, x = k_ref[rows, lanes], extra[:, lanes]
            ka_ref[2 * pair, rows, :] = jnp.where(lower, k, x)
            ka_ref[2 * pair + 1, rows, :] = jnp.where(lower, x, k)
            v_t = v_ref[rows, lanes].astype(F32).T
            for hh in range(2):
                vt_ref[2 * pair + hh, :, rows] = jnp.concatenate(
                    [v_t[hh * HEAD_DIM:(hh + 1) * HEAD_DIM], ones_rows], axis=0).astype(BF16)
        return carry

    lax.fori_loop(0, seq // KV_ROWS, body, 0)


def _extend_q_t(q_t, extras_t):
    out = []
    for hh in range(STEP_HEADS):
        q_h = q_t[hh * HEAD_DIM:(hh + 1) * HEAD_DIM]
        parts = [q_h, extras_t[hh]] if hh % 2 == 0 else [extras_t[hh], q_h]
        out.append(jnp.concatenate(parts, axis=0).astype(BF16))
    return out


def _flash_step(qa_t, ka_ref, vt_ref, m_ref, acc_ref, s_ref,start, n_tiles, tables=None, first=False):
    heads = range(STEP_HEADS)
    m = [None if first else m_ref[hh, 0:1, :] for hh in heads]
    acc = [None if first else acc_ref[hh] for hh in heads]
    n_plain = n_tiles - (len(tables(0)) if tables else 0)
    tile_rows = lambda t: pl.ds(pl.multiple_of(start + t * TK, TK), TK)
    ring = SCORE_LOOKAHEAD + 1

    def score(t):
        for hh in heads:
            s_ref[t % ring, hh] = jnp.dot(ka_ref[hh, tile_rows(t), :], qa_t[hh],
                                          preferred_element_type=F32)

    for t in range(min(SCORE_LOOKAHEAD, n_tiles)):
        score(t)
    for t in range(n_tiles):
        rows = tile_rows(t)
        if t + SCORE_LOOKAHEAD < n_tiles:
            score(t + SCORE_LOOKAHEAD)
        for hh in heads:
            s = s_ref[t % ring, hh]
            if t >= n_plain:
                s = s + tables(hh)[t - n_plain]
            col_max = jnp.max(s, axis=0, keepdims=True)
            m_new = col_max if m[hh] is None else jnp.maximum(m[hh], col_max)
            p = jnp.exp2(s - m_new).astype(BF16)
            pv = jnp.dot(vt_ref[hh, :, rows], p, preferred_element_type=F32)
            acc[hh] = pv if acc[hh] is None else jnp.exp2(m[hh] - m_new) * acc[hh] + pv
            m[hh] = m_new
    for hh in heads:
        m_ref[hh] = jnp.broadcast_to(m[hh], (SUBLANES, TQ))
        acc_ref[hh] = acc[hh]


def _pair_output(o_ref, acc_ref):
    out_t = jnp.concatenate(
        [acc_ref[hh, 0:HEAD_DIM, :] / acc_ref[hh, HEAD_DIM:HEAD_DIM + 1, :] for hh in range(STEP_HEADS)],
        axis=0)
    o_ref[...] = out_t.T.astype(BF16)


def _flash_scratch(seq):
    return [pltpu.VMEM((STEP_HEADS, seq, LANES), BF16), pltpu.VMEM((STEP_HEADS, V_ROWS, seq), BF16),
            pltpu.VMEM((STEP_HEADS, SUBLANES, TQ), F32), pltpu.VMEM((STEP_HEADS, V_ROWS, TQ), F32),
            pltpu.VMEM((SCORE_LOOKAHEAD + 1, STEP_HEADS, TK, TQ), F32)]


def _flash_chunks(qa, ka_ref, vt_ref, m_ref, acc_ref, s_ref,n_chunks):
    def chunk(c, carry):
        _flash_step(qa, ka_ref, vt_ref, m_ref, acc_ref, s_ref,c * (CHUNK_TILES * TK), CHUNK_TILES)
        return carry

    lax.fori_loop(0, n_chunks, chunk, 0)


def _flash_tail(qa, ka_ref, vt_ref, m_ref, acc_ref, s_ref,last_tile, n_loose, tail_tables, guard=None):
    rem = n_loose % CHUNK_TILES
    for r in range(CHUNK_TILES):
        if isinstance(rem, int) and rem != r:
            continue
        tiles = r + len(tail_tables(0))
        cond = guard if isinstance(rem, int) else (rem == r if guard is None else (rem == r) & guard)

        @pl.when(cond)
        def _(tiles=tiles):
            _flash_step(qa, ka_ref, vt_ref, m_ref, acc_ref, s_ref,(last_tile + 1 - tiles) * TK, tiles,
                        tables=tail_tables, first=True)


def _fox_kernel(q_ref, qx_ref, k_ref, v_ref, kx_ref, o_ref, ka_ref, vt_ref, m_ref, acc_ref, s_ref,*, seq):
    qt = pl.program_id(2)

    @pl.when(qt == 0)
    def _():
        _build_kv(k_ref, v_ref, lambda i, rows: kx_ref[rows, :], ka_ref, vt_ref, seq)

    q_t = (q_ref[...].astype(F32) * (SCALE * LOG2E)).T
    qx_t = qx_ref[...].astype(F32).T
    qa = _extend_q_t(q_t, [qx_t[(hh ^ 1) * HEAD_DIM:((hh ^ 1) + 1) * HEAD_DIM] for hh in range(STEP_HEADS)])
    causal = jnp.where(lax.broadcasted_iota(jnp.int32, (TK, TQ), 0)
                       <= lax.broadcasted_iota(jnp.int32, (TK, TQ), 1), 0.0, NEG)
    _flash_tail(qa, ka_ref, vt_ref, m_ref, acc_ref, s_ref,qt, qt, lambda hh: [causal])
    _flash_chunks(qa, ka_ref, vt_ref, m_ref, acc_ref, s_ref,qt // CHUNK_TILES)
    _pair_output(o_ref, acc_ref)


def _fox(qkv, kx, qx, bsz, seq):
    n = qkv.shape[0]
    qpt = seq // TQ
    koff, voff = BRANCH_WIDTH // STEP_LANES, 2 * BRANCH_WIDTH // STEP_LANES
    q_tile = lambda off: pl.BlockSpec((TQ, STEP_LANES), lambda b, p, i: (b * qpt + i, off + p))
    kv_seq = lambda off: pl.BlockSpec((seq, STEP_LANES), lambda b, p, i: (b, off + p))
    return pl.pallas_call(
        functools.partial(_fox_kernel, seq=seq),
        grid=(bsz, N_HEADS // STEP_HEADS, qpt),
        in_specs=[q_tile(0), q_tile(0), kv_seq(koff), kv_seq(voff), kv_seq(0)],
        out_specs=q_tile(0),
        out_shape=jax.ShapeDtypeStruct((n, BRANCH_WIDTH), BF16),
        scratch_shapes=_flash_scratch(seq),
        compiler_params=_params("arbitrary", "arbitrary", "arbitrary"),
        name="fox",
    )(qkv, qx, qkv, qkv, kx)


def _moba_select(qseq_ref, kmean_ref, pen_ref, seq):
    nblk = seq // MOBA_BLOCK
    blk_shift = MOBA_BLOCK.bit_length() - 1
    kmean = kmean_ref[0:nblk, :]
    lane_head = jnp.right_shift(lax.broadcasted_iota(jnp.int32, (nblk, STEP_LANES), 1),
                                HEAD_DIM.bit_length() - 1)
    blk = lax.broadcasted_iota(jnp.int32, (nblk, SELECT_COLS), 0)

    def body(i, carry):
        cols = pl.ds(pl.multiple_of(i * SELECT_COLS, SELECT_COLS), SELECT_COLS)
        q = qseq_ref[cols, :].astype(F32)
        own = jnp.right_shift(i * SELECT_COLS + lax.broadcasted_iota(jnp.int32, (nblk, SELECT_COLS), 1),
                              blk_shift)
        past = blk < own
        for hh in range(STEP_HEADS):
            gate = lax.dot_general(jnp.where(lane_head == hh, kmean, 0.0), q, (((1,), (1,)), ((), ())),
                                   preferred_element_type=F32, precision=lax.Precision.HIGHEST)
            gate = jnp.where(past, gate, NEG)
            sel = blk == own
            for _ in range(MOBA_TOPK):
                top = jnp.max(gate, axis=0, keepdims=True)
                idx = jnp.min(jnp.where(gate == top, blk, nblk), axis=0, keepdims=True)
                pick = blk == idx
                sel = sel | (pick & past)
                gate = jnp.where(pick, -jnp.inf, gate)
            pen_ref[hh, :, cols] = jnp.where(sel, 0.0, NEG)
        return carry

    lax.fori_loop(0, seq // SELECT_COLS, body, 0)


def _moba_kernel(tab_ref, q_ref, qseq_ref, k_ref, v_ref, town_ref, tprev_ref, o_ref,
                 kmean_ref, pen_ref, ka_ref, vt_ref, m_ref, acc_ref, s_ref, *, seq):
    group = pl.program_id(1)
    qt = pl.program_id(2)
    nblk = seq // MOBA_BLOCK
    blk_shift = MOBA_BLOCK.bit_length() - 1

    @pl.when(qt == 0)
    def _():
        blk = lax.broadcasted_iota(jnp.int32, (LANES, seq), 0)
        pos = lax.broadcasted_iota(jnp.int32, (LANES, seq), 1)
        member = jnp.where(jnp.right_shift(pos, blk_shift) == blk, 1.0 / MOBA_BLOCK, 0.0).astype(BF16)
        kmean_ref[...] = jnp.dot(member, k_ref[...], preferred_element_type=F32)
        _moba_select(qseq_ref, kmean_ref, pen_ref, seq)

        def extra_of(i, rows):
            xl = jnp.bitwise_and(lax.broadcasted_iota(jnp.int32, (KV_ROWS, STEP_LANES), 1), HEAD_DIM - 1)
            row_blk = jnp.right_shift(
                i * KV_ROWS + lax.broadcasted_iota(jnp.int32, (KV_ROWS, STEP_LANES), 0), blk_shift)
            hit = (xl < N_PIECES) | (xl - MOBA_SEL_LANE == row_blk)
            return jnp.where(hit, 1.0, 0.0).astype(BF16)

        _build_kv(k_ref, v_ref, extra_of, ka_ref, vt_ref, seq)

    q_t = q_ref[...].astype(F32).T
    piece_row = lax.broadcasted_iota(jnp.int32, (MOBA_SEL_LANE, TQ), 0)
    tile_cols = pl.ds(pl.multiple_of(qt * TQ, TQ), TQ)
    extras = []
    for hh in range(STEP_HEADS):
        far = jnp.full((MOBA_SEL_LANE, TQ), tab_ref[REL_BUCKETS - 1, N_HEADS + STEP_HEADS * group + hh] * LOG2E, F32)
        hi, mid, lo = _pieces(far)
        far_rows = jnp.where(piece_row == 0, hi, jnp.where(piece_row == 1, mid, jnp.where(piece_row == 2, lo, 0.0)))
        extras.append(jnp.concatenate(
            [far_rows, pen_ref[hh, :, tile_cols],
             jnp.zeros((HEAD_DIM - MOBA_SEL_LANE - nblk, TQ), F32)], axis=0))

    qa = _extend_q_t(q_t * (SCALE * LOG2E), extras)
    n_far = jnp.maximum(qt - 1, 0)
    _flash_tail(qa, ka_ref, vt_ref, m_ref, acc_ref, s_ref,qt, 0, lambda hh: [town_ref[hh]], guard=qt == 0)
    _flash_tail(qa, ka_ref, vt_ref, m_ref, acc_ref, s_ref,qt, n_far,
                lambda hh: [tprev_ref[hh], town_ref[hh]], guard=qt >= 1)
    _flash_chunks(qa, ka_ref, vt_ref, m_ref, acc_ref, s_ref,n_far // CHUNK_TILES)
    _pair_output(o_ref, acc_ref)


def _moba(qkv, rel_bias, t_own, t_prev, bsz, seq):
    n = qkv.shape[0]
    qpt = seq // TQ
    koff, voff = BRANCH_WIDTH // STEP_LANES, 2 * BRANCH_WIDTH // STEP_LANES
    q_tile = pl.BlockSpec((TQ, STEP_LANES), lambda b, p, i: (b * qpt + i, p))
    kv_seq = lambda off: pl.BlockSpec((seq, STEP_LANES), lambda b, p, i: (b, off + p))
    head_tables = pl.BlockSpec((STEP_HEADS, MOBA_BLOCK, MOBA_BLOCK), lambda b, p, i: (p, 0, 0))
    return pl.pallas_call(
        functools.partial(_moba_kernel, seq=seq),
        grid=(bsz, N_HEADS // STEP_HEADS, qpt),
        in_specs=[pl.BlockSpec(memory_space=pltpu.SMEM), q_tile, kv_seq(0), kv_seq(koff), kv_seq(voff),
                  head_tables, head_tables],
        out_specs=q_tile,
        out_shape=jax.ShapeDtypeStruct((n, BRANCH_WIDTH), BF16),
        scratch_shapes=[pltpu.VMEM((LANES, STEP_LANES), F32),
                        pltpu.VMEM((STEP_HEADS, seq // MOBA_BLOCK, seq), F32)] + _flash_scratch(seq),
        compiler_params=_params("arbitrary", "arbitrary", "arbitrary"),
        name="moba",
    )(rel_bias, qkv, qkv, qkv, qkv, t_own, t_prev)


def _merge_kernel(x_ref, g_ref, ya_ref, yb_ref, yc_ref, yd_ref, wg_ref, bg_ref, wbr_ref, wout_ref, o_ref):
    x = x_ref[...]
    hb = _rmsnorm(x, g_ref[...]).astype(BF16)
    merged = jnp.zeros((TM, D_MODEL), F32)
    for n, y_ref in enumerate((ya_ref, yb_ref, yc_ref, yd_ref)):
        gz = jnp.dot(hb, wg_ref[:, n * D_MODEL:(n + 1) * D_MODEL], preferred_element_type=F32)
        gate = jax.nn.sigmoid(gz + bg_ref[n:n + 1, :])
        merged = merged + gate * jnp.dot(y_ref[...], wbr_ref[n], preferred_element_type=F32)
    o_ref[...] = x + jnp.dot(merged.astype(BF16), wout_ref[...], preferred_element_type=F32)


def _merge(x, g, ys, wg, bg, wbr, wout, layer):
    n = x.shape[0]
    row = lambda w: pl.BlockSpec((TM, w), lambda i: (i, 0))
    return pl.pallas_call(
        _merge_kernel,
        grid=(n // TM,),
        in_specs=[row(D_MODEL), _resident(g.shape, layer)] + [row(BRANCH_WIDTH)] * N_BRANCH
        + [_resident(w.shape, layer) for w in (wg, bg, wbr, wout)],
        out_specs=row(D_MODEL),
        out_shape=jax.ShapeDtypeStruct((n, D_MODEL), F32),
        compiler_params=_params("arbitrary"),
        name="merge",
    )(x, g, *ys, wg, bg, wbr, wout)


def _ffn_kernel(x_ref, g_ref, wup_ref, cw_ref, cb_ref, wdown_ref, fg_ref, o_ref, halo_ref, buf_ref, a_ref,
                *, final):
    @pl.when(pl.program_id(1) == 0)
    def _():
        halo_ref[...] = jnp.zeros_like(halo_ref)

    x = x_ref[...]
    hb = _rmsnorm(x, g_ref[...]).astype(BF16)
    n_chunks = D_FF // FF_CHUNK
    col_of = lambda c, part: part * D_FF + c * FF_CHUNK

    def up(c):
        for part in range(2):
            col = col_of(c, part)
            buf_ref[c % 2, part, SUBLANES:SUBLANES + TM, :] = jnp.dot(
                hb, wup_ref[:, col:col + FF_CHUNK], preferred_element_type=F32)

    up(0)
    for c in range(n_chunks):
        if c + 1 < n_chunks:
            up(c + 1)
        halves = []
        for part in range(2):
            col = col_of(c, part)
            buf_ref[c % 2, part, 0:SUBLANES, :] = halo_ref[:, col:col + FF_CHUNK]
            halo_ref[:, col:col + FF_CHUNK] = buf_ref[c % 2, part, TM:TM + SUBLANES, :]
            u = cb_ref[:, col:col + FF_CHUNK]
            for k in range(FFN_CONV_WIDTH):
                lo = SUBLANES - (FFN_CONV_WIDTH - 1) + k
                u = u + cw_ref[k:k + 1, col:col + FF_CHUNK] * buf_ref[c % 2, part, lo:lo + TM, :]
            halves.append(u)
        a_ref[:, c * FF_CHUNK:(c + 1) * FF_CHUNK] = (
            (halves[0] * jax.nn.sigmoid(halves[0])) * halves[1]).astype(BF16)
    y = x + jnp.dot(a_ref[...], wdown_ref[...], preferred_element_type=F32)
    o_ref[...] = _rmsnorm(y, fg_ref[...]) if final else y


def _ffn(x, g, wup, cw, cb, wdown, fg, bsz, seq, layer, final):
    n = x.shape[0]
    spt = seq // TM
    row = pl.BlockSpec((TM, D_MODEL), lambda bi, i: (bi * spt + i, 0))
    return pl.pallas_call(
        functools.partial(_ffn_kernel, final=final),
        grid=(bsz, spt),
        in_specs=[row] + [_resident(w.shape, layer) for w in (g, wup, cw, cb, wdown)]
        + [_resident(fg.shape)],
        out_specs=row,
        out_shape=jax.ShapeDtypeStruct((n, D_MODEL), F32),
        scratch_shapes=[pltpu.VMEM((SUBLANES, 2 * D_FF), F32),
                        pltpu.VMEM((2, 2, TM + SUBLANES, FF_CHUNK), F32),
                        pltpu.VMEM((TM, D_FF), BF16)],
        compiler_params=_params("arbitrary", "arbitrary"),
        name="ffn",
    )(x, g, wup, cw, cb, wdown, fg)


def _split_w_in(w_in):
    hd, bw = HEAD_DIM, BRANCH_WIDTH
    o = 0
    qa = w_in[..., o:o + bw]; o += bw
    ka = w_in[..., o:o + SWA_KV_HEADS * hd]; o += SWA_KV_HEADS * hd
    va = w_in[..., o:o + SWA_KV_HEADS * hd]; o += SWA_KV_HEADS * hd
    ub = w_in[..., o:o + 2 * bw]; o += 2 * bw
    qkv_c = w_in[..., o:o + 3 * bw]; o += 3 * bw
    fc = w_in[..., o:o + N_HEADS]; o += N_HEADS
    qkv_d = w_in[..., o:o + 3 * bw]; o += 3 * bw
    gates = w_in[..., o:o + N_BRANCH * D_MODEL]
    dup = lambda w: jnp.concatenate([w[..., i * hd:(i + 1) * hd] for i in (0, 0, 1, 1)], axis=-1)
    wa = jnp.concatenate([qa, dup(ka), dup(va)], axis=-1)
    wf = jnp.pad(fc, ((0, 0), (0, 0), (0, LANES - N_HEADS)))
    return tuple(w.astype(BF16) for w in (wa, ub, qkv_c, wf, qkv_d)), gates.astype(BF16)


def _layer(x, bsz, seq, layer, final, biases, rel_bias, p):
    bias_swa, table_own, table_prev = biases
    za, zb, zc, zf, zd = _inproj(x, p["ln1_g"], p["w_in_groups"], layer)
    ya = _swa(za, p["sinks"], bias_swa, seq, layer)
    yb = _conformer(zb, p["conv_w"], p["conv_b"], p["conv_ln_g"], p["conv_ln_b"], bsz, seq, layer)
    kx, qx = _fgate(zf, p["b_fgate"], bsz, seq, layer)
    yc = _fox(zc, kx, qx, bsz, seq)
    yd = _moba(zd, rel_bias, table_own, table_prev, bsz, seq)
    x = _merge(x, p["ln1_g"], (ya, yb, yc, yd), p["w_gates"], p["b_gate"], p["w_br"], p["w_out"], layer)
    return _ffn(x, p["ln2_g"], p["w_up"], p["ffn_conv_w"], p["ffn_conv_b"], p["w_down"], p["final_g"],
                bsz, seq, layer, final)


def kernel(x, ln1_g, w_in, b_gate, b_fgate, sinks, conv_w, conv_b, conv_ln_g, conv_ln_b, w_br, w_out,
           ln2_g, w_up, ffn_conv_w, ffn_conv_b, w_down, rel_bias, final_g):
    bsz, seq, _ = x.shape
    depth = w_in.shape[0]
    assert seq % TM == 0 and seq % KV_ROWS == 0 and TQ == TK == MOBA_BLOCK
    assert MOBA_SEL_LANE + seq // MOBA_BLOCK <= HEAD_DIM
    biases = _expand_bias(rel_bias)
    rows = lambda v: v.reshape(depth, 1, -1)
    w_in_groups, w_gates = _split_w_in(w_in)
    params = dict(
        ln1_g=rows(ln1_g), w_in_groups=w_in_groups, w_gates=w_gates, b_gate=b_gate,
        b_fgate=rows(jnp.pad(b_fgate, ((0, 0), (0, LANES - N_HEADS)))), sinks=sinks,
        conv_w=conv_w, conv_b=rows(conv_b), conv_ln_g=rows(conv_ln_g), conv_ln_b=rows(conv_ln_b),
        w_br=w_br.astype(BF16), w_out=w_out.astype(BF16), ln2_g=rows(ln2_g), w_up=w_up.astype(BF16),
        ffn_conv_w=ffn_conv_w, ffn_conv_b=rows(ffn_conv_b), w_down=w_down.astype(BF16),
        final_g=final_g.reshape(1, -1))
    y = x.reshape(bsz * seq, D_MODEL)
    for layer in range(depth):
        y = _layer(y, bsz, seq, layer, layer == depth - 1, biases, rel_bias, params)
    return y.reshape(bsz, seq, D_MODEL)
```

```python
import functools
import math

import jax
import jax.numpy as jnp
import numpy as np
from jax import lax
from jax.experimental import pallas as pl
from jax.experimental.pallas import tpu as pltpu

D_MODEL = 1024
HEAD_DIM = 64
N_BRANCH = 4
BRANCH_WIDTH = 512
N_HEADS = 8
N_PAIRS = N_HEADS // 2
SWA_KV_HEADS = 2
SWA_BLOCK = 128
CONV_WIDTH = 31
MOBA_BLOCK = 256
MOBA_TOPK = 3
D_FF = 2816
FFN_CONV_WIDTH = 3
REL_BUCKETS = 32
REL_MAX_DIST = 128
EPS = 1e-6
NEG = -1e30
SCALE = HEAD_DIM ** -0.5
LOG2E = math.log2(math.e)

LANES = 128
SUBLANES = 8
VMEM_LIMIT = 56 * 1024 * 1024

TM = 512
TQ = 256
TK = 256
CHUNK_TILES = 16
SCORE_LOOKAHEAD = 3
KV_ROWS = 512
SELECT_COLS = 1024
CUM_T = 256
FF_CHUNK = 256
CONV_ROWS = 64
CONV_HALO = 32

N_PIECES = 3
PIECE_ROWS = SUBLANES * N_PIECES
MOBA_SEL_LANE = SUBLANES
STEP_HEADS = 2
STEP_LANES = STEP_HEADS * HEAD_DIM
V_ROWS = HEAD_DIM + 16

BF16 = jnp.bfloat16
F32 = jnp.float32


def _params(*sem):
    return pltpu.CompilerParams(dimension_semantics=sem, vmem_limit_bytes=VMEM_LIMIT)


def _resident(shape, layer=None):
    if layer is None:
        zeros = (0,) * len(shape)
        return pl.BlockSpec(shape, lambda *_: zeros, pipeline_mode=pl.Buffered(1))
    index = (layer,) + (0,) * (len(shape) - 1)
    return pl.BlockSpec((None,) + tuple(shape[1:]), lambda *_: index, pipeline_mode=pl.Buffered(1))


def _rmsnorm(x, g):
    return (x * lax.rsqrt(jnp.mean(x * x, axis=-1, keepdims=True) + EPS)) * g


def _pieces(x):
    hi = x.astype(BF16).astype(F32)
    rest = x - hi
    mid = rest.astype(BF16).astype(F32)
    return hi, mid, rest - mid


def _rel_bucket_np(dist):
    n = np.maximum(dist, 0)
    max_exact = REL_BUCKETS // 2
    nf = np.maximum(n, 1).astype(np.float64)
    large = max_exact + (np.log(nf / max_exact) / math.log(REL_MAX_DIST / max_exact)
                         * (REL_BUCKETS - max_exact) + 1e-9).astype(np.int32)
    large = np.minimum(large, REL_BUCKETS - 1)
    return np.where(n < max_exact, n, large).astype(np.int32)


def _bias_kernel(tab_ref, bs_ref, bo_ref, bp_ref, os_ref, oo_ref, op_ref):
    os_ref[...] = jnp.zeros_like(os_ref)
    oo_ref[...] = jnp.zeros_like(oo_ref)
    op_ref[...] = jnp.zeros_like(op_ref)
    bs, bo, bp = bs_ref[...], bo_ref[...], bp_ref[...]

    def body(b, carry):
        ms, mo, mp = bs == b, bo == b, bp == b
        for h in range(N_HEADS):
            os_ref[h] = jnp.where(ms, tab_ref[b, h], os_ref[h])
            oo_ref[h] = jnp.where(mo, tab_ref[b, N_HEADS + h], oo_ref[h])
            op_ref[h] = jnp.where(mp, tab_ref[b, N_HEADS + h], op_ref[h])
        return carry

    lax.fori_loop(0, REL_BUCKETS, body, 0)
    causal = (lax.broadcasted_iota(jnp.int32, (MOBA_BLOCK, MOBA_BLOCK), 0)
              <= lax.broadcasted_iota(jnp.int32, (MOBA_BLOCK, MOBA_BLOCK), 1))
    for h in range(N_HEADS):
        far = tab_ref[REL_BUCKETS - 1, N_HEADS + h]
        oo_ref[h] = jnp.where(causal, (oo_ref[h] - far) * LOG2E, NEG)
        op_ref[h] = (op_ref[h] - far) * LOG2E


def _expand_bias(rel_bias):
    qi = np.arange(SWA_BLOCK)[:, None]
    kj = np.arange(2 * SWA_BLOCK)[None, :]
    b_swa = _rel_bucket_np(qi + SWA_BLOCK - kj)
    qi = np.arange(MOBA_BLOCK)[:, None]
    kj = np.arange(MOBA_BLOCK)[None, :]
    b_own = np.ascontiguousarray(_rel_bucket_np(qi - kj).T)
    b_prev = np.ascontiguousarray(_rel_bucket_np(qi - kj + MOBA_BLOCK).T)
    vm = pl.BlockSpec(memory_space=pltpu.VMEM)
    return pl.pallas_call(
        _bias_kernel,
        out_shape=(jax.ShapeDtypeStruct((N_HEADS, SWA_BLOCK, 2 * SWA_BLOCK), F32),
                   jax.ShapeDtypeStruct((N_HEADS, MOBA_BLOCK, MOBA_BLOCK), F32),
                   jax.ShapeDtypeStruct((N_HEADS, MOBA_BLOCK, MOBA_BLOCK), F32)),
        in_specs=[pl.BlockSpec(memory_space=pltpu.SMEM), vm, vm, vm],
        out_specs=(vm, vm, vm),
        name="bias_expand",
    )(rel_bias, jnp.asarray(b_swa), jnp.asarray(b_own), jnp.asarray(b_prev))


def _inproj_kernel(x_ref, g_ref, wa_ref, wb_ref, wc_ref, wf_ref, wd_ref,
                   oa_ref, ob_ref, oc_ref, of_ref, od_ref):
    h = _rmsnorm(x_ref[...], g_ref[...])
    hb = h.astype(BF16)
    oa_ref[...] = jnp.dot(hb, wa_ref[...], preferred_element_type=F32).astype(BF16)
    ob_ref[...] = jnp.dot(hb, wb_ref[...], preferred_element_type=F32)
    oc_ref[...] = jnp.dot(hb, wc_ref[...], preferred_element_type=F32).astype(BF16)
    od_ref[...] = jnp.dot(hb, wd_ref[...], preferred_element_type=F32).astype(BF16)
    of_ref[...] = jnp.dot(hb, wf_ref[...], preferred_element_type=F32)


def _inproj(x, g, ws, layer):
    n = x.shape[0]
    row = lambda w: pl.BlockSpec((TM, w), lambda i: (i, 0))
    dts = (BF16, F32, BF16, F32, BF16)
    return pl.pallas_call(
        _inproj_kernel,
        grid=(n // TM,),
        in_specs=[row(D_MODEL), _resident(g.shape, layer)] + [_resident(w.shape, layer) for w in ws],
        out_specs=tuple(row(w.shape[-1]) for w in ws),
        out_shape=tuple(jax.ShapeDtypeStruct((n, w.shape[-1]), dt) for w, dt in zip(ws, dts)),
        compiler_params=_params("arbitrary"),
        name="inproj",
    )(x, g, *ws)


def _swa_kernel(sink_ref, q_ref, kc_ref, kp_ref, vc_ref, vp_ref, bias_ref, o_ref, s_ref, *,
                blocks_per_seq, layer):
    blk = pl.program_id(0) % blocks_per_seq
    kk = jnp.concatenate([kp_ref[...], kc_ref[...]], axis=0)
    vv = jnp.concatenate([vp_ref[...], vc_ref[...]], axis=0)
    qi = lax.broadcasted_iota(jnp.int32, (SWA_BLOCK, 2 * SWA_BLOCK), 0)
    kj = lax.broadcasted_iota(jnp.int32, (SWA_BLOCK, 2 * SWA_BLOCK), 1)
    dist = qi + SWA_BLOCK - kj
    mask = (dist >= 0) & (dist < SWA_BLOCK) & ((blk > 0) | (kj >= SWA_BLOCK))
    lane = lax.broadcasted_iota(jnp.int32, (SWA_BLOCK, LANES), 1)
    upper = lane >= HEAD_DIM
    kv_of = lambda h: h // (N_HEADS // SWA_KV_HEADS)
    for h in range(N_HEADS):
        q_pair = q_ref[:, (h // 2) * LANES:(h // 2 + 1) * LANES] * SCALE
        q_h = jnp.where(upper == (h % 2 == 1), q_pair, jnp.zeros_like(q_pair))
        k_kv = kk[:, kv_of(h) * LANES:(kv_of(h) + 1) * LANES]
        s_ref[h] = lax.dot_general(q_h, k_kv, (((1,), (1,)), ((), ())), preferred_element_type=F32)
    probs = []
    for h in range(N_HEADS):
        logits = jnp.where(mask, s_ref[h] + bias_ref[h], NEG)
        sink = sink_ref[layer, h]
        m = jnp.maximum(jnp.max(logits, axis=-1, keepdims=True), sink)
        p = jnp.exp(logits - m)
        denom = jnp.sum(p, axis=-1, keepdims=True) + jnp.exp(sink - m)
        probs.append((p / denom).astype(BF16))
    outs = [jnp.dot(probs[h], vv[:, kv_of(h) * LANES:(kv_of(h) + 1) * LANES], preferred_element_type=F32)
            for h in range(N_HEADS)]
    for pair in range(N_PAIRS):
        o_ref[:, pair * LANES:(pair + 1) * LANES] = jnp.where(
            upper, outs[2 * pair + 1], outs[2 * pair]).astype(BF16)


def _swa(qkv, sinks, bias, seq, layer):
    n = qkv.shape[0]
    bps = seq // SWA_BLOCK
    kcol, vcol = BRANCH_WIDTH // (2 * LANES), BRANCH_WIDTH // (2 * LANES) + 1
    prev = lambda i: jnp.where(i % bps == 0, i, i - 1)
    return pl.pallas_call(
        functools.partial(_swa_kernel, blocks_per_seq=bps, layer=layer),
        grid=(n // SWA_BLOCK,),
        in_specs=[pl.BlockSpec(memory_space=pltpu.SMEM),
                  pl.BlockSpec((SWA_BLOCK, BRANCH_WIDTH), lambda i: (i, 0)),
                  pl.BlockSpec((SWA_BLOCK, 2 * LANES), lambda i: (i, kcol)),
                  pl.BlockSpec((SWA_BLOCK, 2 * LANES), lambda i: (prev(i), kcol)),
                  pl.BlockSpec((SWA_BLOCK, 2 * LANES), lambda i: (i, vcol)),
                  pl.BlockSpec((SWA_BLOCK, 2 * LANES), lambda i: (prev(i), vcol)),
                  _resident(bias.shape)],
        out_specs=pl.BlockSpec((SWA_BLOCK, BRANCH_WIDTH), lambda i: (i, 0)),
        out_shape=jax.ShapeDtypeStruct((n, BRANCH_WIDTH), BF16),
        scratch_shapes=[pltpu.VMEM((N_HEADS, SWA_BLOCK, 2 * SWA_BLOCK), F32)],
        compiler_params=_params("arbitrary"),
        name="swa",
    )(sinks, qkv, qkv, qkv, qkv, qkv, bias)


def _conformer_kernel(u_ref, w_ref, b_ref, g_ref, beta_ref, o_ref, buf_ref, y_ref):
    @pl.when(pl.program_id(1) == 0)
    def _():
        buf_ref[0:CONV_HALO, :] = jnp.zeros((CONV_HALO, BRANCH_WIDTH), F32)

    a = u_ref[:, 0:BRANCH_WIDTH]
    gt = u_ref[:, BRANCH_WIDTH:2 * BRANCH_WIDTH]
    buf_ref[CONV_HALO:CONV_HALO + TM, :] = a * jax.nn.sigmoid(gt)
    n_shifts = SUBLANES
    for r in range(TM // CONV_ROWS):
        row0 = r * CONV_ROWS
        for cg in range(BRANCH_WIDTH // LANES):
            cols = slice(cg * LANES, (cg + 1) * LANES)
            slabs = []
            for step in range((CONV_WIDTH - 1) // n_shifts + 1):
                lo = CONV_HALO - n_shifts * (step + 1) + row0
                slabs.append(buf_ref[lo:lo + CONV_ROWS + n_shifts, cols])
            y = None
            for shift in range(n_shifts):
                z = None
                for step in range((CONV_WIDTH - 1 - shift) // n_shifts + 1):
                    k = CONV_WIDTH - 1 - (n_shifts * step + shift)
                    term = w_ref[k:k + 1, cols] * slabs[step]
                    z = term if z is None else z + term
                shifted = z[n_shifts - shift:n_shifts - shift + CONV_ROWS]
                y = shifted if y is None else y + shifted
            y_ref[row0:row0 + CONV_ROWS, cols] = y
        y = y_ref[row0:row0 + CONV_ROWS, :] + b_ref[...]
        mu = jnp.mean(y, axis=-1, keepdims=True)
        yc = y - mu
        var = jnp.mean(yc * yc, axis=-1, keepdims=True)
        z = yc * lax.rsqrt(var + EPS) * g_ref[...] + beta_ref[...]
        o_ref[r * CONV_ROWS:(r + 1) * CONV_ROWS, :] = (z * jax.nn.sigmoid(z)).astype(BF16)
    buf_ref[0:CONV_HALO, :] = buf_ref[TM:TM + CONV_HALO, :]


def _conformer(u, w, b, g, beta, bsz, seq, layer):
    n = u.shape[0]
    spt = seq // TM
    return pl.pallas_call(
        _conformer_kernel,
        grid=(bsz, spt),
        in_specs=[pl.BlockSpec((TM, 2 * BRANCH_WIDTH), lambda bi, i: (bi * spt + i, 0)),
                  _resident(w.shape, layer), _resident(b.shape, layer), _resident(g.shape, layer),
                  _resident(beta.shape, layer)],
        out_specs=pl.BlockSpec((TM, BRANCH_WIDTH), lambda bi, i: (bi * spt + i, 0)),
        out_shape=jax.ShapeDtypeStruct((n, BRANCH_WIDTH), BF16),
        scratch_shapes=[pltpu.VMEM((TM + CONV_HALO, BRANCH_WIDTH), F32),
                        pltpu.VMEM((TM, BRANCH_WIDTH), F32)],
        compiler_params=_params("arbitrary", "arbitrary"),
        name="conformer",
    )(u, w, b, g, beta)


def _extra_base(h):
    return (h // 2) * LANES + HEAD_DIM * (1 - h % 2)


def _fgate_placement():
    pk = np.zeros((LANES, BRANCH_WIDTH), np.float32)
    pq = np.zeros((LANES, BRANCH_WIDTH), np.float32)
    for h in range(N_HEADS):
        base = _extra_base(h)
        for i in range(N_PIECES):
            pk[SUBLANES * i + h, base + i] = 1.0
            pq[PIECE_ROWS, base + i] = -1.0
            pk[PIECE_ROWS, base + N_PIECES + i] = 1.0
            pq[SUBLANES * i + h, base + N_PIECES + i] = 1.0
    return jnp.asarray(pk, BF16), jnp.asarray(pq, BF16)


def _fgate_kernel(f_ref, b_ref, pk_ref, pq_ref, kx_ref, qx_ref, carry_ref):
    @pl.when(pl.program_id(1) == 0)
    def _():
        carry_ref[...] = jnp.zeros_like(carry_ref)

    z = f_ref[...] + b_ref[...]
    log_f = -(jnp.maximum(-z, 0.0) + jnp.log1p(jnp.exp(-jnp.abs(z))))
    ri = lax.broadcasted_iota(jnp.int32, (CUM_T, CUM_T), 0)
    ci = lax.broadcasted_iota(jnp.int32, (CUM_T, CUM_T), 1)
    tri = (ci <= ri).astype(F32)
    c = jnp.dot(tri, log_f, preferred_element_type=F32,
                precision=lax.Precision.HIGHEST) + carry_ref[0:1, :]
    carry_ref[...] = jnp.broadcast_to(c[CUM_T - 1:CUM_T, :], carry_ref.shape)
    hi, mid, lo = _pieces(c * LOG2E)
    lane = lax.broadcasted_iota(jnp.int32, (CUM_T, LANES), 1)
    packed = jnp.where(lane < SUBLANES, hi,
                       jnp.where(lane < 2 * SUBLANES, pltpu.roll(mid, SUBLANES, axis=1),
                                 jnp.where(lane < PIECE_ROWS, pltpu.roll(lo, 2 * SUBLANES, axis=1),
                                           jnp.where(lane == PIECE_ROWS, 1.0, 0.0)))).astype(BF16)
    kx_ref[...] = jnp.dot(packed, pk_ref[...], preferred_element_type=F32).astype(BF16)
    qx_ref[...] = jnp.dot(packed, pq_ref[...], preferred_element_type=F32).astype(BF16)


def _fgate(f, b, bsz, seq, layer):
    n = f.shape[0]
    spt = seq // CUM_T
    pk, pq = _fgate_placement()
    row = lambda w: pl.BlockSpec((CUM_T, w), lambda bi, i: (bi * spt + i, 0))
    return pl.pallas_call(
        _fgate_kernel,
        grid=(bsz, spt),
        in_specs=[row(LANES), _resident(b.shape, layer), _resident(pk.shape), _resident(pq.shape)],
        out_specs=(row(BRANCH_WIDTH), row(BRANCH_WIDTH)),
        out_shape=(jax.ShapeDtypeStruct((n, BRANCH_WIDTH), BF16),
                   jax.ShapeDtypeStruct((n, BRANCH_WIDTH), BF16)),
        scratch_shapes=[pltpu.VMEM((SUBLANES, LANES), F32)],
        compiler_params=_params("arbitrary", "arbitrary"),
        name="fgate_cumsum",
    )(f, b, pk, pq)


def _build_kv(k_ref, v_ref, extra_of, ka_ref, vt_ref, seq):
    lane = lax.broadcasted_iota(jnp.int32, (KV_ROWS, LANES), 1)
    lower = lane < HEAD_DIM
    ones_rows = jnp.where(lax.broadcasted_iota(jnp.int32, (V_ROWS - HEAD_DIM, KV_ROWS), 0) == 0, 1.0, 0.0)

    def body(i, carry):
        rows = pl.ds(pl.multiple_of(i * KV_ROWS, KV_ROWS), KV_ROWS)
        extra = extra_of(i, rows)
        for pair in range(STEP_HEADS // 2):
            lanes = slice(pair * LANES, (pair + 1) * LANES)
            k, x = k_ref[rows, lanes], extra[:, lanes]
            ka_ref[2 * pair, rows, :] = jnp.where(lower, k, x)
            ka_ref[2 * pair + 1, rows, :] = jnp.where(lower, x, k)
            v_t = v_ref[rows, lanes].astype(F32).T
            for hh in range(2):
                vt_ref[2 * pair + hh, :, rows] = jnp.concatenate(
                    [v_t[hh * HEAD_DIM:(hh + 1) * HEAD_DIM], ones_rows], axis=0).astype(BF16)
        return carry

    lax.fori_loop(0, seq // KV_ROWS, body, 0)


def _extend_q_t(q_t, extras_t):
    out = []
    for hh in range(STEP_HEADS):
        q_h = q_t[hh * HEAD_DIM:(hh + 1) * HEAD_DIM]
        parts = [q_h, extras_t[hh]] if hh % 2 == 0 else [extras_t[hh], q_h]
        out.append(jnp.concatenate(parts, axis=0).astype(BF16))
    return out


def _flash_step(qa_t, ka_ref, vt_ref, m_ref, acc_ref, s_ref,start, n_tiles, tables=None, first=False):
    heads = range(STEP_HEADS)
    m = [None if first else m_ref[hh, 0:1, :] for hh in heads]
    acc = [None if first else acc_ref[hh] for hh in heads]
    n_plain = n_tiles - (len(tables(0)) if tables else 0)
    tile_rows = lambda t: pl.ds(pl.multiple_of(start + t * TK, TK), TK)
    ring = SCORE_LOOKAHEAD + 1

    def score(t):
        for hh in heads:
            s_ref[t % ring, hh] = jnp.dot(ka_ref[hh, tile_rows(t), :], qa_t[hh],
                                          preferred_element_type=F32)

    for t in range(min(SCORE_LOOKAHEAD, n_tiles)):
        score(t)
    for t in range(n_tiles):
        rows = tile_rows(t)
        if t + SCORE_LOOKAHEAD < n_tiles:
            score(t + SCORE_LOOKAHEAD)
        for hh in heads:
            s = s_ref[t % ring, hh]
            if t >= n_plain:
                s = s + tables(hh)[t - n_plain]
            col_max = jnp.max(s, axis=0, keepdims=True)
            m_new = col_max if m[hh] is None else jnp.maximum(m[hh], col_max)
            p = jnp.exp2(s - m_new).astype(BF16)
            pv = jnp.dot(vt_ref[hh, :, rows], p, preferred_element_type=F32)
            acc[hh] = pv if acc[hh] is None else jnp.exp2(m[hh] - m_new) * acc[hh] + pv
            m[hh] = m_new
    for hh in heads:
        m_ref[hh] = jnp.broadcast_to(m[hh], (SUBLANES, TQ))
        acc_ref[hh] = acc[hh]


def _pair_output(o_ref, acc_ref):
    out_t = jnp.concatenate(
        [acc_ref[hh, 0:HEAD_DIM, :] / acc_ref[hh, HEAD_DIM:HEAD_DIM + 1, :] for hh in range(STEP_HEADS)],
        axis=0)
    o_ref[...] = out_t.T.astype(BF16)


def _flash_scratch(seq):
    return [pltpu.VMEM((STEP_HEADS, seq, LANES), BF16), pltpu.VMEM((STEP_HEADS, V_ROWS, seq), BF16),
            pltpu.VMEM((STEP_HEADS, SUBLANES, TQ), F32), pltpu.VMEM((STEP_HEADS, V_ROWS, TQ), F32),
            pltpu.VMEM((SCORE_LOOKAHEAD + 1, STEP_HEADS, TK, TQ), F32)]


def _flash_chunks(qa, ka_ref, vt_ref, m_ref, acc_ref, s_ref,n_chunks):
    def chunk(c, carry):
        _flash_step(qa, ka_ref, vt_ref, m_ref, acc_ref, s_ref,c * (CHUNK_TILES * TK), CHUNK_TILES)
        return carry

    lax.fori_loop(0, n_chunks, chunk, 0)


def _flash_tail(qa, ka_ref, vt_ref, m_ref, acc_ref, s_ref,last_tile, n_loose, tail_tables, guard=None):
    rem = n_loose % CHUNK_TILES
    for r in range(CHUNK_TILES):
        if isinstance(rem, int) and rem != r:
            continue
        tiles = r + len(tail_tables(0))
        cond = guard if isinstance(rem, int) else (rem == r if guard is None else (rem == r) & guard)

        @pl.when(cond)
        def _(tiles=tiles):
            _flash_step(qa, ka_ref, vt_ref, m_ref, acc_ref, s_ref,(last_tile + 1 - tiles) * TK, tiles,
                        tables=tail_tables, first=True)


def _fox_kernel(q_ref, qx_ref, k_ref, v_ref, kx_ref, o_ref, ka_ref, vt_ref, m_ref, acc_ref, s_ref,*, seq):
    _build_kv(k_ref, v_ref, lambda i, rows: kx_ref[rows, :], ka_ref, vt_ref, seq)
    causal = jnp.where(lax.broadcasted_iota(jnp.int32, (TK, TQ), 0)
                       <= lax.broadcasted_iota(jnp.int32, (TK, TQ), 1), 0.0, NEG)

    def query_tile(qt, carry):
        rows = pl.ds(pl.multiple_of(qt * TQ, TQ), TQ)
        q_t = (q_ref[rows, :].astype(F32) * (SCALE * LOG2E)).T
        qx_t = qx_ref[rows, :].astype(F32).T
        qa = _extend_q_t(q_t, [qx_t[(hh ^ 1) * HEAD_DIM:((hh ^ 1) + 1) * HEAD_DIM]
                               for hh in range(STEP_HEADS)])
        _flash_tail(qa, ka_ref, vt_ref, m_ref, acc_ref, s_ref,qt, qt, lambda hh: [causal])
        _flash_chunks(qa, ka_ref, vt_ref, m_ref, acc_ref, s_ref,qt // CHUNK_TILES)
        _pair_output(o_ref.at[rows, :], acc_ref)
        return carry

    lax.fori_loop(0, seq // TQ, query_tile, 0)


def _fox(qkv, kx, qx, bsz, seq):
    n = qkv.shape[0]
    koff, voff = BRANCH_WIDTH // STEP_LANES, 2 * BRANCH_WIDTH // STEP_LANES
    per_seq = lambda off: pl.BlockSpec((seq, STEP_LANES), lambda b, p: (b, off + p))
    return pl.pallas_call(
        functools.partial(_fox_kernel, seq=seq),
        grid=(bsz, N_HEADS // STEP_HEADS),
        in_specs=[per_seq(0), per_seq(0), per_seq(koff), per_seq(voff), per_seq(0)],
        out_specs=per_seq(0),
        out_shape=jax.ShapeDtypeStruct((n, BRANCH_WIDTH), BF16),
        scratch_shapes=_flash_scratch(seq),
        compiler_params=_params("arbitrary", "arbitrary"),
        name="fox",
    )(qkv, qx, qkv, qkv, kx)


def _moba_select(qseq_ref, kmean_ref, pen_ref, seq):
    nblk = seq // MOBA_BLOCK
    blk_shift = MOBA_BLOCK.bit_length() - 1
    kmean = kmean_ref[0:nblk, :]
    lane_head = jnp.right_shift(lax.broadcasted_iota(jnp.int32, (nblk, STEP_LANES), 1),
                                HEAD_DIM.bit_length() - 1)
    blk = lax.broadcasted_iota(jnp.int32, (nblk, SELECT_COLS), 0)

    def body(i, carry):
        cols = pl.ds(pl.multiple_of(i * SELECT_COLS, SELECT_COLS), SELECT_COLS)
        q = qseq_ref[cols, :].astype(F32)
        own = jnp.right_shift(i * SELECT_COLS + lax.broadcasted_iota(jnp.int32, (nblk, SELECT_COLS), 1),
                              blk_shift)
        past = blk < own
        for hh in range(STEP_HEADS):
            gate = lax.dot_general(jnp.where(lane_head == hh, kmean, 0.0), q, (((1,), (1,)), ((), ())),
                                   preferred_element_type=F32, precision=lax.Precision.HIGHEST)
            gate = jnp.where(past, gate, NEG)
            sel = blk == own
            for _ in range(MOBA_TOPK):
                top = jnp.max(gate, axis=0, keepdims=True)
                idx = jnp.min(jnp.where(gate == top, blk, nblk), axis=0, keepdims=True)
                pick = blk == idx
                sel = sel | (pick & past)
                gate = jnp.where(pick, -jnp.inf, gate)
            pen_ref[hh, :, cols] = jnp.where(sel, 0.0, NEG)
        return carry

    lax.fori_loop(0, seq // SELECT_COLS, body, 0)


def _moba_kernel(tab_ref, q_ref, k_ref, v_ref, town_ref, tprev_ref, o_ref,
                 kmean_ref, pen_ref, ka_ref, vt_ref, m_ref, acc_ref, s_ref, *, seq):
    group = pl.program_id(1)
    nblk = seq // MOBA_BLOCK
    blk_shift = MOBA_BLOCK.bit_length() - 1

    blk = lax.broadcasted_iota(jnp.int32, (LANES, seq), 0)
    pos = lax.broadcasted_iota(jnp.int32, (LANES, seq), 1)
    member = jnp.where(jnp.right_shift(pos, blk_shift) == blk, 1.0 / MOBA_BLOCK, 0.0).astype(BF16)
    kmean_ref[...] = jnp.dot(member, k_ref[...], preferred_element_type=F32)
    _moba_select(q_ref, kmean_ref, pen_ref, seq)

    def extra_of(i, rows):
        xl = jnp.bitwise_and(lax.broadcasted_iota(jnp.int32, (KV_ROWS, STEP_LANES), 1), HEAD_DIM - 1)
        row_blk = jnp.right_shift(
            i * KV_ROWS + lax.broadcasted_iota(jnp.int32, (KV_ROWS, STEP_LANES), 0), blk_shift)
        hit = (xl < N_PIECES) | (xl - MOBA_SEL_LANE == row_blk)
        return jnp.where(hit, 1.0, 0.0).astype(BF16)

    _build_kv(k_ref, v_ref, extra_of, ka_ref, vt_ref, seq)
    piece_row = lax.broadcasted_iota(jnp.int32, (MOBA_SEL_LANE, TQ), 0)
    far_rows = []
    for hh in range(STEP_HEADS):
        far = jnp.full((MOBA_SEL_LANE, TQ), tab_ref[REL_BUCKETS - 1, N_HEADS + STEP_HEADS * group + hh] * LOG2E, F32)
        hi, mid, lo = _pieces(far)
        far_rows.append(jnp.where(piece_row == 0, hi, jnp.where(
            piece_row == 1, mid, jnp.where(piece_row == 2, lo, 0.0))))

    def query_tile(qt, carry):
        rows = pl.ds(pl.multiple_of(qt * TQ, TQ), TQ)
        q_t = q_ref[rows, :].astype(F32).T
        extras = [jnp.concatenate(
            [far_rows[hh], pen_ref[hh, :, rows],
             jnp.zeros((HEAD_DIM - MOBA_SEL_LANE - nblk, TQ), F32)], axis=0) for hh in range(STEP_HEADS)]
        qa = _extend_q_t(q_t * (SCALE * LOG2E), extras)
        n_far = jnp.maximum(qt - 1, 0)
        _flash_tail(qa, ka_ref, vt_ref, m_ref, acc_ref, s_ref,qt, 0, lambda hh: [town_ref[hh]],
                    guard=qt == 0)
        _flash_tail(qa, ka_ref, vt_ref, m_ref, acc_ref, s_ref,qt, n_far,
                    lambda hh: [tprev_ref[hh], town_ref[hh]], guard=qt >= 1)
        _flash_chunks(qa, ka_ref, vt_ref, m_ref, acc_ref, s_ref,n_far // CHUNK_TILES)
        _pair_output(o_ref.at[rows, :], acc_ref)
        return carry

    lax.fori_loop(0, seq // TQ, query_tile, 0)


def _moba(qkv, rel_bias, t_own, t_prev, bsz, seq):
    n = qkv.shape[0]
    koff, voff = BRANCH_WIDTH // STEP_LANES, 2 * BRANCH_WIDTH // STEP_LANES
    per_seq = lambda off: pl.BlockSpec((seq, STEP_LANES), lambda b, p: (b, off + p))
    head_tables = pl.BlockSpec((STEP_HEADS, MOBA_BLOCK, MOBA_BLOCK), lambda b, p: (p, 0, 0))
    return pl.pallas_call(
        functools.partial(_moba_kernel, seq=seq),
        grid=(bsz, N_HEADS // STEP_HEADS),
        in_specs=[pl.BlockSpec(memory_space=pltpu.SMEM), per_seq(0), per_seq(koff), per_seq(voff),
                  head_tables, head_tables],
        out_specs=per_seq(0),
        out_shape=jax.ShapeDtypeStruct((n, BRANCH_WIDTH), BF16),
        scratch_shapes=[pltpu.VMEM((LANES, STEP_LANES), F32),
                        pltpu.VMEM((STEP_HEADS, seq // MOBA_BLOCK, seq), F32)] + _flash_scratch(seq),
        compiler_params=_params("arbitrary", "arbitrary"),
        name="moba",
    )(rel_bias, qkv, qkv, qkv, t_own, t_prev)


def _merge_kernel(x_ref, g_ref, ya_ref, yb_ref, yc_ref, yd_ref, wg_ref, bg_ref, wbr_ref, wout_ref, o_ref):
    x = x_ref[...]
    hb = _rmsnorm(x, g_ref[...]).astype(BF16)
    merged = jnp.zeros((TM, D_MODEL), F32)
    for n, y_ref in enumerate((ya_ref, yb_ref, yc_ref, yd_ref)):
        gz = jnp.dot(hb, wg_ref[:, n * D_MODEL:(n + 1) * D_MODEL], preferred_element_type=F32)
        gate = jax.nn.sigmoid(gz + bg_ref[n:n + 1, :])
        merged = merged + gate * jnp.dot(y_ref[...], wbr_ref[n], preferred_element_type=F32)
    o_ref[...] = x + jnp.dot(merged.astype(BF16), wout_ref[...], preferred_element_type=F32)


def _merge(x, g, ys, wg, bg, wbr, wout, layer):
    n = x.shape[0]
    row = lambda w: pl.BlockSpec((TM, w), lambda i: (i, 0))
    return pl.pallas_call(
        _merge_kernel,
        grid=(n // TM,),
        in_specs=[row(D_MODEL), _resident(g.shape, layer)] + [row(BRANCH_WIDTH)] * N_BRANCH
        + [_resident(w.shape, layer) for w in (wg, bg, wbr, wout)],
        out_specs=row(D_MODEL),
        out_shape=jax.ShapeDtypeStruct((n, D_MODEL), F32),
        compiler_params=_params("arbitrary"),
        name="merge",
    )(x, g, *ys, wg, bg, wbr, wout)


def _ffn_kernel(x_ref, g_ref, wup_ref, cw_ref, cb_ref, wdown_ref, fg_ref, o_ref, halo_ref, buf_ref, a_ref,
                *, final):
    @pl.when(pl.program_id(1) == 0)
    def _():
        halo_ref[...] = jnp.zeros_like(halo_ref)

    x = x_ref[...]
    hb = _rmsnorm(x, g_ref[...]).astype(BF16)
    n_chunks = D_FF // FF_CHUNK
    col_of = lambda c, part: part * D_FF + c * FF_CHUNK

    def up(c):
        for part in range(2):
            col = col_of(c, part)
            buf_ref[c % 2, part, SUBLANES:SUBLANES + TM, :] = jnp.dot(
                hb, wup_ref[:, col:col + FF_CHUNK], preferred_element_type=F32)

    up(0)
    for c in range(n_chunks):
        if c + 1 < n_chunks:
            up(c + 1)
        halves = []
        for part in range(2):
            col = col_of(c, part)
            buf_ref[c % 2, part, 0:SUBLANES, :] = halo_ref[:, col:col + FF_CHUNK]
            halo_ref[:, col:col + FF_CHUNK] = buf_ref[c % 2, part, TM:TM + SUBLANES, :]
            u = cb_ref[:, col:col + FF_CHUNK]
            for k in range(FFN_CONV_WIDTH):
                lo = SUBLANES - (FFN_CONV_WIDTH - 1) + k
                u = u + cw_ref[k:k + 1, col:col + FF_CHUNK] * buf_ref[c % 2, part, lo:lo + TM, :]
            halves.append(u)
        a_ref[:, c * FF_CHUNK:(c + 1) * FF_CHUNK] = (
            (halves[0] * jax.nn.sigmoid(halves[0])) * halves[1]).astype(BF16)
    y = x + jnp.dot(a_ref[...], wdown_ref[...], preferred_element_type=F32)
    o_ref[...] = _rmsnorm(y, fg_ref[...]) if final else y


def _ffn(x, g, wup, cw, cb, wdown, fg, bsz, seq, layer, final):
    n = x.shape[0]
    spt = seq // TM
    row = pl.BlockSpec((TM, D_MODEL), lambda bi, i: (bi * spt + i, 0))
    return pl.pallas_call(
        functools.partial(_ffn_kernel, final=final),
        grid=(bsz, spt),
        in_specs=[row] + [_resident(w.shape, layer) for w in (g, wup, cw, cb, wdown)]
        + [_resident(fg.shape)],
        out_specs=row,
        out_shape=jax.ShapeDtypeStruct((n, D_MODEL), F32),
        scratch_shapes=[pltpu.VMEM((SUBLANES, 2 * D_FF), F32),
                        pltpu.VMEM((2, 2, TM + SUBLANES, FF_CHUNK), F32),
                        pltpu.VMEM((TM, D_FF), BF16)],
        compiler_params=_params("arbitrary", "arbitrary"),
        name="ffn",
    )(x, g, wup, cw, cb, wdown, fg)


def _split_w_in(w_in):
    hd, bw = HEAD_DIM, BRANCH_WIDTH
    o = 0
    qa = w_in[..., o:o + bw]; o += bw
    ka = w_in[..., o:o + SWA_KV_HEADS * hd]; o += SWA_KV_HEADS * hd
    va = w_in[..., o:o + SWA_KV_HEADS * hd]; o += SWA_KV_HEADS * hd
    ub = w_in[..., o:o + 2 * bw]; o += 2 * bw
    qkv_c = w_in[..., o:o + 3 * bw]; o += 3 * bw
    fc = w_in[..., o:o + N_HEADS]; o += N_HEADS
    qkv_d = w_in[..., o:o + 3 * bw]; o += 3 * bw
    gates = w_in[..., o:o + N_BRANCH * D_MODEL]
    dup = lambda w: jnp.concatenate([w[..., i * hd:(i + 1) * hd] for i in (0, 0, 1, 1)], axis=-1)
    wa = jnp.concatenate([qa, dup(ka), dup(va)], axis=-1)
    wf = jnp.pad(fc, ((0, 0), (0, 0), (0, LANES - N_HEADS)))
    return tuple(w.astype(BF16) for w in (wa, ub, qkv_c, wf, qkv_d)), gates.astype(BF16)


def _layer(x, bsz, seq, layer, final, biases, rel_bias, p):
    bias_swa, table_own, table_prev = biases
    za, zb, zc, zf, zd = _inproj(x, p["ln1_g"], p["w_in_groups"], layer)
    ya = _swa(za, p["sinks"], bias_swa, seq, layer)
    yb = _conformer(zb, p["conv_w"], p["conv_b"], p["conv_ln_g"], p["conv_ln_b"], bsz, seq, layer)
    kx, qx = _fgate(zf, p["b_fgate"], bsz, seq, layer)
    yc = _fox(zc, kx, qx, bsz, seq)
    yd = _moba(zd, rel_bias, table_own, table_prev, bsz, seq)
    x = _merge(x, p["ln1_g"], (ya, yb, yc, yd), p["w_gates"], p["b_gate"], p["w_br"], p["w_out"], layer)
    return _ffn(x, p["ln2_g"], p["w_up"], p["ffn_conv_w"], p["ffn_conv_b"], p["w_down"], p["final_g"],
                bsz, seq, layer, final)


def kernel(x, ln1_g, w_in, b_gate, b_fgate, sinks, conv_w, conv_b, conv_ln_g, conv_ln_b, w_br, w_out,
           ln2_g, w_up, ffn_conv_w, ffn_conv_b, w_down, rel_bias, final_g):
    bsz, seq, _ = x.shape
    depth = w_in.shape[0]
    assert seq % TM == 0 and seq % KV_ROWS == 0 and TQ == TK == MOBA_BLOCK
    assert MOBA_SEL_LANE + seq // MOBA_BLOCK <= HEAD_DIM
    biases = _expand_bias(rel_bias)
    rows = lambda v: v.reshape(depth, 1, -1)
    w_in_groups, w_gates = _split_w_in(w_in)
    params = dict(
        ln1_g=rows(ln1_g), w_in_groups=w_in_groups, w_gates=w_gates, b_gate=b_gate,
        b_fgate=rows(jnp.pad(b_fgate, ((0, 0), (0, LANES - N_HEADS)))), sinks=sinks,
        conv_w=conv_w, conv_b=rows(conv_b), conv_ln_g=rows(conv_ln_g), conv_ln_b=rows(conv_ln_b),
        w_br=w_br.astype(BF16), w_out=w_out.astype(BF16), ln2_g=rows(ln2_g), w_up=w_up.astype(BF16),
        ffn_conv_w=ffn_conv_w, ffn_conv_b=rows(ffn_conv_b), w_down=w_down.astype(BF16),
        final_g=final_g.reshape(1, -1))
    y = x.reshape(bsz * seq, D_MODEL)
    for layer in range(depth):
        y = _layer(y, bsz, seq, layer, layer == depth - 1, biases, rel_bias, params)
    return y.reshape(bsz, seq, D_MODEL)
```

```python
import functools
import math

import jax
import jax.numpy as jnp
import numpy as np
from jax import lax
from jax.experimental import pallas as pl
from jax.experimental.pallas import tpu as pltpu

D_MODEL = 1024
HEAD_DIM = 64
N_BRANCH = 4
BRANCH_WIDTH = 512
N_HEADS = 8
N_PAIRS = N_HEADS // 2
SWA_KV_HEADS = 2
SWA_BLOCK = 128
SWA_STEP_BLOCKS = 2
CONV_WIDTH = 31
MOBA_BLOCK = 256
MOBA_TOPK = 3
D_FF = 2816
FFN_CONV_WIDTH = 3
REL_BUCKETS = 32
REL_MAX_DIST = 128
EPS = 1e-6
NEG = -1e30
SCALE = HEAD_DIM ** -0.5
LOG2E = math.log2(math.e)

LANES = 128
SUBLANES = 8
VMEM_LIMIT = 56 * 1024 * 1024

TM = 512
TQ = 256
TK = 256
CHUNK_TILES = 16
SCORE_LOOKAHEAD = 3
KV_ROWS = 512
SELECT_COLS = 1024
CUM_T = 256
FF_CHUNK = 256
CONV_ROWS = 64
CONV_HALO = 32

N_PIECES = 3
PIECE_ROWS = SUBLANES * N_PIECES
MOBA_SEL_LANE = SUBLANES
STEP_HEADS = 2
STEP_LANES = STEP_HEADS * HEAD_DIM
V_ROWS = HEAD_DIM + 16

BF16 = jnp.bfloat16
F32 = jnp.float32


def _params(*sem):
    return pltpu.CompilerParams(dimension_semantics=sem, vmem_limit_bytes=VMEM_LIMIT)


def _resident(shape, layer=None):
    if layer is None:
        zeros = (0,) * len(shape)
        return pl.BlockSpec(shape, lambda *_: zeros, pipeline_mode=pl.Buffered(1))
    index = (layer,) + (0,) * (len(shape) - 1)
    return pl.BlockSpec((None,) + tuple(shape[1:]), lambda *_: index, pipeline_mode=pl.Buffered(1))


def _rmsnorm(x, g):
    return (x * lax.rsqrt(jnp.mean(x * x, axis=-1, keepdims=True) + EPS)) * g


def _pieces(x):
    hi = x.astype(BF16).astype(F32)
    rest = x - hi
    mid = rest.astype(BF16).astype(F32)
    return hi, mid, rest - mid


def _rel_bucket_np(dist):
    n = np.maximum(dist, 0)
    max_exact = REL_BUCKETS // 2
    nf = np.maximum(n, 1).astype(np.float64)
    large = max_exact + (np.log(nf / max_exact) / math.log(REL_MAX_DIST / max_exact)
                         * (REL_BUCKETS - max_exact) + 1e-9).astype(np.int32)
    large = np.minimum(large, REL_BUCKETS - 1)
    return np.where(n < max_exact, n, large).astype(np.int32)


def _bias_kernel(tab_ref, bs_ref, bo_ref, bp_ref, os_ref, oo_ref, op_ref):
    os_ref[...] = jnp.zeros_like(os_ref)
    oo_ref[...] = jnp.zeros_like(oo_ref)
    op_ref[...] = jnp.zeros_like(op_ref)
    bs, bo, bp = bs_ref[...], bo_ref[...], bp_ref[...]

    def body(b, carry):
        ms, mo, mp = bs == b, bo == b, bp == b
        for h in range(N_HEADS):
            os_ref[h] = jnp.where(ms, tab_ref[b, h], os_ref[h])
            oo_ref[h] = jnp.where(mo, tab_ref[b, N_HEADS + h], oo_ref[h])
            op_ref[h] = jnp.where(mp, tab_ref[b, N_HEADS + h], op_ref[h])
        return carry

    lax.fori_loop(0, REL_BUCKETS, body, 0)
    causal = (lax.broadcasted_iota(jnp.int32, (MOBA_BLOCK, MOBA_BLOCK), 0)
              <= lax.broadcasted_iota(jnp.int32, (MOBA_BLOCK, MOBA_BLOCK), 1))
    for h in range(N_HEADS):
        far = tab_ref[REL_BUCKETS - 1, N_HEADS + h]
        oo_ref[h] = jnp.where(causal, (oo_ref[h] - far) * LOG2E, NEG)
        op_ref[h] = (op_ref[h] - far) * LOG2E


def _expand_bias(rel_bias):
    qi = np.arange(SWA_BLOCK)[:, None]
    kj = np.arange(2 * SWA_BLOCK)[None, :]
    b_swa = _rel_bucket_np(qi + SWA_BLOCK - kj)
    qi = np.arange(MOBA_BLOCK)[:, None]
    kj = np.arange(MOBA_BLOCK)[None, :]
    b_own = np.ascontiguousarray(_rel_bucket_np(qi - kj).T)
    b_prev = np.ascontiguousarray(_rel_bucket_np(qi - kj + MOBA_BLOCK).T)
    vm = pl.BlockSpec(memory_space=pltpu.VMEM)
    return pl.pallas_call(
        _bias_kernel,
        out_shape=(jax.ShapeDtypeStruct((N_HEADS, SWA_BLOCK, 2 * SWA_BLOCK), F32),
                   jax.ShapeDtypeStruct((N_HEADS, MOBA_BLOCK, MOBA_BLOCK), F32),
                   jax.ShapeDtypeStruct((N_HEADS, MOBA_BLOCK, MOBA_BLOCK), F32)),
        in_specs=[pl.BlockSpec(memory_space=pltpu.SMEM), vm, vm, vm],
        out_specs=(vm, vm, vm),
        name="bias_expand",
    )(rel_bias, jnp.asarray(b_swa), jnp.asarray(b_own), jnp.asarray(b_prev))


def _inproj_kernel(x_ref, g_ref, wa_ref, wb_ref, wc_ref, wf_ref, wd_ref,
                   oa_ref, ob_ref, oc_ref, of_ref, od_ref):
    h = _rmsnorm(x_ref[...], g_ref[...])
    hb = h.astype(BF16)
    oa_ref[...] = jnp.dot(hb, wa_ref[...], preferred_element_type=F32).astype(BF16)
    ob_ref[...] = jnp.dot(hb, wb_ref[...], preferred_element_type=F32)
    oc_ref[...] = jnp.dot(hb, wc_ref[...], preferred_element_type=F32).astype(BF16)
    od_ref[...] = jnp.dot(hb, wd_ref[...], preferred_element_type=F32).astype(BF16)
    of_ref[...] = jnp.dot(hb, wf_ref[...], preferred_element_type=F32)


def _inproj(x, g, ws, layer):
    n = x.shape[0]
    row = lambda w: pl.BlockSpec((TM, w), lambda i: (i, 0))
    dts = (BF16, F32, BF16, F32, BF16)
    return pl.pallas_call(
        _inproj_kernel,
        grid=(n // TM,),
        in_specs=[row(D_MODEL), _resident(g.shape, layer)] + [_resident(w.shape, layer) for w in ws],
        out_specs=tuple(row(w.shape[-1]) for w in ws),
        out_shape=tuple(jax.ShapeDtypeStruct((n, w.shape[-1]), dt) for w, dt in zip(ws, dts)),
        compiler_params=_params("arbitrary"),
        name="inproj",
    )(x, g, *ws)


def _swa_kernel(sink_ref, q_ref, kc_ref, kp_ref, vc_ref, vp_ref, bias_ref, o_ref, s_ref, *,
                blocks_per_seq, layer):
    first_blk = (pl.program_id(0) * SWA_STEP_BLOCKS) % blocks_per_seq
    qi = lax.broadcasted_iota(jnp.int32, (SWA_BLOCK, 2 * SWA_BLOCK), 0)
    kj = lax.broadcasted_iota(jnp.int32, (SWA_BLOCK, 2 * SWA_BLOCK), 1)
    dist = qi + SWA_BLOCK - kj
    window = (dist >= 0) & (dist < SWA_BLOCK)
    lane = lax.broadcasted_iota(jnp.int32, (SWA_BLOCK, LANES), 1)
    upper = lane >= HEAD_DIM
    kv_of = lambda h: h // (N_HEADS // SWA_KV_HEADS)
    block_rows = lambda sb: slice(sb * SWA_BLOCK, (sb + 1) * SWA_BLOCK)
    keys = lambda sb, cur, prev: jnp.concatenate(
        [prev[...] if sb == 0 else cur[block_rows(sb - 1), :], cur[block_rows(sb), :]], axis=0)
    for sb in range(SWA_STEP_BLOCKS):
        kk = keys(sb, kc_ref, kp_ref)
        for h in range(N_HEADS):
            q_pair = q_ref[block_rows(sb), (h // 2) * LANES:(h // 2 + 1) * LANES] * SCALE
            q_h = jnp.where(upper == (h % 2 == 1), q_pair, jnp.zeros_like(q_pair))
            k_kv = kk[:, kv_of(h) * LANES:(kv_of(h) + 1) * LANES]
            s_ref[sb, h] = lax.dot_general(q_h, k_kv, (((1,), (1,)), ((), ())), preferred_element_type=F32)
    probs = []
    for sb in range(SWA_STEP_BLOCKS):
        mask = window & ((first_blk > 0) | (kj >= SWA_BLOCK)) if sb == 0 else window
        for h in range(N_HEADS):
            logits = jnp.where(mask, s_ref[sb, h] + bias_ref[h], NEG)
            sink = sink_ref[layer, h]
            m = jnp.maximum(jnp.max(logits, axis=-1, keepdims=True), sink)
            p = jnp.exp(logits - m)
            denom = jnp.sum(p, axis=-1, keepdims=True) + jnp.exp(sink - m)
            probs.append((p / denom).astype(BF16))
    for sb in range(SWA_STEP_BLOCKS):
        vv = keys(sb, vc_ref, vp_ref)
        outs = [jnp.dot(probs[sb * N_HEADS + h], vv[:, kv_of(h) * LANES:(kv_of(h) + 1) * LANES],
                        preferred_element_type=F32) for h in range(N_HEADS)]
        for pair in range(N_PAIRS):
            o_ref[block_rows(sb), pair * LANES:(pair + 1) * LANES] = jnp.where(
                upper, outs[2 * pair + 1], outs[2 * pair]).astype(BF16)


def _swa(qkv, sinks, bias, seq, layer):
    n = qkv.shape[0]
    bps = seq // SWA_BLOCK
    assert bps % SWA_STEP_BLOCKS == 0
    step_rows = SWA_STEP_BLOCKS * SWA_BLOCK
    kcol, vcol = BRANCH_WIDTH // (2 * LANES), BRANCH_WIDTH // (2 * LANES) + 1
    prev = lambda i: jnp.where((i * SWA_STEP_BLOCKS) % bps == 0, i * SWA_STEP_BLOCKS, i * SWA_STEP_BLOCKS - 1)
    return pl.pallas_call(
        functools.partial(_swa_kernel, blocks_per_seq=bps, layer=layer),
        grid=(n // step_rows,),
        in_specs=[pl.BlockSpec(memory_space=pltpu.SMEM),
                  pl.BlockSpec((step_rows, BRANCH_WIDTH), lambda i: (i, 0)),
                  pl.BlockSpec((step_rows, 2 * LANES), lambda i: (i, kcol)),
                  pl.BlockSpec((SWA_BLOCK, 2 * LANES), lambda i: (prev(i), kcol)),
                  pl.BlockSpec((step_rows, 2 * LANES), lambda i: (i, vcol)),
                  pl.BlockSpec((SWA_BLOCK, 2 * LANES), lambda i: (prev(i), vcol)),
                  _resident(bias.shape)],
        out_specs=pl.BlockSpec((step_rows, BRANCH_WIDTH), lambda i: (i, 0)),
        out_shape=jax.ShapeDtypeStruct((n, BRANCH_WIDTH), BF16),
        scratch_shapes=[pltpu.VMEM((SWA_STEP_BLOCKS, N_HEADS, SWA_BLOCK, 2 * SWA_BLOCK), F32)],
        compiler_params=_params("arbitrary"),
        name="swa",
    )(sinks, qkv, qkv, qkv, qkv, qkv, bias)


def _conformer_kernel(u_ref, w_ref, b_ref, g_ref, beta_ref, o_ref, buf_ref, y_ref):
    @pl.when(pl.program_id(1) == 0)
    def _():
        buf_ref[0:CONV_HALO, :] = jnp.zeros((CONV_HALO, BRANCH_WIDTH), F32)

    a = u_ref[:, 0:BRANCH_WIDTH]
    gt = u_ref[:, BRANCH_WIDTH:2 * BRANCH_WIDTH]
    buf_ref[CONV_HALO:CONV_HALO + TM, :] = a * jax.nn.sigmoid(gt)
    n_shifts = SUBLANES
    for r in range(TM // CONV_ROWS):
        row0 = r * CONV_ROWS
        for cg in range(BRANCH_WIDTH // LANES):
            cols = slice(cg * LANES, (cg + 1) * LANES)
            slabs = []
            for step in range((CONV_WIDTH - 1) // n_shifts + 1):
                lo = CONV_HALO - n_shifts * (step + 1) + row0
                slabs.append(buf_ref[lo:lo + CONV_ROWS + n_shifts, cols])
            y = None
            for shift in range(n_shifts):
                z = None
                for step in range((CONV_WIDTH - 1 - shift) // n_shifts + 1):
                    k = CONV_WIDTH - 1 - (n_shifts * step + shift)
                    term = w_ref[k:k + 1, cols] * slabs[step]
                    z = term if z is None else z + term
                shifted = z[n_shifts - shift:n_shifts - shift + CONV_ROWS]
                y = shifted if y is None else y + shifted
            y_ref[row0:row0 + CONV_ROWS, cols] = y
        y = y_ref[row0:row0 + CONV_ROWS, :] + b_ref[...]
        mu = jnp.mean(y, axis=-1, keepdims=True)
        yc = y - mu
        var = jnp.mean(yc * yc, axis=-1, keepdims=True)
        z = yc * lax.rsqrt(var + EPS) * g_ref[...] + beta_ref[...]
        o_ref[r * CONV_ROWS:(r + 1) * CONV_ROWS, :] = (z * jax.nn.sigmoid(z)).astype(BF16)
    buf_ref[0:CONV_HALO, :] = buf_ref[TM:TM + CONV_HALO, :]


def _conformer(u, w, b, g, beta, bsz, seq, layer):
    n = u.shape[0]
    spt = seq // TM
    return pl.pallas_call(
        _conformer_kernel,
        grid=(bsz, spt),
        in_specs=[pl.BlockSpec((TM, 2 * BRANCH_WIDTH), lambda bi, i: (bi * spt + i, 0)),
                  _resident(w.shape, layer), _resident(b.shape, layer), _resident(g.shape, layer),
                  _resident(beta.shape, layer)],
        out_specs=pl.BlockSpec((TM, BRANCH_WIDTH), lambda bi, i: (bi * spt + i, 0)),
        out_shape=jax.ShapeDtypeStruct((n, BRANCH_WIDTH), BF16),
        scratch_shapes=[pltpu.VMEM((TM + CONV_HALO, BRANCH_WIDTH), F32),
                        pltpu.VMEM((TM, BRANCH_WIDTH), F32)],
        compiler_params=_params("arbitrary", "arbitrary"),
        name="conformer",
    )(u, w, b, g, beta)


def _extra_base(h):
    return (h // 2) * LANES + HEAD_DIM * (1 - h % 2)


def _fgate_placement():
    pk = np.zeros((LANES, BRANCH_WIDTH), np.float32)
    pq = np.zeros((LANES, BRANCH_WIDTH), np.float32)
    for h in range(N_HEADS):
        base = _extra_base(h)
        for i in range(N_PIECES):
            pk[SUBLANES * i + h, base + i] = 1.0
            pq[PIECE_ROWS, base + i] = -1.0
            pk[PIECE_ROWS, base + N_PIECES + i] = 1.0
            pq[SUBLANES * i + h, base + N_PIECES + i] = 1.0
    return jnp.asarray(pk, BF16), jnp.asarray(pq, BF16)


def _fgate_kernel(f_ref, b_ref, pkq_ref, kx_ref, qx_ref, carry_ref):
    @pl.when(pl.program_id(1) == 0)
    def _():
        carry_ref[...] = jnp.zeros_like(carry_ref)

    z = f_ref[...] + b_ref[...]
    log_f = -(jnp.maximum(-z, 0.0) + jnp.log1p(jnp.exp(-jnp.abs(z))))
    ri = lax.broadcasted_iota(jnp.int32, (CUM_T, CUM_T), 0)
    ci = lax.broadcasted_iota(jnp.int32, (CUM_T, CUM_T), 1)
    tri = jnp.where(ci <= ri, 1.0, 0.0).astype(BF16)
    sums = jnp.dot(tri, jnp.concatenate(_pieces(log_f), axis=1).astype(BF16), preferred_element_type=F32)
    c = (sums[:, 0:LANES] + sums[:, LANES:2 * LANES]) + sums[:, 2 * LANES:3 * LANES] + carry_ref[0:1, :]
    carry_ref[...] = jnp.broadcast_to(c[CUM_T - 1:CUM_T, :], carry_ref.shape)
    hi, mid, lo = _pieces(c * LOG2E)
    lane = lax.broadcasted_iota(jnp.int32, (CUM_T, LANES), 1)
    packed = jnp.where(lane < SUBLANES, hi,
                       jnp.where(lane < 2 * SUBLANES, pltpu.roll(mid, SUBLANES, axis=1),
                                 jnp.where(lane < PIECE_ROWS, pltpu.roll(lo, 2 * SUBLANES, axis=1),
                                           jnp.where(lane == PIECE_ROWS, 1.0, 0.0)))).astype(BF16)
    placed = jnp.dot(packed, pkq_ref[...], preferred_element_type=F32).astype(BF16)
    kx_ref[...] = placed[:, 0:BRANCH_WIDTH]
    qx_ref[...] = placed[:, BRANCH_WIDTH:2 * BRANCH_WIDTH]


def _fgate(f, b, bsz, seq, layer):
    n = f.shape[0]
    spt = seq // CUM_T
    pkq = jnp.concatenate(_fgate_placement(), axis=1)
    row = lambda w: pl.BlockSpec((CUM_T, w), lambda bi, i: (bi * spt + i, 0))
    return pl.pallas_call(
        _fgate_kernel,
        grid=(bsz, spt),
        in_specs=[row(LANES), _resident(b.shape, layer), _resident(pkq.shape)],
        out_specs=(row(BRANCH_WIDTH), row(BRANCH_WIDTH)),
        out_shape=(jax.ShapeDtypeStruct((n, BRANCH_WIDTH), BF16),
                   jax.ShapeDtypeStruct((n, BRANCH_WIDTH), BF16)),
        scratch_shapes=[pltpu.VMEM((SUBLANES, LANES), F32)],
        compiler_params=_params("arbitrary", "arbitrary"),
        name="fgate_cumsum",
    )(f, b, pkq)


def _build_kv(k_ref, v_ref, extra_of, ka_ref, vt_ref, seq):
    lane = lax.broadcasted_iota(jnp.int32, (KV_ROWS, LANES), 1)
    lower = lane < HEAD_DIM
    ones_rows = jnp.where(lax.broadcasted_iota(jnp.int32, (V_ROWS - HEAD_DIM, KV_ROWS), 0) == 0, 1.0, 0.0)

    def body(i, carry):
        rows = pl.ds(pl.multiple_of(i * KV_ROWS, KV_ROWS), KV_ROWS)
        extra = extra_of(i, rows)
        for pair in range(STEP_HEADS // 2):
            lanes = slice(pair * LANES, (pair + 1) * LANES)
            k, x = k_ref[rows, lanes], extra[:, lanes]
            ka_ref[2 * pair, rows, :] = jnp.where(lower, k, x)
            ka_ref[2 * pair + 1, rows, :] = jnp.where(lower, x, k)
            v_t = v_ref[rows, lanes].astype(F32).T
            for hh in range(2):
                vt_ref[2 * pair + hh, :, rows] = jnp.concatenate(
                    [v_t[hh * HEAD_DIM:(hh + 1) * HEAD_DIM], ones_rows], axis=0).astype(BF16)
        return carry

    lax.fori_loop(0, seq // KV_ROWS, body, 0)


def _extend_q_t(q_t, extras_t):
    out = []
    for hh in range(STEP_HEADS):
        q_h = q_t[hh * HEAD_DIM:(hh + 1) * HEAD_DIM]
        parts = [q_h, extras_t[hh]] if hh % 2 == 0 else [extras_t[hh], q_h]
        out.append(jnp.concatenate(parts, axis=0).astype(BF16))
    return out


def _flash_step(qa_t, ka_ref, vt_ref, m_ref, acc_ref, s_ref,start, n_tiles, tables=None, first=False):
    heads = range(STEP_HEADS)
    m = [None if first else m_ref[hh, 0:1, :] for hh in heads]
    acc = [None if first else acc_ref[hh] for hh in heads]
    n_plain = n_tiles - (len(tables(0)) if tables else 0)
    tile_rows = lambda t: pl.ds(pl.multiple_of(start + t * TK, TK), TK)
    ring = SCORE_LOOKAHEAD + 1

    def score(t):
        for hh in heads:
            s_ref[t % ring, hh] = jnp.dot(ka_ref[hh, tile_rows(t), :], qa_t[hh],
                                          preferred_element_type=F32)

    for t in range(min(SCORE_LOOKAHEAD, n_tiles)):
        score(t)
    for t in range(n_tiles):
        rows = tile_rows(t)
        if t + SCORE_LOOKAHEAD < n_tiles:
            score(t + SCORE_LOOKAHEAD)
        for hh in heads:
            s = s_ref[t % ring, hh]
            if t >= n_plain:
                s = s + tables(hh)[t - n_plain]
            col_max = jnp.max(s, axis=0, keepdims=True)
            m_new = col_max if m[hh] is None else jnp.maximum(m[hh], col_max)
            p = jnp.exp2(s - m_new).astype(BF16)
            pv = jnp.dot(vt_ref[hh, :, rows], p, preferred_element_type=F32)
            acc[hh] = pv if acc[hh] is None else jnp.exp2(m[hh] - m_new) * acc[hh] + pv
            m[hh] = m_new
    for hh in heads:
        m_ref[hh] = jnp.broadcast_to(m[hh], (SUBLANES, TQ))
        acc_ref[hh] = acc[hh]


def _pair_output(o_ref, acc_ref):
    out_t = jnp.concatenate(
        [acc_ref[hh, 0:HEAD_DIM, :] / acc_ref[hh, HEAD_DIM:HEAD_DIM + 1, :] for hh in range(STEP_HEADS)],
        axis=0)
    o_ref[...] = out_t.T.astype(BF16)


def _flash_scratch(seq):
    return [pltpu.VMEM((STEP_HEADS, seq, LANES), BF16), pltpu.VMEM((STEP_HEADS, V_ROWS, seq), BF16),
            pltpu.VMEM((STEP_HEADS, SUBLANES, TQ), F32), pltpu.VMEM((STEP_HEADS, V_ROWS, TQ), F32),
            pltpu.VMEM((SCORE_LOOKAHEAD + 1, STEP_HEADS, TK, TQ), F32)]


def _flash_chunks(qa, ka_ref, vt_ref, m_ref, acc_ref, s_ref,n_chunks):
    def chunk(c, carry):
        _flash_step(qa, ka_ref, vt_ref, m_ref, acc_ref, s_ref,c * (CHUNK_TILES * TK), CHUNK_TILES)
        return carry

    lax.fori_loop(0, n_chunks, chunk, 0)


def _flash_tail(qa, ka_ref, vt_ref, m_ref, acc_ref, s_ref,last_tile, n_loose, tail_tables, guard=None):
    rem = n_loose % CHUNK_TILES
    for r in range(CHUNK_TILES):
        if isinstance(rem, int) and rem != r:
            continue
        tiles = r + len(tail_tables(0))
        cond = guard if isinstance(rem, int) else (rem == r if guard is None else (rem == r) & guard)

        def run(tiles=tiles):
            _flash_step(qa, ka_ref, vt_ref, m_ref, acc_ref, s_ref,(last_tile + 1 - tiles) * TK, tiles,
                        tables=tail_tables, first=True)

        if cond is None or isinstance(cond, (bool, np.bool_)):
            if cond is None or cond:
                run()
        else:
            pl.when(cond)(run)


def _fox_kernel(q_ref, qx_ref, k_ref, v_ref, kx_ref, o_ref, ka_ref, vt_ref, m_ref, acc_ref, s_ref,*, seq):
    _build_kv(k_ref, v_ref, lambda i, rows: kx_ref[rows, :], ka_ref, vt_ref, seq)
    causal = jnp.where(lax.broadcasted_iota(jnp.int32, (TK, TQ), 0)
                       <= lax.broadcasted_iota(jnp.int32, (TK, TQ), 1), 0.0, NEG)

    def query_tile(qt, carry):
        rows = pl.ds(pl.multiple_of(qt * TQ, TQ), TQ)
        q_t = (q_ref[rows, :].astype(F32) * (SCALE * LOG2E)).T
        qx_t = qx_ref[rows, :].astype(F32).T
        qa = _extend_q_t(q_t, [qx_t[(hh ^ 1) * HEAD_DIM:((hh ^ 1) + 1) * HEAD_DIM]
                               for hh in range(STEP_HEADS)])
        _flash_tail(qa, ka_ref, vt_ref, m_ref, acc_ref, s_ref,qt, qt, lambda hh: [causal])
        _flash_chunks(qa, ka_ref, vt_ref, m_ref, acc_ref, s_ref,qt // CHUNK_TILES)
        _pair_output(o_ref.at[rows, :], acc_ref)
        return carry

    lax.fori_loop(0, seq // TQ, query_tile, 0)


def _fox(qkv, kx, qx, bsz, seq):
    n = qkv.shape[0]
    koff, voff = BRANCH_WIDTH // STEP_LANES, 2 * BRANCH_WIDTH // STEP_LANES
    per_seq = lambda off: pl.BlockSpec((seq, STEP_LANES), lambda b, p: (b, off + p))
    return pl.pallas_call(
        functools.partial(_fox_kernel, seq=seq),
        grid=(bsz, N_HEADS // STEP_HEADS),
        in_specs=[per_seq(0), per_seq(0), per_seq(koff), per_seq(voff), per_seq(0)],
        out_specs=per_seq(0),
        out_shape=jax.ShapeDtypeStruct((n, BRANCH_WIDTH), BF16),
        scratch_shapes=_flash_scratch(seq),
        compiler_params=_params("arbitrary", "arbitrary"),
        name="fox",
    )(qkv, qx, qkv, qkv, kx)


def _moba_select(qseq_ref, kmean_ref, pen_ref, seq):
    nblk = seq // MOBA_BLOCK
    blk_shift = MOBA_BLOCK.bit_length() - 1
    kmean = kmean_ref[0:nblk, :]
    lane_head = jnp.right_shift(lax.broadcasted_iota(jnp.int32, (nblk, STEP_LANES), 1),
                                HEAD_DIM.bit_length() - 1)
    blk = lax.broadcasted_iota(jnp.int32, (nblk, SELECT_COLS), 0)

    def body(i, carry):
        cols = pl.ds(pl.multiple_of(i * SELECT_COLS, SELECT_COLS), SELECT_COLS)
        q = qseq_ref[cols, :].astype(F32)
        own = jnp.right_shift(i * SELECT_COLS + lax.broadcasted_iota(jnp.int32, (nblk, SELECT_COLS), 1),
                              blk_shift)
        past = blk < own
        for hh in range(STEP_HEADS):
            gate = lax.dot_general(jnp.where(lane_head == hh, kmean, 0.0), q, (((1,), (1,)), ((), ())),
                                   preferred_element_type=F32, precision=lax.Precision.HIGHEST)
            gate = jnp.where(past, gate, NEG)
            sel = blk == own
            for _ in range(MOBA_TOPK):
                top = jnp.max(gate, axis=0, keepdims=True)
                idx = jnp.min(jnp.where(gate == top, blk, nblk), axis=0, keepdims=True)
                pick = blk == idx
                sel = sel | (pick & past)
                gate = jnp.where(pick, -jnp.inf, gate)
            pen_ref[hh, :, cols] = jnp.where(sel, 0.0, NEG)
        return carry

    lax.fori_loop(0, seq // SELECT_COLS, body, 0)


def _moba_kernel(tab_ref, q_ref, k_ref, v_ref, town_ref, tprev_ref, o_ref,
                 kmean_ref, pen_ref, ka_ref, vt_ref, m_ref, acc_ref, s_ref, *, seq):
    group = pl.program_id(1)
    nblk = seq // MOBA_BLOCK
    blk_shift = MOBA_BLOCK.bit_length() - 1

    blk = lax.broadcasted_iota(jnp.int32, (LANES, seq), 0)
    pos = lax.broadcasted_iota(jnp.int32, (LANES, seq), 1)
    member = jnp.where(jnp.right_shift(pos, blk_shift) == blk, 1.0 / MOBA_BLOCK, 0.0).astype(BF16)
    kmean_ref[...] = jnp.dot(member, k_ref[...], preferred_element_type=F32)
    _moba_select(q_ref, kmean_ref, pen_ref, seq)

    def extra_of(i, rows):
        xl = jnp.bitwise_and(lax.broadcasted_iota(jnp.int32, (KV_ROWS, STEP_LANES), 1), HEAD_DIM - 1)
        row_blk = jnp.right_shift(
            i * KV_ROWS + lax.broadcasted_iota(jnp.int32, (KV_ROWS, STEP_LANES), 0), blk_shift)
        hit = (xl < N_PIECES) | (xl - MOBA_SEL_LANE == row_blk)
        return jnp.where(hit, 1.0, 0.0).astype(BF16)

    _build_kv(k_ref, v_ref, extra_of, ka_ref, vt_ref, seq)
    piece_row = lax.broadcasted_iota(jnp.int32, (MOBA_SEL_LANE, TQ), 0)
    far_rows = []
    for hh in range(STEP_HEADS):
        far = jnp.full((MOBA_SEL_LANE, TQ), tab_ref[REL_BUCKETS - 1, N_HEADS + STEP_HEADS * group + hh] * LOG2E, F32)
        hi, mid, lo = _pieces(far)
        far_rows.append(jnp.where(piece_row == 0, hi, jnp.where(
            piece_row == 1, mid, jnp.where(piece_row == 2, lo, 0.0))))

    def query_tile(qt, carry):
        rows = pl.ds(pl.multiple_of(qt * TQ, TQ), TQ)
        q_t = q_ref[rows, :].astype(F32).T
        extras = [jnp.concatenate(
            [far_rows[hh], pen_ref[hh, :, rows],
             jnp.zeros((HEAD_DIM - MOBA_SEL_LANE - nblk, TQ), F32)], axis=0) for hh in range(STEP_HEADS)]
        qa = _extend_q_t(q_t * (SCALE * LOG2E), extras)
        n_far = jnp.maximum(qt - 1, 0)
        _flash_tail(qa, ka_ref, vt_ref, m_ref, acc_ref, s_ref,qt, 0, lambda hh: [town_ref[hh]],
                    guard=qt == 0)
        _flash_tail(qa, ka_ref, vt_ref, m_ref, acc_ref, s_ref,qt, n_far,
                    lambda hh: [tprev_ref[hh], town_ref[hh]], guard=qt >= 1)
        _flash_chunks(qa, ka_ref, vt_ref, m_ref, acc_ref, s_ref,n_far // CHUNK_TILES)
        _pair_output(o_ref.at[rows, :], acc_ref)
        return carry

    lax.fori_loop(0, seq // TQ, query_tile, 0)


def _moba(qkv, rel_bias, t_own, t_prev, bsz, seq):
    n = qkv.shape[0]
    koff, voff = BRANCH_WIDTH // STEP_LANES, 2 * BRANCH_WIDTH // STEP_LANES
    per_seq = lambda off: pl.BlockSpec((seq, STEP_LANES), lambda b, p: (b, off + p))
    head_tables = pl.BlockSpec((STEP_HEADS, MOBA_BLOCK, MOBA_BLOCK), lambda b, p: (p, 0, 0))
    return pl.pallas_call(
        functools.partial(_moba_kernel, seq=seq),
        grid=(bsz, N_HEADS // STEP_HEADS),
        in_specs=[pl.BlockSpec(memory_space=pltpu.SMEM), per_seq(0), per_seq(koff), per_seq(voff),
                  head_tables, head_tables],
        out_specs=per_seq(0),
        out_shape=jax.ShapeDtypeStruct((n, BRANCH_WIDTH), BF16),
        scratch_shapes=[pltpu.VMEM((LANES, STEP_LANES), F32),
                        pltpu.VMEM((STEP_HEADS, seq // MOBA_BLOCK, seq), F32)] + _flash_scratch(seq),
        compiler_params=_params("arbitrary", "arbitrary"),
        name="moba",
    )(rel_bias, qkv, qkv, qkv, t_own, t_prev)


def _merge_kernel(x_ref, g_ref, ya_ref, yb_ref, yc_ref, yd_ref, wg_ref, bg_ref, wbr_ref, wout_ref, o_ref):
    x = x_ref[...]
    hb = _rmsnorm(x, g_ref[...]).astype(BF16)
    merged = jnp.zeros((TM, D_MODEL), F32)
    for n, y_ref in enumerate((ya_ref, yb_ref, yc_ref, yd_ref)):
        gz = jnp.dot(hb, wg_ref[:, n * D_MODEL:(n + 1) * D_MODEL], preferred_element_type=F32)
        gate = jax.nn.sigmoid(gz + bg_ref[n:n + 1, :])
        merged = merged + gate * jnp.dot(y_ref[...], wbr_ref[n], preferred_element_type=F32)
    o_ref[...] = x + jnp.dot(merged.astype(BF16), wout_ref[...], preferred_element_type=F32)


def _merge(x, g, ys, wg, bg, wbr, wout, layer):
    n = x.shape[0]
    row = lambda w: pl.BlockSpec((TM, w), lambda i: (i, 0))
    return pl.pallas_call(
        _merge_kernel,
        grid=(n // TM,),
        in_specs=[row(D_MODEL), _resident(g.shape, layer)] + [row(BRANCH_WIDTH)] * N_BRANCH
        + [_resident(w.shape, layer) for w in (wg, bg, wbr, wout)],
        out_specs=row(D_MODEL),
        out_shape=jax.ShapeDtypeStruct((n, D_MODEL), F32),
        compiler_params=_params("arbitrary"),
        name="merge",
    )(x, g, *ys, wg, bg, wbr, wout)


def _ffn_kernel(x_ref, g_ref, wup_ref, cw_ref, cb_ref, wdown_ref, fg_ref, o_ref, halo_ref, buf_ref, a_ref,
                *, final):
    @pl.when(pl.program_id(1) == 0)
    def _():
        halo_ref[...] = jnp.zeros_like(halo_ref)

    x = x_ref[...]
    hb = _rmsnorm(x, g_ref[...]).astype(BF16)
    n_chunks = D_FF // FF_CHUNK
    col_of = lambda c, part: part * D_FF + c * FF_CHUNK

    def up(c):
        for part in range(2):
            col = col_of(c, part)
            buf_ref[c % 2, part, SUBLANES:SUBLANES + TM, :] = jnp.dot(
                hb, wup_ref[:, col:col + FF_CHUNK], preferred_element_type=F32)

    up(0)
    for c in range(n_chunks):
        if c + 1 < n_chunks:
            up(c + 1)
        halves = []
        for part in range(2):
            col = col_of(c, part)
            buf_ref[c % 2, part, 0:SUBLANES, :] = halo_ref[:, col:col + FF_CHUNK]
            halo_ref[:, col:col + FF_CHUNK] = buf_ref[c % 2, part, TM:TM + SUBLANES, :]
            u = cb_ref[:, col:col + FF_CHUNK]
            for k in range(FFN_CONV_WIDTH):
                lo = SUBLANES - (FFN_CONV_WIDTH - 1) + k
                u = u + cw_ref[k:k + 1, col:col + FF_CHUNK] * buf_ref[c % 2, part, lo:lo + TM, :]
            halves.append(u)
        a_ref[:, c * FF_CHUNK:(c + 1) * FF_CHUNK] = (
            (halves[0] * jax.nn.sigmoid(halves[0])) * halves[1]).astype(BF16)
    y = x + jnp.dot(a_ref[...], wdown_ref[...], preferred_element_type=F32)
    o_ref[...] = _rmsnorm(y, fg_ref[...]) if final else y


def _ffn(x, g, wup, cw, cb, wdown, fg, bsz, seq, layer, final):
    n = x.shape[0]
    spt = seq // TM
    row = pl.BlockSpec((TM, D_MODEL), lambda bi, i: (bi * spt + i, 0))
    return pl.pallas_call(
        functools.partial(_ffn_kernel, final=final),
        grid=(bsz, spt),
        in_specs=[row] + [_resident(w.shape, layer) for w in (g, wup, cw, cb, wdown)]
        + [_resident(fg.shape)],
        out_specs=row,
        out_shape=jax.ShapeDtypeStruct((n, D_MODEL), F32),
        scratch_shapes=[pltpu.VMEM((SUBLANES, 2 * D_FF), F32),
                        pltpu.VMEM((2, 2, TM + SUBLANES, FF_CHUNK), F32),
                        pltpu.VMEM((TM, D_FF), BF16)],
        compiler_params=_params("arbitrary", "arbitrary"),
        name="ffn",
    )(x, g, wup, cw, cb, wdown, fg)


def _split_w_in(w_in):
    hd, bw = HEAD_DIM, BRANCH_WIDTH
    w_in = w_in.astype(BF16)
    o = 0
    qa = w_in[..., o:o + bw]; o += bw
    ka = w_in[..., o:o + SWA_KV_HEADS * hd]; o += SWA_KV_HEADS * hd
    va = w_in[..., o:o + SWA_KV_HEADS * hd]; o += SWA_KV_HEADS * hd
    ub = w_in[..., o:o + 2 * bw]; o += 2 * bw
    qkv_c = w_in[..., o:o + 3 * bw]; o += 3 * bw
    fc = w_in[..., o:o + N_HEADS]; o += N_HEADS
    qkv_d = w_in[..., o:o + 3 * bw]; o += 3 * bw
    gates = w_in[..., o:o + N_BRANCH * D_MODEL]
    dup = lambda w: jnp.concatenate([w[..., i * hd:(i + 1) * hd] for i in (0, 0, 1, 1)], axis=-1)
    wa = jnp.concatenate([qa, dup(ka), dup(va)], axis=-1)
    wf = jnp.pad(fc, ((0, 0), (0, 0), (0, LANES - N_HEADS)))
    return (wa, ub, qkv_c, wf, qkv_d), gates


def _layer(x, bsz, seq, layer, final, biases, rel_bias, p):
    bias_swa, table_own, table_prev = biases
    za, zb, zc, zf, zd = _inproj(x, p["ln1_g"], p["w_in_groups"], layer)
    ya = _swa(za, p["sinks"], bias_swa, seq, layer)
    yb = _conformer(zb, p["conv_w"], p["conv_b"], p["conv_ln_g"], p["conv_ln_b"], bsz, seq, layer)
    kx, qx = _fgate(zf, p["b_fgate"], bsz, seq, layer)
    yc = _fox(zc, kx, qx, bsz, seq)
    yd = _moba(zd, rel_bias, table_own, table_prev, bsz, seq)
    x = _merge(x, p["ln1_g"], (ya, yb, yc, yd), p["w_gates"], p["b_gate"], p["w_br"], p["w_out"], layer)
    return _ffn(x, p["ln2_g"], p["w_up"], p["ffn_conv_w"], p["ffn_conv_b"], p["w_down"], p["final_g"],
                bsz, seq, layer, final)


def kernel(x, ln1_g, w_in, b_gate, b_fgate, sinks, conv_w, conv_b, conv_ln_g, conv_ln_b, w_br, w_out,
           ln2_g, w_up, ffn_conv_w, ffn_conv_b, w_down, rel_bias, final_g):
    bsz, seq, _ = x.shape
    depth = w_in.shape[0]
    assert seq % TM == 0 and seq % KV_ROWS == 0 and TQ == TK == MOBA_BLOCK
    assert MOBA_SEL_LANE + seq // MOBA_BLOCK <= HEAD_DIM
    biases = _expand_bias(rel_bias)
    rows = lambda v: v.reshape(depth, 1, -1)
    w_in_groups, w_gates = _split_w_in(w_in)
    params = dict(
        ln1_g=rows(ln1_g), w_in_groups=w_in_groups, w_gates=w_gates, b_gate=b_gate,
        b_fgate=rows(jnp.pad(b_fgate, ((0, 0), (0, LANES - N_HEADS)))), sinks=sinks,
        conv_w=conv_w, conv_b=rows(conv_b), conv_ln_g=rows(conv_ln_g), conv_ln_b=rows(conv_ln_b),
        w_br=w_br.astype(BF16), w_out=w_out.astype(BF16), ln2_g=rows(ln2_g), w_up=w_up.astype(BF16),
        ffn_conv_w=ffn_conv_w, ffn_conv_b=rows(ffn_conv_b), w_down=w_down.astype(BF16),
        final_g=final_g.reshape(1, -1))
    y = x.reshape(bsz * seq, D_MODEL)
    for layer in range(depth):
        y = _layer(y, bsz, seq, layer, layer == depth - 1, biases, rel_bias, params)
    return y.reshape(bsz, seq, D_MODEL)
```

```python
import functools
import math

import jax
import jax.numpy as jnp
import numpy as np
from jax import lax
from jax.experimental import pallas as pl
from jax.experimental.pallas import tpu as pltpu

D_MODEL = 1024
HEAD_DIM = 64
N_BRANCH = 4
BRANCH_WIDTH = 512
N_HEADS = 8
N_PAIRS = N_HEADS // 2
SWA_KV_HEADS = 2
SWA_BLOCK = 128
SWA_STEP_BLOCKS = 2
CONV_WIDTH = 31
MOBA_BLOCK = 256
MOBA_TOPK = 3
D_FF = 2816
FFN_CONV_WIDTH = 3
REL_BUCKETS = 32
REL_MAX_DIST = 128
EPS = 1e-6
NEG = -1e30
SCALE = HEAD_DIM ** -0.5
LOG2E = math.log2(math.e)

LANES = 128
SUBLANES = 8
VMEM_LIMIT = 56 * 1024 * 1024

TM = 512
TQ = 256
TK = 256
CHUNK_TILES = 16
SCORE_LOOKAHEAD = 3
KV_ROWS = 512
SELECT_COLS = 1024
CUM_T = 256
FF_CHUNK = 256
CONV_ROWS = 64
CONV_HALO = 32

N_PIECES = 3
PIECE_ROWS = SUBLANES * N_PIECES
MOBA_SEL_LANE = SUBLANES
STEP_HEADS = 2
STEP_LANES = STEP_HEADS * HEAD_DIM
V_ROWS = HEAD_DIM + 16

BF16 = jnp.bfloat16
F32 = jnp.float32


def _params(*sem):
    return pltpu.CompilerParams(dimension_semantics=sem, vmem_limit_bytes=VMEM_LIMIT)


def _resident(shape, layer=None):
    if layer is None:
        zeros = (0,) * len(shape)
        return pl.BlockSpec(shape, lambda *_: zeros, pipeline_mode=pl.Buffered(1))
    index = (layer,) + (0,) * (len(shape) - 1)
    return pl.BlockSpec((None,) + tuple(shape[1:]), lambda *_: index, pipeline_mode=pl.Buffered(1))


def _rmsnorm(x, g):
    return (x * lax.rsqrt(jnp.mean(x * x, axis=-1, keepdims=True) + EPS)) * g


def _pieces(x):
    hi = x.astype(BF16).astype(F32)
    rest = x - hi
    mid = rest.astype(BF16).astype(F32)
    return hi, mid, rest - mid


def _rel_bucket_np(dist):
    n = np.maximum(dist, 0)
    max_exact = REL_BUCKETS // 2
    nf = np.maximum(n, 1).astype(np.float64)
    large = max_exact + (np.log(nf / max_exact) / math.log(REL_MAX_DIST / max_exact)
                         * (REL_BUCKETS - max_exact) + 1e-9).astype(np.int32)
    large = np.minimum(large, REL_BUCKETS - 1)
    return np.where(n < max_exact, n, large).astype(np.int32)


def _bias_kernel(tab_ref, bs_ref, bo_ref, bp_ref, os_ref, oo_ref, op_ref):
    os_ref[...] = jnp.zeros_like(os_ref)
    oo_ref[...] = jnp.zeros_like(oo_ref)
    op_ref[...] = jnp.zeros_like(op_ref)
    bs, bo, bp = bs_ref[...], bo_ref[...], bp_ref[...]

    def body(b, carry):
        ms, mo, mp = bs == b, bo == b, bp == b
        for h in range(N_HEADS):
            os_ref[h] = jnp.where(ms, tab_ref[b, h], os_ref[h])
            oo_ref[h] = jnp.where(mo, tab_ref[b, N_HEADS + h], oo_ref[h])
            op_ref[h] = jnp.where(mp, tab_ref[b, N_HEADS + h], op_ref[h])
        return carry

    lax.fori_loop(0, REL_BUCKETS, body, 0)
    causal = (lax.broadcasted_iota(jnp.int32, (MOBA_BLOCK, MOBA_BLOCK), 0)
              <= lax.broadcasted_iota(jnp.int32, (MOBA_BLOCK, MOBA_BLOCK), 1))
    for h in range(N_HEADS):
        far = tab_ref[REL_BUCKETS - 1, N_HEADS + h]
        oo_ref[h] = jnp.where(causal, (oo_ref[h] - far) * LOG2E, NEG)
        op_ref[h] = (op_ref[h] - far) * LOG2E


def _expand_bias(rel_bias):
    qi = np.arange(SWA_BLOCK)[:, None]
    kj = np.arange(2 * SWA_BLOCK)[None, :]
    b_swa = _rel_bucket_np(qi + SWA_BLOCK - kj)
    qi = np.arange(MOBA_BLOCK)[:, None]
    kj = np.arange(MOBA_BLOCK)[None, :]
    b_own = np.ascontiguousarray(_rel_bucket_np(qi - kj).T)
    b_prev = np.ascontiguousarray(_rel_bucket_np(qi - kj + MOBA_BLOCK).T)
    vm = pl.BlockSpec(memory_space=pltpu.VMEM)
    return pl.pallas_call(
        _bias_kernel,
        out_shape=(jax.ShapeDtypeStruct((N_HEADS, SWA_BLOCK, 2 * SWA_BLOCK), F32),
                   jax.ShapeDtypeStruct((N_HEADS, MOBA_BLOCK, MOBA_BLOCK), F32),
                   jax.ShapeDtypeStruct((N_HEADS, MOBA_BLOCK, MOBA_BLOCK), F32)),
        in_specs=[pl.BlockSpec(memory_space=pltpu.SMEM), vm, vm, vm],
        out_specs=(vm, vm, vm),
        name="bias_expand",
    )(rel_bias, jnp.asarray(b_swa), jnp.asarray(b_own), jnp.asarray(b_prev))


def _inproj_kernel(x_ref, g_ref, wa_ref, wb_ref, wc_ref, wf_ref, wd_ref,
                   oa_ref, ob_ref, oc_ref, of_ref, od_ref):
    h = _rmsnorm(x_ref[...], g_ref[...])
    hb = h.astype(BF16)
    oa_ref[...] = jnp.dot(hb, wa_ref[...], preferred_element_type=F32).astype(BF16)
    ob_ref[...] = jnp.dot(hb, wb_ref[...], preferred_element_type=F32)
    oc_ref[...] = jnp.dot(hb, wc_ref[...], preferred_element_type=F32).astype(BF16)
    od_ref[...] = jnp.dot(hb, wd_ref[...], preferred_element_type=F32).astype(BF16)
    of_ref[...] = jnp.dot(hb, wf_ref[...], preferred_element_type=F32)


def _inproj(x, g, ws, layer):
    n = x.shape[0]
    row = lambda w: pl.BlockSpec((TM, w), lambda i: (i, 0))
    dts = (BF16, F32, BF16, F32, BF16)
    return pl.pallas_call(
        _inproj_kernel,
        grid=(n // TM,),
        in_specs=[row(D_MODEL), _resident(g.shape, layer)] + [_resident(w.shape, layer) for w in ws],
        out_specs=tuple(row(w.shape[-1]) for w in ws),
        out_shape=tuple(jax.ShapeDtypeStruct((n, w.shape[-1]), dt) for w, dt in zip(ws, dts)),
        compiler_params=_params("arbitrary"),
        name="inproj",
    )(x, g, *ws)


def _swa_kernel(sink_ref, q_ref, kc_ref, kp_ref, vc_ref, vp_ref, bias_ref, o_ref, s_ref, *,
                blocks_per_seq, layer):
    first_blk = (pl.program_id(0) * SWA_STEP_BLOCKS) % blocks_per_seq
    qi = lax.broadcasted_iota(jnp.int32, (SWA_BLOCK, 2 * SWA_BLOCK), 0)
    kj = lax.broadcasted_iota(jnp.int32, (SWA_BLOCK, 2 * SWA_BLOCK), 1)
    dist = qi + SWA_BLOCK - kj
    window = (dist >= 0) & (dist < SWA_BLOCK)
    lane = lax.broadcasted_iota(jnp.int32, (SWA_BLOCK, LANES), 1)
    upper = lane >= HEAD_DIM
    kv_of = lambda h: h // (N_HEADS // SWA_KV_HEADS)
    block_rows = lambda sb: slice(sb * SWA_BLOCK, (sb + 1) * SWA_BLOCK)
    keys = lambda sb, cur, prev: jnp.concatenate(
        [prev[...] if sb == 0 else cur[block_rows(sb - 1), :], cur[block_rows(sb), :]], axis=0)
    for sb in range(SWA_STEP_BLOCKS):
        kk = keys(sb, kc_ref, kp_ref)
        for h in range(N_HEADS):
            q_pair = q_ref[block_rows(sb), (h // 2) * LANES:(h // 2 + 1) * LANES] * SCALE
            q_h = jnp.where(upper == (h % 2 == 1), q_pair, jnp.zeros_like(q_pair))
            k_kv = kk[:, kv_of(h) * LANES:(kv_of(h) + 1) * LANES]
            s_ref[sb, h] = lax.dot_general(q_h, k_kv, (((1,), (1,)), ((), ())), preferred_element_type=F32)
    probs = []
    for sb in range(SWA_STEP_BLOCKS):
        mask = window & ((first_blk > 0) | (kj >= SWA_BLOCK)) if sb == 0 else window
        for h in range(N_HEADS):
            logits = jnp.where(mask, s_ref[sb, h] + bias_ref[h], NEG)
            sink = sink_ref[layer, h]
            m = jnp.maximum(jnp.max(logits, axis=-1, keepdims=True), sink)
            p = jnp.exp(logits - m)
            denom = jnp.sum(p, axis=-1, keepdims=True) + jnp.exp(sink - m)
            probs.append((p / denom).astype(BF16))
    for sb in range(SWA_STEP_BLOCKS):
        vv = keys(sb, vc_ref, vp_ref)
        outs = [jnp.dot(probs[sb * N_HEADS + h], vv[:, kv_of(h) * LANES:(kv_of(h) + 1) * LANES],
                        preferred_element_type=F32) for h in range(N_HEADS)]
        for pair in range(N_PAIRS):
            o_ref[block_rows(sb), pair * LANES:(pair + 1) * LANES] = jnp.where(
                upper, outs[2 * pair + 1], outs[2 * pair]).astype(BF16)


def _swa(qkv, sinks, bias, seq, layer):
    n = qkv.shape[0]
    bps = seq // SWA_BLOCK
    assert bps % SWA_STEP_BLOCKS == 0
    step_rows = SWA_STEP_BLOCKS * SWA_BLOCK
    kcol, vcol = BRANCH_WIDTH // (2 * LANES), BRANCH_WIDTH // (2 * LANES) + 1
    prev = lambda i: jnp.where((i * SWA_STEP_BLOCKS) % bps == 0, i * SWA_STEP_BLOCKS, i * SWA_STEP_BLOCKS - 1)
    return pl.pallas_call(
        functools.partial(_swa_kernel, blocks_per_seq=bps, layer=layer),
        grid=(n // step_rows,),
        in_specs=[pl.BlockSpec(memory_space=pltpu.SMEM),
                  pl.BlockSpec((step_rows, BRANCH_WIDTH), lambda i: (i, 0)),
                  pl.BlockSpec((step_rows, 2 * LANES), lambda i: (i, kcol)),
                  pl.BlockSpec((SWA_BLOCK, 2 * LANES), lambda i: (prev(i), kcol)),
                  pl.BlockSpec((step_rows, 2 * LANES), lambda i: (i, vcol)),
                  pl.BlockSpec((SWA_BLOCK, 2 * LANES), lambda i: (prev(i), vcol)),
                  _resident(bias.shape)],
        out_specs=pl.BlockSpec((step_rows, BRANCH_WIDTH), lambda i: (i, 0)),
        out_shape=jax.ShapeDtypeStruct((n, BRANCH_WIDTH), BF16),
        scratch_shapes=[pltpu.VMEM((SWA_STEP_BLOCKS, N_HEADS, SWA_BLOCK, 2 * SWA_BLOCK), F32)],
        compiler_params=_params("arbitrary"),
        name="swa",
    )(sinks, qkv, qkv, qkv, qkv, qkv, bias)


def _conformer_kernel(u_ref, w_ref, b_ref, g_ref, beta_ref, o_ref, buf_ref, y_ref):
    @pl.when(pl.program_id(1) == 0)
    def _():
        buf_ref[0:CONV_HALO, :] = jnp.zeros((CONV_HALO, BRANCH_WIDTH), F32)

    a = u_ref[:, 0:BRANCH_WIDTH]
    gt = u_ref[:, BRANCH_WIDTH:2 * BRANCH_WIDTH]
    buf_ref[CONV_HALO:CONV_HALO + TM, :] = a * jax.nn.sigmoid(gt)
    n_shifts = SUBLANES
    for r in range(TM // CONV_ROWS):
        row0 = r * CONV_ROWS
        for cg in range(BRANCH_WIDTH // LANES):
            cols = slice(cg * LANES, (cg + 1) * LANES)
            slabs = []
            for step in range((CONV_WIDTH - 1) // n_shifts + 1):
                lo = CONV_HALO - n_shifts * (step + 1) + row0
                slabs.append(buf_ref[lo:lo + CONV_ROWS + n_shifts, cols])
            y = None
            for shift in range(n_shifts):
                z = None
                for step in range((CONV_WIDTH - 1 - shift) // n_shifts + 1):
                    k = CONV_WIDTH - 1 - (n_shifts * step + shift)
                    term = w_ref[k:k + 1, cols] * slabs[step]
                    z = term if z is None else z + term
                shifted = z[n_shifts - shift:n_shifts - shift + CONV_ROWS]
                y = shifted if y is None else y + shifted
            y_ref[row0:row0 + CONV_ROWS, cols] = y
        y = y_ref[row0:row0 + CONV_ROWS, :] + b_ref[...]
        mu = jnp.mean(y, axis=-1, keepdims=True)
        yc = y - mu
        var = jnp.mean(yc * yc, axis=-1, keepdims=True)
        z = yc * lax.rsqrt(var + EPS) * g_ref[...] + beta_ref[...]
        o_ref[r * CONV_ROWS:(r + 1) * CONV_ROWS, :] = (z * jax.nn.sigmoid(z)).astype(BF16)
    buf_ref[0:CONV_HALO, :] = buf_ref[TM:TM + CONV_HALO, :]


def _conformer(u, w, b, g, beta, bsz, seq, layer):
    n = u.shape[0]
    spt = seq // TM
    return pl.pallas_call(
        _conformer_kernel,
        grid=(bsz, spt),
        in_specs=[pl.BlockSpec((TM, 2 * BRANCH_WIDTH), lambda bi, i: (bi * spt + i, 0)),
                  _resident(w.shape, layer), _resident(b.shape, layer), _resident(g.shape, layer),
                  _resident(beta.shape, layer)],
        out_specs=pl.BlockSpec((TM, BRANCH_WIDTH), lambda bi, i: (bi * spt + i, 0)),
        out_shape=jax.ShapeDtypeStruct((n, BRANCH_WIDTH), BF16),
        scratch_shapes=[pltpu.VMEM((TM + CONV_HALO, BRANCH_WIDTH), F32),
                        pltpu.VMEM((TM, BRANCH_WIDTH), F32)],
        compiler_params=_params("arbitrary", "arbitrary"),
        name="conformer",
    )(u, w, b, g, beta)


def _extra_base(h):
    return (h // 2) * LANES + HEAD_DIM * (1 - h % 2)


def _fgate_placement():
    pk = np.zeros((LANES, BRANCH_WIDTH), np.float32)
    pq = np.zeros((LANES, BRANCH_WIDTH), np.float32)
    for h in range(N_HEADS):
        base = _extra_base(h)
        for i in range(N_PIECES):
            pk[SUBLANES * i + h, base + i] = 1.0
            pq[PIECE_ROWS, base + i] = -1.0
            pk[PIECE_ROWS, base + N_PIECES + i] = 1.0
            pq[SUBLANES * i + h, base + N_PIECES + i] = 1.0
    return jnp.asarray(pk, BF16), jnp.asarray(pq, BF16)


def _fgate_kernel(f_ref, b_ref, pkq_ref, kx_ref, qx_ref, carry_ref):
    @pl.when(pl.program_id(1) == 0)
    def _():
        carry_ref[...] = jnp.zeros_like(carry_ref)

    z = f_ref[...] + b_ref[...]
    log_f = -(jnp.maximum(-z, 0.0) + jnp.log1p(jnp.exp(-jnp.abs(z))))
    ri = lax.broadcasted_iota(jnp.int32, (CUM_T, CUM_T), 0)
    ci = lax.broadcasted_iota(jnp.int32, (CUM_T, CUM_T), 1)
    tri = jnp.where(ci <= ri, 1.0, 0.0).astype(BF16)
    sums = jnp.dot(tri, jnp.concatenate(_pieces(log_f), axis=1).astype(BF16), preferred_element_type=F32)
    c = (sums[:, 0:LANES] + sums[:, LANES:2 * LANES]) + sums[:, 2 * LANES:3 * LANES] + carry_ref[0:1, :]
    carry_ref[...] = jnp.broadcast_to(c[CUM_T - 1:CUM_T, :], carry_ref.shape)
    hi, mid, lo = _pieces(c * LOG2E)
    lane = lax.broadcasted_iota(jnp.int32, (CUM_T, LANES), 1)
    packed = jnp.where(lane < SUBLANES, hi,
                       jnp.where(lane < 2 * SUBLANES, pltpu.roll(mid, SUBLANES, axis=1),
                                 jnp.where(lane < PIECE_ROWS, pltpu.roll(lo, 2 * SUBLANES, axis=1),
                                           jnp.where(lane == PIECE_ROWS, 1.0, 0.0)))).astype(BF16)
    placed = jnp.dot(packed, pkq_ref[...], preferred_element_type=F32).astype(BF16)
    kx_ref[...] = placed[:, 0:BRANCH_WIDTH]
    qx_ref[...] = placed[:, BRANCH_WIDTH:2 * BRANCH_WIDTH]


def _fgate(f, b, bsz, seq, layer):
    n = f.shape[0]
    spt = seq // CUM_T
    pkq = jnp.concatenate(_fgate_placement(), axis=1)
    row = lambda w: pl.BlockSpec((CUM_T, w), lambda bi, i: (bi * spt + i, 0))
    return pl.pallas_call(
        _fgate_kernel,
        grid=(bsz, spt),
        in_specs=[row(LANES), _resident(b.shape, layer), _resident(pkq.shape)],
        out_specs=(row(BRANCH_WIDTH), row(BRANCH_WIDTH)),
        out_shape=(jax.ShapeDtypeStruct((n, BRANCH_WIDTH), BF16),
                   jax.ShapeDtypeStruct((n, BRANCH_WIDTH), BF16)),
        scratch_shapes=[pltpu.VMEM((SUBLANES, LANES), F32)],
        compiler_params=_params("arbitrary", "arbitrary"),
        name="fgate_cumsum",
    )(f, b, pkq)


def _build_kv(k_ref, v_ref, extra_of, ka_ref, vt_ref, seq):
    lane = lax.broadcasted_iota(jnp.int32, (KV_ROWS, LANES), 1)
    lower = lane < HEAD_DIM
    ones_rows = jnp.where(lax.broadcasted_iota(jnp.int32, (V_ROWS - HEAD_DIM, KV_ROWS), 0) == 0, 1.0, 0.0)

    def body(i, carry):
        rows = pl.ds(pl.multiple_of(i * KV_ROWS, KV_ROWS), KV_ROWS)
        extra = extra_of(i, rows)
        for pair in range(STEP_HEADS // 2):
            lanes = slice(pair * LANES, (pair + 1) * LANES)
            k, x = k_ref[rows, lanes], extra[:, lanes]
            ka_ref[2 * pair, rows, :] = jnp.where(lower, k, x)
            ka_ref[2 * pair + 1, rows, :] = jnp.where(lower, x, k)
            v_t = v_ref[rows, lanes].astype(F32).T
            for hh in range(2):
                vt_ref[2 * pair + hh, :, rows] = jnp.concatenate(
                    [v_t[hh * HEAD_DIM:(hh + 1) * HEAD_DIM], ones_rows], axis=0).astype(BF16)
        return carry

    lax.fori_loop(0, seq // KV_ROWS, body, 0)


def _extend_q_t(q_t, extras_t):
    out = []
    for hh in range(STEP_HEADS):
        q_h = q_t[hh * HEAD_DIM:(hh + 1) * HEAD_DIM]
        parts = [q_h, extras_t[hh]] if hh % 2 == 0 else [extras_t[hh], q_h]
        out.append(jnp.concatenate(parts, axis=0).astype(BF16))
    return out


def _flash_step(qa_t, ka_ref, vt_ref, state, start, n_tiles, tables=None, first=False):
    m_ref, acc_ref, s_ref = state
    heads = range(STEP_HEADS)
    m = [None if first else m_ref[hh, 0:1, :] for hh in heads]
    acc = [None if first else acc_ref[hh] for hh in heads]
    n_plain = n_tiles - (len(tables(0)) if tables else 0)
    tile_rows = lambda t: pl.ds(pl.multiple_of(start + t * TK, TK), TK)
    ring = SCORE_LOOKAHEAD + 1

    def score(t):
        for hh in heads:
            s_ref[t % ring, hh] = jnp.dot(ka_ref[hh, tile_rows(t), :], qa_t[hh],
                                          preferred_element_type=F32)

    for t in range(min(SCORE_LOOKAHEAD, n_tiles)):
        score(t)
    for t in range(n_tiles):
        rows = tile_rows(t)
        if t + SCORE_LOOKAHEAD < n_tiles:
            score(t + SCORE_LOOKAHEAD)
        for hh in heads:
            s = s_ref[t % ring, hh]
            if t >= n_plain:
                s = s + tables(hh)[t - n_plain]
            col_max = jnp.max(s, axis=0, keepdims=True)
            m_new = col_max if m[hh] is None else jnp.maximum(m[hh], col_max)
            p = jnp.exp2(s - m_new).astype(BF16)
            pv = jnp.dot(vt_ref[hh, :, rows], p, preferred_element_type=F32)
            acc[hh] = pv if acc[hh] is None else jnp.exp2(m[hh] - m_new) * acc[hh] + pv
            m[hh] = m_new
    for hh in heads:
        m_ref[hh] = jnp.broadcast_to(m[hh], (SUBLANES, TQ))
        acc_ref[hh] = acc[hh]


def _pair_output(o_ref, acc_ref):
    out_t = jnp.concatenate(
        [acc_ref[hh, 0:HEAD_DIM, :] / acc_ref[hh, HEAD_DIM:HEAD_DIM + 1, :] for hh in range(STEP_HEADS)],
        axis=0)
    o_ref[...] = out_t.T.astype(BF16)


def _flash_scratch(seq):
    return [pltpu.VMEM((STEP_HEADS, seq, LANES), BF16), pltpu.VMEM((STEP_HEADS, V_ROWS, seq), BF16),
            pltpu.VMEM((STEP_HEADS, SUBLANES, TQ), F32), pltpu.VMEM((STEP_HEADS, V_ROWS, TQ), F32),
            pltpu.VMEM((SCORE_LOOKAHEAD + 1, STEP_HEADS, TK, TQ), F32)]


def _flash_chunks(qa, ka_ref, vt_ref, state, n_chunks):
    def chunk(c, carry):
        _flash_step(qa, ka_ref, vt_ref, state, c * (CHUNK_TILES * TK), CHUNK_TILES)
        return carry

    lax.fori_loop(0, n_chunks, chunk, 0)


def _flash_tail(qa, ka_ref, vt_ref, state, last_tile, n_loose, tail_tables, guard=None):
    rem = n_loose % CHUNK_TILES
    for r in range(CHUNK_TILES):
        if isinstance(rem, int) and rem != r:
            continue
        tiles = r + len(tail_tables(0))
        cond = guard if isinstance(rem, int) else (rem == r if guard is None else (rem == r) & guard)

        def run(tiles=tiles):
            _flash_step(qa, ka_ref, vt_ref, state, (last_tile + 1 - tiles) * TK, tiles,
                        tables=tail_tables, first=True)

        if cond is None or isinstance(cond, (bool, np.bool_)):
            if cond is None or cond:
                run()
        else:
            pl.when(cond)(run)


def _fox_kernel(q_ref, qx_ref, k_ref, v_ref, kx_ref, o_ref, ka_ref, vt_ref, *state, seq):
    _build_kv(k_ref, v_ref, lambda i, rows: kx_ref[rows, :], ka_ref, vt_ref, seq)
    causal = jnp.where(lax.broadcasted_iota(jnp.int32, (TK, TQ), 0)
                       <= lax.broadcasted_iota(jnp.int32, (TK, TQ), 1), 0.0, NEG)

    def query_tile(qt, carry):
        rows = pl.ds(pl.multiple_of(qt * TQ, TQ), TQ)
        q_t = (q_ref[rows, :].astype(F32) * (SCALE * LOG2E)).T
        qx_t = qx_ref[rows, :].astype(F32).T
        qa = _extend_q_t(q_t, [qx_t[(hh ^ 1) * HEAD_DIM:((hh ^ 1) + 1) * HEAD_DIM]
                               for hh in range(STEP_HEADS)])
        _flash_tail(qa, ka_ref, vt_ref, state, qt, qt, lambda hh: [causal])
        _flash_chunks(qa, ka_ref, vt_ref, state, qt // CHUNK_TILES)
        _pair_output(o_ref.at[rows, :], state[1])
        return carry

    lax.fori_loop(0, seq // TQ, query_tile, 0)


def _fox(qkv, kx, qx, bsz, seq):
    n = qkv.shape[0]
    koff, voff = BRANCH_WIDTH // STEP_LANES, 2 * BRANCH_WIDTH // STEP_LANES
    per_seq = lambda off: pl.BlockSpec((seq, STEP_LANES), lambda b, p: (b, off + p))
    return pl.pallas_call(
        functools.partial(_fox_kernel, seq=seq),
        grid=(bsz, N_HEADS // STEP_HEADS),
        in_specs=[per_seq(0), per_seq(0), per_seq(koff), per_seq(voff), per_seq(0)],
        out_specs=per_seq(0),
        out_shape=jax.ShapeDtypeStruct((n, BRANCH_WIDTH), BF16),
        scratch_shapes=_flash_scratch(seq),
        compiler_params=_params("arbitrary", "arbitrary"),
        name="fox",
    )(qkv, qx, qkv, qkv, kx)


def _moba_select(qseq_ref, kmean_ref, pen_ref, seq):
    nblk = seq // MOBA_BLOCK
    blk_shift = MOBA_BLOCK.bit_length() - 1
    kmean = kmean_ref[0:nblk, :]
    lane_head = jnp.right_shift(lax.broadcasted_iota(jnp.int32, (nblk, STEP_LANES), 1),
                                HEAD_DIM.bit_length() - 1)
    blk = lax.broadcasted_iota(jnp.int32, (nblk, SELECT_COLS), 0)
    kmean_pieces = [[piece.astype(BF16) for piece in _pieces(jnp.where(lane_head == hh, kmean, 0.0))]
                    for hh in range(STEP_HEADS)]

    def body(i, carry):
        cols = pl.ds(pl.multiple_of(i * SELECT_COLS, SELECT_COLS), SELECT_COLS)
        q = qseq_ref[cols, :]
        own = jnp.right_shift(i * SELECT_COLS + lax.broadcasted_iota(jnp.int32, (nblk, SELECT_COLS), 1),
                              blk_shift)
        past = blk < own
        for hh in range(STEP_HEADS):
            hi, mid, lo = [lax.dot_general(piece, q, (((1,), (1,)), ((), ())), preferred_element_type=F32)
                           for piece in kmean_pieces[hh]]
            gate = jnp.where(past, (hi + mid) + lo, NEG)
            sel = blk == own
            for _ in range(MOBA_TOPK):
                top = jnp.max(gate, axis=0, keepdims=True)
                idx = jnp.min(jnp.where(gate == top, blk, nblk), axis=0, keepdims=True)
                pick = blk == idx
                sel = sel | (pick & past)
                gate = jnp.where(pick, -jnp.inf, gate)
            pen_ref[hh, :, cols] = jnp.where(sel, 0.0, NEG)
        return carry

    lax.fori_loop(0, seq // SELECT_COLS, body, 0)


def _moba_kernel(tab_ref, q_ref, k_ref, v_ref, town_ref, tprev_ref, o_ref,
                 kmean_ref, pen_ref, ka_ref, vt_ref, *state, seq):
    group = pl.program_id(1)
    nblk = seq // MOBA_BLOCK
    blk_shift = MOBA_BLOCK.bit_length() - 1

    blk = lax.broadcasted_iota(jnp.int32, (LANES, seq), 0)
    pos = lax.broadcasted_iota(jnp.int32, (LANES, seq), 1)
    member = jnp.where(jnp.right_shift(pos, blk_shift) == blk, 1.0 / MOBA_BLOCK, 0.0).astype(BF16)
    kmean_ref[...] = jnp.dot(member, k_ref[...], preferred_element_type=F32)
    _moba_select(q_ref, kmean_ref, pen_ref, seq)

    def extra_of(i, rows):
        xl = jnp.bitwise_and(lax.broadcasted_iota(jnp.int32, (KV_ROWS, STEP_LANES), 1), HEAD_DIM - 1)
        row_blk = jnp.right_shift(
            i * KV_ROWS + lax.broadcasted_iota(jnp.int32, (KV_ROWS, STEP_LANES), 0), blk_shift)
        hit = (xl < N_PIECES) | (xl - MOBA_SEL_LANE == row_blk)
        return jnp.where(hit, 1.0, 0.0).astype(BF16)

    _build_kv(k_ref, v_ref, extra_of, ka_ref, vt_ref, seq)
    piece_row = lax.broadcasted_iota(jnp.int32, (MOBA_SEL_LANE, TQ), 0)
    far_rows = []
    for hh in range(STEP_HEADS):
        far = jnp.full((MOBA_SEL_LANE, TQ), tab_ref[REL_BUCKETS - 1, N_HEADS + STEP_HEADS * group + hh] * LOG2E, F32)
        hi, mid, lo = _pieces(far)
        far_rows.append(jnp.where(piece_row == 0, hi, jnp.where(
            piece_row == 1, mid, jnp.where(piece_row == 2, lo, 0.0))))

    def query_tile(qt, carry):
        rows = pl.ds(pl.multiple_of(qt * TQ, TQ), TQ)
        q_t = q_ref[rows, :].astype(F32).T
        extras = [jnp.concatenate(
            [far_rows[hh], pen_ref[hh, :, rows],
             jnp.zeros((HEAD_DIM - MOBA_SEL_LANE - nblk, TQ), F32)], axis=0) for hh in range(STEP_HEADS)]
        qa = _extend_q_t(q_t * (SCALE * LOG2E), extras)
        n_far = jnp.maximum(qt - 1, 0)
        _flash_tail(qa, ka_ref, vt_ref, state, qt, 0, lambda hh: [town_ref[hh]],
                    guard=qt == 0)
        _flash_tail(qa, ka_ref, vt_ref, state, qt, n_far,
                    lambda hh: [tprev_ref[hh], town_ref[hh]], guard=qt >= 1)
        _flash_chunks(qa, ka_ref, vt_ref, state, n_far // CHUNK_TILES)
        _pair_output(o_ref.at[rows, :], state[1])
        return carry

    lax.fori_loop(0, seq // TQ, query_tile, 0)


def _moba(qkv, rel_bias, t_own, t_prev, bsz, seq):
    n = qkv.shape[0]
    koff, voff = BRANCH_WIDTH // STEP_LANES, 2 * BRANCH_WIDTH // STEP_LANES
    per_seq = lambda off: pl.BlockSpec((seq, STEP_LANES), lambda b, p: (b, off + p))
    head_tables = pl.BlockSpec((STEP_HEADS, MOBA_BLOCK, MOBA_BLOCK), lambda b, p: (p, 0, 0))
    return pl.pallas_call(
        functools.partial(_moba_kernel, seq=seq),
        grid=(bsz, N_HEADS // STEP_HEADS),
        in_specs=[pl.BlockSpec(memory_space=pltpu.SMEM), per_seq(0), per_seq(koff), per_seq(voff),
                  head_tables, head_tables],
        out_specs=per_seq(0),
        out_shape=jax.ShapeDtypeStruct((n, BRANCH_WIDTH), BF16),
        scratch_shapes=[pltpu.VMEM((LANES, STEP_LANES), F32),
                        pltpu.VMEM((STEP_HEADS, seq // MOBA_BLOCK, seq), F32)] + _flash_scratch(seq),
        compiler_params=_params("arbitrary", "arbitrary"),
        name="moba",
    )(rel_bias, qkv, qkv, qkv, t_own, t_prev)


def _merge_kernel(x_ref, g_ref, ya_ref, yb_ref, yc_ref, yd_ref, wg_ref, bg_ref, wbr_ref, wout_ref, o_ref):
    x = x_ref[...]
    hb = _rmsnorm(x, g_ref[...]).astype(BF16)
    merged = jnp.zeros((TM, D_MODEL), F32)
    for n, y_ref in enumerate((ya_ref, yb_ref, yc_ref, yd_ref)):
        gz = jnp.dot(hb, wg_ref[:, n * D_MODEL:(n + 1) * D_MODEL], preferred_element_type=F32)
        gate = jax.nn.sigmoid(gz + bg_ref[n:n + 1, :])
        merged = merged + gate * jnp.dot(y_ref[...], wbr_ref[n], preferred_element_type=F32)
    o_ref[...] = x + jnp.dot(merged.astype(BF16), wout_ref[...], preferred_element_type=F32)


def _merge(x, g, ys, wg, bg, wbr, wout, layer):
    n = x.shape[0]
    row = lambda w: pl.BlockSpec((TM, w), lambda i: (i, 0))
    return pl.pallas_call(
        _merge_kernel,
        grid=(n // TM,),
        in_specs=[row(D_MODEL), _resident(g.shape, layer)] + [row(BRANCH_WIDTH)] * N_BRANCH
        + [_resident(w.shape, layer) for w in (wg, bg, wbr, wout)],
        out_specs=row(D_MODEL),
        out_shape=jax.ShapeDtypeStruct((n, D_MODEL), F32),
        compiler_params=_params("arbitrary"),
        name="merge",
    )(x, g, *ys, wg, bg, wbr, wout)


def _ffn_kernel(x_ref, g_ref, wup_ref, cw_ref, cb_ref, wdown_ref, fg_ref, o_ref, halo_ref, buf_ref, a_ref,
                *, final):
    @pl.when(pl.program_id(1) == 0)
    def _():
        halo_ref[...] = jnp.zeros_like(halo_ref)

    x = x_ref[...]
    hb = _rmsnorm(x, g_ref[...]).astype(BF16)
    n_chunks = D_FF // FF_CHUNK
    col_of = lambda c, part: part * D_FF + c * FF_CHUNK

    def up(c):
        for part in range(2):
            col = col_of(c, part)
            buf_ref[c % 2, part, SUBLANES:SUBLANES + TM, :] = jnp.dot(
                hb, wup_ref[:, col:col + FF_CHUNK], preferred_element_type=F32)

    up(0)
    for c in range(n_chunks):
        if c + 1 < n_chunks:
            up(c + 1)
        halves = []
        for part in range(2):
            col = col_of(c, part)
            buf_ref[c % 2, part, 0:SUBLANES, :] = halo_ref[:, col:col + FF_CHUNK]
            halo_ref[:, col:col + FF_CHUNK] = buf_ref[c % 2, part, TM:TM + SUBLANES, :]
            u = cb_ref[:, col:col + FF_CHUNK]
            for k in range(FFN_CONV_WIDTH):
                lo = SUBLANES - (FFN_CONV_WIDTH - 1) + k
                u = u + cw_ref[k:k + 1, col:col + FF_CHUNK] * buf_ref[c % 2, part, lo:lo + TM, :]
            halves.append(u)
        a_ref[:, c * FF_CHUNK:(c + 1) * FF_CHUNK] = (
            (halves[0] * jax.nn.sigmoid(halves[0])) * halves[1]).astype(BF16)
    y = x + jnp.dot(a_ref[...], wdown_ref[...], preferred_element_type=F32)
    o_ref[...] = _rmsnorm(y, fg_ref[...]) if final else y


def _ffn(x, g, wup, cw, cb, wdown, fg, bsz, seq, layer, final):
    n = x.shape[0]
    spt = seq // TM
    row = pl.BlockSpec((TM, D_MODEL), lambda bi, i: (bi * spt + i, 0))
    return pl.pallas_call(
        functools.partial(_ffn_kernel, final=final),
        grid=(bsz, spt),
        in_specs=[row] + [_resident(w.shape, layer) for w in (g, wup, cw, cb, wdown)]
        + [_resident(fg.shape)],
        out_specs=row,
        out_shape=jax.ShapeDtypeStruct((n, D_MODEL), F32),
        scratch_shapes=[pltpu.VMEM((SUBLANES, 2 * D_FF), F32),
                        pltpu.VMEM((2, 2, TM + SUBLANES, FF_CHUNK), F32),
                        pltpu.VMEM((TM, D_FF), BF16)],
        compiler_params=_params("arbitrary", "arbitrary"),
        name="ffn",
    )(x, g, wup, cw, cb, wdown, fg)


def _split_w_in(w_in):
    hd, bw = HEAD_DIM, BRANCH_WIDTH
    w_in = w_in.astype(BF16)
    o = 0
    qa = w_in[..., o:o + bw]; o += bw
    ka = w_in[..., o:o + SWA_KV_HEADS * hd]; o += SWA_KV_HEADS * hd
    va = w_in[..., o:o + SWA_KV_HEADS * hd]; o += SWA_KV_HEADS * hd
    ub = w_in[..., o:o + 2 * bw]; o += 2 * bw
    qkv_c = w_in[..., o:o + 3 * bw]; o += 3 * bw
    fc = w_in[..., o:o + N_HEADS]; o += N_HEADS
    qkv_d = w_in[..., o:o + 3 * bw]; o += 3 * bw
    gates = w_in[..., o:o + N_BRANCH * D_MODEL]
    dup = lambda w: jnp.concatenate([w[..., i * hd:(i + 1) * hd] for i in (0, 0, 1, 1)], axis=-1)
    wa = jnp.concatenate([qa, dup(ka), dup(va)], axis=-1)
    wf = jnp.pad(fc, ((0, 0), (0, 0), (0, LANES - N_HEADS)))
    return (wa, ub, qkv_c, wf, qkv_d), gates


def _layer(x, bsz, seq, layer, final, biases, rel_bias, p):
    bias_swa, table_own, table_prev = biases
    za, zb, zc, zf, zd = _inproj(x, p["ln1_g"], p["w_in_groups"], layer)
    ya = _swa(za, p["sinks"], bias_swa, seq, layer)
    yb = _conformer(zb, p["conv_w"], p["conv_b"], p["conv_ln_g"], p["conv_ln_b"], bsz, seq, layer)
    kx, qx = _fgate(zf, p["b_fgate"], bsz, seq, layer)
    yc = _fox(zc, kx, qx, bsz, seq)
    yd = _moba(zd, rel_bias, table_own, table_prev, bsz, seq)
    x = _merge(x, p["ln1_g"], (ya, yb, yc, yd), p["w_gates"], p["b_gate"], p["w_br"], p["w_out"], layer)
    return _ffn(x, p["ln2_g"], p["w_up"], p["ffn_conv_w"], p["ffn_conv_b"], p["w_down"], p["final_g"],
                bsz, seq, layer, final)


def kernel(x, ln1_g, w_in, b_gate, b_fgate, sinks, conv_w, conv_b, conv_ln_g, conv_ln_b, w_br, w_out,
           ln2_g, w_up, ffn_conv_w, ffn_conv_b, w_down, rel_bias, final_g):
    bsz, seq, _ = x.shape
    depth = w_in.shape[0]
    assert seq % TM == 0 and seq % KV_ROWS == 0 and TQ == TK == MOBA_BLOCK
    assert MOBA_SEL_LANE + seq // MOBA_BLOCK <= HEAD_DIM
    biases = _expand_bias(rel_bias)
    rows = lambda v: v.reshape(depth, 1, -1)
    w_in_groups, w_gates = _split_w_in(w_in)
    params = dict(
        ln1_g=rows(ln1_g), w_in_groups=w_in_groups, w_gates=w_gates, b_gate=b_gate,
        b_fgate=rows(jnp.pad(b_fgate, ((0, 0), (0, LANES - N_HEADS)))), sinks=sinks,
        conv_w=conv_w, conv_b=rows(conv_b), conv_ln_g=rows(conv_ln_g), conv_ln_b=rows(conv_ln_b),
        w_br=w_br.astype(BF16), w_out=w_out.astype(BF16), ln2_g=rows(ln2_g), w_up=w_up.astype(BF16),
        ffn_conv_w=ffn_conv_w, ffn_conv_b=rows(ffn_conv_b), w_down=w_down.astype(BF16),
        final_g=final_g.reshape(1, -1))
    y = x.reshape(bsz * seq, D_MODEL)
    for layer in range(depth):
        y = _layer(y, bsz, seq, layer, layer == depth - 1, biases, rel_bias, params)
    return y.reshape(bsz, seq, D_MODEL)
```

```python
import functools
import math

import jax
import jax.numpy as jnp
import numpy as np
from jax import lax
from jax.experimental import pallas as pl
from jax.experimental.pallas import tpu as pltpu

D_MODEL = 1024
HEAD_DIM = 64
N_BRANCH = 4
BRANCH_WIDTH = 512
N_HEADS = 8
N_PAIRS = N_HEADS // 2
SWA_KV_HEADS = 2
SWA_BLOCK = 128
SWA_STEP_BLOCKS = 2
CONV_WIDTH = 31
MOBA_BLOCK = 256
MOBA_TOPK = 3
D_FF = 2816
FFN_CONV_WIDTH = 3
REL_BUCKETS = 32
REL_MAX_DIST = 128
EPS = 1e-6
NEG = -1e30
SCALE = HEAD_DIM ** -0.5
LOG2E = math.log2(math.e)

LANES = 128
SUBLANES = 8
VMEM_LIMIT = 56 * 1024 * 1024

TM = 512
TQ = 256
TK = 256
CHUNK_TILES = 16
SCORE_LOOKAHEAD = 3
KV_ROWS = 512
SELECT_COLS = 1024
CUM_T = 256
FF_CHUNK = 256
CONV_ROWS = 64
CONV_HALO = 32

N_PIECES = 3
PIECE_ROWS = SUBLANES * N_PIECES
MOBA_SEL_LANE = SUBLANES
STEP_HEADS = 2
STEP_LANES = STEP_HEADS * HEAD_DIM
BF16_ROWS = 2 * SUBLANES
V_ROWS = HEAD_DIM + BF16_ROWS

BF16 = jnp.bfloat16
F32 = jnp.float32


def _params(*sem):
    return pltpu.CompilerParams(dimension_semantics=sem, vmem_limit_bytes=VMEM_LIMIT)


def _resident(shape, layer=None):
    if layer is None:
        zeros = (0,) * len(shape)
        return pl.BlockSpec(shape, lambda *_: zeros, pipeline_mode=pl.Buffered(1))
    index = (layer,) + (0,) * (len(shape) - 1)
    return pl.BlockSpec((None,) + tuple(shape[1:]), lambda *_: index, pipeline_mode=pl.Buffered(1))


def _rmsnorm(x, g):
    return (x * lax.rsqrt(jnp.mean(x * x, axis=-1, keepdims=True) + EPS)) * g


def _sigmoid(x):
    return 0.5 + 0.5 * jnp.tanh(0.5 * x)


def _silu(x):
    half = 0.5 * x
    return half + half * jnp.tanh(half)


def _pieces(x):
    hi = x.astype(BF16).astype(F32)
    rest = x - hi
    mid = rest.astype(BF16).astype(F32)
    return hi, mid, rest - mid


def _rel_bucket_np(dist):
    n = np.maximum(dist, 0)
    max_exact = REL_BUCKETS // 2
    nf = np.maximum(n, 1).astype(np.float64)
    large = max_exact + (np.log(nf / max_exact) / math.log(REL_MAX_DIST / max_exact)
                         * (REL_BUCKETS - max_exact) + 1e-9).astype(np.int32)
    large = np.minimum(large, REL_BUCKETS - 1)
    return np.where(n < max_exact, n, large).astype(np.int32)


def _bias_kernel(tab_ref, bs_ref, bo_ref, bp_ref, os_ref, oo_ref, op_ref):
    os_ref[...] = jnp.zeros_like(os_ref)
    oo_ref[...] = jnp.zeros_like(oo_ref)
    op_ref[...] = jnp.zeros_like(op_ref)
    bs, bo, bp = bs_ref[...], bo_ref[...], bp_ref[...]

    def body(b, carry):
        ms, mo, mp = bs == b, bo == b, bp == b
        for h in range(N_HEADS):
            os_ref[h] = jnp.where(ms, tab_ref[b, h], os_ref[h])
            oo_ref[h] = jnp.where(mo, tab_ref[b, N_HEADS + h], oo_ref[h])
            op_ref[h] = jnp.where(mp, tab_ref[b, N_HEADS + h], op_ref[h])
        return carry

    lax.fori_loop(0, REL_BUCKETS, body, 0)
    causal = (lax.broadcasted_iota(jnp.int32, (MOBA_BLOCK, MOBA_BLOCK), 0)
              <= lax.broadcasted_iota(jnp.int32, (MOBA_BLOCK, MOBA_BLOCK), 1))
    for h in range(N_HEADS):
        far = tab_ref[REL_BUCKETS - 1, N_HEADS + h]
        oo_ref[h] = jnp.where(causal, (oo_ref[h] - far) * LOG2E, NEG)
        op_ref[h] = (op_ref[h] - far) * LOG2E


def _expand_bias(rel_bias):
    qi = np.arange(SWA_BLOCK)[:, None]
    kj = np.arange(2 * SWA_BLOCK)[None, :]
    b_swa = _rel_bucket_np(qi + SWA_BLOCK - kj)
    qi = np.arange(MOBA_BLOCK)[:, None]
    kj = np.arange(MOBA_BLOCK)[None, :]
    b_own = np.ascontiguousarray(_rel_bucket_np(qi - kj).T)
    b_prev = np.ascontiguousarray(_rel_bucket_np(qi - kj + MOBA_BLOCK).T)
    vm = pl.BlockSpec(memory_space=pltpu.VMEM)
    return pl.pallas_call(
        _bias_kernel,
        out_shape=(jax.ShapeDtypeStruct((N_HEADS, SWA_BLOCK, 2 * SWA_BLOCK), F32),
                   jax.ShapeDtypeStruct((N_HEADS, MOBA_BLOCK, MOBA_BLOCK), F32),
                   jax.ShapeDtypeStruct((N_HEADS, MOBA_BLOCK, MOBA_BLOCK), F32)),
        in_specs=[pl.BlockSpec(memory_space=pltpu.SMEM), vm, vm, vm],
        out_specs=(vm, vm, vm),
        name="bias_expand",
    )(rel_bias, jnp.asarray(b_swa), jnp.asarray(b_own), jnp.asarray(b_prev))


def _inproj_kernel(x_ref, g_ref, wa_ref, wb_ref, wc_ref, wf_ref, wd_ref,
                   oa_ref, ob_ref, oc_ref, of_ref, od_ref):
    h = _rmsnorm(x_ref[...], g_ref[...])
    hb = h.astype(BF16)
    oa_ref[...] = jnp.dot(hb, wa_ref[...], preferred_element_type=F32).astype(BF16)
    ob_ref[...] = jnp.dot(hb, wb_ref[...], preferred_element_type=F32)
    oc_ref[...] = jnp.dot(hb, wc_ref[...], preferred_element_type=F32).astype(BF16)
    od_ref[...] = jnp.dot(hb, wd_ref[...], preferred_element_type=F32).astype(BF16)
    of_ref[...] = jnp.dot(hb, wf_ref[...], preferred_element_type=F32)


def _inproj(x, g, ws, layer):
    n = x.shape[0]
    row = lambda w: pl.BlockSpec((TM, w), lambda i: (i, 0))
    dts = (BF16, F32, BF16, F32, BF16)
    return pl.pallas_call(
        _inproj_kernel,
        grid=(n // TM,),
        in_specs=[row(D_MODEL), _resident(g.shape, layer)] + [_resident(w.shape, layer) for w in ws],
        out_specs=tuple(row(w.shape[-1]) for w in ws),
        out_shape=tuple(jax.ShapeDtypeStruct((n, w.shape[-1]), dt) for w, dt in zip(ws, dts)),
        compiler_params=_params("arbitrary"),
        name="inproj",
    )(x, g, *ws)


def _swa_kernel(sink_ref, q_ref, kc_ref, kp_ref, vc_ref, vp_ref, bias_ref, o_ref, s_ref, *,
                blocks_per_seq, layer):
    first_blk = (pl.program_id(0) * SWA_STEP_BLOCKS) % blocks_per_seq
    qi = lax.broadcasted_iota(jnp.int32, (SWA_BLOCK, 2 * SWA_BLOCK), 0)
    kj = lax.broadcasted_iota(jnp.int32, (SWA_BLOCK, 2 * SWA_BLOCK), 1)
    dist = qi + SWA_BLOCK - kj
    window = (dist >= 0) & (dist < SWA_BLOCK)
    lane = lax.broadcasted_iota(jnp.int32, (SWA_BLOCK, LANES), 1)
    upper = lane >= HEAD_DIM
    kv_of = lambda h: h // (N_HEADS // SWA_KV_HEADS)
    block_rows = lambda sb: slice(sb * SWA_BLOCK, (sb + 1) * SWA_BLOCK)
    keys = lambda sb, cur, prev: jnp.concatenate(
        [prev[...] if sb == 0 else cur[block_rows(sb - 1), :], cur[block_rows(sb), :]], axis=0)
    for sb in range(SWA_STEP_BLOCKS):
        kk = keys(sb, kc_ref, kp_ref)
        for h in range(N_HEADS):
            q_pair = q_ref[block_rows(sb), (h // 2) * LANES:(h // 2 + 1) * LANES] * SCALE
            q_h = jnp.where(upper == (h % 2 == 1), q_pair, jnp.zeros_like(q_pair))
            k_kv = kk[:, kv_of(h) * LANES:(kv_of(h) + 1) * LANES]
            s_ref[sb, h] = lax.dot_general(q_h, k_kv, (((1,), (1,)), ((), ())), preferred_element_type=F32)
    probs = []
    for sb in range(SWA_STEP_BLOCKS):
        mask = window & ((first_blk > 0) | (kj >= SWA_BLOCK)) if sb == 0 else window
        for h in range(N_HEADS):
            logits = jnp.where(mask, s_ref[sb, h] + bias_ref[h], NEG)
            sink = sink_ref[layer, h]
            m = jnp.maximum(jnp.max(logits, axis=-1, keepdims=True), sink)
            p = jnp.exp(logits - m)
            denom = jnp.sum(p, axis=-1, keepdims=True) + jnp.exp(sink - m)
            probs.append((p / denom).astype(BF16))
    for sb in range(SWA_STEP_BLOCKS):
        vv = keys(sb, vc_ref, vp_ref)
        outs = [jnp.dot(probs[sb * N_HEADS + h], vv[:, kv_of(h) * LANES:(kv_of(h) + 1) * LANES],
                        preferred_element_type=F32) for h in range(N_HEADS)]
        for pair in range(N_PAIRS):
            o_ref[block_rows(sb), pair * LANES:(pair + 1) * LANES] = jnp.where(
                upper, outs[2 * pair + 1], outs[2 * pair]).astype(BF16)


def _swa(qkv, sinks, bias, seq, layer):
    n = qkv.shape[0]
    bps = seq // SWA_BLOCK
    assert bps % SWA_STEP_BLOCKS == 0
    step_rows = SWA_STEP_BLOCKS * SWA_BLOCK
    kcol, vcol = BRANCH_WIDTH // (2 * LANES), BRANCH_WIDTH // (2 * LANES) + 1
    prev = lambda i: jnp.where((i * SWA_STEP_BLOCKS) % bps == 0, i * SWA_STEP_BLOCKS, i * SWA_STEP_BLOCKS - 1)
    return pl.pallas_call(
        functools.partial(_swa_kernel, blocks_per_seq=bps, layer=layer),
        grid=(n // step_rows,),
        in_specs=[pl.BlockSpec(memory_space=pltpu.SMEM),
                  pl.BlockSpec((step_rows, BRANCH_WIDTH), lambda i: (i, 0)),
                  pl.BlockSpec((step_rows, 2 * LANES), lambda i: (i, kcol)),
                  pl.BlockSpec((SWA_BLOCK, 2 * LANES), lambda i: (prev(i), kcol)),
                  pl.BlockSpec((step_rows, 2 * LANES), lambda i: (i, vcol)),
                  pl.BlockSpec((SWA_BLOCK, 2 * LANES), lambda i: (prev(i), vcol)),
                  _resident(bias.shape)],
        out_specs=pl.BlockSpec((step_rows, BRANCH_WIDTH), lambda i: (i, 0)),
        out_shape=jax.ShapeDtypeStruct((n, BRANCH_WIDTH), BF16),
        scratch_shapes=[pltpu.VMEM((SWA_STEP_BLOCKS, N_HEADS, SWA_BLOCK, 2 * SWA_BLOCK), F32)],
        compiler_params=_params("arbitrary"),
        name="swa",
    )(sinks, qkv, qkv, qkv, qkv, qkv, bias)


def _conformer_kernel(u_ref, w_ref, b_ref, g_ref, beta_ref, o_ref, buf_ref, y_ref):
    @pl.when(pl.program_id(1) == 0)
    def _():
        buf_ref[0:CONV_HALO, :] = jnp.zeros((CONV_HALO, BRANCH_WIDTH), F32)

    a = u_ref[:, 0:BRANCH_WIDTH]
    gt = u_ref[:, BRANCH_WIDTH:2 * BRANCH_WIDTH]
    buf_ref[CONV_HALO:CONV_HALO + TM, :] = a * _sigmoid(gt)
    n_shifts = SUBLANES
    for r in range(TM // CONV_ROWS):
        row0 = r * CONV_ROWS
        for cg in range(BRANCH_WIDTH // LANES):
            cols = slice(cg * LANES, (cg + 1) * LANES)
            slabs = []
            for step in range((CONV_WIDTH - 1) // n_shifts + 1):
                lo = CONV_HALO - n_shifts * (step + 1) + row0
                slabs.append(buf_ref[lo:lo + CONV_ROWS + n_shifts, cols])
            y = None
            for shift in range(n_shifts):
                z = None
                for step in range((CONV_WIDTH - 1 - shift) // n_shifts + 1):
                    k = CONV_WIDTH - 1 - (n_shifts * step + shift)
                    term = w_ref[k:k + 1, cols] * slabs[step]
                    z = term if z is None else z + term
                shifted = z[n_shifts - shift:n_shifts - shift + CONV_ROWS]
                y = shifted if y is None else y + shifted
            y_ref[row0:row0 + CONV_ROWS, cols] = y
        y = y_ref[row0:row0 + CONV_ROWS, :] + b_ref[...]
        mu = jnp.mean(y, axis=-1, keepdims=True)
        yc = y - mu
        var = jnp.mean(yc * yc, axis=-1, keepdims=True)
        z = yc * lax.rsqrt(var + EPS) * g_ref[...] + beta_ref[...]
        o_ref[r * CONV_ROWS:(r + 1) * CONV_ROWS, :] = _silu(z).astype(BF16)
    buf_ref[0:CONV_HALO, :] = buf_ref[TM:TM + CONV_HALO, :]


def _conformer(u, w, b, g, beta, bsz, seq, layer):
    n = u.shape[0]
    spt = seq // TM
    return pl.pallas_call(
        _conformer_kernel,
        grid=(bsz, spt),
        in_specs=[pl.BlockSpec((TM, 2 * BRANCH_WIDTH), lambda bi, i: (bi * spt + i, 0)),
                  _resident(w.shape, layer), _resident(b.shape, layer), _resident(g.shape, layer),
                  _resident(beta.shape, layer)],
        out_specs=pl.BlockSpec((TM, BRANCH_WIDTH), lambda bi, i: (bi * spt + i, 0)),
        out_shape=jax.ShapeDtypeStruct((n, BRANCH_WIDTH), BF16),
        scratch_shapes=[pltpu.VMEM((TM + CONV_HALO, BRANCH_WIDTH), F32),
                        pltpu.VMEM((TM, BRANCH_WIDTH), F32)],
        compiler_params=_params("arbitrary", "arbitrary"),
        name="conformer",
    )(u, w, b, g, beta)


def _extra_base(h):
    return (h // 2) * LANES + HEAD_DIM * (1 - h % 2)


def _fgate_placement():
    pk = np.zeros((LANES, BRANCH_WIDTH), np.float32)
    pq = np.zeros((LANES, BRANCH_WIDTH), np.float32)
    for h in range(N_HEADS):
        base = _extra_base(h)
        for i in range(N_PIECES):
            pk[SUBLANES * i + h, base + i] = 1.0
            pq[PIECE_ROWS, base + i] = -1.0
            pk[PIECE_ROWS, base + N_PIECES + i] = 1.0
            pq[SUBLANES * i + h, base + N_PIECES + i] = 1.0
    return jnp.asarray(pk, BF16), jnp.asarray(pq, BF16)


def _fgate_kernel(f_ref, b_ref, pkq_ref, kx_ref, qx_ref, carry_ref):
    @pl.when(pl.program_id(1) == 0)
    def _():
        carry_ref[...] = jnp.zeros_like(carry_ref)

    z = f_ref[...] + b_ref[...]
    log_f = -(jnp.maximum(-z, 0.0) + jnp.log1p(jnp.exp(-jnp.abs(z))))
    ri = lax.broadcasted_iota(jnp.int32, (CUM_T, CUM_T), 0)
    ci = lax.broadcasted_iota(jnp.int32, (CUM_T, CUM_T), 1)
    tri = jnp.where(ci <= ri, 1.0, 0.0).astype(BF16)
    sums = jnp.dot(tri, jnp.concatenate(_pieces(log_f), axis=1).astype(BF16), preferred_element_type=F32)
    c = (sums[:, 0:LANES] + sums[:, LANES:2 * LANES]) + sums[:, 2 * LANES:3 * LANES] + carry_ref[0:1, :]
    carry_ref[...] = jnp.broadcast_to(c[CUM_T - 1:CUM_T, :], carry_ref.shape)
    hi, mid, lo = _pieces(c * LOG2E)
    lane = lax.broadcasted_iota(jnp.int32, (CUM_T, LANES), 1)
    packed = jnp.where(lane < SUBLANES, hi,
                       jnp.where(lane < 2 * SUBLANES, pltpu.roll(mid, SUBLANES, axis=1),
                                 jnp.where(lane < PIECE_ROWS, pltpu.roll(lo, 2 * SUBLANES, axis=1),
                                           jnp.where(lane == PIECE_ROWS, 1.0, 0.0)))).astype(BF16)
    placed = jnp.dot(packed, pkq_ref[...], preferred_element_type=F32).astype(BF16)
    kx_ref[...] = placed[:, 0:BRANCH_WIDTH]
    qx_ref[...] = placed[:, BRANCH_WIDTH:2 * BRANCH_WIDTH]


def _fgate(f, b, bsz, seq, layer):
    n = f.shape[0]
    spt = seq // CUM_T
    pkq = jnp.concatenate(_fgate_placement(), axis=1)
    row = lambda w: pl.BlockSpec((CUM_T, w), lambda bi, i: (bi * spt + i, 0))
    return pl.pallas_call(
        _fgate_kernel,
        grid=(bsz, spt),
        in_specs=[row(LANES), _resident(b.shape, layer), _resident(pkq.shape)],
        out_specs=(row(BRANCH_WIDTH), row(BRANCH_WIDTH)),
        out_shape=(jax.ShapeDtypeStruct((n, BRANCH_WIDTH), BF16),
                   jax.ShapeDtypeStruct((n, BRANCH_WIDTH), BF16)),
        scratch_shapes=[pltpu.VMEM((SUBLANES, LANES), F32)],
        compiler_params=_params("arbitrary", "arbitrary"),
        name="fgate_cumsum",
    )(f, b, pkq)


def _build_kv(k_ref, v_ref, extra_of, ka_ref, vt_ref, seq):
    lane = lax.broadcasted_iota(jnp.int32, (KV_ROWS, LANES), 1)
    lower = lane < HEAD_DIM
    ones_rows = jnp.where(lax.broadcasted_iota(jnp.int32, (V_ROWS - HEAD_DIM, KV_ROWS), 0) == 0, 1.0, 0.0)

    def body(i, carry):
        rows = pl.ds(pl.multiple_of(i * KV_ROWS, KV_ROWS), KV_ROWS)
        extra = extra_of(i, rows)
        for pair in range(STEP_HEADS // 2):
            lanes = slice(pair * LANES, (pair + 1) * LANES)
            k, x = k_ref[rows, lanes], extra[:, lanes]
            ka_ref[2 * pair, rows, :] = jnp.where(lower, k, x)
            ka_ref[2 * pair + 1, rows, :] = jnp.where(lower, x, k)
            v_t = v_ref[rows, lanes].astype(F32).T
            for hh in range(2):
                vt_ref[2 * pair + hh, :, rows] = jnp.concatenate(
                    [v_t[hh * HEAD_DIM:(hh + 1) * HEAD_DIM], ones_rows], axis=0).astype(BF16)
        return carry

    lax.fori_loop(0, seq // KV_ROWS, body, 0)


def _extend_q_t(q_t, extras_t):
    out = []
    for hh in range(STEP_HEADS):
        q_h = q_t[hh * HEAD_DIM:(hh + 1) * HEAD_DIM]
        parts = [q_h, extras_t[hh]] if hh % 2 == 0 else [extras_t[hh], q_h]
        out.append(jnp.concatenate(parts, axis=0).astype(BF16))
    return out


def _flash_step(qa_t, ka_ref, vt_ref, state, start, n_tiles, tables=None, first=False):
    m_ref, acc_ref, s_ref = state
    heads = range(STEP_HEADS)
    m = [None if first else m_ref[hh, 0:1, :] for hh in heads]
    acc = [None if first else acc_ref[hh] for hh in heads]
    n_plain = n_tiles - (len(tables(0)) if tables else 0)
    tile_rows = lambda t: pl.ds(pl.multiple_of(start + t * TK, TK), TK)
    ring = SCORE_LOOKAHEAD + 1

    def score(t):
        for hh in heads:
            s_ref[t % ring, hh] = jnp.dot(ka_ref[hh, tile_rows(t), :], qa_t[hh],
                                          preferred_element_type=F32)

    for t in range(min(SCORE_LOOKAHEAD, n_tiles)):
        score(t)
    for t in range(n_tiles):
        rows = tile_rows(t)
        if t + SCORE_LOOKAHEAD < n_tiles:
            score(t + SCORE_LOOKAHEAD)
        for hh in heads:
            s = s_ref[t % ring, hh]
            if t >= n_plain:
                s = s + tables(hh)[t - n_plain]
            col_max = jnp.max(s, axis=0, keepdims=True)
            m_new = col_max if m[hh] is None else jnp.maximum(m[hh], col_max)
            p = jnp.exp2(s - m_new).astype(BF16)
            pv = jnp.dot(vt_ref[hh, :, rows], p, preferred_element_type=F32)
            acc[hh] = pv if acc[hh] is None else jnp.exp2(m[hh] - m_new) * acc[hh] + pv
            m[hh] = m_new
    for hh in heads:
        m_ref[hh] = jnp.broadcast_to(m[hh], (SUBLANES, TQ))
        acc_ref[hh] = acc[hh]


def _pair_output(o_ref, acc_ref):
    out_t = jnp.concatenate(
        [acc_ref[hh, 0:HEAD_DIM, :] / acc_ref[hh, HEAD_DIM:HEAD_DIM + 1, :] for hh in range(STEP_HEADS)],
        axis=0)
    o_ref[...] = out_t.T.astype(BF16)


def _flash_scratch(seq):
    return [pltpu.VMEM((STEP_HEADS, seq, LANES), BF16), pltpu.VMEM((STEP_HEADS, V_ROWS, seq), BF16),
            pltpu.VMEM((STEP_HEADS, SUBLANES, TQ), F32), pltpu.VMEM((STEP_HEADS, V_ROWS, TQ), F32),
            pltpu.VMEM((SCORE_LOOKAHEAD + 1, STEP_HEADS, TK, TQ), F32)]


def _flash_chunks(qa, ka_ref, vt_ref, state, n_chunks):
    def chunk(c, carry):
        _flash_step(qa, ka_ref, vt_ref, state, c * (CHUNK_TILES * TK), CHUNK_TILES)
        return carry

    lax.fori_loop(0, n_chunks, chunk, 0)


def _flash_tail(qa, ka_ref, vt_ref, state, last_tile, n_loose, tail_tables, guard=None):
    rem = n_loose % CHUNK_TILES
    for r in range(CHUNK_TILES):
        if isinstance(rem, int) and rem != r:
            continue
        tiles = r + len(tail_tables(0))
        cond = guard if isinstance(rem, int) else (rem == r if guard is None else (rem == r) & guard)

        def run(tiles=tiles):
            _flash_step(qa, ka_ref, vt_ref, state, (last_tile + 1 - tiles) * TK, tiles,
                        tables=tail_tables, first=True)

        if cond is None or isinstance(cond, (bool, np.bool_)):
            if cond is None or cond:
                run()
        else:
            pl.when(cond)(run)


def _fox_kernel(q_ref, qx_ref, k_ref, v_ref, kx_ref, o_ref, ka_ref, vt_ref, *state, seq):
    _build_kv(k_ref, v_ref, lambda i, rows: kx_ref[rows, :], ka_ref, vt_ref, seq)
    causal = jnp.where(lax.broadcasted_iota(jnp.int32, (TK, TQ), 0)
                       <= lax.broadcasted_iota(jnp.int32, (TK, TQ), 1), 0.0, NEG)

    def query_tile(qt, carry):
        rows = pl.ds(pl.multiple_of(qt * TQ, TQ), TQ)
        q_t = (q_ref[rows, :].astype(F32) * (SCALE * LOG2E)).T
        qx_t = qx_ref[rows, :].astype(F32).T
        qa = _extend_q_t(q_t, [qx_t[(hh ^ 1) * HEAD_DIM:((hh ^ 1) + 1) * HEAD_DIM]
                               for hh in range(STEP_HEADS)])
        _flash_tail(qa, ka_ref, vt_ref, state, qt, qt, lambda hh: [causal])
        _flash_chunks(qa, ka_ref, vt_ref, state, qt // CHUNK_TILES)
        _pair_output(o_ref.at[rows, :], state[1])
        return carry

    lax.fori_loop(0, seq // TQ, query_tile, 0)


def _fox(qkv, kx, qx, bsz, seq):
    n = qkv.shape[0]
    koff, voff = BRANCH_WIDTH // STEP_LANES, 2 * BRANCH_WIDTH // STEP_LANES
    per_seq = lambda off: pl.BlockSpec((seq, STEP_LANES), lambda b, p: (b, off + p))
    return pl.pallas_call(
        functools.partial(_fox_kernel, seq=seq),
        grid=(bsz, N_HEADS // STEP_HEADS),
        in_specs=[per_seq(0), per_seq(0), per_seq(koff), per_seq(voff), per_seq(0)],
        out_specs=per_seq(0),
        out_shape=jax.ShapeDtypeStruct((n, BRANCH_WIDTH), BF16),
        scratch_shapes=_flash_scratch(seq),
        compiler_params=_params("arbitrary", "arbitrary"),
        name="fox",
    )(qkv, qx, qkv, qkv, kx)


def _moba_select(qseq_ref, kmean_ref, pen_ref, seq):
    nblk = seq // MOBA_BLOCK
    blk_shift = MOBA_BLOCK.bit_length() - 1
    kmean = kmean_ref[0:nblk, :]
    lane_head = jnp.right_shift(lax.broadcasted_iota(jnp.int32, (nblk, STEP_LANES), 1),
                                HEAD_DIM.bit_length() - 1)
    blk = lax.broadcasted_iota(jnp.int32, (nblk, SELECT_COLS), 0)
    kmean_pieces = [[piece.astype(BF16) for piece in _pieces(jnp.where(lane_head == hh, kmean, 0.0))]
                    for hh in range(STEP_HEADS)]

    def body(i, carry):
        cols = pl.ds(pl.multiple_of(i * SELECT_COLS, SELECT_COLS), SELECT_COLS)
        q = qseq_ref[cols, :]
        own = jnp.right_shift(i * SELECT_COLS + lax.broadcasted_iota(jnp.int32, (nblk, SELECT_COLS), 1),
                              blk_shift)
        past = blk < own
        for hh in range(STEP_HEADS):
            hi, mid, lo = [lax.dot_general(piece, q, (((1,), (1,)), ((), ())), preferred_element_type=F32)
                           for piece in kmean_pieces[hh]]
            gate = jnp.where(past, (hi + mid) + lo, NEG)
            sel = blk == own
            for _ in range(MOBA_TOPK):
                top = jnp.max(gate, axis=0, keepdims=True)
                idx = jnp.min(jnp.where(gate == top, blk, nblk), axis=0, keepdims=True)
                pick = blk == idx
                sel = sel | (pick & past)
                gate = jnp.where(pick, -jnp.inf, gate)
            pen_ref[hh, :, cols] = jnp.where(sel, 0.0, NEG)
        return carry

    lax.fori_loop(0, seq // SELECT_COLS, body, 0)


def _moba_kernel(tab_ref, q_ref, k_ref, v_ref, town_ref, tprev_ref, o_ref,
                 kmean_ref, pen_ref, ka_ref, vt_ref, *state, seq):
    group = pl.program_id(1)
    nblk = seq // MOBA_BLOCK
    blk_shift = MOBA_BLOCK.bit_length() - 1

    blk = lax.broadcasted_iota(jnp.int32, (LANES, seq), 0)
    pos = lax.broadcasted_iota(jnp.int32, (LANES, seq), 1)
    member = jnp.where(jnp.right_shift(pos, blk_shift) == blk, 1.0 / MOBA_BLOCK, 0.0).astype(BF16)
    kmean_ref[...] = jnp.dot(member, k_ref[...], preferred_element_type=F32)
    _moba_select(q_ref, kmean_ref, pen_ref, seq)

    def extra_of(i, rows):
        xl = jnp.bitwise_and(lax.broadcasted_iota(jnp.int32, (KV_ROWS, STEP_LANES), 1), HEAD_DIM - 1)
        row_blk = jnp.right_shift(
            i * KV_ROWS + lax.broadcasted_iota(jnp.int32, (KV_ROWS, STEP_LANES), 0), blk_shift)
        hit = (xl < N_PIECES) | (xl - MOBA_SEL_LANE == row_blk)
        return jnp.where(hit, 1.0, 0.0).astype(BF16)

    _build_kv(k_ref, v_ref, extra_of, ka_ref, vt_ref, seq)
    piece_row = lax.broadcasted_iota(jnp.int32, (MOBA_SEL_LANE, TQ), 0)
    far_rows = []
    for hh in range(STEP_HEADS):
        far = jnp.full((MOBA_SEL_LANE, TQ), tab_ref[REL_BUCKETS - 1, N_HEADS + STEP_HEADS * group + hh] * LOG2E, F32)
        hi, mid, lo = _pieces(far)
        far_rows.append(jnp.where(piece_row == 0, hi, jnp.where(
            piece_row == 1, mid, jnp.where(piece_row == 2, lo, 0.0))))

    def query_tile(qt, carry):
        rows = pl.ds(pl.multiple_of(qt * TQ, TQ), TQ)
        q_t = q_ref[rows, :].astype(F32).T
        extras = [jnp.concatenate(
            [far_rows[hh], pen_ref[hh, :, rows],
             jnp.zeros((HEAD_DIM - MOBA_SEL_LANE - nblk, TQ), F32)], axis=0) for hh in range(STEP_HEADS)]
        qa = _extend_q_t(q_t * (SCALE * LOG2E), extras)
        n_far = jnp.maximum(qt - 1, 0)
        _flash_tail(qa, ka_ref, vt_ref, state, qt, 0, lambda hh: [town_ref[hh]],
                    guard=qt == 0)
        _flash_tail(qa, ka_ref, vt_ref, state, qt, n_far,
                    lambda hh: [tprev_ref[hh], town_ref[hh]], guard=qt >= 1)
        _flash_chunks(qa, ka_ref, vt_ref, state, n_far // CHUNK_TILES)
        _pair_output(o_ref.at[rows, :], state[1])
        return carry

    lax.fori_loop(0, seq // TQ, query_tile, 0)


def _moba(qkv, rel_bias, t_own, t_prev, bsz, seq):
    n = qkv.shape[0]
    koff, voff = BRANCH_WIDTH // STEP_LANES, 2 * BRANCH_WIDTH // STEP_LANES
    per_seq = lambda off: pl.BlockSpec((seq, STEP_LANES), lambda b, p: (b, off + p))
    head_tables = pl.BlockSpec((STEP_HEADS, MOBA_BLOCK, MOBA_BLOCK), lambda b, p: (p, 0, 0))
    return pl.pallas_call(
        functools.partial(_moba_kernel, seq=seq),
        grid=(bsz, N_HEADS // STEP_HEADS),
        in_specs=[pl.BlockSpec(memory_space=pltpu.SMEM), per_seq(0), per_seq(koff), per_seq(voff),
                  head_tables, head_tables],
        out_specs=per_seq(0),
        out_shape=jax.ShapeDtypeStruct((n, BRANCH_WIDTH), BF16),
        scratch_shapes=[pltpu.VMEM((LANES, STEP_LANES), F32),
                        pltpu.VMEM((STEP_HEADS, seq // MOBA_BLOCK, seq), F32)] + _flash_scratch(seq),
        compiler_params=_params("arbitrary", "arbitrary"),
        name="moba",
    )(rel_bias, qkv, qkv, qkv, t_own, t_prev)


def _merge_kernel(x_ref, g_ref, ya_ref, yb_ref, yc_ref, yd_ref, wg_ref, bg_ref, wbr_ref, wout_ref, o_ref):
    x = x_ref[...]
    hb = _rmsnorm(x, g_ref[...]).astype(BF16)
    merged = jnp.zeros((TM, D_MODEL), F32)
    for n, y_ref in enumerate((ya_ref, yb_ref, yc_ref, yd_ref)):
        gz = jnp.dot(hb, wg_ref[:, n * D_MODEL:(n + 1) * D_MODEL], preferred_element_type=F32)
        gate = _sigmoid(gz + bg_ref[n:n + 1, :])
        merged = merged + gate * jnp.dot(y_ref[...], wbr_ref[n], preferred_element_type=F32)
    o_ref[...] = x + jnp.dot(merged.astype(BF16), wout_ref[...], preferred_element_type=F32)


def _merge(x, g, ys, wg, bg, wbr, wout, layer):
    n = x.shape[0]
    row = lambda w: pl.BlockSpec((TM, w), lambda i: (i, 0))
    return pl.pallas_call(
        _merge_kernel,
        grid=(n // TM,),
        in_specs=[row(D_MODEL), _resident(g.shape, layer)] + [row(BRANCH_WIDTH)] * N_BRANCH
        + [_resident(w.shape, layer) for w in (wg, bg, wbr, wout)],
        out_specs=row(D_MODEL),
        out_shape=jax.ShapeDtypeStruct((n, D_MODEL), F32),
        compiler_params=_params("arbitrary"),
        name="merge",
    )(x, g, *ys, wg, bg, wbr, wout)


def _ffn_kernel(x_ref, g_ref, wup_ref, cw_ref, cb_ref, wdown_ref, fg_ref, o_ref, halo_ref, buf_ref, a_ref,
                *, final):
    @pl.when(pl.program_id(1) == 0)
    def _():
        halo_ref[...] = jnp.zeros_like(halo_ref)

    x = x_ref[...]
    hb = _rmsnorm(x, g_ref[...]).astype(BF16)
    n_chunks = D_FF // FF_CHUNK
    col_of = lambda c, part: part * D_FF + c * FF_CHUNK

    def up(c):
        for part in range(2):
            col = col_of(c, part)
            buf_ref[c % 2, part, SUBLANES:SUBLANES + TM, :] = jnp.dot(
                hb, wup_ref[:, col:col + FF_CHUNK], preferred_element_type=F32)

    up(0)
    for c in range(n_chunks):
        if c + 1 < n_chunks:
            up(c + 1)
        halves = []
        for part in range(2):
            col = col_of(c, part)
            buf_ref[c % 2, part, 0:SUBLANES, :] = halo_ref[:, col:col + FF_CHUNK]
            halo_ref[:, col:col + FF_CHUNK] = buf_ref[c % 2, part, TM:TM + SUBLANES, :]
            u = cb_ref[:, col:col + FF_CHUNK]
            for k in range(FFN_CONV_WIDTH):
                lo = SUBLANES - (FFN_CONV_WIDTH - 1) + k
                u = u + cw_ref[k:k + 1, col:col + FF_CHUNK] * buf_ref[c % 2, part, lo:lo + TM, :]
            halves.append(u)
        a_ref[:, c * FF_CHUNK:(c + 1) * FF_CHUNK] = (
            _silu(halves[0]) * halves[1]).astype(BF16)
    y = x + jnp.dot(a_ref[...], wdown_ref[...], preferred_element_type=F32)
    o_ref[...] = _rmsnorm(y, fg_ref[...]) if final else y


def _ffn(x, g, wup, cw, cb, wdown, fg, bsz, seq, layer, final):
    n = x.shape[0]
    spt = seq // TM
    row = pl.BlockSpec((TM, D_MODEL), lambda bi, i: (bi * spt + i, 0))
    return pl.pallas_call(
        functools.partial(_ffn_kernel, final=final),
        grid=(bsz, spt),
        in_specs=[row] + [_resident(w.shape, layer) for w in (g, wup, cw, cb, wdown)]
        + [_resident(fg.shape)],
        out_specs=row,
        out_shape=jax.ShapeDtypeStruct((n, D_MODEL), F32),
        scratch_shapes=[pltpu.VMEM((SUBLANES, 2 * D_FF), F32),
                        pltpu.VMEM((2, 2, TM + SUBLANES, FF_CHUNK), F32),
                        pltpu.VMEM((TM, D_FF), BF16)],
        compiler_params=_params("arbitrary", "arbitrary"),
        name="ffn",
    )(x, g, wup, cw, cb, wdown, fg)


def _split_w_in(w_in):
    hd, bw = HEAD_DIM, BRANCH_WIDTH
    w_in = w_in.astype(BF16)
    o = 0
    qa = w_in[..., o:o + bw]; o += bw
    ka = w_in[..., o:o + SWA_KV_HEADS * hd]; o += SWA_KV_HEADS * hd
    va = w_in[..., o:o + SWA_KV_HEADS * hd]; o += SWA_KV_HEADS * hd
    ub = w_in[..., o:o + 2 * bw]; o += 2 * bw
    qkv_c = w_in[..., o:o + 3 * bw]; o += 3 * bw
    fc = w_in[..., o:o + N_HEADS]; o += N_HEADS
    qkv_d = w_in[..., o:o + 3 * bw]; o += 3 * bw
    gates = w_in[..., o:o + N_BRANCH * D_MODEL]
    dup = lambda w: jnp.concatenate([w[..., i * hd:(i + 1) * hd] for i in (0, 0, 1, 1)], axis=-1)
    wa = jnp.concatenate([qa, dup(ka), dup(va)], axis=-1)
    wf = jnp.pad(fc, ((0, 0), (0, 0), (0, LANES - N_HEADS)))
    return (wa, ub, qkv_c, wf, qkv_d), gates


def _layer(x, bsz, seq, layer, final, biases, rel_bias, p):
    bias_swa, table_own, table_prev = biases
    za, zb, zc, zf, zd = _inproj(x, p["ln1_g"], p["w_in_groups"], layer)
    ya = _swa(za, p["sinks"], bias_swa, seq, layer)
    yb = _conformer(zb, p["conv_w"], p["conv_b"], p["conv_ln_g"], p["conv_ln_b"], bsz, seq, layer)
    kx, qx = _fgate(zf, p["b_fgate"], bsz, seq, layer)
    yc = _fox(zc, kx, qx, bsz, seq)
    yd = _moba(zd, rel_bias, table_own, table_prev, bsz, seq)
    x = _merge(x, p["ln1_g"], (ya, yb, yc, yd), p["w_gates"], p["b_gate"], p["w_br"], p["w_out"], layer)
    return _ffn(x, p["ln2_g"], p["w_up"], p["ffn_conv_w"], p["ffn_conv_b"], p["w_down"], p["final_g"],
                bsz, seq, layer, final)


def kernel(x, ln1_g, w_in, b_gate, b_fgate, sinks, conv_w, conv_b, conv_ln_g, conv_ln_b, w_br, w_out,
           ln2_g, w_up, ffn_conv_w, ffn_conv_b, w_down, rel_bias, final_g):
    bsz, seq, _ = x.shape
    depth = w_in.shape[0]
    assert TQ == TK == MOBA_BLOCK
    assert all(seq % rows == 0 for rows in (TM, KV_ROWS, SELECT_COLS, CUM_T, TQ))
    n_blocks = seq // MOBA_BLOCK
    assert n_blocks % SUBLANES == 0 and MOBA_SEL_LANE + n_blocks <= HEAD_DIM
    biases = _expand_bias(rel_bias)
    rows = lambda v: v.reshape(depth, 1, -1)
    w_in_groups, w_gates = _split_w_in(w_in)
    params = dict(
        ln1_g=rows(ln1_g), w_in_groups=w_in_groups, w_gates=w_gates, b_gate=b_gate,
        b_fgate=rows(jnp.pad(b_fgate, ((0, 0), (0, LANES - N_HEADS)))), sinks=sinks,
        conv_w=conv_w, conv_b=rows(conv_b), conv_ln_g=rows(conv_ln_g), conv_ln_b=rows(conv_ln_b),
        w_br=w_br.astype(BF16), w_out=w_out.astype(BF16), ln2_g=rows(ln2_g), w_up=w_up.astype(BF16),
        ffn_conv_w=ffn_conv_w, ffn_conv_b=rows(ffn_conv_b), w_down=w_down.astype(BF16),
        final_g=final_g.reshape(1, -1))
    y = x.reshape(bsz * seq, D_MODEL)
    for layer in range(depth):
        y = _layer(y, bsz, seq, layer, layer == depth - 1, biases, rel_bias, params)
    return y.reshape(bsz, seq, D_MODEL)
```

```python
import functools
import math

import jax
import jax.numpy as jnp
import numpy as np
from jax import lax
from jax.experimental import pallas as pl
from jax.experimental.pallas import tpu as pltpu

D_MODEL = 1024
HEAD_DIM = 64
N_BRANCH = 4
BRANCH_WIDTH = 512
N_HEADS = 8
N_PAIRS = N_HEADS // 2
SWA_KV_HEADS = 2
SWA_BLOCK = 128
SWA_STEP_BLOCKS = 2
CONV_WIDTH = 31
MOBA_BLOCK = 256
MOBA_TOPK = 3
D_FF = 2816
FFN_CONV_WIDTH = 3
REL_BUCKETS = 32
REL_MAX_DIST = 128
EPS = 1e-6
NEG = -1e30
SCALE = HEAD_DIM ** -0.5
LOG2E = math.log2(math.e)

LANES = 128
SUBLANES = 8
VMEM_LIMIT = 56 * 1024 * 1024

TM = 512
TQ = 256
TK = 256
CHUNK_TILES = 16
SCORE_LOOKAHEAD = 3
KV_ROWS = 512
SELECT_COLS = 1024
CUM_T = 512
FF_CHUNK = 256
CONV_ROWS = 64
CONV_HALO = 32

N_PIECES = 3
PIECE_ROWS = SUBLANES * N_PIECES
MOBA_SEL_LANE = SUBLANES
STEP_HEADS = 2
STEP_LANES = STEP_HEADS * HEAD_DIM
BF16_ROWS = 2 * SUBLANES
V_ROWS = HEAD_DIM + BF16_ROWS

BF16 = jnp.bfloat16
F32 = jnp.float32


def _params(*sem):
    return pltpu.CompilerParams(dimension_semantics=sem, vmem_limit_bytes=VMEM_LIMIT)


def _resident(shape, layer=None):
    if layer is None:
        zeros = (0,) * len(shape)
        return pl.BlockSpec(shape, lambda *_: zeros, pipeline_mode=pl.Buffered(1))
    index = (layer,) + (0,) * (len(shape) - 1)
    return pl.BlockSpec((None,) + tuple(shape[1:]), lambda *_: index, pipeline_mode=pl.Buffered(1))


def _rmsnorm(x, g):
    return (x * lax.rsqrt(jnp.mean(x * x, axis=-1, keepdims=True) + EPS)) * g


def _sigmoid(x):
    return 0.5 + 0.5 * jnp.tanh(0.5 * x)


def _silu(x):
    half = 0.5 * x
    return half + half * jnp.tanh(half)


def _pieces(x):
    hi = x.astype(BF16).astype(F32)
    rest = x - hi
    mid = rest.astype(BF16).astype(F32)
    return hi, mid, rest - mid


def _rel_bucket_np(dist):
    n = np.maximum(dist, 0)
    max_exact = REL_BUCKETS // 2
    nf = np.maximum(n, 1).astype(np.float64)
    large = max_exact + (np.log(nf / max_exact) / math.log(REL_MAX_DIST / max_exact)
                         * (REL_BUCKETS - max_exact) + 1e-9).astype(np.int32)
    large = np.minimum(large, REL_BUCKETS - 1)
    return np.where(n < max_exact, n, large).astype(np.int32)


def _bias_kernel(tab_ref, bs_ref, bo_ref, bp_ref, os_ref, oo_ref, op_ref):
    os_ref[...] = jnp.zeros_like(os_ref)
    oo_ref[...] = jnp.zeros_like(oo_ref)
    op_ref[...] = jnp.zeros_like(op_ref)
    bs, bo, bp = bs_ref[...], bo_ref[...], bp_ref[...]

    def body(b, carry):
        ms, mo, mp = bs == b, bo == b, bp == b
        for h in range(N_HEADS):
            os_ref[h] = jnp.where(ms, tab_ref[b, h], os_ref[h])
            oo_ref[h] = jnp.where(mo, tab_ref[b, N_HEADS + h], oo_ref[h])
            op_ref[h] = jnp.where(mp, tab_ref[b, N_HEADS + h], op_ref[h])
        return carry

    lax.fori_loop(0, REL_BUCKETS, body, 0)
    causal = (lax.broadcasted_iota(jnp.int32, (MOBA_BLOCK, MOBA_BLOCK), 0)
              <= lax.broadcasted_iota(jnp.int32, (MOBA_BLOCK, MOBA_BLOCK), 1))
    for h in range(N_HEADS):
        far = tab_ref[REL_BUCKETS - 1, N_HEADS + h]
        oo_ref[h] = jnp.where(causal, (oo_ref[h] - far) * LOG2E, NEG)
        op_ref[h] = (op_ref[h] - far) * LOG2E


def _expand_bias(rel_bias):
    qi = np.arange(SWA_BLOCK)[:, None]
    kj = np.arange(2 * SWA_BLOCK)[None, :]
    b_swa = _rel_bucket_np(qi + SWA_BLOCK - kj)
    qi = np.arange(MOBA_BLOCK)[:, None]
    kj = np.arange(MOBA_BLOCK)[None, :]
    b_own = np.ascontiguousarray(_rel_bucket_np(qi - kj).T)
    b_prev = np.ascontiguousarray(_rel_bucket_np(qi - kj + MOBA_BLOCK).T)
    vm = pl.BlockSpec(memory_space=pltpu.VMEM)
    return pl.pallas_call(
        _bias_kernel,
        out_shape=(jax.ShapeDtypeStruct((N_HEADS, SWA_BLOCK, 2 * SWA_BLOCK), F32),
                   jax.ShapeDtypeStruct((N_HEADS, MOBA_BLOCK, MOBA_BLOCK), F32),
                   jax.ShapeDtypeStruct((N_HEADS, MOBA_BLOCK, MOBA_BLOCK), F32)),
        in_specs=[pl.BlockSpec(memory_space=pltpu.SMEM), vm, vm, vm],
        out_specs=(vm, vm, vm),
        name="bias_expand",
    )(rel_bias, jnp.asarray(b_swa), jnp.asarray(b_own), jnp.asarray(b_prev))


def _inproj_kernel(x_ref, g_ref, wa_ref, wb_ref, wc_ref, wf_ref, wd_ref,
                   oa_ref, ob_ref, oc_ref, of_ref, od_ref):
    h = _rmsnorm(x_ref[...], g_ref[...])
    hb = h.astype(BF16)
    oa_ref[...] = jnp.dot(hb, wa_ref[...], preferred_element_type=F32).astype(BF16)
    ob_ref[...] = jnp.dot(hb, wb_ref[...], preferred_element_type=F32)
    oc_ref[...] = jnp.dot(hb, wc_ref[...], preferred_element_type=F32).astype(BF16)
    od_ref[...] = jnp.dot(hb, wd_ref[...], preferred_element_type=F32).astype(BF16)
    of_ref[...] = jnp.dot(hb, wf_ref[...], preferred_element_type=F32)


def _inproj(x, g, ws, layer):
    n = x.shape[0]
    row = lambda w: pl.BlockSpec((TM, w), lambda i: (i, 0))
    dts = (BF16, F32, BF16, F32, BF16)
    return pl.pallas_call(
        _inproj_kernel,
        grid=(n // TM,),
        in_specs=[row(D_MODEL), _resident(g.shape, layer)] + [_resident(w.shape, layer) for w in ws],
        out_specs=tuple(row(w.shape[-1]) for w in ws),
        out_shape=tuple(jax.ShapeDtypeStruct((n, w.shape[-1]), dt) for w, dt in zip(ws, dts)),
        compiler_params=_params("arbitrary"),
        name="inproj",
    )(x, g, *ws)


def _swa_kernel(sink_ref, q_ref, kc_ref, kp_ref, vc_ref, vp_ref, bias_ref, o_ref, s_ref, *,
                blocks_per_seq, layer):
    first_blk = (pl.program_id(0) * SWA_STEP_BLOCKS) % blocks_per_seq
    qi = lax.broadcasted_iota(jnp.int32, (SWA_BLOCK, 2 * SWA_BLOCK), 0)
    kj = lax.broadcasted_iota(jnp.int32, (SWA_BLOCK, 2 * SWA_BLOCK), 1)
    dist = qi + SWA_BLOCK - kj
    window = (dist >= 0) & (dist < SWA_BLOCK)
    lane = lax.broadcasted_iota(jnp.int32, (SWA_BLOCK, LANES), 1)
    upper = lane >= HEAD_DIM
    kv_of = lambda h: h // (N_HEADS // SWA_KV_HEADS)
    block_rows = lambda sb: slice(sb * SWA_BLOCK, (sb + 1) * SWA_BLOCK)
    keys = lambda sb, cur, prev: jnp.concatenate(
        [prev[...] if sb == 0 else cur[block_rows(sb - 1), :], cur[block_rows(sb), :]], axis=0)
    for sb in range(SWA_STEP_BLOCKS):
        kk = keys(sb, kc_ref, kp_ref)
        for h in range(N_HEADS):
            q_pair = q_ref[block_rows(sb), (h // 2) * LANES:(h // 2 + 1) * LANES] * SCALE
            q_h = jnp.where(upper == (h % 2 == 1), q_pair, jnp.zeros_like(q_pair))
            k_kv = kk[:, kv_of(h) * LANES:(kv_of(h) + 1) * LANES]
            s_ref[sb, h] = lax.dot_general(q_h, k_kv, (((1,), (1,)), ((), ())), preferred_element_type=F32)
    probs = []
    for sb in range(SWA_STEP_BLOCKS):
        mask = window & ((first_blk > 0) | (kj >= SWA_BLOCK)) if sb == 0 else window
        for h in range(N_HEADS):
            logits = jnp.where(mask, s_ref[sb, h] + bias_ref[h], NEG)
            sink = sink_ref[layer, h]
            m = jnp.maximum(jnp.max(logits, axis=-1, keepdims=True), sink)
            p = jnp.exp(logits - m)
            denom = jnp.sum(p, axis=-1, keepdims=True) + jnp.exp(sink - m)
            probs.append((p / denom).astype(BF16))
    for sb in range(SWA_STEP_BLOCKS):
        vv = keys(sb, vc_ref, vp_ref)
        outs = [jnp.dot(probs[sb * N_HEADS + h], vv[:, kv_of(h) * LANES:(kv_of(h) + 1) * LANES],
                        preferred_element_type=F32) for h in range(N_HEADS)]
        for pair in range(N_PAIRS):
            o_ref[block_rows(sb), pair * LANES:(pair + 1) * LANES] = jnp.where(
                upper, outs[2 * pair + 1], outs[2 * pair]).astype(BF16)


def _swa(qkv, sinks, bias, seq, layer):
    n = qkv.shape[0]
    bps = seq // SWA_BLOCK
    assert bps % SWA_STEP_BLOCKS == 0
    step_rows = SWA_STEP_BLOCKS * SWA_BLOCK
    kcol, vcol = BRANCH_WIDTH // (2 * LANES), BRANCH_WIDTH // (2 * LANES) + 1
    prev = lambda i: jnp.where((i * SWA_STEP_BLOCKS) % bps == 0, i * SWA_STEP_BLOCKS, i * SWA_STEP_BLOCKS - 1)
    return pl.pallas_call(
        functools.partial(_swa_kernel, blocks_per_seq=bps, layer=layer),
        grid=(n // step_rows,),
        in_specs=[pl.BlockSpec(memory_space=pltpu.SMEM),
                  pl.BlockSpec((step_rows, BRANCH_WIDTH), lambda i: (i, 0)),
                  pl.BlockSpec((step_rows, 2 * LANES), lambda i: (i, kcol)),
                  pl.BlockSpec((SWA_BLOCK, 2 * LANES), lambda i: (prev(i), kcol)),
                  pl.BlockSpec((step_rows, 2 * LANES), lambda i: (i, vcol)),
                  pl.BlockSpec((SWA_BLOCK, 2 * LANES), lambda i: (prev(i), vcol)),
                  _resident(bias.shape)],
        out_specs=pl.BlockSpec((step_rows, BRANCH_WIDTH), lambda i: (i, 0)),
        out_shape=jax.ShapeDtypeStruct((n, BRANCH_WIDTH), BF16),
        scratch_shapes=[pltpu.VMEM((SWA_STEP_BLOCKS, N_HEADS, SWA_BLOCK, 2 * SWA_BLOCK), F32)],
        compiler_params=_params("arbitrary"),
        name="swa",
    )(sinks, qkv, qkv, qkv, qkv, qkv, bias)


def _conformer_kernel(u_ref, w_ref, b_ref, g_ref, beta_ref, o_ref, buf_ref, y_ref):
    @pl.when(pl.program_id(1) == 0)
    def _():
        buf_ref[0:CONV_HALO, :] = jnp.zeros((CONV_HALO, BRANCH_WIDTH), F32)

    a = u_ref[:, 0:BRANCH_WIDTH]
    gt = u_ref[:, BRANCH_WIDTH:2 * BRANCH_WIDTH]
    buf_ref[CONV_HALO:CONV_HALO + TM, :] = a * _sigmoid(gt)
    n_shifts = SUBLANES
    for r in range(TM // CONV_ROWS):
        row0 = r * CONV_ROWS
        for cg in range(BRANCH_WIDTH // LANES):
            cols = slice(cg * LANES, (cg + 1) * LANES)
            slabs = []
            for step in range((CONV_WIDTH - 1) // n_shifts + 1):
                lo = CONV_HALO - n_shifts * (step + 1) + row0
                slabs.append(buf_ref[lo:lo + CONV_ROWS + n_shifts, cols])
            y = None
            for shift in range(n_shifts):
                z = None
                for step in range((CONV_WIDTH - 1 - shift) // n_shifts + 1):
                    k = CONV_WIDTH - 1 - (n_shifts * step + shift)
                    term = w_ref[k:k + 1, cols] * slabs[step]
                    z = term if z is None else z + term
                shifted = z[n_shifts - shift:n_shifts - shift + CONV_ROWS]
                y = shifted if y is None else y + shifted
            y_ref[row0:row0 + CONV_ROWS, cols] = y
        y = y_ref[row0:row0 + CONV_ROWS, :] + b_ref[...]
        mu = jnp.mean(y, axis=-1, keepdims=True)
        yc = y - mu
        var = jnp.mean(yc * yc, axis=-1, keepdims=True)
        z = yc * lax.rsqrt(var + EPS) * g_ref[...] + beta_ref[...]
        o_ref[r * CONV_ROWS:(r + 1) * CONV_ROWS, :] = _silu(z).astype(BF16)
    buf_ref[0:CONV_HALO, :] = buf_ref[TM:TM + CONV_HALO, :]


def _conformer(u, w, b, g, beta, bsz, seq, layer):
    n = u.shape[0]
    spt = seq // TM
    return pl.pallas_call(
        _conformer_kernel,
        grid=(bsz, spt),
        in_specs=[pl.BlockSpec((TM, 2 * BRANCH_WIDTH), lambda bi, i: (bi * spt + i, 0)),
                  _resident(w.shape, layer), _resident(b.shape, layer), _resident(g.shape, layer),
                  _resident(beta.shape, layer)],
        out_specs=pl.BlockSpec((TM, BRANCH_WIDTH), lambda bi, i: (bi * spt + i, 0)),
        out_shape=jax.ShapeDtypeStruct((n, BRANCH_WIDTH), BF16),
        scratch_shapes=[pltpu.VMEM((TM + CONV_HALO, BRANCH_WIDTH), F32),
                        pltpu.VMEM((TM, BRANCH_WIDTH), F32)],
        compiler_params=_params("arbitrary", "arbitrary"),
        name="conformer",
    )(u, w, b, g, beta)


def _extra_base(h):
    return (h // 2) * LANES + HEAD_DIM * (1 - h % 2)


def _fgate_placement():
    pk = np.zeros((LANES, BRANCH_WIDTH), np.float32)
    pq = np.zeros((LANES, BRANCH_WIDTH), np.float32)
    for h in range(N_HEADS):
        base = _extra_base(h)
        for i in range(N_PIECES):
            pk[SUBLANES * i + h, base + i] = 1.0
            pq[PIECE_ROWS, base + i] = -1.0
            pk[PIECE_ROWS, base + N_PIECES + i] = 1.0
            pq[SUBLANES * i + h, base + N_PIECES + i] = 1.0
    return jnp.asarray(pk, BF16), jnp.asarray(pq, BF16)


def _fgate_kernel(f_ref, b_ref, pkq_ref, kx_ref, qx_ref, carry_ref):
    @pl.when(pl.program_id(1) == 0)
    def _():
        carry_ref[...] = jnp.zeros_like(carry_ref)

    z = f_ref[...] + b_ref[...]
    log_f = -(jnp.maximum(-z, 0.0) + jnp.log1p(jnp.exp(-jnp.abs(z))))
    ri = lax.broadcasted_iota(jnp.int32, (CUM_T, CUM_T), 0)
    ci = lax.broadcasted_iota(jnp.int32, (CUM_T, CUM_T), 1)
    tri = jnp.where(ci <= ri, 1.0, 0.0).astype(BF16)
    sums = jnp.dot(tri, jnp.concatenate(_pieces(log_f), axis=1).astype(BF16), preferred_element_type=F32)
    c = (sums[:, 0:LANES] + sums[:, LANES:2 * LANES]) + sums[:, 2 * LANES:3 * LANES] + carry_ref[0:1, :]
    carry_ref[...] = jnp.broadcast_to(c[CUM_T - 1:CUM_T, :], carry_ref.shape)
    hi, mid, lo = _pieces(c * LOG2E)
    lane = lax.broadcasted_iota(jnp.int32, (CUM_T, LANES), 1)
    packed = jnp.where(lane < SUBLANES, hi,
                       jnp.where(lane < 2 * SUBLANES, pltpu.roll(mid, SUBLANES, axis=1),
                                 jnp.where(lane < PIECE_ROWS, pltpu.roll(lo, 2 * SUBLANES, axis=1),
                                           jnp.where(lane == PIECE_ROWS, 1.0, 0.0)))).astype(BF16)
    placed = jnp.dot(packed, pkq_ref[...], preferred_element_type=F32).astype(BF16)
    kx_ref[...] = placed[:, 0:BRANCH_WIDTH]
    qx_ref[...] = placed[:, BRANCH_WIDTH:2 * BRANCH_WIDTH]


def _fgate(f, b, bsz, seq, layer):
    n = f.shape[0]
    spt = seq // CUM_T
    pkq = jnp.concatenate(_fgate_placement(), axis=1)
    row = lambda w: pl.BlockSpec((CUM_T, w), lambda bi, i: (bi * spt + i, 0))
    return pl.pallas_call(
        _fgate_kernel,
        grid=(bsz, spt),
        in_specs=[row(LANES), _resident(b.shape, layer), _resident(pkq.shape)],
        out_specs=(row(BRANCH_WIDTH), row(BRANCH_WIDTH)),
        out_shape=(jax.ShapeDtypeStruct((n, BRANCH_WIDTH), BF16),
                   jax.ShapeDtypeStruct((n, BRANCH_WIDTH), BF16)),
        scratch_shapes=[pltpu.VMEM((SUBLANES, LANES), F32)],
        compiler_params=_params("arbitrary", "arbitrary"),
        name="fgate_cumsum",
    )(f, b, pkq)


def _build_kv(k_ref, v_ref, extra_of, ka_ref, vt_ref, seq):
    lane = lax.broadcasted_iota(jnp.int32, (KV_ROWS, LANES), 1)
    lower = lane < HEAD_DIM
    ones_rows = jnp.where(lax.broadcasted_iota(jnp.int32, (V_ROWS - HEAD_DIM, KV_ROWS), 0) == 0, 1.0, 0.0)

    def body(i, carry):
        rows = pl.ds(pl.multiple_of(i * KV_ROWS, KV_ROWS), KV_ROWS)
        extra = extra_of(i, rows)
        for pair in range(STEP_HEADS // 2):
            lanes = slice(pair * LANES, (pair + 1) * LANES)
            k, x = k_ref[rows, lanes], extra[:, lanes]
            ka_ref[2 * pair, rows, :] = jnp.where(lower, k, x)
            ka_ref[2 * pair + 1, rows, :] = jnp.where(lower, x, k)
            v_t = v_ref[rows, lanes].astype(F32).T
            for hh in range(2):
                vt_ref[2 * pair + hh, :, rows] = jnp.concatenate(
                    [v_t[hh * HEAD_DIM:(hh + 1) * HEAD_DIM], ones_rows], axis=0).astype(BF16)
        return carry

    lax.fori_loop(0, seq // KV_ROWS, body, 0)


def _extend_q_t(q_t, extras_t):
    out = []
    for hh in range(STEP_HEADS):
        q_h = q_t[hh * HEAD_DIM:(hh + 1) * HEAD_DIM]
        parts = [q_h, extras_t[hh]] if hh % 2 == 0 else [extras_t[hh], q_h]
        out.append(jnp.concatenate(parts, axis=0).astype(BF16))
    return out


def _flash_step(qa_t, ka_ref, vt_ref, state, start, n_tiles, tables=None, first=False):
    m_ref, acc_ref, s_ref = state
    heads = range(STEP_HEADS)
    m = [None if first else m_ref[hh, 0:1, :] for hh in heads]
    acc = [None if first else acc_ref[hh] for hh in heads]
    n_plain = n_tiles - (len(tables(0)) if tables else 0)
    tile_rows = lambda t: pl.ds(pl.multiple_of(start + t * TK, TK), TK)
    ring = SCORE_LOOKAHEAD + 1

    def score(t):
        for hh in heads:
            s_ref[t % ring, hh] = jnp.dot(ka_ref[hh, tile_rows(t), :], qa_t[hh],
                                          preferred_element_type=F32)

    for t in range(min(SCORE_LOOKAHEAD, n_tiles)):
        score(t)
    for t in range(n_tiles):
        rows = tile_rows(t)
        if t + SCORE_LOOKAHEAD < n_tiles:
            score(t + SCORE_LOOKAHEAD)
        for hh in heads:
            s = s_ref[t % ring, hh]
            if t >= n_plain:
                s = s + tables(hh)[t - n_plain]
            col_max = jnp.max(s, axis=0, keepdims=True)
            m_new = col_max if m[hh] is None else jnp.maximum(m[hh], col_max)
            p = jnp.exp2(s - m_new).astype(BF16)
            pv = jnp.dot(vt_ref[hh, :, rows], p, preferred_element_type=F32)
            acc[hh] = pv if acc[hh] is None else jnp.exp2(m[hh] - m_new) * acc[hh] + pv
            m[hh] = m_new
    for hh in heads:
        m_ref[hh] = jnp.broadcast_to(m[hh], (SUBLANES, TQ))
        acc_ref[hh] = acc[hh]


def _pair_output(o_ref, acc_ref):
    out_t = jnp.concatenate(
        [acc_ref[hh, 0:HEAD_DIM, :] / acc_ref[hh, HEAD_DIM:HEAD_DIM + 1, :] for hh in range(STEP_HEADS)],
        axis=0)
    o_ref[...] = out_t.T.astype(BF16)


def _flash_scratch(seq):
    return [pltpu.VMEM((STEP_HEADS, seq, LANES), BF16), pltpu.VMEM((STEP_HEADS, V_ROWS, seq), BF16),
            pltpu.VMEM((STEP_HEADS, SUBLANES, TQ), F32), pltpu.VMEM((STEP_HEADS, V_ROWS, TQ), F32),
            pltpu.VMEM((SCORE_LOOKAHEAD + 1, STEP_HEADS, TK, TQ), F32)]


def _flash_chunks(qa, ka_ref, vt_ref, state, n_chunks):
    def chunk(c, carry):
        _flash_step(qa, ka_ref, vt_ref, state, c * (CHUNK_TILES * TK), CHUNK_TILES)
        return carry

    lax.fori_loop(0, n_chunks, chunk, 0)


def _flash_tail(qa, ka_ref, vt_ref, state, last_tile, n_loose, tail_tables, guard=None):
    rem = n_loose % CHUNK_TILES
    for r in range(CHUNK_TILES):
        if isinstance(rem, int) and rem != r:
            continue
        tiles = r + len(tail_tables(0))
        cond = guard if isinstance(rem, int) else (rem == r if guard is None else (rem == r) & guard)

        def run(tiles=tiles):
            _flash_step(qa, ka_ref, vt_ref, state, (last_tile + 1 - tiles) * TK, tiles,
                        tables=tail_tables, first=True)

        if cond is None or isinstance(cond, (bool, np.bool_)):
            if cond is None or cond:
                run()
        else:
            pl.when(cond)(run)


def _fox_kernel(q_ref, qx_ref, k_ref, v_ref, kx_ref, o_ref, ka_ref, vt_ref, *state, seq):
    _build_kv(k_ref, v_ref, lambda i, rows: kx_ref[rows, :], ka_ref, vt_ref, seq)
    causal = jnp.where(lax.broadcasted_iota(jnp.int32, (TK, TQ), 0)
                       <= lax.broadcasted_iota(jnp.int32, (TK, TQ), 1), 0.0, NEG)

    def query_tile(qt, carry):
        rows = pl.ds(pl.multiple_of(qt * TQ, TQ), TQ)
        q_t = (q_ref[rows, :].astype(F32) * (SCALE * LOG2E)).T
        qx_t = qx_ref[rows, :].astype(F32).T
        qa = _extend_q_t(q_t, [qx_t[(hh ^ 1) * HEAD_DIM:((hh ^ 1) + 1) * HEAD_DIM]
                               for hh in range(STEP_HEADS)])
        _flash_tail(qa, ka_ref, vt_ref, state, qt, qt, lambda hh: [causal])
        _flash_chunks(qa, ka_ref, vt_ref, state, qt // CHUNK_TILES)
        _pair_output(o_ref.at[rows, :], state[1])
        return carry

    lax.fori_loop(0, seq // TQ, query_tile, 0)


def _fox(qkv, kx, qx, bsz, seq):
    n = qkv.shape[0]
    koff, voff = BRANCH_WIDTH // STEP_LANES, 2 * BRANCH_WIDTH // STEP_LANES
    per_seq = lambda off: pl.BlockSpec((seq, STEP_LANES), lambda b, p: (b, off + p))
    return pl.pallas_call(
        functools.partial(_fox_kernel, seq=seq),
        grid=(bsz, N_HEADS // STEP_HEADS),
        in_specs=[per_seq(0), per_seq(0), per_seq(koff), per_seq(voff), per_seq(0)],
        out_specs=per_seq(0),
        out_shape=jax.ShapeDtypeStruct((n, BRANCH_WIDTH), BF16),
        scratch_shapes=_flash_scratch(seq),
        compiler_params=_params("arbitrary", "arbitrary"),
        name="fox",
    )(qkv, qx, qkv, qkv, kx)


def _moba_select(qseq_ref, kmean_ref, pen_ref, seq):
    nblk = seq // MOBA_BLOCK
    blk_shift = MOBA_BLOCK.bit_length() - 1
    kmean = kmean_ref[0:nblk, :]
    lane_head = jnp.right_shift(lax.broadcasted_iota(jnp.int32, (nblk, STEP_LANES), 1),
                                HEAD_DIM.bit_length() - 1)
    blk = lax.broadcasted_iota(jnp.int32, (nblk, SELECT_COLS), 0)
    kmean_pieces = [[piece.astype(BF16) for piece in _pieces(jnp.where(lane_head == hh, kmean, 0.0))]
                    for hh in range(STEP_HEADS)]

    def body(i, carry):
        cols = pl.ds(pl.multiple_of(i * SELECT_COLS, SELECT_COLS), SELECT_COLS)
        q = qseq_ref[cols, :]
        own = jnp.right_shift(i * SELECT_COLS + lax.broadcasted_iota(jnp.int32, (nblk, SELECT_COLS), 1),
                              blk_shift)
        past = blk < own
        for hh in range(STEP_HEADS):
            hi, mid, lo = [lax.dot_general(piece, q, (((1,), (1,)), ((), ())), preferred_element_type=F32)
                           for piece in kmean_pieces[hh]]
            gate = jnp.where(past, (hi + mid) + lo, NEG)
            sel = blk == own
            for _ in range(MOBA_TOPK):
                top = jnp.max(gate, axis=0, keepdims=True)
                idx = jnp.min(jnp.where(gate == top, blk, nblk), axis=0, keepdims=True)
                pick = blk == idx
                sel = sel | (pick & past)
                gate = jnp.where(pick, -jnp.inf, gate)
            pen_ref[hh, :, cols] = jnp.where(sel, 0.0, NEG)
        return carry

    lax.fori_loop(0, seq // SELECT_COLS, body, 0)


def _moba_kernel(tab_ref, q_ref, k_ref, v_ref, town_ref, tprev_ref, o_ref,
                 kmean_ref, pen_ref, ka_ref, vt_ref, *state, seq):
    group = pl.program_id(1)
    nblk = seq // MOBA_BLOCK
    blk_shift = MOBA_BLOCK.bit_length() - 1

    blk = lax.broadcasted_iota(jnp.int32, (LANES, seq), 0)
    pos = lax.broadcasted_iota(jnp.int32, (LANES, seq), 1)
    member = jnp.where(jnp.right_shift(pos, blk_shift) == blk, 1.0 / MOBA_BLOCK, 0.0).astype(BF16)
    kmean_ref[...] = jnp.dot(member, k_ref[...], preferred_element_type=F32)
    _moba_select(q_ref, kmean_ref, pen_ref, seq)

    def extra_of(i, rows):
        xl = jnp.bitwise_and(lax.broadcasted_iota(jnp.int32, (KV_ROWS, STEP_LANES), 1), HEAD_DIM - 1)
        row_blk = jnp.right_shift(
            i * KV_ROWS + lax.broadcasted_iota(jnp.int32, (KV_ROWS, STEP_LANES), 0), blk_shift)
        hit = (xl < N_PIECES) | (xl - MOBA_SEL_LANE == row_blk)
        return jnp.where(hit, 1.0, 0.0).astype(BF16)

    _build_kv(k_ref, v_ref, extra_of, ka_ref, vt_ref, seq)
    piece_row = lax.broadcasted_iota(jnp.int32, (MOBA_SEL_LANE, TQ), 0)
    far_rows = []
    for hh in range(STEP_HEADS):
        far = jnp.full((MOBA_SEL_LANE, TQ), tab_ref[REL_BUCKETS - 1, N_HEADS + STEP_HEADS * group + hh] * LOG2E, F32)
        hi, mid, lo = _pieces(far)
        far_rows.append(jnp.where(piece_row == 0, hi, jnp.where(
            piece_row == 1, mid, jnp.where(piece_row == 2, lo, 0.0))))

    def query_tile(qt, carry):
        rows = pl.ds(pl.multiple_of(qt * TQ, TQ), TQ)
        q_t = q_ref[rows, :].astype(F32).T
        extras = [jnp.concatenate(
            [far_rows[hh], pen_ref[hh, :, rows],
             jnp.zeros((HEAD_DIM - MOBA_SEL_LANE - nblk, TQ), F32)], axis=0) for hh in range(STEP_HEADS)]
        qa = _extend_q_t(q_t * (SCALE * LOG2E), extras)
        n_far = jnp.maximum(qt - 1, 0)
        _flash_tail(qa, ka_ref, vt_ref, state, qt, 0, lambda hh: [town_ref[hh]],
                    guard=qt == 0)
        _flash_tail(qa, ka_ref, vt_ref, state, qt, n_far,
                    lambda hh: [tprev_ref[hh], town_ref[hh]], guard=qt >= 1)
        _flash_chunks(qa, ka_ref, vt_ref, state, n_far // CHUNK_TILES)
        _pair_output(o_ref.at[rows, :], state[1])
        return carry

    lax.fori_loop(0, seq // TQ, query_tile, 0)


def _moba(qkv, rel_bias, t_own, t_prev, bsz, seq):
    n = qkv.shape[0]
    koff, voff = BRANCH_WIDTH // STEP_LANES, 2 * BRANCH_WIDTH // STEP_LANES
    per_seq = lambda off: pl.BlockSpec((seq, STEP_LANES), lambda b, p: (b, off + p))
    head_tables = pl.BlockSpec((STEP_HEADS, MOBA_BLOCK, MOBA_BLOCK), lambda b, p: (p, 0, 0))
    return pl.pallas_call(
        functools.partial(_moba_kernel, seq=seq),
        grid=(bsz, N_HEADS // STEP_HEADS),
        in_specs=[pl.BlockSpec(memory_space=pltpu.SMEM), per_seq(0), per_seq(koff), per_seq(voff),
                  head_tables, head_tables],
        out_specs=per_seq(0),
        out_shape=jax.ShapeDtypeStruct((n, BRANCH_WIDTH), BF16),
        scratch_shapes=[pltpu.VMEM((LANES, STEP_LANES), F32),
                        pltpu.VMEM((STEP_HEADS, seq // MOBA_BLOCK, seq), F32)] + _flash_scratch(seq),
        compiler_params=_params("arbitrary", "arbitrary"),
        name="moba",
    )(rel_bias, qkv, qkv, qkv, t_own, t_prev)


def _merge_kernel(x_ref, g_ref, ya_ref, yb_ref, yc_ref, yd_ref, wg_ref, bg_ref, wbr_ref, wout_ref, o_ref):
    x = x_ref[...]
    hb = _rmsnorm(x, g_ref[...]).astype(BF16)
    merged = jnp.zeros((TM, D_MODEL), F32)
    for n, y_ref in enumerate((ya_ref, yb_ref, yc_ref, yd_ref)):
        gz = jnp.dot(hb, wg_ref[:, n * D_MODEL:(n + 1) * D_MODEL], preferred_element_type=F32)
        gate = _sigmoid(gz + bg_ref[n:n + 1, :])
        merged = merged + gate * jnp.dot(y_ref[...], wbr_ref[n], preferred_element_type=F32)
    o_ref[...] = x + jnp.dot(merged.astype(BF16), wout_ref[...], preferred_element_type=F32)


def _merge(x, g, ys, wg, bg, wbr, wout, layer):
    n = x.shape[0]
    row = lambda w: pl.BlockSpec((TM, w), lambda i: (i, 0))
    return pl.pallas_call(
        _merge_kernel,
        grid=(n // TM,),
        in_specs=[row(D_MODEL), _resident(g.shape, layer)] + [row(BRANCH_WIDTH)] * N_BRANCH
        + [_resident(w.shape, layer) for w in (wg, bg, wbr, wout)],
        out_specs=row(D_MODEL),
        out_shape=jax.ShapeDtypeStruct((n, D_MODEL), F32),
        compiler_params=_params("arbitrary"),
        name="merge",
    )(x, g, *ys, wg, bg, wbr, wout)


def _ffn_kernel(x_ref, g_ref, wup_ref, cw_ref, cb_ref, wdown_ref, fg_ref, o_ref, halo_ref, buf_ref, a_ref,
                *, final):
    @pl.when(pl.program_id(1) == 0)
    def _():
        halo_ref[...] = jnp.zeros_like(halo_ref)

    x = x_ref[...]
    hb = _rmsnorm(x, g_ref[...]).astype(BF16)
    n_chunks = D_FF // FF_CHUNK
    col_of = lambda c, part: part * D_FF + c * FF_CHUNK

    def up(c):
        for part in range(2):
            col = col_of(c, part)
            buf_ref[c % 2, part, SUBLANES:SUBLANES + TM, :] = jnp.dot(
                hb, wup_ref[:, col:col + FF_CHUNK], preferred_element_type=F32)

    up(0)
    for c in range(n_chunks):
        if c + 1 < n_chunks:
            up(c + 1)
        halves = []
        for part in range(2):
            col = col_of(c, part)
            buf_ref[c % 2, part, 0:SUBLANES, :] = halo_ref[:, col:col + FF_CHUNK]
            halo_ref[:, col:col + FF_CHUNK] = buf_ref[c % 2, part, TM:TM + SUBLANES, :]
            u = cb_ref[:, col:col + FF_CHUNK]
            for k in range(FFN_CONV_WIDTH):
                lo = SUBLANES - (FFN_CONV_WIDTH - 1) + k
                u = u + cw_ref[k:k + 1, col:col + FF_CHUNK] * buf_ref[c % 2, part, lo:lo + TM, :]
            halves.append(u)
        a_ref[:, c * FF_CHUNK:(c + 1) * FF_CHUNK] = (
            _silu(halves[0]) * halves[1]).astype(BF16)
    y = x + jnp.dot(a_ref[...], wdown_ref[...], preferred_element_type=F32)
    o_ref[...] = _rmsnorm(y, fg_ref[...]) if final else y


def _ffn(x, g, wup, cw, cb, wdown, fg, bsz, seq, layer, final):
    n = x.shape[0]
    spt = seq // TM
    row = pl.BlockSpec((TM, D_MODEL), lambda bi, i: (bi * spt + i, 0))
    return pl.pallas_call(
        functools.partial(_ffn_kernel, final=final),
        grid=(bsz, spt),
        in_specs=[row] + [_resident(w.shape, layer) for w in (g, wup, cw, cb, wdown)]
        + [_resident(fg.shape)],
        out_specs=row,
        out_shape=jax.ShapeDtypeStruct((n, D_MODEL), F32),
        scratch_shapes=[pltpu.VMEM((SUBLANES, 2 * D_FF), F32),
                        pltpu.VMEM((2, 2, TM + SUBLANES, FF_CHUNK), F32),
                        pltpu.VMEM((TM, D_FF), BF16)],
        compiler_params=_params("arbitrary", "arbitrary"),
        name="ffn",
    )(x, g, wup, cw, cb, wdown, fg)


def _split_w_in(w_in):
    hd, bw = HEAD_DIM, BRANCH_WIDTH
    w_in = w_in.astype(BF16)
    o = 0
    qa = w_in[..., o:o + bw]; o += bw
    ka = w_in[..., o:o + SWA_KV_HEADS * hd]; o += SWA_KV_HEADS * hd
    va = w_in[..., o:o + SWA_KV_HEADS * hd]; o += SWA_KV_HEADS * hd
    ub = w_in[..., o:o + 2 * bw]; o += 2 * bw
    qkv_c = w_in[..., o:o + 3 * bw]; o += 3 * bw
    fc = w_in[..., o:o + N_HEADS]; o += N_HEADS
    qkv_d = w_in[..., o:o + 3 * bw]; o += 3 * bw
    gates = w_in[..., o:o + N_BRANCH * D_MODEL]
    dup = lambda w: jnp.concatenate([w[..., i * hd:(i + 1) * hd] for i in (0, 0, 1, 1)], axis=-1)
    wa = jnp.concatenate([qa, dup(ka), dup(va)], axis=-1)
    wf = jnp.pad(fc, ((0, 0), (0, 0), (0, LANES - N_HEADS)))
    return (wa, ub, qkv_c, wf, qkv_d), gates


def _layer(x, bsz, seq, layer, final, biases, rel_bias, p):
    bias_swa, table_own, table_prev = biases
    za, zb, zc, zf, zd = _inproj(x, p["ln1_g"], p["w_in_groups"], layer)
    ya = _swa(za, p["sinks"], bias_swa, seq, layer)
    yb = _conformer(zb, p["conv_w"], p["conv_b"], p["conv_ln_g"], p["conv_ln_b"], bsz, seq, layer)
    kx, qx = _fgate(zf, p["b_fgate"], bsz, seq, layer)
    yc = _fox(zc, kx, qx, bsz, seq)
    yd = _moba(zd, rel_bias, table_own, table_prev, bsz, seq)
    x = _merge(x, p["ln1_g"], (ya, yb, yc, yd), p["w_gates"], p["b_gate"], p["w_br"], p["w_out"], layer)
    return _ffn(x, p["ln2_g"], p["w_up"], p["ffn_conv_w"], p["ffn_conv_b"], p["w_down"], p["final_g"],
                bsz, seq, layer, final)


def kernel(x, ln1_g, w_in, b_gate, b_fgate, sinks, conv_w, conv_b, conv_ln_g, conv_ln_b, w_br, w_out,
           ln2_g, w_up, ffn_conv_w, ffn_conv_b, w_down, rel_bias, final_g):
    bsz, seq, _ = x.shape
    depth = w_in.shape[0]
    assert TQ == TK == MOBA_BLOCK
    assert all(seq % rows == 0 for rows in (TM, KV_ROWS, SELECT_COLS, CUM_T, TQ))
    n_blocks = seq // MOBA_BLOCK
    assert n_blocks % SUBLANES == 0 and MOBA_SEL_LANE + n_blocks <= HEAD_DIM
    biases = _expand_bias(rel_bias)
    rows = lambda v: v.reshape(depth, 1, -1)
    w_in_groups, w_gates = _split_w_in(w_in)
    params = dict(
        ln1_g=rows(ln1_g), w_in_groups=w_in_groups, w_gates=w_gates, b_gate=b_gate,
        b_fgate=rows(jnp.pad(b_fgate, ((0, 0), (0, LANES - N_HEADS)))), sinks=sinks,
        conv_w=conv_w, conv_b=rows(conv_b), conv_ln_g=rows(conv_ln_g), conv_ln_b=rows(conv_ln_b),
        w_br=w_br.astype(BF16), w_out=w_out.astype(BF16), ln2_g=rows(ln2_g), w_up=w_up.astype(BF16),
        ffn_conv_w=ffn_conv_w, ffn_conv_b=rows(ffn_conv_b), w_down=w_down.astype(BF16),
        final_g=final_g.reshape(1, -1))
    y = x.reshape(bsz * seq, D_MODEL)
    for layer in range(depth):
        y = _layer(y, bsz, seq, layer, layer == depth - 1, biases, rel_bias, params)
    return y.reshape(bsz, seq, D_MODEL)
```
